```python
import math
import jax, jax.numpy as jnp
from jax import lax
import numpy as np

D_MODEL = 1024
BATCH = 8
SEQ = 2048
DEPTH = 2

ATT_HEADS = 6
HEAD_DIM = 64
ATT_W = ATT_HEADS * HEAD_DIM
ROPE_DIM = HEAD_DIM // 4
ROPE_THETA = 500000.0
MOBA_BLOCK = 256
MOBA_TOPK = 3
Q_BLOCK = 128
SSD_HEADS = 6
SSD_HEAD_DIM = 64
SSD_W = SSD_HEADS * SSD_HEAD_DIM
SSD_GROUPS = 2
SSD_STATE = 128
SSD_CONV = 4
SSD_CHUNK = 128
SSD_CONV_W = SSD_W + 2 * SSD_GROUPS * SSD_STATE
GM_GROUPS = 4
GM_W = 256
GM_CHUNK = 128
D_MIX = ATT_W + SSD_W + GM_W
D_IN = 3 * ATT_W + SSD_W + SSD_CONV_W + SSD_HEADS + 2 * GM_W
D_FF = 2816
N_EXPERTS = 8
TOP_K = 2
D_FF_EXPERT = 3584
N_DENSE = (DEPTH + 1) // 2
N_MOE = DEPTH // 2
ALPHA = (2.0 * DEPTH) ** 0.25
BETA = (8.0 * DEPTH) ** -0.25
NEG = -1e30

kernel_name = "hybrid_moba_ssd_gmlp_moe_deepnorm"


def layer_norm(x, g, b, eps=1e-5):
    xf = x.astype(jnp.float32)
    mu = jnp.mean(xf, -1, keepdims=True)
    var = jnp.mean(jnp.square(xf - mu), -1, keepdims=True)
    return ((xf - mu) * lax.rsqrt(var + eps)).astype(x.dtype) * g + b


def rms_norm(x, g, eps=1e-5):
    xf = x.astype(jnp.float32)
    return (xf * lax.rsqrt(jnp.mean(xf * xf, -1, keepdims=True) + eps)).astype(x.dtype) * g


def rope_tables(positions):
    inv = ROPE_THETA ** (-jnp.arange(0, ROPE_DIM, 2, dtype=jnp.float32) / ROPE_DIM)
    ang = positions.astype(jnp.float32)[..., None] * inv
    return jnp.cos(ang), jnp.sin(ang)


def apply_partial_rope(x, cos, sin):
    xr, xp = x[..., :ROPE_DIM], x[..., ROPE_DIM:]
    x1, x2 = jnp.split(xr, 2, axis=-1)
    c = cos[:, :, None, :].astype(x.dtype)
    s = sin[:, :, None, :].astype(x.dtype)
    return jnp.concatenate([x1 * c - x2 * s, x2 * c + x1 * s, xp], axis=-1)


def moba_attention(q, k, v):
    B_, S, H, D = q.shape
    nb = -(-S // MOBA_BLOCK)
    pad = nb * MOBA_BLOCK - S
    qh = q.transpose(0, 2, 1, 3)
    kh = jnp.pad(k.transpose(0, 2, 1, 3), ((0, 0), (0, 0), (0, pad), (0, 0)))
    vh = jnp.pad(v.transpose(0, 2, 1, 3), ((0, 0), (0, 0), (0, pad), (0, 0)))
    kb = kh.reshape(B_, H, nb, MOBA_BLOCK, D)
    vb = vh.reshape(B_, H, nb, MOBA_BLOCK, D)
    k_mean = jnp.mean(kb, axis=3)
    gate = jnp.einsum('bhsd,bhnd->bhsn', qh, k_mean).astype(jnp.float32)
    q_blk = jnp.arange(S) // MOBA_BLOCK
    past = jnp.arange(nb)[None, :] < q_blk[:, None]
    gate = jnp.where(past, gate, NEG)
    k_sel = min(MOBA_TOPK, nb)
    top_val, top_idx = lax.top_k(gate, k_sel)
    top_ok = top_val > NEG / 2

    nq = S // Q_BLOCK
    def to_blocks(t):
        return t.reshape(B_, H, nq, Q_BLOCK, *t.shape[3:]).swapaxes(1, 2).reshape(B_ * nq, H, Q_BLOCK, *t.shape[3:])
    qs, idx, ok = to_blocks(qh), to_blocks(top_idx), to_blocks(top_ok)
    flat = jnp.arange(B_ * nq, dtype=jnp.int32)
    b_ids, qb_ids = flat // nq, flat % nq
    scale = D ** -0.5
    causal_in_block = None

    def one_block(args):
        qi, idx_i, ok_i, b, j = args
        kb_b, vb_b = kb[b], vb[b]
        k_g = jax.vmap(lambda kk, ii: kk[ii])(kb_b, idx_i)
        v_g = jax.vmap(lambda vv, ii: vv[ii])(vb_b, idx_i)
        s_sel = jnp.einsum('hqd,hqkld->hqkl', qi, k_g).astype(jnp.float32) * scale
        s_sel = jnp.where(ok_i[..., None], s_sel, NEG)
        q_pos = j * Q_BLOCK + jnp.arange(Q_BLOCK)
        own = (j * Q_BLOCK) // MOBA_BLOCK
        k_own, v_own = kb_b[:, own], vb_b[:, own]
        k_pos = own * MOBA_BLOCK + jnp.arange(MOBA_BLOCK)
        s_own = jnp.einsum('hqd,hld->hql', qi, k_own).astype(jnp.float32) * scale
        s_own = jnp.where(k_pos[None, :] <= q_pos[:, None], s_own, NEG)
        s = jnp.concatenate([s_sel.reshape(H, Q_BLOCK, k_sel * MOBA_BLOCK), s_own], axis=-1)
        p = jax.nn.softmax(s, axis=-1).astype(qi.dtype)
        p_sel = p[..., :k_sel * MOBA_BLOCK].reshape(H, Q_BLOCK, k_sel, MOBA_BLOCK)
        p_own = p[..., k_sel * MOBA_BLOCK:]
        return jnp.einsum('hqkl,hqkld->hqd', p_sel, v_g) + jnp.einsum('hql,hld->hqd', p_own, v_own)

    out = lax.map(one_block, (qs, idx, ok, b_ids, qb_ids))
    return out.reshape(B_, nq, H, Q_BLOCK, D).transpose(0, 1, 3, 2, 4).reshape(B_, S, H * D)


def causal_depthwise_conv(x, w, b):
    K, C = w.shape
    y = lax.conv_general_dilated(x, w[:, None, :].astype(x.dtype), window_strides=(1,), padding=[(K - 1, 0)],
                                 dimension_numbers=('NWC', 'WIO', 'NWC'), feature_group_count=C)
    return y + b


def ssd_chunked(x, dt, A, Bm, Cm):
    in_dtype = x.dtype
    B_, S, H, P = x.shape
    nc = S // SSD_CHUNK
    rep = H // SSD_GROUPS
    xf = x.astype(jnp.float32)
    Bh = jnp.repeat(Bm.astype(jnp.float32), rep, axis=2)
    Ch = jnp.repeat(Cm.astype(jnp.float32), rep, axis=2)
    xdt = xf * dt[..., None]
    a = dt * A.astype(jnp.float32)
    def ch(t):
        return t.reshape(B_, nc, SSD_CHUNK, *t.shape[2:])
    xdt, Bh, Ch = ch(xdt), ch(Bh), ch(Ch)
    a_cum = jnp.cumsum(ch(a).transpose(0, 3, 1, 2), axis=-1)
    causal = jnp.tril(jnp.ones((SSD_CHUNK, SSD_CHUNK), bool))
    seg = a_cum[..., :, None] - a_cum[..., None, :]
    Lmat = jnp.exp(jnp.where(causal, seg, -jnp.inf))
    CB = jnp.einsum('bclhn,bcshn->bhcls', Ch, Bh)
    y_diag = jnp.einsum('bhcls,bcshp->bclhp', CB * Lmat, xdt)
    decay_states = jnp.exp(a_cum[..., -1:] - a_cum)
    states = jnp.einsum('bclhn,bhcl,bclhp->bchpn', Bh, decay_states, xdt)
    chunk_decay = jnp.exp(a_cum[..., -1])

    def step(hs, inp):
        st, dec = inp
        return hs * dec[..., None, None] + st, hs

    h0 = jnp.zeros((B_, H, P, SSD_STATE), jnp.float32)
    _, prev = lax.scan(step, h0, (states.transpose(1, 0, 2, 3, 4), chunk_decay.transpose(2, 0, 1)))
    prev = prev.transpose(1, 0, 2, 3, 4)
    y_off = jnp.einsum('bclhn,bchpn,bhcl->bclhp', Ch, prev, jnp.exp(a_cum))
    return (y_diag + y_off).reshape(B_, S, H, P).astype(in_dtype)


def ssd_mixer(z, xbc, dt_raw, conv_w, conv_b, dt_bias, a_log, d_skip, norm_g):
    B_, S, _ = z.shape
    xbc = jax.nn.silu(causal_depthwise_conv(xbc, conv_w, conv_b))
    xs, Bm, Cm = jnp.split(xbc, [SSD_W, SSD_W + SSD_GROUPS * SSD_STATE], axis=-1)
    xs = xs.reshape(B_, S, SSD_HEADS, SSD_HEAD_DIM)
    Bm = Bm.reshape(B_, S, SSD_GROUPS, SSD_STATE)
    Cm = Cm.reshape(B_, S, SSD_GROUPS, SSD_STATE)
    dt = jax.nn.softplus(dt_raw.astype(jnp.float32) + dt_bias.astype(jnp.float32))
    A = -jnp.exp(a_log.astype(jnp.float32))
    y = ssd_chunked(xs, dt, A, Bm, Cm) + d_skip[:, None] * xs
    yz = (y.reshape(B_, S, SSD_W) * jax.nn.silu(z)).astype(jnp.float32)
    yg = yz.reshape(B_, S, SSD_GROUPS, SSD_W // SSD_GROUPS)
    yg = yg * lax.rsqrt(jnp.mean(yg * yg, -1, keepdims=True) + 1e-5)
    return yg.reshape(B_, S, SSD_W).astype(z.dtype) * norm_g


def gmlp_sgu(u, v, ln_g, ln_b, w_s, b_s):
    B_, S, _ = v.shape
    nc = S // GM_CHUNK
    v = layer_norm(v, ln_g, ln_b)
    vc = v.reshape(B_, nc, GM_CHUNK, GM_GROUPS, GM_W // GM_GROUPS)
    w = jnp.where(jnp.tril(jnp.ones((GM_CHUNK, GM_CHUNK), bool)), w_s, 0.0)
    mixed = jnp.einsum('gts,bcsgd->bctgd', w.astype(v.dtype), vc) + b_s.T[None, None, :, :, None]
    return u * mixed.reshape(B_, S, GM_W)


def hybrid_mixer(h, cos, sin, w_in, conv_w, conv_b, dt_bias, a_log, d_skip, ssd_norm_g, att_norm_g,
                 gm_ln_g, gm_ln_b, gm_w_s, gm_b_s, gm_norm_g, w_out):
    B_, S, _ = h.shape
    proj = h @ w_in
    c1 = ATT_W
    c2 = 2 * ATT_W
    c3 = 3 * ATT_W
    c4 = c3 + SSD_W
    c5 = c4 + SSD_CONV_W
    c6 = c5 + SSD_HEADS
    c7 = c6 + GM_W
    q, k, v, z, xbc, dt_raw, gu, gv = jnp.split(proj, [c1, c2, c3, c4, c5, c6, c7], axis=-1)
    q = apply_partial_rope(q.reshape(B_, S, ATT_HEADS, HEAD_DIM), cos, sin)
    k = apply_partial_rope(k.reshape(B_, S, ATT_HEADS, HEAD_DIM), cos, sin)
    v = v.reshape(B_, S, ATT_HEADS, HEAD_DIM)
    att = rms_norm(moba_attention(q, k, v), att_norm_g)
    ssd = ssd_mixer(z, xbc, dt_raw, conv_w, conv_b, dt_bias, a_log, d_skip, ssd_norm_g)
    gm = rms_norm(gmlp_sgu(jax.nn.gelu(gu), jax.nn.gelu(gv), gm_ln_g, gm_ln_b, gm_w_s, gm_b_s), gm_norm_g)
    return jnp.concatenate([att, ssd, gm], axis=-1) @ w_out


def swiglu(h, w_gate, w_up, w_down):
    return (jax.nn.silu(h @ w_gate) * (h @ w_up)) @ w_down


def moe_ffn(h, w_router, e_gate, e_up, e_down):
    logits = (h @ w_router).astype(jnp.float32)
    top_val, top_idx = lax.top_k(logits, TOP_K)
    probs = jax.nn.softmax(top_val, axis=-1)
    gates = jnp.sum(jax.nn.one_hot(top_idx, N_EXPERTS, dtype=jnp.float32) * probs[..., None], axis=-2)
    gates = gates.astype(h.dtype)
    out = jnp.zeros_like(h)
    for e in range(N_EXPERTS):
        out = out + gates[..., e:e + 1] * swiglu(h, e_gate[e], e_up[e], e_down[e])
    return out


def setup_inputs(seed: int = 0) -> dict:
    key = jax.random.key(seed)
    ks = jax.random.split(key, 32)
    f32 = jnp.float32

    def nrm(k, shape, scale):
        return jax.random.normal(k, shape, f32) * scale

    def gain(k, shape):
        return 1.0 + 0.01 * jax.random.normal(k, shape, f32)

    x = nrm(ks[0], (BATCH, SEQ, D_MODEL), 1.0)
    positions = jnp.broadcast_to(jnp.arange(SEQ, dtype=jnp.int32), (BATCH, SEQ))
    ln_in_g = gain(ks[1], (D_MODEL,))
    ln_in_b = nrm(ks[2], (D_MODEL,), 0.01)
    col_scale = jnp.concatenate([jnp.ones((2 * ATT_W,), f32), jnp.full((ATT_W,), BETA, f32),
                                 jnp.ones((D_IN - 3 * ATT_W,), f32)])
    w_in = nrm(ks[3], (DEPTH, D_MODEL, D_IN), D_MODEL ** -0.5) * col_scale
    conv_w = nrm(ks[4], (DEPTH, SSD_CONV, SSD_CONV_W), SSD_CONV ** -0.5)
    conv_b = nrm(ks[5], (DEPTH, SSD_CONV_W), 0.01)
    dt0 = jnp.exp(jax.random.uniform(ks[6], (DEPTH, SSD_HEADS), f32, math.log(1e-3), math.log(1e-1)))
    dt_bias = dt0 + jnp.log(-jnp.expm1(-dt0))
    a_log = jnp.log(jax.random.uniform(ks[7], (DEPTH, SSD_HEADS), f32, 1.0, 16.0))
    d_skip = gain(ks[8], (DEPTH, SSD_HEADS))
    ssd_norm_g = gain(ks[9], (DEPTH, SSD_W))
    att_norm_g = gain(ks[10], (DEPTH, ATT_W))
    gm_ln_g = gain(ks[11], (DEPTH, GM_W))
    gm_ln_b = nrm(ks[12], (DEPTH, GM_W), 0.01)
    gm_w_s = nrm(ks[13], (DEPTH, GM_GROUPS, GM_CHUNK, GM_CHUNK), GM_CHUNK ** -0.5)
    gm_b_s = gain(ks[14], (DEPTH, GM_GROUPS, GM_CHUNK))
    gm_norm_g = gain(ks[15], (DEPTH, GM_W))
    w_out = nrm(ks[16], (DEPTH, D_MIX, D_MODEL), D_MIX ** -0.5 * BETA)
    ln1_g = gain(ks[17], (DEPTH, D_MODEL))
    ln1_b = nrm(ks[18], (DEPTH, D_MODEL), 0.01)
    ln2_g = gain(ks[19], (DEPTH, D_MODEL))
    ln2_b = nrm(ks[20], (DEPTH, D_MODEL), 0.01)
    ffn_w_gate = nrm(ks[21], (N_DENSE, D_MODEL, D_FF), D_MODEL ** -0.5)
    ffn_w_up = nrm(ks[22], (N_DENSE, D_MODEL, D_FF), D_MODEL ** -0.5)
    ffn_w_down = nrm(ks[23], (N_DENSE, D_FF, D_MODEL), D_FF ** -0.5 * BETA)
    router_w = nrm(ks[24], (N_MOE, D_MODEL, N_EXPERTS), D_MODEL ** -0.5)
    moe_w_gate = nrm(ks[25], (N_MOE, N_EXPERTS, D_MODEL, D_FF_EXPERT), D_MODEL ** -0.5)
    moe_w_up = nrm(ks[26], (N_MOE, N_EXPERTS, D_MODEL, D_FF_EXPERT), D_MODEL ** -0.5)
    moe_w_down = nrm(ks[27], (N_MOE, N_EXPERTS, D_FF_EXPERT, D_MODEL), D_FF_EXPERT ** -0.5 * BETA)
    return {"x": x, "positions": positions, "ln_in_g": ln_in_g, "ln_in_b": ln_in_b, "w_in": w_in,
            "conv_w": conv_w, "conv_b": conv_b, "dt_bias": dt_bias, "a_log": a_log, "d_skip": d_skip,
            "ssd_norm_g": ssd_norm_g, "att_norm_g": att_norm_g, "gm_ln_g": gm_ln_g, "gm_ln_b": gm_ln_b,
            "gm_w_s": gm_w_s, "gm_b_s": gm_b_s, "gm_norm_g": gm_norm_g, "w_out": w_out,
            "ln1_g": ln1_g, "ln1_b": ln1_b, "ln2_g": ln2_g, "ln2_b": ln2_b,
            "ffn_w_gate": ffn_w_gate, "ffn_w_up": ffn_w_up, "ffn_w_down": ffn_w_down,
            "router_w": router_w, "moe_w_gate": moe_w_gate, "moe_w_up": moe_w_up, "moe_w_down": moe_w_down}


def reference(x, positions, ln_in_g, ln_in_b, w_in, conv_w, conv_b, dt_bias, a_log, d_skip, ssd_norm_g,
              att_norm_g, gm_ln_g, gm_ln_b, gm_w_s, gm_b_s, gm_norm_g, w_out, ln1_g, ln1_b, ln2_g, ln2_b,
              ffn_w_gate, ffn_w_up, ffn_w_down, router_w, moe_w_gate, moe_w_up, moe_w_down):
    cos, sin = rope_tables(positions)
    h = layer_norm(x, ln_in_g, ln_in_b)
    for i in range(DEPTH):
        mix = hybrid_mixer(h, cos, sin, w_in[i], conv_w[i], conv_b[i], dt_bias[i], a_log[i], d_skip[i],
                           ssd_norm_g[i], att_norm_g[i], gm_ln_g[i], gm_ln_b[i], gm_w_s[i], gm_b_s[i],
                           gm_norm_g[i], w_out[i])
        h = layer_norm(ALPHA * h + mix, ln1_g[i], ln1_b[i])
        if i % 2 == 0:
            f = swiglu(h, ffn_w_gate[i // 2], ffn_w_up[i // 2], ffn_w_down[i // 2])
        else:
            f = moe_ffn(h, router_w[i // 2], moe_w_gate[i // 2], moe_w_up[i // 2], moe_w_down[i // 2])
        h = layer_norm(ALPHA * h + f, ln2_g[i], ln2_b[i])
    return h
```

```python
import functools

import jax
import jax.numpy as jnp
from jax import lax
from jax.experimental import pallas as pl
from jax.experimental.pallas import tpu as pltpu

F32 = jnp.float32
BF16 = jnp.bfloat16
HIGHEST = lax.Precision.HIGHEST

D_MODEL = 1024
ATT_HEADS = 6
HEAD_DIM = 64
ATT_W = ATT_HEADS * HEAD_DIM
ROPE_DIM = HEAD_DIM // 4
ROPE_HALF = ROPE_DIM // 2
ROPE_THETA = 500000.0
MOBA_BLOCK = 256
MOBA_TOPK = 3
SSD_HEADS = 6
SSD_HEAD_DIM = 64
SSD_W = SSD_HEADS * SSD_HEAD_DIM
SSD_GROUPS = 2
SSD_STATE = 128
SSD_CONV = 4
SSD_CHUNK = 128
GM_GROUPS = 4
GM_W = 256
GM_CHUNK = 128
N_EXPERTS = 8
DEPTH = 2
ALPHA = (2.0 * DEPTH) ** 0.25
NEG = -1e30
LN_EPS = 1e-5

LANES = 128
GROUP_W = SSD_W // SSD_GROUPS
GROUP_PAD = 256
SSD_PAD = SSD_GROUPS * GROUP_PAD
HEADS_PER_GROUP = SSD_HEADS // SSD_GROUPS

COL_Z = 0
COL_XS = 512
COL_B = 1024
COL_C = 1280
COL_GU = 1536
COL_GV = 1792
COL_Q = 2048
COL_K = 2432
COL_V = 2816
COL_DT = 3200
D_PROJ = 3328

VMEM_LIMIT = 56 * 1024 * 1024


def _params(sem, vmem=VMEM_LIMIT):
    return pltpu.CompilerParams(dimension_semantics=sem, vmem_limit_bytes=vmem)


def _layer_norm(x, g, b):
    mu = jnp.mean(x, axis=-1, keepdims=True)
    xc = x - mu
    var = jnp.mean(xc * xc, axis=-1, keepdims=True)
    return xc * lax.rsqrt(var + LN_EPS) * g + b


def _silu(x):
    return x / (1.0 + jnp.exp(-x))


def _gelu_tanh(x):
    return 0.5 * x * (1.0 + jnp.tanh(0.7978845608028654 * (x + 0.044715 * x * x * x)))


PROJ_TM = 512
PROJ_CH = 256


def _in_proj_kernel(*refs, apply_ln):
    if apply_ln:
        x_ref, g_ref, b_ref, w_ref, proj_ref, h_ref = refs
        x = _layer_norm(x_ref[...], g_ref[...], b_ref[...])
        h_ref[...] = x
    else:
        x_ref, w_ref, proj_ref = refs
        x = x_ref[...]
    xb = x.astype(BF16)
    for j in range(D_PROJ // PROJ_CH):
        cols = slice(j * PROJ_CH, (j + 1) * PROJ_CH)
        proj_ref[:, cols] = jnp.dot(xb, w_ref[:, cols], preferred_element_type=F32)


def _in_proj(x, w, ln=None):
    m = x.shape[0]
    row = lambda i: (i, 0)
    const = lambda i: (0, 0)
    in_specs = [pl.BlockSpec((PROJ_TM, D_MODEL), row)]
    args = [x]
    out_shape = [jax.ShapeDtypeStruct((m, D_PROJ), F32)]
    out_specs = [pl.BlockSpec((PROJ_TM, D_PROJ), row)]
    if ln is not None:
        in_specs += [pl.BlockSpec((1, D_MODEL), const), pl.BlockSpec((1, D_MODEL), const)]
        args += [ln[0].reshape(1, D_MODEL), ln[1].reshape(1, D_MODEL)]
        out_shape.append(jax.ShapeDtypeStruct((m, D_MODEL), F32))
        out_specs.append(pl.BlockSpec((PROJ_TM, D_MODEL), row))
    in_specs.append(pl.BlockSpec((D_MODEL, D_PROJ), const))
    args.append(w)
    return pl.pallas_call(
        functools.partial(_in_proj_kernel, apply_ln=ln is not None),
        grid=(m // PROJ_TM,),
        in_specs=in_specs,
        out_specs=out_specs,
        out_shape=out_shape,
        compiler_params=_params(("parallel",)),
        name="in_proj",
    )(*args)


def _attn_kernel(q_ref, k_ref, v_ref, c_ref, sa_ref, sb_ref, o_ref, kaug_ref, vb_ref, *, seq):
    nb = seq // MOBA_BLOCK
    lane = lax.broadcasted_iota(jnp.int32, (1, LANES), 1)

    def rope(x, rows):
        return (x * c_ref[0, rows, :] + pltpu.roll(x, LANES - ROPE_HALF, 1) * sa_ref[0, rows, :]
                + pltpu.roll(x, ROPE_HALF, 1) * sb_ref[0, rows, :])

    k = rope(k_ref[0], slice(None))
    kaug_ref[:, 0:LANES] = k.astype(BF16)
    key_blk = lax.broadcasted_iota(jnp.int32, (seq, LANES), 0) // MOBA_BLOCK
    key_lane = lax.broadcasted_iota(jnp.int32, (seq, LANES), 1)
    kaug_ref[:, LANES:2 * LANES] = jnp.where(key_blk == key_lane, 1.0, 0.0).astype(BF16)
    vb_ref[...] = v_ref[0].astype(BF16)
    k_mean = jnp.mean(k.reshape(nb, MOBA_BLOCK, LANES), axis=1)
    km_pad = jnp.concatenate([k_mean, jnp.zeros((LANES - nb, LANES), F32)], axis=0)

    contract_last = (((1,), (1,)), ((), ()))
    scale = HEAD_DIM ** -0.5
    tq = MOBA_BLOCK
    q_pos = lax.broadcasted_iota(jnp.int32, (tq, tq), 0)
    k_pos = lax.broadcasted_iota(jnp.int32, (tq, tq), 1)
    causal = k_pos <= q_pos

    for j in range(nb):
        rows = slice(j * tq, (j + 1) * tq)
        n_past = j * MOBA_BLOCK
        q = rope(q_ref[0, rows, :], rows) * scale
        out = None
        for hh in range(2):
            head_lanes = (lane < HEAD_DIM) if hh == 0 else (lane >= HEAD_DIM)
            qh = jnp.where(head_lanes, q, 0.0)
            if j > MOBA_TOPK:
                gate = lax.dot_general(qh, km_pad, contract_last, precision=HIGHEST,
                                       preferred_element_type=F32)
                rank = jnp.zeros((tq, LANES), F32)
                for m in range(j):
                    col = gate[:, m:m + 1]
                    beats = (col > gate) | ((col == gate) & (lane > m))
                    rank = rank + jnp.where(beats & (lane != m), 1.0, 0.0)
                dropped = (lane < j) & (rank >= MOBA_TOPK)
                bias = jnp.where(dropped, NEG, 0.0)
                q_aug = jnp.concatenate([qh.astype(BF16), bias.astype(BF16)], axis=1)
                s = lax.dot_general(q_aug, kaug_ref[0:n_past + tq, :], contract_last,
                                    preferred_element_type=F32)
            else:
                s = lax.dot_general(qh.astype(BF16), kaug_ref[0:n_past + tq, 0:LANES], contract_last,
                                    preferred_element_type=F32)
            s_own = jnp.where(causal, s[:, n_past:], NEG)
            mx = jnp.max(s_own, axis=-1, keepdims=True)
            if j > 0:
                s_past = s[:, :n_past]
                mx = jnp.maximum(mx, jnp.max(s_past, axis=-1, keepdims=True))
                p_past = jnp.exp(s_past - mx)
            p_own = jnp.exp(s_own - mx)
            den = jnp.sum(p_own, axis=-1, keepdims=True)
            acc = jnp.dot(p_own.astype(BF16), vb_ref[n_past:n_past + tq, :], preferred_element_type=F32)
            if j > 0:
                den = den + jnp.sum(p_past, axis=-1, keepdims=True)
                acc = acc + jnp.dot(p_past.astype(BF16), vb_ref[0:n_past, :], preferred_element_type=F32)
            o = acc / den
            out = o if out is None else jnp.where(lane < HEAD_DIM, out, o)
        o_ref[0, rows, :] = out


def _attention(proj3, rope_c, rope_sa, rope_sb):
    bsz, seq, _ = proj3.shape
    blk = (1, seq, LANES)
    col = lambda base: (lambda b, p: (b, 0, base // LANES + p))
    tab = lambda b, p: (b, 0, 0)
    return pl.pallas_call(
        functools.partial(_attn_kernel, seq=seq),
        grid=(bsz, ATT_W // LANES),
        in_specs=[pl.BlockSpec(blk, col(COL_Q)), pl.BlockSpec(blk, col(COL_K)), pl.BlockSpec(blk, col(COL_V)),
                  pl.BlockSpec(blk, tab), pl.BlockSpec(blk, tab), pl.BlockSpec(blk, tab)],
        out_specs=pl.BlockSpec(blk, lambda b, p: (b, 0, p)),
        out_shape=jax.ShapeDtypeStruct((bsz, seq, ATT_W), F32),
        scratch_shapes=[pltpu.VMEM((seq, 2 * LANES), BF16), pltpu.VMEM((seq, LANES), BF16)],
        compiler_params=_params(("parallel", "arbitrary")),
        name="moba_attention",
    )(proj3, proj3, proj3, rope_c, rope_sa, rope_sb)


def _expand_heads(small, width_groups=SSD_GROUPS):
    rows = small.shape[0]
    slab = lax.broadcasted_iota(jnp.int32, (1, SSD_PAD), 1) // SSD_HEAD_DIM
    out = jnp.zeros((rows, SSD_PAD), F32)
    for h in range(SSD_HEADS):
        g, r = divmod(h, HEADS_PER_GROUP)
        out = jnp.where(slab == g * (GROUP_PAD // SSD_HEAD_DIM) + r, small[:, h:h + 1], out)
    return out


def _ssd_kernel(z_ref, xs_ref, b_ref, c_ref, dt_ref, cwx_ref, cwb_ref, cwc_ref, cbx_ref, cbb_ref, cbc_ref,
                dtb_ref, alog_ref, dskip_ref, ng_ref, o_ref, state_ref, *, seq):
    q = SSD_CHUNK
    n = SSD_STATE
    state_ref[...] = jnp.zeros_like(state_ref)
    row_i = lax.broadcasted_iota(jnp.int32, (q, q), 0)
    col_i = lax.broadcasted_iota(jnp.int32, (q, q), 1)
    lower = row_i >= col_i
    tril = jnp.where(lower, 1.0, 0.0)
    slab = lax.broadcasted_iota(jnp.int32, (1, GROUP_PAD), 1) // SSD_HEAD_DIM
    a_neg = -jnp.exp(alog_ref[...])

    def conv_silu(ref, w_ref, bias_ref, c):
        start = pl.multiple_of(c * q, q)
        cur = ref[0, pl.ds(start, q), :]
        prev_start = pl.multiple_of(jnp.maximum(c * q - 8, 0), 8)
        prev = jnp.where(c > 0, ref[0, pl.ds(prev_start, 8), :], 0.0)
        full = jnp.concatenate([prev, cur], axis=0)
        y = bias_ref[...]
        for i in range(SSD_CONV):
            shift = SSD_CONV - 1 - i
            y = y + w_ref[i:i + 1, :] * full[8 - shift:8 - shift + q, :]
        return _silu(y)

    def chunk(c, carry):
        start = pl.multiple_of(c * q, q)
        xs = conv_silu(xs_ref, cwx_ref, cbx_ref, c)
        bm = conv_silu(b_ref, cwb_ref, cbb_ref, c)
        cm = conv_silu(c_ref, cwc_ref, cbc_ref, c)
        x = dt_ref[0, pl.ds(start, q), :] + dtb_ref[...]
        dt = jnp.maximum(x, 0.0) + jnp.log(1.0 + jnp.exp(-jnp.abs(x)))
        a = dt * a_neg
        a_cum = jnp.dot(tril, a, precision=HIGHEST, preferred_element_type=F32)
        a_cum_t = a_cum.T
        dt_e = _expand_heads(dt)
        acum_e = _expand_heads(a_cum)
        alast_e = acum_e[q - 1:q, :]
        xdt = xs * dt_e
        decay_e = jnp.exp(alast_e - acum_e)
        y_parts = []
        for g in range(SSD_GROUPS):
            gl = slice(g * GROUP_PAD, (g + 1) * GROUP_PAD)
            nl = slice(g * n, (g + 1) * n)
            b_g = bm[:, nl].astype(BF16)
            b_gt = bm[:, nl].T.astype(BF16)
            c_g = cm[:, nl].astype(BF16)
            cb = lax.dot_general(c_g, b_g, (((1,), (1,)), ((), ())), preferred_element_type=F32)
            xdt_g = xdt[:, gl]
            xdt_b = xdt_g.astype(BF16)
            y_g = jnp.zeros((q, GROUP_PAD), F32)
            for r in range(HEADS_PER_GROUP):
                h = g * HEADS_PER_GROUP + r
                seg = a_cum[:, h:h + 1] - a_cum_t[h:h + 1, :]
                l_h = jnp.where(lower, jnp.exp(jnp.where(lower, seg, 0.0)), 0.0)
                y_h = jnp.dot((cb * l_h).astype(BF16), xdt_b, preferred_element_type=F32)
                y_g = jnp.where(slab == r, y_h, y_g)
            prev = state_ref[g]
            y_off = jnp.dot(c_g, prev.astype(BF16), preferred_element_type=F32) * jnp.exp(acum_e[:, gl])
            new = jnp.dot(b_gt, (xdt_g * decay_e[:, gl]).astype(BF16), preferred_element_type=F32)
            state_ref[g] = prev * jnp.exp(alast_e[:, gl]) + new
            y_parts.append(y_g + y_off)
        y = jnp.concatenate(y_parts, axis=1) + dskip_ref[...] * xs
        yz = y * _silu(z_ref[0, pl.ds(start, q), :])
        outs = []
        for g in range(SSD_GROUPS):
            yg = yz[:, g * GROUP_PAD:(g + 1) * GROUP_PAD]
            ms = jnp.sum(yg * yg, axis=-1, keepdims=True) * (1.0 / GROUP_W)
            outs.append(yg * lax.rsqrt(ms + LN_EPS))
        o_ref[0, pl.ds(start, q), :] = jnp.concatenate(outs, axis=1) * ng_ref[...]
        return carry

    lax.fori_loop(0, seq // q, chunk, 0)


def _ssd(proj3, p):
    bsz, seq, _ = proj3.shape
    slab = lambda width, base: pl.BlockSpec((1, seq, width), lambda b: (b, 0, base // width))
    const2 = lambda a: pl.BlockSpec(a.shape, lambda b: (0, 0))
    small = [p["conv_w_xs"], p["conv_w_b"], p["conv_w_c"], p["conv_b_xs"], p["conv_b_b"], p["conv_b_c"],
             p["dt_bias"], p["a_log"], p["d_skip"], p["ssd_norm_g"]]
    return pl.pallas_call(
        functools.partial(_ssd_kernel, seq=seq),
        grid=(bsz,),
        in_specs=[slab(SSD_PAD, COL_Z), slab(SSD_PAD, COL_XS), slab(2 * SSD_STATE, COL_B),
                  slab(2 * SSD_STATE, COL_C), slab(LANES, COL_DT)] + [const2(a) for a in small],
        out_specs=pl.BlockSpec((1, seq, SSD_PAD), lambda b: (b, 0, 0)),
        out_shape=jax.ShapeDtypeStruct((bsz, seq, SSD_PAD), F32),
        scratch_shapes=[pltpu.VMEM((SSD_GROUPS, SSD_STATE, GROUP_PAD), F32)],
        compiler_params=_params(("parallel",)),
        name="ssd",
    )(proj3, proj3, proj3, proj3, proj3, *small)


GM_ROWS = 1024


def _gmlp_kernel(u_ref, v_ref, lng_ref, lnb_ref, ws_ref, bs_ref, ng_ref, o_ref):
    q = GM_CHUNK
    row_i = lax.broadcasted_iota(jnp.int32, (q, q), 0)
    col_i = lax.broadcasted_iota(jnp.int32, (q, q), 1)
    lower = row_i >= col_i
    grp = lax.broadcasted_iota(jnp.int32, (1, GM_W), 1) // (GM_W // GM_GROUPS)
    w = [jnp.where(lower, ws_ref[g], 0.0).astype(BF16) for g in range(GM_GROUPS)]
    for c in range(GM_ROWS // q):
        rows = slice(c * q, (c + 1) * q)
        u = _gelu_tanh(u_ref[rows, :])
        v = _layer_norm(_gelu_tanh(v_ref[rows, :]), lng_ref[...], lnb_ref[...])
        vb = v.astype(BF16)
        mixed = jnp.zeros((q, GM_W), F32)
        for g in range(GM_GROUPS):
            mixed = jnp.where(grp == g, jnp.dot(w[g], vb, preferred_element_type=F32), mixed)
        gm = u * (mixed + bs_ref[...])
        ms = jnp.mean(gm * gm, axis=-1, keepdims=True)
        o_ref[rows, :] = gm * lax.rsqrt(ms + LN_EPS) * ng_ref[...]


def _gmlp(proj, p):
    m = proj.shape[0]
    const = lambda a: pl.BlockSpec(a.shape, lambda i: (0,) * a.ndim)
    small = [p["gm_ln_g"], p["gm_ln_b"], p["gm_w_s"], p["gm_b_s"], p["gm_norm_g"]]
    return pl.pallas_call(
        _gmlp_kernel,
        grid=(m // GM_ROWS,),
        in_specs=[pl.BlockSpec((GM_ROWS, GM_W), lambda i: (i, COL_GU // GM_W)),
                  pl.BlockSpec((GM_ROWS, GM_W), lambda i: (i, COL_GV // GM_W))] + [const(a) for a in small],
        out_specs=pl.BlockSpec((GM_ROWS, GM_W), lambda i: (i, 0)),
        out_shape=jax.ShapeDtypeStruct((m, GM_W), F32),
        compiler_params=_params(("parallel",)),
        name="gmlp",
    )(proj, proj, *small)


OUT_TM = 512


def _out_proj_kernel(att_ref, ssd_ref, gm_ref, h_ref, ag_ref, wa_ref, ws_ref, wg_ref, g_ref, b_ref, o_ref):
    att = att_ref[...]
    ms = jnp.mean(att * att, axis=-1, keepdims=True)
    att = att * lax.rsqrt(ms + LN_EPS) * ag_ref[...]
    mix = jnp.dot(att.astype(BF16), wa_ref[...], preferred_element_type=F32)
    mix = mix + jnp.dot(ssd_ref[...].astype(BF16), ws_ref[...], preferred_element_type=F32)
    mix = mix + jnp.dot(gm_ref[...].astype(BF16), wg_ref[...], preferred_element_type=F32)
    o_ref[...] = _layer_norm(ALPHA * h_ref[...] + mix, g_ref[...], b_ref[...])


def _out_proj(att, ssd, gm, h, p):
    m = h.shape[0]
    row = lambda i: (i, 0)
    const = lambda a: pl.BlockSpec(a.shape, lambda i: (0, 0))
    small = [p["att_norm_g"], p["w_out_att"], p["w_out_ssd"], p["w_out_gm"], p["ln1_g"], p["ln1_b"]]
    return pl.pallas_call(
        _out_proj_kernel,
        grid=(m // OUT_TM,),
        in_specs=[pl.BlockSpec((OUT_TM, ATT_W), row), pl.BlockSpec((OUT_TM, SSD_PAD), row),
                  pl.BlockSpec((OUT_TM, GM_W), row), pl.BlockSpec((OUT_TM, D_MODEL), row)]
                 + [const(a) for a in small],
        out_specs=pl.BlockSpec((OUT_TM, D_MODEL), row),
        out_shape=jax.ShapeDtypeStruct((m, D_MODEL), F32),
        compiler_params=_params(("parallel",)),
        name="out_proj_ln1",
    )(att, ssd, gm, h, *small)


FFN_TM = 512


def _ffn_kernel(x_ref, gate_ref, wg_ref, wu_ref, wd_ref, g_ref, b_ref, o_ref, acc_ref, *, gated):
    e = pl.program_id(1)
    f = pl.program_id(2)

    @pl.when((e == 0) & (f == 0))
    def _():
        acc_ref[...] = jnp.zeros_like(acc_ref)

    x = x_ref[...]
    xb = x.astype(BF16)
    hg = jnp.dot(xb, wg_ref[0], preferred_element_type=F32)
    hu = jnp.dot(xb, wu_ref[0], preferred_element_type=F32)
    hid = _silu(hg) * hu
    if gated:
        hid = hid * gate_ref[0]
    acc_ref[...] += jnp.dot(hid.astype(BF16), wd_ref[0], preferred_element_type=F32)

    @pl.when((e == pl.num_programs(1) - 1) & (f == pl.num_programs(2) - 1))
    def _():
        o_ref[...] = _layer_norm(ALPHA * x + acc_ref[...], g_ref[...], b_ref[...])


def _ffn(x, gates, wg, wu, wd, ln_g, ln_b, tf):
    m = x.shape[0]
    n_e, _, d_ff = wg.shape
    gated = gates is not None
    if not gated:
        gates = jnp.ones((1, m, 1), F32)
    return pl.pallas_call(
        functools.partial(_ffn_kernel, gated=gated),
        grid=(m // FFN_TM, n_e, d_ff // tf),
        in_specs=[pl.BlockSpec((FFN_TM, D_MODEL), lambda i, e, f: (i, 0)),
                  pl.BlockSpec((1, FFN_TM, 1), lambda i, e, f: (e, i, 0)),
                  pl.BlockSpec((1, D_MODEL, tf), lambda i, e, f: (e, 0, f)),
                  pl.BlockSpec((1, D_MODEL, tf), lambda i, e, f: (e, 0, f)),
                  pl.BlockSpec((1, tf, D_MODEL), lambda i, e, f: (e, f, 0)),
                  pl.BlockSpec((1, D_MODEL), lambda i, e, f: (0, 0)),
                  pl.BlockSpec((1, D_MODEL), lambda i, e, f: (0, 0))],
        out_specs=pl.BlockSpec((FFN_TM, D_MODEL), lambda i, e, f: (i, 0)),
        out_shape=jax.ShapeDtypeStruct((m, D_MODEL), F32),
        scratch_shapes=[pltpu.VMEM((FFN_TM, D_MODEL), F32)],
        compiler_params=_params(("parallel", "arbitrary", "arbitrary")),
        name="ffn_ln2",
    )(x, gates, wg, wu, wd, ln_g.reshape(1, D_MODEL), ln_b.reshape(1, D_MODEL))


ROUTER_TM = 1024


def _router_kernel(x_ref, w_ref, o_ref):
    logits = jnp.dot(x_ref[...], w_ref[...], precision=HIGHEST, preferred_element_type=F32)
    lane = lax.broadcasted_iota(jnp.int32, logits.shape, 1)
    logits = jnp.where(lane < N_EXPERTS, logits, -jnp.inf)
    m1 = jnp.max(logits, axis=-1, keepdims=True)
    i1 = jnp.min(jnp.where(logits == m1, lane, LANES), axis=-1, keepdims=True)
    rest = jnp.where(lane == i1, -jnp.inf, logits)
    m2 = jnp.max(rest, axis=-1, keepdims=True)
    i2 = jnp.min(jnp.where(rest == m2, lane, LANES), axis=-1, keepdims=True)
    e2 = jnp.exp(m2 - m1)
    p1 = 1.0 / (1.0 + e2)
    o_ref[...] = jnp.where(lane == i1, p1, 0.0) + jnp.where(lane == i2, e2 * p1, 0.0)


def _router(x, w_router):
    m = x.shape[0]
    w = jnp.pad(w_router, ((0, 0), (0, LANES - N_EXPERTS)))
    return pl.pallas_call(
        _router_kernel,
        grid=(m // ROUTER_TM,),
        in_specs=[pl.BlockSpec((ROUTER_TM, D_MODEL), lambda i: (i, 0)),
                  pl.BlockSpec((D_MODEL, LANES), lambda i: (0, 0))],
        out_specs=pl.BlockSpec((ROUTER_TM, LANES), lambda i: (i, 0)),
        out_shape=jax.ShapeDtypeStruct((m, LANES), F32),
        compiler_params=_params(("parallel",)),
        name="router_top2",
    )(x, w)


def _moe_layer(h, w_router, w_gate, w_up, w_down, ln_g, ln_b):
    m = h.shape[0]
    gates = _router(h, w_router)
    gates_t = gates[:, :N_EXPERTS].T.reshape(N_EXPERTS, m, 1)
    return _ffn(h, gates_t, w_gate.astype(BF16), w_up.astype(BF16), w_down.astype(BF16), ln_g, ln_b, tf=896)


def _pad_groups(a):
    lead = a.shape[:-1]
    a = a.reshape(*lead, SSD_GROUPS, GROUP_W)
    a = jnp.pad(a, [(0, 0)] * len(lead) + [(0, 0), (0, GROUP_PAD - GROUP_W)])
    return a.reshape(*lead, SSD_PAD)


def _head_lanes(v):
    return jnp.pad(v, (0, LANES - SSD_HEADS)).reshape(1, LANES)


def _layer_params(i, w_in, conv_w, conv_b, dt_bias, a_log, d_skip, ssd_norm_g, att_norm_g, gm_ln_g, gm_ln_b,
                  gm_w_s, gm_b_s, gm_norm_g, w_out, ln1_g, ln1_b):
    w = w_in[i]
    c1, c2, c3 = ATT_W, 2 * ATT_W, 3 * ATT_W
    c4 = c3 + SSD_W
    c5 = c4 + SSD_W
    c6 = c5 + SSD_GROUPS * SSD_STATE
    c7 = c6 + SSD_GROUPS * SSD_STATE
    c8 = c7 + SSD_HEADS
    c9 = c8 + GM_W
    w_dt = jnp.pad(w[:, c7:c8], ((0, 0), (0, LANES - SSD_HEADS)))
    w_perm = jnp.concatenate([_pad_groups(w[:, c3:c4]), _pad_groups(w[:, c4:c5]), w[:, c5:c6], w[:, c6:c7],
                              w[:, c8:c9], w[:, c9:], w[:, :c1], w[:, c1:c2], w[:, c2:c3], w_dt], axis=1)
    cw, cb = conv_w[i], conv_b[i]
    nbc = SSD_GROUPS * SSD_STATE
    wo = w_out[i]
    return {
        "w_in": w_perm.astype(BF16),
        "conv_w_xs": _pad_groups(cw[:, :SSD_W]),
        "conv_w_b": cw[:, SSD_W:SSD_W + nbc],
        "conv_w_c": cw[:, SSD_W + nbc:],
        "conv_b_xs": _pad_groups(cb[:SSD_W]).reshape(1, SSD_PAD),
        "conv_b_b": cb[SSD_W:SSD_W + nbc].reshape(1, nbc),
        "conv_b_c": cb[SSD_W + nbc:].reshape(1, nbc),
        "dt_bias": _head_lanes(dt_bias[i]),
        "a_log": _head_lanes(a_log[i]),
        "d_skip": _pad_groups(jnp.repeat(d_skip[i], SSD_HEAD_DIM)).reshape(1, SSD_PAD),
        "ssd_norm_g": _pad_groups(ssd_norm_g[i]).reshape(1, SSD_PAD),
        "att_norm_g": att_norm_g[i].reshape(1, ATT_W),
        "gm_ln_g": gm_ln_g[i].reshape(1, GM_W),
        "gm_ln_b": gm_ln_b[i].reshape(1, GM_W),
        "gm_w_s": gm_w_s[i],
        "gm_b_s": jnp.repeat(gm_b_s[i].T, GM_W // GM_GROUPS, axis=1),
        "gm_norm_g": gm_norm_g[i].reshape(1, GM_W),
        "w_out_att": wo[:ATT_W].astype(BF16),
        "w_out_ssd": _pad_groups(wo[ATT_W:ATT_W + SSD_W].T).T.astype(BF16),
        "w_out_gm": wo[ATT_W + SSD_W:].astype(BF16),
        "ln1_g": ln1_g[i].reshape(1, D_MODEL),
        "ln1_b": ln1_b[i].reshape(1, D_MODEL),
    }


def _rope_tables(positions):
    inv = ROPE_THETA ** (-jnp.arange(0, ROPE_DIM, 2, dtype=F32) / ROPE_DIM)
    ang = positions.astype(F32)[..., None] * inv
    cos, sin = jnp.cos(ang), jnp.sin(ang)
    pad = HEAD_DIM - ROPE_DIM
    one_head = lambda lo, hi, fill: jnp.concatenate(
        [lo, hi, jnp.full(cos.shape[:-1] + (pad,), fill, F32)], axis=-1)
    zero = jnp.zeros_like(sin)
    two = lambda t: jnp.concatenate([t, t], axis=-1)
    return two(one_head(cos, cos, 1.0)), two(one_head(-sin, zero, 0.0)), two(one_head(zero, sin, 0.0))


def kernel(x, positions, ln_in_g, ln_in_b, w_in, conv_w, conv_b, dt_bias, a_log, d_skip, ssd_norm_g, att_norm_g, gm_ln_g, gm_ln_b, gm_w_s, gm_b_s, gm_norm_g, w_out, ln1_g, ln1_b, ln2_g, ln2_b, ffn_w_gate, ffn_w_up, ffn_w_down, router_w, moe_w_gate, moe_w_up, moe_w_down):
    bsz, seq, _ = x.shape
    m = bsz * seq
    rope_c, rope_sa, rope_sb = _rope_tables(positions)
    h = x.reshape(m, D_MODEL)
    for i in range(DEPTH):
        p = _layer_params(i, w_in, conv_w, conv_b, dt_bias, a_log, d_skip, ssd_norm_g, att_norm_g, gm_ln_g,
                          gm_ln_b, gm_w_s, gm_b_s, gm_norm_g, w_out, ln1_g, ln1_b)
        if i == 0:
            proj, h = _in_proj(h, p["w_in"], ln=(ln_in_g, ln_in_b))
        else:
            (proj,) = _in_proj(h, p["w_in"])
        proj3 = proj.reshape(bsz, seq, D_PROJ)
        att = _attention(proj3, rope_c, rope_sa, rope_sb).reshape(m, ATT_W)
        ssd = _ssd(proj3, p).reshape(m, SSD_PAD)
        gm = _gmlp(proj, p)
        h = _out_proj(att, ssd, gm, h, p)
        if i % 2 == 0:
            j = i // 2
            h = _ffn(h, None, ffn_w_gate[j][None].astype(BF16), ffn_w_up[j][None].astype(BF16),
                     ffn_w_down[j][None].astype(BF16), ln2_g[i], ln2_b[i], tf=1408)
        else:
            j = i // 2
            h = _moe_layer(h, router_w[j], moe_w_gate[j], moe_w_up[j], moe_w_down[j], ln2_g[i], ln2_b[i])
    return h.reshape(bsz, seq, D_MODEL)
```

```python
import functools

import jax
import jax.numpy as jnp
from jax import lax
from jax.experimental import pallas as pl
from jax.experimental.pallas import tpu as pltpu

F32 = jnp.float32
BF16 = jnp.bfloat16
HIGHEST = lax.Precision.HIGHEST

D_MODEL = 1024
ATT_HEADS = 6
HEAD_DIM = 64
ATT_W = ATT_HEADS * HEAD_DIM
ROPE_DIM = HEAD_DIM // 4
ROPE_HALF = ROPE_DIM // 2
ROPE_THETA = 500000.0
MOBA_BLOCK = 256
MOBA_TOPK = 3
SSD_HEADS = 6
SSD_HEAD_DIM = 64
SSD_W = SSD_HEADS * SSD_HEAD_DIM
SSD_GROUPS = 2
SSD_STATE = 128
SSD_CONV = 4
SSD_CHUNK = 128
GM_GROUPS = 4
GM_W = 256
GM_CHUNK = 128
N_EXPERTS = 8
DEPTH = 2
ALPHA = (2.0 * DEPTH) ** 0.25
NEG = -1e30
LN_EPS = 1e-5

LANES = 128
GROUP_W = SSD_W // SSD_GROUPS
GROUP_PAD = 256
SSD_PAD = SSD_GROUPS * GROUP_PAD
HEADS_PER_GROUP = SSD_HEADS // SSD_GROUPS

COL_Z = 0
COL_XS = 512
COL_B = 1024
COL_C = 1280
COL_GU = 1536
COL_GV = 1792
COL_Q = 2048
COL_K = 2432
COL_V = 2816
COL_DT = 3200
D_PROJ = 3328

VMEM_LIMIT = 56 * 1024 * 1024


def _params(sem, vmem=VMEM_LIMIT):
    return pltpu.CompilerParams(dimension_semantics=sem, vmem_limit_bytes=vmem)


def _layer_norm(x, g, b):
    mu = jnp.mean(x, axis=-1, keepdims=True)
    xc = x - mu
    var = jnp.mean(xc * xc, axis=-1, keepdims=True)
    return xc * lax.rsqrt(var + LN_EPS) * g + b


def _silu(x):
    return x / (1.0 + jnp.exp(-x))


def _gelu_tanh(x):
    return 0.5 * x * (1.0 + jnp.tanh(0.7978845608028654 * (x + 0.044715 * x * x * x)))


PROJ_TM = 512
PROJ_CH = 256


def _in_proj_kernel(*refs, apply_ln):
    if apply_ln:
        x_ref, g_ref, b_ref, w_ref, proj_ref, h_ref = refs
        x = _layer_norm(x_ref[...], g_ref[...], b_ref[...])
        h_ref[...] = x
    else:
        x_ref, w_ref, proj_ref = refs
        x = x_ref[...]
    xb = x.astype(BF16)
    for j in range(D_PROJ // PROJ_CH):
        cols = slice(j * PROJ_CH, (j + 1) * PROJ_CH)
        proj_ref[:, cols] = jnp.dot(xb, w_ref[:, cols], preferred_element_type=F32)


def _in_proj(x, w, ln=None):
    m = x.shape[0]
    row = lambda i: (i, 0)
    const = lambda i: (0, 0)
    in_specs = [pl.BlockSpec((PROJ_TM, D_MODEL), row)]
    args = [x]
    out_shape = [jax.ShapeDtypeStruct((m, D_PROJ), F32)]
    out_specs = [pl.BlockSpec((PROJ_TM, D_PROJ), row)]
    if ln is not None:
        in_specs += [pl.BlockSpec((1, D_MODEL), const), pl.BlockSpec((1, D_MODEL), const)]
        args += [ln[0].reshape(1, D_MODEL), ln[1].reshape(1, D_MODEL)]
        out_shape.append(jax.ShapeDtypeStruct((m, D_MODEL), F32))
        out_specs.append(pl.BlockSpec((PROJ_TM, D_MODEL), row))
    in_specs.append(pl.BlockSpec((D_MODEL, D_PROJ), const))
    args.append(w)
    return pl.pallas_call(
        functools.partial(_in_proj_kernel, apply_ln=ln is not None),
        grid=(m // PROJ_TM,),
        in_specs=in_specs,
        out_specs=out_specs,
        out_shape=out_shape,
        compiler_params=_params(("parallel",)),
        name="in_proj",
    )(*args)


def _attn_kernel(q_ref, k_ref, v_ref, c_ref, sa_ref, sb_ref, o_ref, kaug_ref, vb_ref, *, seq):
    nb = seq // MOBA_BLOCK
    lane = lax.broadcasted_iota(jnp.int32, (1, LANES), 1)

    def rope(x, rows):
        return (x * c_ref[0, rows, :] + pltpu.roll(x, LANES - ROPE_HALF, 1) * sa_ref[0, rows, :]
                + pltpu.roll(x, ROPE_HALF, 1) * sb_ref[0, rows, :])

    k = rope(k_ref[0], slice(None))
    kaug_ref[:, 0:LANES] = k.astype(BF16)
    key_blk = lax.broadcasted_iota(jnp.int32, (seq, LANES), 0) // MOBA_BLOCK
    key_lane = lax.broadcasted_iota(jnp.int32, (seq, LANES), 1)
    kaug_ref[:, LANES:2 * LANES] = jnp.where(key_blk == key_lane, 1.0, 0.0).astype(BF16)
    vb_ref[...] = v_ref[0].astype(BF16)
    k_mean = jnp.mean(k.reshape(nb, MOBA_BLOCK, LANES), axis=1)
    km_pad = jnp.concatenate([k_mean, jnp.zeros((LANES - nb, LANES), F32)], axis=0)

    contract_last = (((1,), (1,)), ((), ()))
    scale = HEAD_DIM ** -0.5
    tq = MOBA_BLOCK
    q_pos = lax.broadcasted_iota(jnp.int32, (tq, tq), 0)
    k_pos = lax.broadcasted_iota(jnp.int32, (tq, tq), 1)
    causal = k_pos <= q_pos

    for j in range(nb):
        rows = slice(j * tq, (j + 1) * tq)
        n_past = j * MOBA_BLOCK
        q = rope(q_ref[0, rows, :], rows) * scale
        out = None
        for hh in range(2):
            head_lanes = (lane < HEAD_DIM) if hh == 0 else (lane >= HEAD_DIM)
            qh = jnp.where(head_lanes, q, 0.0)
            if j > MOBA_TOPK:
                gate = lax.dot_general(qh, km_pad, contract_last, precision=HIGHEST,
                                       preferred_element_type=F32)
                rank = jnp.zeros((tq, LANES), F32)
                for m in range(j):
                    col = gate[:, m:m + 1]
                    beats = (col > gate) | ((col == gate) & (lane > m))
                    rank = rank + jnp.where(beats & (lane != m), 1.0, 0.0)
                dropped = (lane < j) & (rank >= MOBA_TOPK)
                bias = jnp.where(dropped, NEG, 0.0)
                q_aug = jnp.concatenate([qh.astype(BF16), bias.astype(BF16)], axis=1)
                s = lax.dot_general(q_aug, kaug_ref[0:n_past + tq, :], contract_last,
                                    preferred_element_type=F32)
            else:
                s = lax.dot_general(qh.astype(BF16), kaug_ref[0:n_past + tq, 0:LANES], contract_last,
                                    preferred_element_type=F32)
            s_own = jnp.where(causal, s[:, n_past:], NEG)
            mx = jnp.max(s_own, axis=-1, keepdims=True)
            if j > 0:
                s_past = s[:, :n_past]
                mx = jnp.maximum(mx, jnp.max(s_past, axis=-1, keepdims=True))
                p_past = jnp.exp(s_past - mx)
            p_own = jnp.exp(s_own - mx)
            den = jnp.sum(p_own, axis=-1, keepdims=True)
            acc = jnp.dot(p_own.astype(BF16), vb_ref[n_past:n_past + tq, :], preferred_element_type=F32)
            if j > 0:
                den = den + jnp.sum(p_past, axis=-1, keepdims=True)
                acc = acc + jnp.dot(p_past.astype(BF16), vb_ref[0:n_past, :], preferred_element_type=F32)
            o = acc / den
            out = o if out is None else jnp.where(lane < HEAD_DIM, out, o)
        o_ref[0, rows, :] = out


def _attention(proj3, rope_c, rope_sa, rope_sb):
    bsz, seq, _ = proj3.shape
    blk = (1, seq, LANES)
    col = lambda base: (lambda b, p: (b, 0, base // LANES + p))
    tab = lambda b, p: (b, 0, 0)
    return pl.pallas_call(
        functools.partial(_attn_kernel, seq=seq),
        grid=(bsz, ATT_W // LANES),
        in_specs=[pl.BlockSpec(blk, col(COL_Q)), pl.BlockSpec(blk, col(COL_K)), pl.BlockSpec(blk, col(COL_V)),
                  pl.BlockSpec(blk, tab), pl.BlockSpec(blk, tab), pl.BlockSpec(blk, tab)],
        out_specs=pl.BlockSpec(blk, lambda b, p: (b, 0, p)),
        out_shape=jax.ShapeDtypeStruct((bsz, seq, ATT_W), F32),
        scratch_shapes=[pltpu.VMEM((seq, 2 * LANES), BF16), pltpu.VMEM((seq, LANES), BF16)],
        compiler_params=_params(("parallel", "arbitrary")),
        name="moba_attention",
    )(proj3, proj3, proj3, rope_c, rope_sa, rope_sb)


def _expand_heads(small, width_groups=SSD_GROUPS):
    rows = small.shape[0]
    slab = lax.broadcasted_iota(jnp.int32, (1, SSD_PAD), 1) // SSD_HEAD_DIM
    out = jnp.zeros((rows, SSD_PAD), F32)
    for h in range(SSD_HEADS):
        g, r = divmod(h, HEADS_PER_GROUP)
        out = jnp.where(slab == g * (GROUP_PAD // SSD_HEAD_DIM) + r, small[:, h:h + 1], out)
    return out


def _ssd_kernel(z_ref, xs_ref, b_ref, c_ref, dt_ref, cwx_ref, cwb_ref, cwc_ref, cbx_ref, cbb_ref, cbc_ref,
                dtb_ref, alog_ref, dskip_ref, ng_ref, o_ref, state_ref, *, seq):
    q = SSD_CHUNK
    n = SSD_STATE
    state_ref[...] = jnp.zeros_like(state_ref)
    row_i = lax.broadcasted_iota(jnp.int32, (q, q), 0)
    col_i = lax.broadcasted_iota(jnp.int32, (q, q), 1)
    lower = row_i >= col_i
    tril = jnp.where(lower, 1.0, 0.0)
    slab = lax.broadcasted_iota(jnp.int32, (1, GROUP_PAD), 1) // SSD_HEAD_DIM
    a_neg = -jnp.exp(alog_ref[...])

    def conv_silu(ref, w_ref, bias_ref, c):
        start = pl.multiple_of(c * q, q)
        cur = ref[0, pl.ds(start, q), :]
        prev_start = pl.multiple_of(jnp.maximum(c * q - 8, 0), 8)
        prev = jnp.where(c > 0, ref[0, pl.ds(prev_start, 8), :], 0.0)
        full = jnp.concatenate([prev, cur], axis=0)
        y = bias_ref[...]
        for i in range(SSD_CONV):
            shift = SSD_CONV - 1 - i
            y = y + w_ref[i:i + 1, :] * full[8 - shift:8 - shift + q, :]
        return _silu(y)

    def chunk(c, carry):
        start = pl.multiple_of(c * q, q)
        xs = conv_silu(xs_ref, cwx_ref, cbx_ref, c)
        bm = conv_silu(b_ref, cwb_ref, cbb_ref, c)
        cm = conv_silu(c_ref, cwc_ref, cbc_ref, c)
        x = dt_ref[0, pl.ds(start, q), :] + dtb_ref[...]
        dt = jnp.maximum(x, 0.0) + jnp.log(1.0 + jnp.exp(-jnp.abs(x)))
        a = dt * a_neg
        a_cum = jnp.dot(tril, a, precision=HIGHEST, preferred_element_type=F32)
        a_cum_t = a_cum.T
        dt_e = _expand_heads(dt)
        acum_e = _expand_heads(a_cum)
        alast_e = acum_e[q - 1:q, :]
        xdt = xs * dt_e
        decay_e = jnp.exp(alast_e - acum_e)
        y_parts = []
        for g in range(SSD_GROUPS):
            gl = slice(g * GROUP_PAD, (g + 1) * GROUP_PAD)
            nl = slice(g * n, (g + 1) * n)
            b_g = bm[:, nl].astype(BF16)
            b_gt = bm[:, nl].T.astype(BF16)
            c_g = cm[:, nl].astype(BF16)
            cb = lax.dot_general(c_g, b_g, (((1,), (1,)), ((), ())), preferred_element_type=F32)
            xdt_g = xdt[:, gl]
            xdt_b = xdt_g.astype(BF16)
            y_g = jnp.zeros((q, GROUP_PAD), F32)
            for r in range(HEADS_PER_GROUP):
                h = g * HEADS_PER_GROUP + r
                seg = a_cum[:, h:h + 1] - a_cum_t[h:h + 1, :]
                l_h = jnp.where(lower, jnp.exp(jnp.where(lower, seg, 0.0)), 0.0)
                y_h = jnp.dot((cb * l_h).astype(BF16), xdt_b, preferred_element_type=F32)
                y_g = jnp.where(slab == r, y_h, y_g)
            prev = state_ref[g]
            y_off = jnp.dot(c_g, prev.astype(BF16), preferred_element_type=F32) * jnp.exp(acum_e[:, gl])
            new = jnp.dot(b_gt, (xdt_g * decay_e[:, gl]).astype(BF16), preferred_element_type=F32)
            state_ref[g] = prev * jnp.exp(alast_e[:, gl]) + new
            y_parts.append(y_g + y_off)
        y = jnp.concatenate(y_parts, axis=1) + dskip_ref[...] * xs
        yz = y * _silu(z_ref[0, pl.ds(start, q), :])
        outs = []
        for g in range(SSD_GROUPS):
            yg = yz[:, g * GROUP_PAD:(g + 1) * GROUP_PAD]
            ms = jnp.sum(yg * yg, axis=-1, keepdims=True) * (1.0 / GROUP_W)
            outs.append(yg * lax.rsqrt(ms + LN_EPS))
        o_ref[0, pl.ds(start, q), :] = jnp.concatenate(outs, axis=1) * ng_ref[...]
        return carry

    lax.fori_loop(0, seq // q, chunk, 0)


def _ssd(proj3, p):
    bsz, seq, _ = proj3.shape
    slab = lambda width, base: pl.BlockSpec((1, seq, width), lambda b: (b, 0, base // width))
    const2 = lambda a: pl.BlockSpec(a.shape, lambda b: (0, 0))
    small = [p["conv_w_xs"], p["conv_w_b"], p["conv_w_c"], p["conv_b_xs"], p["conv_b_b"], p["conv_b_c"],
             p["dt_bias"], p["a_log"], p["d_skip"], p["ssd_norm_g"]]
    return pl.pallas_call(
        functools.partial(_ssd_kernel, seq=seq),
        grid=(bsz,),
        in_specs=[slab(SSD_PAD, COL_Z), slab(SSD_PAD, COL_XS), slab(2 * SSD_STATE, COL_B),
                  slab(2 * SSD_STATE, COL_C), slab(LANES, COL_DT)] + [const2(a) for a in small],
        out_specs=pl.BlockSpec((1, seq, SSD_PAD), lambda b: (b, 0, 0)),
        out_shape=jax.ShapeDtypeStruct((bsz, seq, SSD_PAD), F32),
        scratch_shapes=[pltpu.VMEM((SSD_GROUPS, SSD_STATE, GROUP_PAD), F32)],
        compiler_params=_params(("parallel",)),
        name="ssd",
    )(proj3, proj3, proj3, proj3, proj3, *small)


GM_ROWS = 1024


def _gmlp_kernel(u_ref, v_ref, lng_ref, lnb_ref, ws_ref, bs_ref, ng_ref, o_ref):
    q = GM_CHUNK
    row_i = lax.broadcasted_iota(jnp.int32, (q, q), 0)
    col_i = lax.broadcasted_iota(jnp.int32, (q, q), 1)
    lower = row_i >= col_i
    grp = lax.broadcasted_iota(jnp.int32, (1, GM_W), 1) // (GM_W // GM_GROUPS)
    w = [jnp.where(lower, ws_ref[g], 0.0).astype(BF16) for g in range(GM_GROUPS)]
    for c in range(GM_ROWS // q):
        rows = slice(c * q, (c + 1) * q)
        u = _gelu_tanh(u_ref[rows, :])
        v = _layer_norm(_gelu_tanh(v_ref[rows, :]), lng_ref[...], lnb_ref[...])
        vb = v.astype(BF16)
        mixed = jnp.zeros((q, GM_W), F32)
        for g in range(GM_GROUPS):
            mixed = jnp.where(grp == g, jnp.dot(w[g], vb, preferred_element_type=F32), mixed)
        gm = u * (mixed + bs_ref[...])
        ms = jnp.mean(gm * gm, axis=-1, keepdims=True)
        o_ref[rows, :] = gm * lax.rsqrt(ms + LN_EPS) * ng_ref[...]


def _gmlp(proj, p):
    m = proj.shape[0]
    const = lambda a: pl.BlockSpec(a.shape, lambda i: (0,) * a.ndim)
    small = [p["gm_ln_g"], p["gm_ln_b"], p["gm_w_s"], p["gm_b_s"], p["gm_norm_g"]]
    return pl.pallas_call(
        _gmlp_kernel,
        grid=(m // GM_ROWS,),
        in_specs=[pl.BlockSpec((GM_ROWS, GM_W), lambda i: (i, COL_GU // GM_W)),
                  pl.BlockSpec((GM_ROWS, GM_W), lambda i: (i, COL_GV // GM_W))] + [const(a) for a in small],
        out_specs=pl.BlockSpec((GM_ROWS, GM_W), lambda i: (i, 0)),
        out_shape=jax.ShapeDtypeStruct((m, GM_W), F32),
        compiler_params=_params(("parallel",)),
        name="gmlp",
    )(proj, proj, *small)


OUT_TM = 512


def _out_proj_kernel(att_ref, ssd_ref, gm_ref, h_ref, ag_ref, wa_ref, ws_ref, wg_ref, g_ref, b_ref, o_ref):
    att = att_ref[...]
    ms = jnp.mean(att * att, axis=-1, keepdims=True)
    att = att * lax.rsqrt(ms + LN_EPS) * ag_ref[...]
    mix = jnp.dot(att.astype(BF16), wa_ref[...], preferred_element_type=F32)
    mix = mix + jnp.dot(ssd_ref[...].astype(BF16), ws_ref[...], preferred_element_type=F32)
    mix = mix + jnp.dot(gm_ref[...].astype(BF16), wg_ref[...], preferred_element_type=F32)
    o_ref[...] = _layer_norm(ALPHA * h_ref[...] + mix, g_ref[...], b_ref[...])


def _out_proj(att, ssd, gm, h, p):
    m = h.shape[0]
    row = lambda i: (i, 0)
    const = lambda a: pl.BlockSpec(a.shape, lambda i: (0, 0))
    small = [p["att_norm_g"], p["w_out_att"], p["w_out_ssd"], p["w_out_gm"], p["ln1_g"], p["ln1_b"]]
    return pl.pallas_call(
        _out_proj_kernel,
        grid=(m // OUT_TM,),
        in_specs=[pl.BlockSpec((OUT_TM, ATT_W), row), pl.BlockSpec((OUT_TM, SSD_PAD), row),
                  pl.BlockSpec((OUT_TM, GM_W), row), pl.BlockSpec((OUT_TM, D_MODEL), row)]
                 + [const(a) for a in small],
        out_specs=pl.BlockSpec((OUT_TM, D_MODEL), row),
        out_shape=jax.ShapeDtypeStruct((m, D_MODEL), F32),
        compiler_params=_params(("parallel",)),
        name="out_proj_ln1",
    )(att, ssd, gm, h, *small)


FFN_TM = 512


def _ffn_kernel(x_ref, gate_ref, wg_ref, wu_ref, wd_ref, g_ref, b_ref, o_ref, acc_ref, *, gated):
    e = pl.program_id(1)
    f = pl.program_id(2)

    @pl.when((e == 0) & (f == 0))
    def _():
        acc_ref[...] = jnp.zeros_like(acc_ref)

    x = x_ref[...]
    xb = x.astype(BF16)
    hg = jnp.dot(xb, wg_ref[0], preferred_element_type=F32)
    hu = jnp.dot(xb, wu_ref[0], preferred_element_type=F32)
    hid = _silu(hg) * hu
    if gated:
        hid = hid * gate_ref[0]
    acc_ref[...] += jnp.dot(hid.astype(BF16), wd_ref[0], preferred_element_type=F32)

    @pl.when((e == pl.num_programs(1) - 1) & (f == pl.num_programs(2) - 1))
    def _():
        o_ref[...] = _layer_norm(ALPHA * x + acc_ref[...], g_ref[...], b_ref[...])


def _ffn(x, gates, wg, wu, wd, ln_g, ln_b, tf):
    m = x.shape[0]
    n_e, _, d_ff = wg.shape
    gated = gates is not None
    if not gated:
        gates = jnp.ones((1, m, 1), F32)
    return pl.pallas_call(
        functools.partial(_ffn_kernel, gated=gated),
        grid=(m // FFN_TM, n_e, d_ff // tf),
        in_specs=[pl.BlockSpec((FFN_TM, D_MODEL), lambda i, e, f: (i, 0)),
                  pl.BlockSpec((1, FFN_TM, 1), lambda i, e, f: (e, i, 0)),
                  pl.BlockSpec((1, D_MODEL, tf), lambda i, e, f: (e, 0, f)),
                  pl.BlockSpec((1, D_MODEL, tf), lambda i, e, f: (e, 0, f)),
                  pl.BlockSpec((1, tf, D_MODEL), lambda i, e, f: (e, f, 0)),
                  pl.BlockSpec((1, D_MODEL), lambda i, e, f: (0, 0)),
                  pl.BlockSpec((1, D_MODEL), lambda i, e, f: (0, 0))],
        out_specs=pl.BlockSpec((FFN_TM, D_MODEL), lambda i, e, f: (i, 0)),
        out_shape=jax.ShapeDtypeStruct((m, D_MODEL), F32),
        scratch_shapes=[pltpu.VMEM((FFN_TM, D_MODEL), F32)],
        compiler_params=_params(("parallel", "arbitrary", "arbitrary")),
        name="ffn_ln2",
    )(x, gates, wg, wu, wd, ln_g.reshape(1, D_MODEL), ln_b.reshape(1, D_MODEL))


MOE_TR = 256
MOE_TM = 1024
MOE_TF = 512
SEG_ALIGN = 16
INFO_GATE = N_EXPERTS


def _expert_cap(m):
    n_tiles = m // MOE_TR
    rows = m + (SEG_ALIGN - 1) * n_tiles + MOE_TR + MOE_TM
    return -(-rows // MOE_TM) * MOE_TM


def _dispatch_kernel(h_ref, wt_ref, info_ref, offs_ref, lens_ref, xs_hbm, run_ref, xbuf_ref, zbuf_ref,
                     sem_ref, zsem_ref, *, n_tiles, cap):
    t = pl.program_id(0)
    tr = MOE_TR
    slot = t % 2

    @pl.when(t == 0)
    def _():
        for e in range(N_EXPERTS):
            run_ref[e] = 0

    x = h_ref[...]
    xb = x.astype(BF16)
    lg = lax.dot_general(wt_ref[...], x, (((1,), (1,)), ((), ())), precision=HIGHEST,
                         preferred_element_type=F32)
    row = lax.broadcasted_iota(jnp.int32, (N_EXPERTS, tr), 0)
    m1 = jnp.max(lg, axis=0, keepdims=True)
    i1 = jnp.min(jnp.where(lg == m1, row, N_EXPERTS), axis=0, keepdims=True)
    rest = jnp.where(row == i1, -jnp.inf, lg)
    m2 = jnp.max(rest, axis=0, keepdims=True)
    i2 = jnp.min(jnp.where(rest == m2, row, N_EXPERTS), axis=0, keepdims=True)
    e2 = jnp.exp(m2 - m1)
    p1 = 1.0 / (1.0 + e2)
    gate_t = jnp.where(row == i1, p1, 0.0) + jnp.where(row == i2, e2 * p1, 0.0)
    sel_t = jnp.where((row == i1) | (row == i2), 1.0, 0.0)
    before = (lax.broadcasted_iota(jnp.int32, (tr, tr), 0) < lax.broadcasted_iota(jnp.int32, (tr, tr), 1))
    rank_t = jnp.dot(sel_t.astype(BF16), jnp.where(before, 1.0, 0.0).astype(BF16),
                     preferred_element_type=F32)
    cnt = jnp.sum(sel_t, axis=1, keepdims=True).astype(jnp.int32)
    info_t = jnp.concatenate([rank_t, gate_t, jnp.zeros((LANES - 2 * N_EXPERTS, tr), F32)], axis=0)
    info_ref[...] = info_t.T

    def seg_copy(sl, e, off):
        return pltpu.make_async_copy(xbuf_ref.at[sl, e], xs_hbm.at[pl.ds(pl.multiple_of(off, SEG_ALIGN), tr)],
                                     sem_ref.at[sl, e])

    dst_row = lax.broadcasted_iota(jnp.int32, (tr, tr), 0).astype(F32)
    for e in range(N_EXPERTS):
        onehot = jnp.where((rank_t[e:e + 1, :] == dst_row) & (sel_t[e:e + 1, :] > 0.0), 1.0, 0.0).astype(BF16)
        xbuf_ref[slot, e] = jnp.dot(onehot, xb, preferred_element_type=F32).astype(BF16)

        @pl.when(t > 0)
        def _():
            seg_copy(1 - slot, e, 0).wait()

        off = e * cap + run_ref[e]
        offs_ref[t * N_EXPERTS + e] = off
        seg_copy(slot, e, off).start()
        run_ref[e] = run_ref[e] + jnp.bitwise_and(cnt[e, 0] + (SEG_ALIGN - 1), -SEG_ALIGN)

    @pl.when(t == n_tiles - 1)
    def _():
        zbuf_ref[...] = jnp.zeros_like(zbuf_ref)
        for e in range(N_EXPERTS):
            seg_copy(slot, e, 0).wait()
        tails = []
        for e in range(N_EXPERTS):
            lens_ref[e] = run_ref[e]
            tail = pl.multiple_of(e * cap + run_ref[e], SEG_ALIGN)
            tails.append(pltpu.make_async_copy(zbuf_ref, xs_hbm.at[pl.ds(tail, MOE_TR + MOE_TM)], zsem_ref.at[e]))
            tails[-1].start()
        for cp in tails:
            cp.wait()


def _dispatch(h, w_router):
    m = h.shape[0]
    n_tiles = m // MOE_TR
    cap = _expert_cap(m)
    smem = pl.BlockSpec(memory_space=pltpu.SMEM)
    return pl.pallas_call(
        functools.partial(_dispatch_kernel, n_tiles=n_tiles, cap=cap),
        grid=(n_tiles,),
        in_specs=[pl.BlockSpec((MOE_TR, D_MODEL), lambda t: (t, 0)),
                  pl.BlockSpec((N_EXPERTS, D_MODEL), lambda t: (0, 0))],
        out_specs=[pl.BlockSpec((MOE_TR, LANES), lambda t: (t, 0)), smem, smem,
                   pl.BlockSpec(memory_space=pl.ANY)],
        out_shape=[jax.ShapeDtypeStruct((m, LANES), F32),
                   jax.ShapeDtypeStruct((n_tiles * N_EXPERTS,), jnp.int32),
                   jax.ShapeDtypeStruct((N_EXPERTS,), jnp.int32),
                   jax.ShapeDtypeStruct((N_EXPERTS * cap, D_MODEL), BF16)],
        scratch_shapes=[pltpu.SMEM((N_EXPERTS,), jnp.int32),
                        pltpu.VMEM((2, N_EXPERTS, MOE_TR, D_MODEL), BF16),
                        pltpu.VMEM((MOE_TR + MOE_TM, D_MODEL), BF16),
                        pltpu.SemaphoreType.DMA((2, N_EXPERTS)),
                        pltpu.SemaphoreType.DMA((N_EXPERTS,))],
        compiler_params=_params(("arbitrary",)),
        name="moe_dispatch",
    )(h, w_router.T)


def _expert_ffn_kernel(exp_ref, blk_ref, valid_ref, x_ref, wg_ref, wu_ref, wd_ref, y_ref, acc_ref):
    w = pl.program_id(0)
    f = pl.program_id(1)

    @pl.when(valid_ref[w] == 1)
    def _():
        @pl.when(f == 0)
        def _():
            acc_ref[...] = jnp.zeros_like(acc_ref)

        xb = x_ref[...]
        hg = jnp.dot(xb, wg_ref[0].astype(BF16), preferred_element_type=F32)
        hu = jnp.dot(xb, wu_ref[0].astype(BF16), preferred_element_type=F32)
        hid = (_silu(hg) * hu).astype(BF16)
        acc_ref[...] += jnp.dot(hid, wd_ref[0].astype(BF16), preferred_element_type=F32)

        @pl.when(f == pl.num_programs(1) - 1)
        def _():
            y_ref[...] = acc_ref[...].astype(BF16)


def _expert_ffn(xs, exp_w, blk_w, valid_w, w_gate, w_up, w_down):
    d_ff = w_gate.shape[-1]
    n_f = d_ff // MOE_TF
    fcol = lambda w, f, exp, blk, valid: jnp.where(valid[w] == 1, f, n_f - 1)
    grid_spec = pltpu.PrefetchScalarGridSpec(
        num_scalar_prefetch=3,
        grid=(exp_w.shape[0], n_f),
        in_specs=[pl.BlockSpec((MOE_TM, D_MODEL), lambda w, f, exp, blk, valid: (blk[w], 0)),
                  pl.BlockSpec((1, D_MODEL, MOE_TF), lambda w, f, exp, blk, valid: (exp[w], 0, fcol(w, f, exp, blk, valid))),
                  pl.BlockSpec((1, D_MODEL, MOE_TF), lambda w, f, exp, blk, valid: (exp[w], 0, fcol(w, f, exp, blk, valid))),
                  pl.BlockSpec((1, MOE_TF, D_MODEL), lambda w, f, exp, blk, valid: (exp[w], fcol(w, f, exp, blk, valid), 0))],
        out_specs=pl.BlockSpec((MOE_TM, D_MODEL), lambda w, f, exp, blk, valid: (blk[w], 0)),
        scratch_shapes=[pltpu.VMEM((MOE_TM, D_MODEL), F32)])
    return pl.pallas_call(
        _expert_ffn_kernel,
        grid_spec=grid_spec,
        out_shape=jax.ShapeDtypeStruct(xs.shape, BF16),
        compiler_params=_params(("arbitrary", "arbitrary")),
        name="moe_expert_ffn",
    )(exp_w, blk_w, valid_w, xs, w_gate, w_up, w_down)


def _combine_kernel(offs_ref, info_ref, h_ref, g_ref, b_ref, ys_hbm, o_ref, ybuf_ref, sem_ref, *, n_tiles):
    t = pl.program_id(0)
    tr = MOE_TR
    slot = t % 2

    def seg_copy(tile, sl, e):
        off = pl.multiple_of(offs_ref[tile * N_EXPERTS + e], SEG_ALIGN)
        return pltpu.make_async_copy(ys_hbm.at[pl.ds(off, tr)], ybuf_ref.at[sl, e], sem_ref.at[sl, e])

    @pl.when(t == 0)
    def _():
        for e in range(N_EXPERTS):
            seg_copy(0, 0, e).start()

    @pl.when(t + 1 < n_tiles)
    def _():
        for e in range(N_EXPERTS):
            seg_copy(t + 1, 1 - slot, e).start()

    info = info_ref[...]
    src_row = lax.broadcasted_iota(jnp.int32, (tr, tr), 1).astype(F32)
    acc = jnp.zeros((tr, D_MODEL), F32)
    for e in range(N_EXPERTS):
        seg_copy(t, slot, e).wait()
        rank_c = info[:, e:e + 1]
        gate_c = info[:, INFO_GATE + e:INFO_GATE + e + 1]
        onehot = jnp.where((rank_c == src_row) & (gate_c != 0.0), 1.0, 0.0).astype(BF16)
        acc = acc + gate_c * jnp.dot(onehot, ybuf_ref[slot, e], preferred_element_type=F32)
    o_ref[...] = _layer_norm(ALPHA * h_ref[...] + acc, g_ref[...], b_ref[...])


def _combine(offs, info, h, ys, ln_g, ln_b):
    m = h.shape[0]
    n_tiles = m // MOE_TR
    grid_spec = pltpu.PrefetchScalarGridSpec(
        num_scalar_prefetch=1,
        grid=(n_tiles,),
        in_specs=[pl.BlockSpec((MOE_TR, LANES), lambda t, offs: (t, 0)),
                  pl.BlockSpec((MOE_TR, D_MODEL), lambda t, offs: (t, 0)),
                  pl.BlockSpec((1, D_MODEL), lambda t, offs: (0, 0)),
                  pl.BlockSpec((1, D_MODEL), lambda t, offs: (0, 0)),
                  pl.BlockSpec(memory_space=pl.ANY)],
        out_specs=pl.BlockSpec((MOE_TR, D_MODEL), lambda t, offs: (t, 0)),
        scratch_shapes=[pltpu.VMEM((2, N_EXPERTS, MOE_TR, D_MODEL), BF16),
                        pltpu.SemaphoreType.DMA((2, N_EXPERTS))])
    return pl.pallas_call(
        functools.partial(_combine_kernel, n_tiles=n_tiles),
        grid_spec=grid_spec,
        out_shape=jax.ShapeDtypeStruct((m, D_MODEL), F32),
        compiler_params=_params(("arbitrary",)),
        name="moe_combine_ln2",
    )(offs, info, h, ln_g.reshape(1, D_MODEL), ln_b.reshape(1, D_MODEL), ys)


def _work_list(lens, m):
    cap = _expert_cap(m)
    n_tiles = m // MOE_TR
    max_rows = 2 * m + (SEG_ALIGN - 1) * min(N_EXPERTS * n_tiles, 2 * m) + N_EXPERTS * MOE_TR
    w_max = -(-max_rows // MOE_TM) + N_EXPERTS
    tiles_e = (lens + MOE_TR + MOE_TM - 1) // MOE_TM
    ends = jnp.cumsum(tiles_e)
    w = jnp.arange(w_max, dtype=jnp.int32)
    wc = jnp.minimum(w, ends[-1] - 1)
    exp_w = jnp.sum((wc[:, None] >= ends[None, :]).astype(jnp.int32), axis=1)
    blk_w = exp_w * (cap // MOE_TM) + wc - (ends - tiles_e)[exp_w]
    return exp_w.astype(jnp.int32), blk_w.astype(jnp.int32), (w < ends[-1]).astype(jnp.int32)


def _moe_layer(h, w_router, w_gate, w_up, w_down, ln_g, ln_b):
    m = h.shape[0]
    info, offs, lens, xs = _dispatch(h, w_router)
    exp_w, blk_w, valid_w = _work_list(lens, m)
    ys = _expert_ffn(xs, exp_w, blk_w, valid_w, w_gate, w_up, w_down)
    return _combine(offs, info, h, ys, ln_g, ln_b)


def _pad_groups(a):
    lead = a.shape[:-1]
    a = a.reshape(*lead, SSD_GROUPS, GROUP_W)
    a = jnp.pad(a, [(0, 0)] * len(lead) + [(0, 0), (0, GROUP_PAD - GROUP_W)])
    return a.reshape(*lead, SSD_PAD)


def _head_lanes(v):
    return jnp.pad(v, (0, LANES - SSD_HEADS)).reshape(1, LANES)


def _layer_params(i, w_in, conv_w, conv_b, dt_bias, a_log, d_skip, ssd_norm_g, att_norm_g, gm_ln_g, gm_ln_b,
                  gm_w_s, gm_b_s, gm_norm_g, w_out, ln1_g, ln1_b):
    w = w_in[i]
    c1, c2, c3 = ATT_W, 2 * ATT_W, 3 * ATT_W
    c4 = c3 + SSD_W
    c5 = c4 + SSD_W
    c6 = c5 + SSD_GROUPS * SSD_STATE
    c7 = c6 + SSD_GROUPS * SSD_STATE
    c8 = c7 + SSD_HEADS
    c9 = c8 + GM_W
    w_dt = jnp.pad(w[:, c7:c8], ((0, 0), (0, LANES - SSD_HEADS)))
    w_perm = jnp.concatenate([_pad_groups(w[:, c3:c4]), _pad_groups(w[:, c4:c5]), w[:, c5:c6], w[:, c6:c7],
                              w[:, c8:c9], w[:, c9:], w[:, :c1], w[:, c1:c2], w[:, c2:c3], w_dt], axis=1)
    cw, cb = conv_w[i], conv_b[i]
    nbc = SSD_GROUPS * SSD_STATE
    wo = w_out[i]
    return {
        "w_in": w_perm.astype(BF16),
        "conv_w_xs": _pad_groups(cw[:, :SSD_W]),
        "conv_w_b": cw[:, SSD_W:SSD_W + nbc],
        "conv_w_c": cw[:, SSD_W + nbc:],
        "conv_b_xs": _pad_groups(cb[:SSD_W]).reshape(1, SSD_PAD),
        "conv_b_b": cb[SSD_W:SSD_W + nbc].reshape(1, nbc),
        "conv_b_c": cb[SSD_W + nbc:].reshape(1, nbc),
        "dt_bias": _head_lanes(dt_bias[i]),
        "a_log": _head_lanes(a_log[i]),
        "d_skip": _pad_groups(jnp.repeat(d_skip[i], SSD_HEAD_DIM)).reshape(1, SSD_PAD),
        "ssd_norm_g": _pad_groups(ssd_norm_g[i]).reshape(1, SSD_PAD),
        "att_norm_g": att_norm_g[i].reshape(1, ATT_W),
        "gm_ln_g": gm_ln_g[i].reshape(1, GM_W),
        "gm_ln_b": gm_ln_b[i].reshape(1, GM_W),
        "gm_w_s": gm_w_s[i],
        "gm_b_s": jnp.repeat(gm_b_s[i].T, GM_W // GM_GROUPS, axis=1),
        "gm_norm_g": gm_norm_g[i].reshape(1, GM_W),
        "w_out_att": wo[:ATT_W].astype(BF16),
        "w_out_ssd": _pad_groups(wo[ATT_W:ATT_W + SSD_W].T).T.astype(BF16),
        "w_out_gm": wo[ATT_W + SSD_W:].astype(BF16),
        "ln1_g": ln1_g[i].reshape(1, D_MODEL),
        "ln1_b": ln1_b[i].reshape(1, D_MODEL),
    }


def _rope_tables(positions):
    inv = ROPE_THETA ** (-jnp.arange(0, ROPE_DIM, 2, dtype=F32) / ROPE_DIM)
    ang = positions.astype(F32)[..., None] * inv
    cos, sin = jnp.cos(ang), jnp.sin(ang)
    pad = HEAD_DIM - ROPE_DIM
    one_head = lambda lo, hi, fill: jnp.concatenate(
        [lo, hi, jnp.full(cos.shape[:-1] + (pad,), fill, F32)], axis=-1)
    zero = jnp.zeros_like(sin)
    two = lambda t: jnp.concatenate([t, t], axis=-1)
    return two(one_head(cos, cos, 1.0)), two(one_head(-sin, zero, 0.0)), two(one_head(zero, sin, 0.0))


def kernel(x, positions, ln_in_g, ln_in_b, w_in, conv_w, conv_b, dt_bias, a_log, d_skip, ssd_norm_g, att_norm_g, gm_ln_g, gm_ln_b, gm_w_s, gm_b_s, gm_norm_g, w_out, ln1_g, ln1_b, ln2_g, ln2_b, ffn_w_gate, ffn_w_up, ffn_w_down, router_w, moe_w_gate, moe_w_up, moe_w_down):
    bsz, seq, _ = x.shape
    m = bsz * seq
    rope_c, rope_sa, rope_sb = _rope_tables(positions)
    h = x.reshape(m, D_MODEL)
    for i in range(DEPTH):
        p = _layer_params(i, w_in, conv_w, conv_b, dt_bias, a_log, d_skip, ssd_norm_g, att_norm_g, gm_ln_g,
                          gm_ln_b, gm_w_s, gm_b_s, gm_norm_g, w_out, ln1_g, ln1_b)
        if i == 0:
            proj, h = _in_proj(h, p["w_in"], ln=(ln_in_g, ln_in_b))
        else:
            (proj,) = _in_proj(h, p["w_in"])
        proj3 = proj.reshape(bsz, seq, D_PROJ)
        att = _attention(proj3, rope_c, rope_sa, rope_sb).reshape(m, ATT_W)
        ssd = _ssd(proj3, p).reshape(m, SSD_PAD)
        gm = _gmlp(proj, p)
        h = _out_proj(att, ssd, gm, h, p)
        if i % 2 == 0:
            j = i // 2
            h = _ffn(h, None, ffn_w_gate[j][None].astype(BF16), ffn_w_up[j][None].astype(BF16),
                     ffn_w_down[j][None].astype(BF16), ln2_g[i], ln2_b[i], tf=1408)
        else:
            j = i // 2
            h = _moe_layer(h, router_w[j], moe_w_gate[j], moe_w_up[j], moe_w_down[j], ln2_g[i], ln2_b[i])
    return h.reshape(bsz, seq, D_MODEL)
```

```python
import functools

import jax
import jax.numpy as jnp
from jax import lax
from jax.experimental import pallas as pl
from jax.experimental.pallas import tpu as pltpu

F32 = jnp.float32
BF16 = jnp.bfloat16
HIGHEST = lax.Precision.HIGHEST

D_MODEL = 1024
ATT_HEADS = 6
HEAD_DIM = 64
ATT_W = ATT_HEADS * HEAD_DIM
ROPE_DIM = HEAD_DIM // 4
ROPE_HALF = ROPE_DIM // 2
ROPE_THETA = 500000.0
MOBA_BLOCK = 256
MOBA_TOPK = 3
SSD_HEADS = 6
SSD_HEAD_DIM = 64
SSD_W = SSD_HEADS * SSD_HEAD_DIM
SSD_GROUPS = 2
SSD_STATE = 128
SSD_CONV = 4
SSD_CHUNK = 128
GM_GROUPS = 4
GM_W = 256
GM_CHUNK = 128
N_EXPERTS = 8
DEPTH = 2
ALPHA = (2.0 * DEPTH) ** 0.25
NEG = -1e30
LN_EPS = 1e-5

LANES = 128
GROUP_W = SSD_W // SSD_GROUPS
GROUP_PAD = 256
SSD_PAD = SSD_GROUPS * GROUP_PAD
HEADS_PER_GROUP = SSD_HEADS // SSD_GROUPS

COL_Z = 0
COL_XS = 512
COL_B = 1024
COL_C = 1280
COL_GU = 1536
COL_GV = 1792
COL_Q = 2048
COL_K = 2432
COL_V = 2816
COL_DT = 3200
D_PROJ = 3328

VMEM_LIMIT = 56 * 1024 * 1024


def _params(sem, vmem=VMEM_LIMIT):
    return pltpu.CompilerParams(dimension_semantics=sem, vmem_limit_bytes=vmem)


def _layer_norm(x, g, b):
    mu = jnp.mean(x, axis=-1, keepdims=True)
    xc = x - mu
    var = jnp.mean(xc * xc, axis=-1, keepdims=True)
    return xc * lax.rsqrt(var + LN_EPS) * g + b


def _silu(x):
    half = 0.5 * x
    return half + half * jnp.tanh(half)


def _split_bf16(x):
    hi = x.astype(BF16)
    return hi, (x - hi.astype(F32)).astype(BF16)


def _dot_bf16x3(a, b):
    a_hi, a_lo = _split_bf16(a)
    b_hi, b_lo = _split_bf16(b)
    dot = functools.partial(jnp.dot, preferred_element_type=F32)
    return dot(a_hi, b_hi) + (dot(a_lo, b_hi) + dot(a_hi, b_lo))


def _gelu_tanh(x):
    return 0.5 * x * (1.0 + jnp.tanh(0.7978845608028654 * (x + 0.044715 * x * x * x)))


PROJ_TM = 512
PROJ_CH = 256


def _in_proj_kernel(*refs, apply_ln):
    if apply_ln:
        x_ref, g_ref, b_ref, w_ref, proj_ref, h_ref = refs
        x = _layer_norm(x_ref[...], g_ref[...], b_ref[...])
        h_ref[...] = x
    else:
        x_ref, w_ref, proj_ref = refs
        x = x_ref[...]
    xb = x.astype(BF16)
    for j in range(D_PROJ // PROJ_CH):
        cols = slice(j * PROJ_CH, (j + 1) * PROJ_CH)
        proj_ref[:, cols] = jnp.dot(xb, w_ref[:, cols], preferred_element_type=F32)


def _in_proj(x, w, ln=None):
    m = x.shape[0]
    row = lambda i: (i, 0)
    const = lambda i: (0, 0)
    in_specs = [pl.BlockSpec((PROJ_TM, D_MODEL), row)]
    args = [x]
    out_shape = [jax.ShapeDtypeStruct((m, D_PROJ), F32)]
    out_specs = [pl.BlockSpec((PROJ_TM, D_PROJ), row)]
    if ln is not None:
        in_specs += [pl.BlockSpec((1, D_MODEL), const), pl.BlockSpec((1, D_MODEL), const)]
        args += [ln[0].reshape(1, D_MODEL), ln[1].reshape(1, D_MODEL)]
        out_shape.append(jax.ShapeDtypeStruct((m, D_MODEL), F32))
        out_specs.append(pl.BlockSpec((PROJ_TM, D_MODEL), row))
    in_specs.append(pl.BlockSpec((D_MODEL, D_PROJ), const))
    args.append(w)
    return pl.pallas_call(
        functools.partial(_in_proj_kernel, apply_ln=ln is not None),
        grid=(m // PROJ_TM,),
        in_specs=in_specs,
        out_specs=out_specs,
        out_shape=out_shape,
        compiler_params=_params(("parallel",)),
        name="in_proj",
    )(*args)


VT_ROWS = HEAD_DIM + 16
LOG2E = 1.4426950408889634


def _attn_kernel(q_ref, k_ref, v_ref, c_ref, sa_ref, sb_ref, o_ref, kb_ref, vt_ref, *, seq):
    nb = seq // MOBA_BLOCK
    tq = MOBA_BLOCK
    lane = lax.broadcasted_iota(jnp.int32, (1, LANES), 1)

    def rope(x, rows):
        return (x * c_ref[0, rows, :] + pltpu.roll(x, LANES - ROPE_HALF, 1) * sa_ref[0, rows, :]
                + pltpu.roll(x, ROPE_HALF, 1) * sb_ref[0, rows, :])

    k = rope(k_ref[0], slice(None))
    kb_ref[...] = k.astype(BF16)
    k_mean = jnp.mean(k.reshape(nb, MOBA_BLOCK, LANES), axis=1)
    v_t = v_ref[0].T
    ones_rows = jnp.where(lax.broadcasted_iota(jnp.int32, (VT_ROWS - HEAD_DIM, seq), 0) == 0, 1.0, 0.0)
    for hh in range(2):
        vt_ref[hh] = jnp.concatenate([v_t[hh * HEAD_DIM:(hh + 1) * HEAD_DIM], ones_rows], axis=0).astype(BF16)

    qscale = HEAD_DIM ** -0.5 * LOG2E
    causal_t = (lax.broadcasted_iota(jnp.int32, (tq, tq), 0) <= lax.broadcasted_iota(jnp.int32, (tq, tq), 1))
    blk_row = lax.broadcasted_iota(jnp.int32, (nb, tq), 0)

    q_tiles = {}

    def scores(j, hh):
        if hh == 0:
            rows = slice(j * tq, (j + 1) * tq)
            q_tiles[j] = rope(q_ref[0, rows, :], rows) * qscale
        q = q_tiles[j] if hh == 0 else q_tiles.pop(j)
        head_lanes = (lane < HEAD_DIM) if hh == 0 else (lane >= HEAD_DIM)
        qh_t = jnp.where(head_lanes, q, 0.0).T
        s_t = jnp.dot(kb_ref[0:(j + 1) * tq, :], qh_t.astype(BF16), preferred_element_type=F32)
        gate_t = _dot_bf16x3(k_mean, qh_t) if j > MOBA_TOPK else None
        return s_t, gate_t

    def probs(j, s_t, gate_t):
        n_past = j * tq
        own = jnp.where(causal_t, s_t[n_past:], NEG)
        mx = jnp.max(own, axis=0, keepdims=True)
        shifts = []
        if j > 0:
            blk_max = [jnp.max(s_t[n * tq:(n + 1) * tq], axis=0, keepdims=True) for n in range(j)]
            if j > MOBA_TOPK:
                rank = jnp.zeros((nb, tq), F32)
                for m in range(j):
                    g_m = gate_t[m:m + 1, :]
                    beats = (g_m > gate_t) | ((g_m == gate_t) & (blk_row > m))
                    rank = rank + jnp.where(beats & (blk_row != m), 1.0, 0.0)
                bias = jnp.where((blk_row < j) & (rank >= MOBA_TOPK), NEG, 0.0)
                bias_n = [bias[n:n + 1, :] for n in range(j)]
            else:
                bias_n = [0.0] * j
            for n in range(j):
                mx = jnp.maximum(mx, blk_max[n] + bias_n[n])
            shifts = [mx - bias_n[n] for n in range(j)]
        parts = [jnp.exp2(s_t[n * tq:(n + 1) * tq] - shifts[n]).astype(BF16) for n in range(j)]
        parts.append(jnp.exp2(own - mx).astype(BF16))
        return jnp.concatenate(parts, axis=0) if j > 0 else parts[0]

    def values(j, hh, p_t):
        o_t = jnp.dot(vt_ref[hh, :, 0:(j + 1) * tq], p_t, preferred_element_type=F32)
        return o_t[0:HEAD_DIM] / o_t[HEAD_DIM:HEAD_DIM + 1]

    chains = [(j, hh) for j in range(nb) for hh in range(2)]
    s_state = {}
    p_state = {}
    outs = {}
    for step in range(len(chains) + 2):
        if step >= 2:
            j, hh = chains[step - 2]
            outs[(j, hh)] = values(j, hh, p_state.pop((j, hh)))
            if hh == 1:
                o_ref[0, j * tq:(j + 1) * tq, :] = jnp.concatenate([outs.pop((j, 0)), outs.pop((j, 1))], axis=0).T
        if step < len(chains):
            j, hh = chains[step]
            s_state[(j, hh)] = scores(j, hh)
        if 1 <= step <= len(chains):
            j, hh = chains[step - 1]
            p_state[(j, hh)] = probs(j, *s_state.pop((j, hh)))


def _attention(proj3, rope_c, rope_sa, rope_sb):
    bsz, seq, _ = proj3.shape
    blk = (1, seq, LANES)
    col = lambda base: (lambda b, p: (b, 0, base // LANES + p))
    tab = lambda b, p: (b, 0, 0)
    return pl.pallas_call(
        functools.partial(_attn_kernel, seq=seq),
        grid=(bsz, ATT_W // LANES),
        in_specs=[pl.BlockSpec(blk, col(COL_Q)), pl.BlockSpec(blk, col(COL_K)), pl.BlockSpec(blk, col(COL_V)),
                  pl.BlockSpec(blk, tab), pl.BlockSpec(blk, tab), pl.BlockSpec(blk, tab)],
        out_specs=pl.BlockSpec(blk, lambda b, p: (b, 0, p)),
        out_shape=jax.ShapeDtypeStruct((bsz, seq, ATT_W), F32),
        scratch_shapes=[pltpu.VMEM((seq, LANES), BF16), pltpu.VMEM((2, VT_ROWS, seq), BF16)],
        compiler_params=_params(("parallel", "arbitrary")),
        name="moba_attention",
    )(proj3, proj3, proj3, rope_c, rope_sa, rope_sb)


assert HEADS_PER_GROUP == 3 and GROUP_PAD == 2 * LANES and SSD_HEAD_DIM * 2 == LANES


def _expand_heads(small):
    low = lax.broadcasted_iota(jnp.int32, (1, LANES), 1) < SSD_HEAD_DIM
    slabs = []
    for g in range(SSD_GROUPS):
        h0 = g * HEADS_PER_GROUP
        slabs.append(jnp.where(low, small[:, h0:h0 + 1], small[:, h0 + 1:h0 + 2]))
        slabs.append(jnp.where(low, small[:, h0 + 2:h0 + 3], 0.0))
    return jnp.concatenate(slabs, axis=1)


def _ssd_kernel(z_ref, xs_ref, b_ref, c_ref, dt_ref, cwx_ref, cwb_ref, cwc_ref, cbx_ref, cbb_ref, cbc_ref,
                dtb_ref, alog_ref, dskip_ref, ng_ref, o_ref, state_ref, *, seq):
    q = SSD_CHUNK
    n = SSD_STATE
    state_ref[...] = jnp.zeros_like(state_ref)
    row_i = lax.broadcasted_iota(jnp.int32, (q, q), 0)
    col_i = lax.broadcasted_iota(jnp.int32, (q, q), 1)
    lower = row_i >= col_i
    tril = jnp.where(lower, 1.0, 0.0)
    slab = lax.broadcasted_iota(jnp.int32, (1, GROUP_PAD), 1) // SSD_HEAD_DIM
    a_neg = -jnp.exp(alog_ref[...])

    def conv_silu(ref, w_ref, bias_ref, c):
        start = pl.multiple_of(c * q, q)
        cur = ref[0, pl.ds(start, q), :]
        prev_start = pl.multiple_of(jnp.maximum(c * q - 8, 0), 8)
        prev = jnp.where(c > 0, ref[0, pl.ds(prev_start, 8), :], 0.0)
        full = jnp.concatenate([prev, cur], axis=0)
        y = bias_ref[...]
        for i in range(SSD_CONV):
            shift = SSD_CONV - 1 - i
            y = y + w_ref[i:i + 1, :] * full[8 - shift:8 - shift + q, :]
        return _silu(y)

    def chunk(c, carry):
        start = pl.multiple_of(c * q, q)
        xs = conv_silu(xs_ref, cwx_ref, cbx_ref, c)
        bm = conv_silu(b_ref, cwb_ref, cbb_ref, c)
        cm = conv_silu(c_ref, cwc_ref, cbc_ref, c)
        x = dt_ref[0, pl.ds(start, q), :] + dtb_ref[...]
        dt = jnp.maximum(x, 0.0) + jnp.log(1.0 + jnp.exp(-jnp.abs(x)))
        a = dt * a_neg
        a_cum = jnp.dot(tril, a, precision=HIGHEST, preferred_element_type=F32)
        a_cum_t = a_cum.T
        dt_e = _expand_heads(dt)
        acum_e = _expand_heads(a_cum)
        alast_e = acum_e[q - 1:q, :]
        xdt = xs * dt_e
        decay_e = jnp.exp(alast_e - acum_e)
        y_parts = []
        for g in range(SSD_GROUPS):
            gl = slice(g * GROUP_PAD, (g + 1) * GROUP_PAD)
            nl = slice(g * n, (g + 1) * n)
            b_g = bm[:, nl].astype(BF16)
            b_gt = bm[:, nl].T.astype(BF16)
            c_g = cm[:, nl].astype(BF16)
            cb = lax.dot_general(c_g, b_g, (((1,), (1,)), ((), ())), preferred_element_type=F32)
            xdt_g = xdt[:, gl]
            xdt_b = xdt_g.astype(BF16)
            y_g = jnp.zeros((q, GROUP_PAD), F32)
            for r in range(HEADS_PER_GROUP):
                h = g * HEADS_PER_GROUP + r
                seg = a_cum[:, h:h + 1] - a_cum_t[h:h + 1, :]
                l_h = jnp.where(lower, jnp.exp(jnp.where(lower, seg, 0.0)), 0.0)
                y_h = jnp.dot((cb * l_h).astype(BF16), xdt_b, preferred_element_type=F32)
                y_g = jnp.where(slab == r, y_h, y_g)
            prev = state_ref[g]
            y_off = jnp.dot(c_g, prev.astype(BF16), preferred_element_type=F32) * jnp.exp(acum_e[:, gl])
            new = jnp.dot(b_gt, (xdt_g * decay_e[:, gl]).astype(BF16), preferred_element_type=F32)
            state_ref[g] = prev * jnp.exp(alast_e[:, gl]) + new
            y_parts.append(y_g + y_off)
        y = jnp.concatenate(y_parts, axis=1) + dskip_ref[...] * xs
        yz = y * _silu(z_ref[0, pl.ds(start, q), :])
        outs = []
        for g in range(SSD_GROUPS):
            yg = yz[:, g * GROUP_PAD:(g + 1) * GROUP_PAD]
            ms = jnp.sum(yg * yg, axis=-1, keepdims=True) * (1.0 / GROUP_W)
            outs.append(yg * lax.rsqrt(ms + LN_EPS))
        o_ref[0, pl.ds(start, q), :] = jnp.concatenate(outs, axis=1) * ng_ref[...]
        return carry

    lax.fori_loop(0, seq // q, chunk, 0)


def _ssd(proj3, p):
    bsz, seq, _ = proj3.shape
    slab = lambda width, base: pl.BlockSpec((1, seq, width), lambda b: (b, 0, base // width))
    const2 = lambda a: pl.BlockSpec(a.shape, lambda b: (0, 0))
    small = [p["conv_w_xs"], p["conv_w_b"], p["conv_w_c"], p["conv_b_xs"], p["conv_b_b"], p["conv_b_c"],
             p["dt_bias"], p["a_log"], p["d_skip"], p["ssd_norm_g"]]
    return pl.pallas_call(
        functools.partial(_ssd_kernel, seq=seq),
        grid=(bsz,),
        in_specs=[slab(SSD_PAD, COL_Z), slab(SSD_PAD, COL_XS), slab(2 * SSD_STATE, COL_B),
                  slab(2 * SSD_STATE, COL_C), slab(LANES, COL_DT)] + [const2(a) for a in small],
        out_specs=pl.BlockSpec((1, seq, SSD_PAD), lambda b: (b, 0, 0)),
        out_shape=jax.ShapeDtypeStruct((bsz, seq, SSD_PAD), F32),
        scratch_shapes=[pltpu.VMEM((SSD_GROUPS, SSD_STATE, GROUP_PAD), F32)],
        compiler_params=_params(("parallel",)),
        name="ssd",
    )(proj3, proj3, proj3, proj3, proj3, *small)


GM_ROWS = 1024


def _gmlp_kernel(u_ref, v_ref, lng_ref, lnb_ref, ws_ref, bs_ref, ng_ref, o_ref):
    q = GM_CHUNK
    row_i = lax.broadcasted_iota(jnp.int32, (q, q), 0)
    col_i = lax.broadcasted_iota(jnp.int32, (q, q), 1)
    lower = row_i >= col_i
    grp = lax.broadcasted_iota(jnp.int32, (1, GM_W), 1) // (GM_W // GM_GROUPS)
    w = [jnp.where(lower, ws_ref[g], 0.0).astype(BF16) for g in range(GM_GROUPS)]
    for c in range(GM_ROWS // q):
        rows = slice(c * q, (c + 1) * q)
        u = _gelu_tanh(u_ref[rows, :])
        v = _layer_norm(_gelu_tanh(v_ref[rows, :]), lng_ref[...], lnb_ref[...])
        vb = v.astype(BF16)
        mixed = jnp.zeros((q, GM_W), F32)
        for g in range(GM_GROUPS):
            mixed = jnp.where(grp == g, jnp.dot(w[g], vb, preferred_element_type=F32), mixed)
        gm = u * (mixed + bs_ref[...])
        ms = jnp.mean(gm * gm, axis=-1, keepdims=True)
        o_ref[rows, :] = gm * lax.rsqrt(ms + LN_EPS) * ng_ref[...]


def _gmlp(proj, p):
    m = proj.shape[0]
    const = lambda a: pl.BlockSpec(a.shape, lambda i: (0,) * a.ndim)
    small = [p["gm_ln_g"], p["gm_ln_b"], p["gm_w_s"], p["gm_b_s"], p["gm_norm_g"]]
    return pl.pallas_call(
        _gmlp_kernel,
        grid=(m // GM_ROWS,),
        in_specs=[pl.BlockSpec((GM_ROWS, GM_W), lambda i: (i, COL_GU // GM_W)),
                  pl.BlockSpec((GM_ROWS, GM_W), lambda i: (i, COL_GV // GM_W))] + [const(a) for a in small],
        out_specs=pl.BlockSpec((GM_ROWS, GM_W), lambda i: (i, 0)),
        out_shape=jax.ShapeDtypeStruct((m, GM_W), F32),
        compiler_params=_params(("parallel",)),
        name="gmlp",
    )(proj, proj, *small)


OUT_TM = 512
LOGIT_ROWS = 16


def _out_proj_kernel(att_ref, ssd_ref, gm_ref, h_ref, ag_ref, wa_ref, ws_ref, wg_ref, g_ref, b_ref, o_ref,
                     router=None):
    att = att_ref[...]
    ms = jnp.mean(att * att, axis=-1, keepdims=True)
    att = att * lax.rsqrt(ms + LN_EPS) * ag_ref[...]
    mix = jnp.dot(att.astype(BF16), wa_ref[...], preferred_element_type=F32)
    mix = mix + jnp.dot(ssd_ref[...].astype(BF16), ws_ref[...], preferred_element_type=F32)
    mix = mix + jnp.dot(gm_ref[...].astype(BF16), wg_ref[...], preferred_element_type=F32)
    h1 = _layer_norm(ALPHA * h_ref[...] + mix, g_ref[...], b_ref[...])
    o_ref[...] = h1
    if router:
        wr_ref, logit_ref = router
        logit_ref[...] = _dot_bf16x3(wr_ref[...], h1.T)


def _out_proj_kernel_routed(att_ref, ssd_ref, gm_ref, h_ref, ag_ref, wa_ref, ws_ref, wg_ref, g_ref, b_ref, wr_ref,
                            o_ref, logit_ref):
    _out_proj_kernel(att_ref, ssd_ref, gm_ref, h_ref, ag_ref, wa_ref, ws_ref, wg_ref, g_ref, b_ref, o_ref,
                     router=(wr_ref, logit_ref))


def _out_proj(att, ssd, gm, h, p, w_router=None):
    m = h.shape[0]
    row = lambda i: (i, 0)
    const = lambda a: pl.BlockSpec(a.shape, lambda i: (0, 0))
    small = [p["att_norm_g"], p["w_out_att"], p["w_out_ssd"], p["w_out_gm"], p["ln1_g"], p["ln1_b"]]
    out_specs = [pl.BlockSpec((OUT_TM, D_MODEL), row)]
    out_shape = [jax.ShapeDtypeStruct((m, D_MODEL), F32)]
    body = _out_proj_kernel
    if w_router is not None:
        small.append(jnp.pad(w_router.T, ((0, LOGIT_ROWS - N_EXPERTS), (0, 0))))
        out_specs.append(pl.BlockSpec((LOGIT_ROWS, OUT_TM), lambda i: (0, i)))
        out_shape.append(jax.ShapeDtypeStruct((LOGIT_ROWS, m), F32))
        body = _out_proj_kernel_routed
    return pl.pallas_call(
        body,
        grid=(m // OUT_TM,),
        in_specs=[pl.BlockSpec((OUT_TM, ATT_W), row), pl.BlockSpec((OUT_TM, SSD_PAD), row),
                  pl.BlockSpec((OUT_TM, GM_W), row), pl.BlockSpec((OUT_TM, D_MODEL), row)]
                 + [const(a) for a in small],
        out_specs=out_specs,
        out_shape=out_shape,
        compiler_params=_params(("parallel",)),
        name="out_proj_ln1",
    )(att, ssd, gm, h, *small)


FFN_TM = 512


def _ffn_kernel(x_ref, gate_ref, wg_ref, wu_ref, wd_ref, g_ref, b_ref, o_ref, acc_ref, *, gated):
    e = pl.program_id(1)
    f = pl.program_id(2)

    @pl.when((e == 0) & (f == 0))
    def _():
        acc_ref[...] = jnp.zeros_like(acc_ref)

    x = x_ref[...]
    xb = x.astype(BF16)
    hg = jnp.dot(xb, wg_ref[0], preferred_element_type=F32)
    hu = jnp.dot(xb, wu_ref[0], preferred_element_type=F32)
    hid = _silu(hg) * hu
    if gated:
        hid = hid * gate_ref[0]
    acc_ref[...] += jnp.dot(hid.astype(BF16), wd_ref[0], preferred_element_type=F32)

    @pl.when((e == pl.num_programs(1) - 1) & (f == pl.num_programs(2) - 1))
    def _():
        o_ref[...] = _layer_norm(ALPHA * x + acc_ref[...], g_ref[...], b_ref[...])


def _ffn(x, gates, wg, wu, wd, ln_g, ln_b, tf):
    m = x.shape[0]
    n_e, _, d_ff = wg.shape
    gated = gates is not None
    if not gated:
        gates = jnp.ones((1, m, 1), F32)
    return pl.pallas_call(
        functools.partial(_ffn_kernel, gated=gated),
        grid=(m // FFN_TM, n_e, d_ff // tf),
        in_specs=[pl.BlockSpec((FFN_TM, D_MODEL), lambda i, e, f: (i, 0)),
                  pl.BlockSpec((1, FFN_TM, 1), lambda i, e, f: (e, i, 0)),
                  pl.BlockSpec((1, D_MODEL, tf), lambda i, e, f: (e, 0, f)),
                  pl.BlockSpec((1, D_MODEL, tf), lambda i, e, f: (e, 0, f)),
                  pl.BlockSpec((1, tf, D_MODEL), lambda i, e, f: (e, f, 0)),
                  pl.BlockSpec((1, D_MODEL), lambda i, e, f: (0, 0)),
                  pl.BlockSpec((1, D_MODEL), lambda i, e, f: (0, 0))],
        out_specs=pl.BlockSpec((FFN_TM, D_MODEL), lambda i, e, f: (i, 0)),
        out_shape=jax.ShapeDtypeStruct((m, D_MODEL), F32),
        scratch_shapes=[pltpu.VMEM((FFN_TM, D_MODEL), F32)],
        compiler_params=_params(("parallel", "arbitrary", "arbitrary")),
        name="ffn_ln2",
    )(x, gates, wg, wu, wd, ln_g.reshape(1, D_MODEL), ln_b.reshape(1, D_MODEL))


MOE_TR = 256
MOE_TM = 1024
MOE_TF = 512
SEG_ALIGN = 16
INFO_GATE = N_EXPERTS


def _expert_cap(m):
    n_tiles = m // MOE_TR
    rows = m + (SEG_ALIGN - 1) * n_tiles + MOE_TR + MOE_TM
    return -(-rows // MOE_TM) * MOE_TM


def _dispatch_kernel(h_ref, logit_ref, info_ref, offs_ref, lens_ref, xs_hbm, run_ref, xbuf_ref, zbuf_ref,
                     sem_ref, zsem_ref, *, n_tiles, cap):
    t = pl.program_id(0)
    tr = MOE_TR
    slot = t % 2

    @pl.when(t == 0)
    def _():
        for e in range(N_EXPERTS):
            run_ref[e] = 0

    xb = h_ref[...].astype(BF16)
    lg = logit_ref[0:N_EXPERTS, :]
    row = lax.broadcasted_iota(jnp.int32, (N_EXPERTS, tr), 0)
    m1 = jnp.max(lg, axis=0, keepdims=True)
    i1 = jnp.min(jnp.where(lg == m1, row, N_EXPERTS), axis=0, keepdims=True)
    rest = jnp.where(row == i1, -jnp.inf, lg)
    m2 = jnp.max(rest, axis=0, keepdims=True)
    i2 = jnp.min(jnp.where(rest == m2, row, N_EXPERTS), axis=0, keepdims=True)
    e2 = jnp.exp(m2 - m1)
    p1 = 1.0 / (1.0 + e2)
    gate_t = jnp.where(row == i1, p1, 0.0) + jnp.where(row == i2, e2 * p1, 0.0)
    sel_t = jnp.where((row == i1) | (row == i2), 1.0, 0.0)
    before = (lax.broadcasted_iota(jnp.int32, (tr, tr), 0) < lax.broadcasted_iota(jnp.int32, (tr, tr), 1))
    rank_t = jnp.dot(sel_t.astype(BF16), jnp.where(before, 1.0, 0.0).astype(BF16),
                     preferred_element_type=F32)
    cnt = jnp.sum(sel_t, axis=1, keepdims=True).astype(jnp.int32)
    info_t = jnp.concatenate([rank_t, gate_t, jnp.zeros((LANES - 2 * N_EXPERTS, tr), F32)], axis=0)
    info_ref[...] = info_t.T

    def seg_copy(sl, e, off):
        return pltpu.make_async_copy(xbuf_ref.at[sl, e], xs_hbm.at[pl.ds(pl.multiple_of(off, SEG_ALIGN), tr)],
                                     sem_ref.at[sl, e])

    dst_row = lax.broadcasted_iota(jnp.int32, (tr, tr), 0).astype(F32)
    onehots = [jnp.where((rank_t[e:e + 1, :] == dst_row) & (sel_t[e:e + 1, :] > 0.0), 1.0, 0.0).astype(BF16)
               for e in range(N_EXPERTS)]
    onehot_all = jnp.concatenate(onehots, axis=0)
    half = D_MODEL // 2
    for c in range(2):
        packed = jnp.dot(onehot_all, xb[:, c * half:(c + 1) * half], preferred_element_type=F32)
        xbuf_ref[slot, :, :, c * half:(c + 1) * half] = packed.astype(BF16).reshape(N_EXPERTS, tr, half)

    @pl.when(t > 0)
    def _():
        for e in range(N_EXPERTS):
            seg_copy(1 - slot, e, 0).wait()

    for e in range(N_EXPERTS):
        off = e * cap + run_ref[e]
        offs_ref[t * N_EXPERTS + e] = off
        seg_copy(slot, e, off).start()
        run_ref[e] = run_ref[e] + jnp.bitwise_and(cnt[e, 0] + (SEG_ALIGN - 1), -SEG_ALIGN)

    @pl.when(t == n_tiles - 1)
    def _():
        zbuf_ref[...] = jnp.zeros_like(zbuf_ref)
        for e in range(N_EXPERTS):
            seg_copy(slot, e, 0).wait()
        tails = []
        for e in range(N_EXPERTS):
            lens_ref[e] = run_ref[e]
            tail = pl.multiple_of(e * cap + run_ref[e], SEG_ALIGN)
            tails.append(pltpu.make_async_copy(zbuf_ref, xs_hbm.at[pl.ds(tail, MOE_TR + MOE_TM)], zsem_ref.at[e]))
            tails[-1].start()
        for cp in tails:
            cp.wait()


def _dispatch(h, logits):
    m = h.shape[0]
    n_tiles = m // MOE_TR
    cap = _expert_cap(m)
    smem = pl.BlockSpec(memory_space=pltpu.SMEM)
    return pl.pallas_call(
        functools.partial(_dispatch_kernel, n_tiles=n_tiles, cap=cap),
        grid=(n_tiles,),
        in_specs=[pl.BlockSpec((MOE_TR, D_MODEL), lambda t: (t, 0)),
                  pl.BlockSpec((LOGIT_ROWS, MOE_TR), lambda t: (0, t))],
        out_specs=[pl.BlockSpec((MOE_TR, LANES), lambda t: (t, 0)), smem, smem,
                   pl.BlockSpec(memory_space=pl.ANY)],
        out_shape=[jax.ShapeDtypeStruct((m, LANES), F32),
                   jax.ShapeDtypeStruct((n_tiles * N_EXPERTS,), jnp.int32),
                   jax.ShapeDtypeStruct((N_EXPERTS,), jnp.int32),
                   jax.ShapeDtypeStruct((N_EXPERTS * cap, D_MODEL), BF16)],
        scratch_shapes=[pltpu.SMEM((N_EXPERTS,), jnp.int32),
                        pltpu.VMEM((2, N_EXPERTS, MOE_TR, D_MODEL), BF16),
                        pltpu.VMEM((MOE_TR + MOE_TM, D_MODEL), BF16),
                        pltpu.SemaphoreType.DMA((2, N_EXPERTS)),
                        pltpu.SemaphoreType.DMA((N_EXPERTS,))],
        compiler_params=_params(("arbitrary",)),
        name="moe_dispatch",
    )(h, logits)


def _expert_ffn_kernel(exp_ref, blk_ref, valid_ref, x_ref, wg_ref, wu_ref, wd_ref, y_ref, acc_ref):
    w = pl.program_id(0)
    f = pl.program_id(1)

    @pl.when(valid_ref[w] == 1)
    def _():
        @pl.when(f == 0)
        def _():
            acc_ref[...] = jnp.zeros_like(acc_ref)

        xb = x_ref[...]
        hg = jnp.dot(xb, wg_ref[0].astype(BF16), preferred_element_type=F32)
        hu = jnp.dot(xb, wu_ref[0].astype(BF16), preferred_element_type=F32)
        hid = (_silu(hg) * hu).astype(BF16)
        acc_ref[...] += jnp.dot(hid, wd_ref[0].astype(BF16), preferred_element_type=F32)

        @pl.when(f == pl.num_programs(1) - 1)
        def _():
            y_ref[...] = acc_ref[...].astype(BF16)


def _expert_ffn(xs, exp_w, blk_w, valid_w, w_gate, w_up, w_down):
    d_ff = w_gate.shape[-1]
    n_f = d_ff // MOE_TF
    fcol = lambda w, f, exp, blk, valid: jnp.where(valid[w] == 1, f, n_f - 1)
    grid_spec = pltpu.PrefetchScalarGridSpec(
        num_scalar_prefetch=3,
        grid=(exp_w.shape[0], n_f),
        in_specs=[pl.BlockSpec((MOE_TM, D_MODEL), lambda w, f, exp, blk, valid: (blk[w], 0)),
                  pl.BlockSpec((1, D_MODEL, MOE_TF), lambda w, f, exp, blk, valid: (exp[w], 0, fcol(w, f, exp, blk, valid))),
                  pl.BlockSpec((1, D_MODEL, MOE_TF), lambda w, f, exp, blk, valid: (exp[w], 0, fcol(w, f, exp, blk, valid))),
                  pl.BlockSpec((1, MOE_TF, D_MODEL), lambda w, f, exp, blk, valid: (exp[w], fcol(w, f, exp, blk, valid), 0))],
        out_specs=pl.BlockSpec((MOE_TM, D_MODEL), lambda w, f, exp, blk, valid: (blk[w], 0)),
        scratch_shapes=[pltpu.VMEM((MOE_TM, D_MODEL), F32)])
    return pl.pallas_call(
        _expert_ffn_kernel,
        grid_spec=grid_spec,
        out_shape=jax.ShapeDtypeStruct(xs.shape, BF16),
        compiler_params=_params(("arbitrary", "arbitrary")),
        name="moe_expert_ffn",
    )(exp_w, blk_w, valid_w, xs, w_gate, w_up, w_down)


def _combine_kernel(offs_ref, info_ref, h_ref, g_ref, b_ref, ys_hbm, o_ref, ybuf_ref, sem_ref, *, n_tiles):
    t = pl.program_id(0)
    tr = MOE_TR
    slot = t % 2

    def seg_copy(tile, sl, e):
        off = pl.multiple_of(offs_ref[tile * N_EXPERTS + e], SEG_ALIGN)
        return pltpu.make_async_copy(ys_hbm.at[pl.ds(off, tr)], ybuf_ref.at[sl, e], sem_ref.at[sl, e])

    @pl.when(t == 0)
    def _():
        for e in range(N_EXPERTS):
            seg_copy(0, 0, e).start()

    @pl.when(t + 1 < n_tiles)
    def _():
        for e in range(N_EXPERTS):
            seg_copy(t + 1, 1 - slot, e).start()

    info = info_ref[...]
    src_row = lax.broadcasted_iota(jnp.int32, (tr, tr), 1).astype(F32)
    acc = jnp.zeros((tr, D_MODEL), F32)
    for e in range(N_EXPERTS):
        seg_copy(t, slot, e).wait()
    for e in range(N_EXPERTS):
        rank_c = info[:, e:e + 1]
        gate_c = info[:, INFO_GATE + e:INFO_GATE + e + 1]
        onehot = jnp.where((rank_c == src_row) & (gate_c != 0.0), 1.0, 0.0).astype(BF16)
        acc = acc + gate_c * jnp.dot(onehot, ybuf_ref[slot, e], preferred_element_type=F32)
    o_ref[...] = _layer_norm(ALPHA * h_ref[...] + acc, g_ref[...], b_ref[...])


def _combine(offs, info, h, ys, ln_g, ln_b):
    m = h.shape[0]
    n_tiles = m // MOE_TR
    grid_spec = pltpu.PrefetchScalarGridSpec(
        num_scalar_prefetch=1,
        grid=(n_tiles,),
        in_specs=[pl.BlockSpec((MOE_TR, LANES), lambda t, offs: (t, 0)),
                  pl.BlockSpec((MOE_TR, D_MODEL), lambda t, offs: (t, 0)),
                  pl.BlockSpec((1, D_MODEL), lambda t, offs: (0, 0)),
                  pl.BlockSpec((1, D_MODEL), lambda t, offs: (0, 0)),
                  pl.BlockSpec(memory_space=pl.ANY)],
        out_specs=pl.BlockSpec((MOE_TR, D_MODEL), lambda t, offs: (t, 0)),
        scratch_shapes=[pltpu.VMEM((2, N_EXPERTS, MOE_TR, D_MODEL), BF16),
                        pltpu.SemaphoreType.DMA((2, N_EXPERTS))])
    return pl.pallas_call(
        functools.partial(_combine_kernel, n_tiles=n_tiles),
        grid_spec=grid_spec,
        out_shape=jax.ShapeDtypeStruct((m, D_MODEL), F32),
        compiler_params=_params(("arbitrary",)),
        name="moe_combine_ln2",
    )(offs, info, h, ln_g.reshape(1, D_MODEL), ln_b.reshape(1, D_MODEL), ys)


def _work_list(lens, m):
    cap = _expert_cap(m)
    n_tiles = m // MOE_TR
    max_rows = 2 * m + (SEG_ALIGN - 1) * min(N_EXPERTS * n_tiles, 2 * m) + N_EXPERTS * MOE_TR
    w_max = -(-max_rows // MOE_TM) + N_EXPERTS
    tiles_e = (lens + MOE_TR + MOE_TM - 1) // MOE_TM
    ends = jnp.cumsum(tiles_e)
    w = jnp.arange(w_max, dtype=jnp.int32)
    wc = jnp.minimum(w, ends[-1] - 1)
    exp_w = jnp.sum((wc[:, None] >= ends[None, :]).astype(jnp.int32), axis=1)
    blk_w = exp_w * (cap // MOE_TM) + wc - (ends - tiles_e)[exp_w]
    return exp_w.astype(jnp.int32), blk_w.astype(jnp.int32), (w < ends[-1]).astype(jnp.int32)


def _moe_layer(h, logits, w_gate, w_up, w_down, ln_g, ln_b):
    m = h.shape[0]
    info, offs, lens, xs = _dispatch(h, logits)
    exp_w, blk_w, valid_w = _work_list(lens, m)
    ys = _expert_ffn(xs, exp_w, blk_w, valid_w, w_gate, w_up, w_down)
    return _combine(offs, info, h, ys, ln_g, ln_b)


def _pad_groups(a):
    lead = a.shape[:-1]
    a = a.reshape(*lead, SSD_GROUPS, GROUP_W)
    a = jnp.pad(a, [(0, 0)] * len(lead) + [(0, 0), (0, GROUP_PAD - GROUP_W)])
    return a.reshape(*lead, SSD_PAD)


def _head_lanes(v):
    return jnp.pad(v, (0, LANES - SSD_HEADS)).reshape(1, LANES)


D_IN = 3 * ATT_W + SSD_W + (SSD_W + 2 * SSD_GROUPS * SSD_STATE) + SSD_HEADS + 2 * GM_W
PERM_ROWS = 256


def _permute_w_kernel(w_ref, o_ref):
    src_z = 3 * ATT_W
    src_xs = src_z + SSD_W
    src_b = src_xs + SSD_W
    src_c = src_b + SSD_GROUPS * SSD_STATE
    src_dt = src_c + SSD_GROUPS * SSD_STATE
    src_gu = src_dt + SSD_HEADS
    src_gv = src_gu + GM_W
    o_ref[...] = jnp.zeros_like(o_ref)

    def put(dst, src, width):
        o_ref[:, dst:dst + width] = w_ref[0, :, src:src + width].astype(BF16)

    for g in range(SSD_GROUPS):
        put(COL_Z + g * GROUP_PAD, src_z + g * GROUP_W, GROUP_W)
        put(COL_XS + g * GROUP_PAD, src_xs + g * GROUP_W, GROUP_W)
    put(COL_B, src_b, SSD_GROUPS * SSD_STATE)
    put(COL_C, src_c, SSD_GROUPS * SSD_STATE)
    put(COL_GU, src_gu, GM_W)
    put(COL_GV, src_gv, GM_W)
    put(COL_Q, 0, ATT_W)
    put(COL_K, ATT_W, ATT_W)
    put(COL_V, 2 * ATT_W, ATT_W)
    put(COL_DT, src_dt, SSD_HEADS)


def _permute_w_in(w_in, layer):
    return pl.pallas_call(
        _permute_w_kernel,
        grid=(D_MODEL // PERM_ROWS,),
        in_specs=[pl.BlockSpec((1, PERM_ROWS, D_IN), lambda i: (layer, i, 0))],
        out_specs=pl.BlockSpec((PERM_ROWS, D_PROJ), lambda i: (i, 0)),
        out_shape=jax.ShapeDtypeStruct((D_MODEL, D_PROJ), BF16),
        compiler_params=_params(("parallel",)),
        name="permute_w_in",
    )(w_in)


def _layer_params(i, w_in, conv_w, conv_b, dt_bias, a_log, d_skip, ssd_norm_g, att_norm_g, gm_ln_g, gm_ln_b,
                  gm_w_s, gm_b_s, gm_norm_g, w_out, ln1_g, ln1_b):
    cw, cb = conv_w[i], conv_b[i]
    nbc = SSD_GROUPS * SSD_STATE
    wo = w_out[i]
    return {
        "w_in": _permute_w_in(w_in, i),
        "conv_w_xs": _pad_groups(cw[:, :SSD_W]),
        "conv_w_b": cw[:, SSD_W:SSD_W + nbc],
        "conv_w_c": cw[:, SSD_W + nbc:],
        "conv_b_xs": _pad_groups(cb[:SSD_W]).reshape(1, SSD_PAD),
        "conv_b_b": cb[SSD_W:SSD_W + nbc].reshape(1, nbc),
        "conv_b_c": cb[SSD_W + nbc:].reshape(1, nbc),
        "dt_bias": _head_lanes(dt_bias[i]),
        "a_log": _head_lanes(a_log[i]),
        "d_skip": _pad_groups(jnp.repeat(d_skip[i], SSD_HEAD_DIM)).reshape(1, SSD_PAD),
        "ssd_norm_g": _pad_groups(ssd_norm_g[i]).reshape(1, SSD_PAD),
        "att_norm_g": att_norm_g[i].reshape(1, ATT_W),
        "gm_ln_g": gm_ln_g[i].reshape(1, GM_W),
        "gm_ln_b": gm_ln_b[i].reshape(1, GM_W),
        "gm_w_s": gm_w_s[i],
        "gm_b_s": jnp.repeat(gm_b_s[i].T, GM_W // GM_GROUPS, axis=1),
        "gm_norm_g": gm_norm_g[i].reshape(1, GM_W),
        "w_out_att": wo[:ATT_W].astype(BF16),
        "w_out_ssd": _pad_groups(wo[ATT_W:ATT_W + SSD_W].T).T.astype(BF16),
        "w_out_gm": wo[ATT_W + SSD_W:].astype(BF16),
        "ln1_g": ln1_g[i].reshape(1, D_MODEL),
        "ln1_b": ln1_b[i].reshape(1, D_MODEL),
    }


def _rope_tables(positions):
    inv = ROPE_THETA ** (-jnp.arange(0, ROPE_DIM, 2, dtype=F32) / ROPE_DIM)
    d = jnp.arange(LANES) % HEAD_DIM
    inv_lane = jnp.where(d < ROPE_DIM, inv[d % ROPE_HALF], 0.0)
    ang = positions.astype(F32)[..., None] * inv_lane
    cos, sin = jnp.cos(ang), jnp.sin(ang)
    return cos, jnp.where(d < ROPE_HALF, -sin, 0.0), jnp.where(d >= ROPE_HALF, sin, 0.0)


def kernel(x, positions, ln_in_g, ln_in_b, w_in, conv_w, conv_b, dt_bias, a_log, d_skip, ssd_norm_g, att_norm_g, gm_ln_g, gm_ln_b, gm_w_s, gm_b_s, gm_norm_g, w_out, ln1_g, ln1_b, ln2_g, ln2_b, ffn_w_gate, ffn_w_up, ffn_w_down, router_w, moe_w_gate, moe_w_up, moe_w_down):
    bsz, seq, _ = x.shape
    m = bsz * seq
    rope_c, rope_sa, rope_sb = _rope_tables(positions)
    h = x.reshape(m, D_MODEL)
    for i in range(DEPTH):
        p = _layer_params(i, w_in, conv_w, conv_b, dt_bias, a_log, d_skip, ssd_norm_g, att_norm_g, gm_ln_g,
                          gm_ln_b, gm_w_s, gm_b_s, gm_norm_g, w_out, ln1_g, ln1_b)
        if i == 0:
            proj, h = _in_proj(h, p["w_in"], ln=(ln_in_g, ln_in_b))
        else:
            (proj,) = _in_proj(h, p["w_in"])
        proj3 = proj.reshape(bsz, seq, D_PROJ)
        att = _attention(proj3, rope_c, rope_sa, rope_sb).reshape(m, ATT_W)
        ssd = _ssd(proj3, p).reshape(m, SSD_PAD)
        gm = _gmlp(proj, p)
        j = i // 2
        if i % 2 == 0:
            (h,) = _out_proj(att, ssd, gm, h, p)
            h = _ffn(h, None, ffn_w_gate[j][None].astype(BF16), ffn_w_up[j][None].astype(BF16),
                     ffn_w_down[j][None].astype(BF16), ln2_g[i], ln2_b[i], tf=1408)
        else:
            h, logits = _out_proj(att, ssd, gm, h, p, w_router=router_w[j])
            h = _moe_layer(h, logits, moe_w_gate[j], moe_w_up[j], moe_w_down[j], ln2_g[i], ln2_b[i])
    return h.reshape(bsz, seq, D_MODEL)
```

```python
import functools

import jax
import jax.numpy as jnp
from jax import lax
from jax.experimental import pallas as pl
from jax.experimental.pallas import tpu as pltpu

F32 = jnp.float32
BF16 = jnp.bfloat16
HIGHEST = lax.Precision.HIGHEST

D_MODEL = 1024
ATT_HEADS = 6
HEAD_DIM = 64
ATT_W = ATT_HEADS * HEAD_DIM
ROPE_DIM = HEAD_DIM // 4
ROPE_HALF = ROPE_DIM // 2
ROPE_THETA = 500000.0
MOBA_BLOCK = 256
MOBA_TOPK = 3
SSD_HEADS = 6
SSD_HEAD_DIM = 64
SSD_W = SSD_HEADS * SSD_HEAD_DIM
SSD_GROUPS = 2
SSD_STATE = 128
SSD_CONV = 4
SSD_CHUNK = 128
GM_GROUPS = 4
GM_W = 256
GM_CHUNK = 128
N_EXPERTS = 8
DEPTH = 2
ALPHA = (2.0 * DEPTH) ** 0.25
NEG = -1e30
LN_EPS = 1e-5

LANES = 128
GROUP_W = SSD_W // SSD_GROUPS
GROUP_PAD = 256
SSD_PAD = SSD_GROUPS * GROUP_PAD
HEADS_PER_GROUP = SSD_HEADS // SSD_GROUPS

COL_Z = 0
COL_XS = 512
COL_B = 1024
COL_C = 1280
COL_GU = 1536
COL_GV = 1792
COL_Q = 2048
COL_K = 2432
COL_V = 2816
COL_DT = 3200
D_PROJ = 3328

VMEM_LIMIT = 56 * 1024 * 1024


def _params(sem, vmem=VMEM_LIMIT):
    return pltpu.CompilerParams(dimension_semantics=sem, vmem_limit_bytes=vmem)


def _layer_norm(x, g, b):
    mu = jnp.mean(x, axis=-1, keepdims=True)
    xc = x - mu
    var = jnp.mean(xc * xc, axis=-1, keepdims=True)
    return xc * lax.rsqrt(var + LN_EPS) * g + b


def _silu(x):
    half = 0.5 * x
    return half + half * jnp.tanh(half)


def _split_bf16(x):
    hi = x.astype(BF16)
    return hi, (x - hi.astype(F32)).astype(BF16)


def _dot_bf16x3(a, b):
    a_hi, a_lo = _split_bf16(a)
    b_hi, b_lo = _split_bf16(b)
    dot = functools.partial(jnp.dot, preferred_element_type=F32)
    return dot(a_hi, b_hi) + (dot(a_lo, b_hi) + dot(a_hi, b_lo))


def _gelu_tanh(x):
    return 0.5 * x * (1.0 + jnp.tanh(0.7978845608028654 * (x + 0.044715 * x * x * x)))


PROJ_TM = 512
PROJ_CH = 256


def _in_proj_kernel(*refs, apply_ln):
    if apply_ln:
        x_ref, g_ref, b_ref, w_ref, proj_ref, h_ref = refs
        x = _layer_norm(x_ref[...], g_ref[...], b_ref[...])
        h_ref[...] = x
    else:
        x_ref, w_ref, proj_ref = refs
        x = x_ref[...]
    xb = x.astype(BF16)
    for j in range(D_PROJ // PROJ_CH):
        cols = slice(j * PROJ_CH, (j + 1) * PROJ_CH)
        proj_ref[:, cols] = jnp.dot(xb, w_ref[:, cols], preferred_element_type=F32)


def _in_proj(x, w, ln=None):
    m = x.shape[0]
    row = lambda i: (i, 0)
    const = lambda i: (0, 0)
    in_specs = [pl.BlockSpec((PROJ_TM, D_MODEL), row)]
    args = [x]
    out_shape = [jax.ShapeDtypeStruct((m, D_PROJ), F32)]
    out_specs = [pl.BlockSpec((PROJ_TM, D_PROJ), row)]
    if ln is not None:
        in_specs += [pl.BlockSpec((1, D_MODEL), const), pl.BlockSpec((1, D_MODEL), const)]
        args += [ln[0].reshape(1, D_MODEL), ln[1].reshape(1, D_MODEL)]
        out_shape.append(jax.ShapeDtypeStruct((m, D_MODEL), F32))
        out_specs.append(pl.BlockSpec((PROJ_TM, D_MODEL), row))
    in_specs.append(pl.BlockSpec((D_MODEL, D_PROJ), const))
    args.append(w)
    return pl.pallas_call(
        functools.partial(_in_proj_kernel, apply_ln=ln is not None),
        grid=(m // PROJ_TM,),
        in_specs=in_specs,
        out_specs=out_specs,
        out_shape=out_shape,
        compiler_params=_params(("parallel",)),
        name="in_proj",
    )(*args)


VT_ROWS = HEAD_DIM + 16
LOG2E = 1.4426950408889634
ATT_VALUE_LAG = 3


def _attn_kernel(q_ref, k_ref, v_ref, c_ref, sa_ref, sb_ref, o_ref, kb_ref, vt_ref, *, seq):
    nb = seq // MOBA_BLOCK
    tq = MOBA_BLOCK
    lane = lax.broadcasted_iota(jnp.int32, (1, LANES), 1)

    def rope(x, rows):
        return (x * c_ref[0, rows, :] + pltpu.roll(x, LANES - ROPE_HALF, 1) * sa_ref[0, rows, :]
                + pltpu.roll(x, ROPE_HALF, 1) * sb_ref[0, rows, :])

    k = rope(k_ref[0], slice(None))
    kb_ref[...] = k.astype(BF16)
    k_mean = jnp.mean(k.reshape(nb, MOBA_BLOCK, LANES), axis=1)
    v_t = v_ref[0].T
    ones_rows = jnp.where(lax.broadcasted_iota(jnp.int32, (VT_ROWS - HEAD_DIM, seq), 0) == 0, 1.0, 0.0)
    for hh in range(2):
        vt_ref[hh] = jnp.concatenate([v_t[hh * HEAD_DIM:(hh + 1) * HEAD_DIM], ones_rows], axis=0).astype(BF16)

    qscale = HEAD_DIM ** -0.5 * LOG2E
    causal_t = (lax.broadcasted_iota(jnp.int32, (tq, tq), 0) <= lax.broadcasted_iota(jnp.int32, (tq, tq), 1))
    blk_row = lax.broadcasted_iota(jnp.int32, (nb, tq), 0)

    q_tiles = {}

    def scores(j, hh):
        if hh == 0:
            rows = slice(j * tq, (j + 1) * tq)
            q_tiles[j] = rope(q_ref[0, rows, :], rows) * qscale
        q = q_tiles[j] if hh == 0 else q_tiles.pop(j)
        head_lanes = (lane < HEAD_DIM) if hh == 0 else (lane >= HEAD_DIM)
        qh_t = jnp.where(head_lanes, q, 0.0).T
        s_t = jnp.dot(kb_ref[0:(j + 1) * tq, :], qh_t.astype(BF16), preferred_element_type=F32)
        gate_t = _dot_bf16x3(k_mean, qh_t) if j > MOBA_TOPK else None
        return s_t, gate_t

    def probs(j, s_t, gate_t):
        n_past = j * tq
        own = jnp.where(causal_t, s_t[n_past:], NEG)
        mx = jnp.max(own, axis=0, keepdims=True)
        shifts = []
        if j > 0:
            blk_max = [jnp.max(s_t[n * tq:(n + 1) * tq], axis=0, keepdims=True) for n in range(j)]
            if j > MOBA_TOPK:
                rank = jnp.zeros((nb, tq), F32)
                for m in range(j):
                    g_m = gate_t[m:m + 1, :]
                    beats = (g_m > gate_t) | ((g_m == gate_t) & (blk_row > m))
                    rank = rank + jnp.where(beats & (blk_row != m), 1.0, 0.0)
                bias = jnp.where((blk_row < j) & (rank >= MOBA_TOPK), NEG, 0.0)
                bias_n = [bias[n:n + 1, :] for n in range(j)]
            else:
                bias_n = [0.0] * j
            for n in range(j):
                mx = jnp.maximum(mx, blk_max[n] + bias_n[n])
            shifts = [mx - bias_n[n] for n in range(j)]
        parts = [jnp.exp2(s_t[n * tq:(n + 1) * tq] - shifts[n]).astype(BF16) for n in range(j)]
        parts.append(jnp.exp2(own - mx).astype(BF16))
        return jnp.concatenate(parts, axis=0) if j > 0 else parts[0]

    def values(j, hh, p_t):
        o_t = jnp.dot(vt_ref[hh, :, 0:(j + 1) * tq], p_t, preferred_element_type=F32)
        return o_t[0:HEAD_DIM] / o_t[HEAD_DIM:HEAD_DIM + 1]

    chains = [(j, hh) for j in range(nb) for hh in range(2)]
    s_state = {}
    p_state = {}
    outs = {}
    for step in range(len(chains) + ATT_VALUE_LAG):
        if step >= ATT_VALUE_LAG:
            j, hh = chains[step - ATT_VALUE_LAG]
            outs[(j, hh)] = values(j, hh, p_state.pop((j, hh)))
            if hh == 1:
                o_ref[0, j * tq:(j + 1) * tq, :] = jnp.concatenate([outs.pop((j, 0)), outs.pop((j, 1))], axis=0).T
        if step < len(chains):
            j, hh = chains[step]
            s_state[(j, hh)] = scores(j, hh)
        if 1 <= step <= len(chains):
            j, hh = chains[step - 1]
            p_state[(j, hh)] = probs(j, *s_state.pop((j, hh)))


def _attention(proj3, rope_c, rope_sa, rope_sb):
    bsz, seq, _ = proj3.shape
    blk = (1, seq, LANES)
    col = lambda base: (lambda b, p: (b, 0, base // LANES + p))
    tab = lambda b, p: (b, 0, 0)
    return pl.pallas_call(
        functools.partial(_attn_kernel, seq=seq),
        grid=(bsz, ATT_W // LANES),
        in_specs=[pl.BlockSpec(blk, col(COL_Q)), pl.BlockSpec(blk, col(COL_K)), pl.BlockSpec(blk, col(COL_V)),
                  pl.BlockSpec(blk, tab), pl.BlockSpec(blk, tab), pl.BlockSpec(blk, tab)],
        out_specs=pl.BlockSpec(blk, lambda b, p: (b, 0, p)),
        out_shape=jax.ShapeDtypeStruct((bsz, seq, ATT_W), F32),
        scratch_shapes=[pltpu.VMEM((seq, LANES), BF16), pltpu.VMEM((2, VT_ROWS, seq), BF16)],
        compiler_params=_params(("parallel", "arbitrary")),
        name="moba_attention",
    )(proj3, proj3, proj3, rope_c, rope_sa, rope_sb)


assert HEADS_PER_GROUP == 3 and GROUP_PAD == 2 * LANES and SSD_HEAD_DIM * 2 == LANES


def _expand_heads(small):
    low = lax.broadcasted_iota(jnp.int32, (1, LANES), 1) < SSD_HEAD_DIM
    slabs = []
    for g in range(SSD_GROUPS):
        h0 = g * HEADS_PER_GROUP
        slabs.append(jnp.where(low, small[:, h0:h0 + 1], small[:, h0 + 1:h0 + 2]))
        slabs.append(jnp.where(low, small[:, h0 + 2:h0 + 3], 0.0))
    return jnp.concatenate(slabs, axis=1)


def _ssd_kernel(z_ref, xs_ref, b_ref, c_ref, dt_ref, cwx_ref, cwb_ref, cwc_ref, cbx_ref, cbb_ref, cbc_ref,
                dtb_ref, alog_ref, dskip_ref, ng_ref, o_ref, state_ref, *, seq):
    q = SSD_CHUNK
    n = SSD_STATE
    state_ref[...] = jnp.zeros_like(state_ref)
    row_i = lax.broadcasted_iota(jnp.int32, (q, q), 0)
    col_i = lax.broadcasted_iota(jnp.int32, (q, q), 1)
    lower = row_i >= col_i
    tril = jnp.where(lower, 1.0, 0.0)
    slab = lax.broadcasted_iota(jnp.int32, (1, GROUP_PAD), 1) // SSD_HEAD_DIM
    a_neg = -jnp.exp(alog_ref[...])

    def conv_silu(ref, w_ref, bias_ref, c):
        start = pl.multiple_of(c * q, q)
        cur = ref[0, pl.ds(start, q), :]
        prev_start = pl.multiple_of(jnp.maximum(c * q - 8, 0), 8)
        prev = jnp.where(c > 0, ref[0, pl.ds(prev_start, 8), :], 0.0)
        full = jnp.concatenate([prev, cur], axis=0)
        y = bias_ref[...] + w_ref[SSD_CONV - 1:SSD_CONV, :] * cur
        for i in range(SSD_CONV - 1):
            shift = SSD_CONV - 1 - i
            y = y + w_ref[i:i + 1, :] * pltpu.roll(full, shift, 0)[8:, :]
        return _silu(y)

    def chunk(c, carry):
        start = pl.multiple_of(c * q, q)
        xs = conv_silu(xs_ref, cwx_ref, cbx_ref, c)
        bm = conv_silu(b_ref, cwb_ref, cbb_ref, c)
        cm = conv_silu(c_ref, cwc_ref, cbc_ref, c)
        x = dt_ref[0, pl.ds(start, q), :] + dtb_ref[...]
        dt = jnp.maximum(x, 0.0) + jnp.log(1.0 + jnp.exp(-jnp.abs(x)))
        a = dt * a_neg
        a_cum = jnp.dot(tril, a, precision=HIGHEST, preferred_element_type=F32)
        a_cum_t = a_cum.T
        dt_e = _expand_heads(dt)
        acum_e = _expand_heads(a_cum)
        alast_e = acum_e[q - 1:q, :]
        xdt = xs * dt_e
        decay_e = jnp.exp(alast_e - acum_e)
        y_parts = []
        for g in range(SSD_GROUPS):
            gl = slice(g * GROUP_PAD, (g + 1) * GROUP_PAD)
            nl = slice(g * n, (g + 1) * n)
            b_g = bm[:, nl].astype(BF16)
            b_gt = bm[:, nl].T.astype(BF16)
            c_g = cm[:, nl].astype(BF16)
            cb = lax.dot_general(c_g, b_g, (((1,), (1,)), ((), ())), preferred_element_type=F32)
            xdt_g = xdt[:, gl]
            xdt_b = xdt_g.astype(BF16)
            y_g = jnp.zeros((q, GROUP_PAD), F32)
            for r in range(HEADS_PER_GROUP):
                h = g * HEADS_PER_GROUP + r
                seg = a_cum[:, h:h + 1] - a_cum_t[h:h + 1, :]
                l_h = jnp.where(lower, jnp.exp(jnp.where(lower, seg, 0.0)), 0.0)
                y_h = jnp.dot((cb * l_h).astype(BF16), xdt_b, preferred_element_type=F32)
                y_g = jnp.where(slab == r, y_h, y_g)
            prev = state_ref[g]
            y_off = jnp.dot(c_g, prev.astype(BF16), preferred_element_type=F32) * jnp.exp(acum_e[:, gl])
            new = jnp.dot(b_gt, (xdt_g * decay_e[:, gl]).astype(BF16), preferred_element_type=F32)
            state_ref[g] = prev * jnp.exp(alast_e[:, gl]) + new
            y_parts.append(y_g + y_off)
        y = jnp.concatenate(y_parts, axis=1) + dskip_ref[...] * xs
        yz = y * _silu(z_ref[0, pl.ds(start, q), :])
        outs = []
        for g in range(SSD_GROUPS):
            yg = yz[:, g * GROUP_PAD:(g + 1) * GROUP_PAD]
            ms = jnp.sum(yg * yg, axis=-1, keepdims=True) * (1.0 / GROUP_W)
            outs.append(yg * lax.rsqrt(ms + LN_EPS))
        o_ref[0, pl.ds(start, q), :] = jnp.concatenate(outs, axis=1) * ng_ref[...]
        return carry

    lax.fori_loop(0, seq // q, chunk, 0)


def _ssd(proj3, p):
    bsz, seq, _ = proj3.shape
    slab = lambda width, base: pl.BlockSpec((1, seq, width), lambda b: (b, 0, base // width))
    const2 = lambda a: pl.BlockSpec(a.shape, lambda b: (0, 0))
    small = [p["conv_w_xs"], p["conv_w_b"], p["conv_w_c"], p["conv_b_xs"], p["conv_b_b"], p["conv_b_c"],
             p["dt_bias"], p["a_log"], p["d_skip"], p["ssd_norm_g"]]
    return pl.pallas_call(
        functools.partial(_ssd_kernel, seq=seq),
        grid=(bsz,),
        in_specs=[slab(SSD_PAD, COL_Z), slab(SSD_PAD, COL_XS), slab(2 * SSD_STATE, COL_B),
                  slab(2 * SSD_STATE, COL_C), slab(LANES, COL_DT)] + [const2(a) for a in small],
        out_specs=pl.BlockSpec((1, seq, SSD_PAD), lambda b: (b, 0, 0)),
        out_shape=jax.ShapeDtypeStruct((bsz, seq, SSD_PAD), F32),
        scratch_shapes=[pltpu.VMEM((SSD_GROUPS, SSD_STATE, GROUP_PAD), F32)],
        compiler_params=_params(("parallel",)),
        name="ssd",
    )(proj3, proj3, proj3, proj3, proj3, *small)


GM_ROWS = 1024


def _gmlp_kernel(u_ref, v_ref, lng_ref, lnb_ref, ws_ref, bs_ref, ng_ref, o_ref):
    q = GM_CHUNK
    row_i = lax.broadcasted_iota(jnp.int32, (q, q), 0)
    col_i = lax.broadcasted_iota(jnp.int32, (q, q), 1)
    lower = row_i >= col_i
    grp = lax.broadcasted_iota(jnp.int32, (1, GM_W), 1) // (GM_W // GM_GROUPS)
    w = [jnp.where(lower, ws_ref[g], 0.0).astype(BF16) for g in range(GM_GROUPS)]
    for c in range(GM_ROWS // q):
        rows = slice(c * q, (c + 1) * q)
        u = _gelu_tanh(u_ref[rows, :])
        v = _layer_norm(_gelu_tanh(v_ref[rows, :]), lng_ref[...], lnb_ref[...])
        vb = v.astype(BF16)
        mixed = jnp.zeros((q, GM_W), F32)
        for g in range(GM_GROUPS):
            mixed = jnp.where(grp == g, jnp.dot(w[g], vb, preferred_element_type=F32), mixed)
        gm = u * (mixed + bs_ref[...])
        ms = jnp.mean(gm * gm, axis=-1, keepdims=True)
        o_ref[rows, :] = gm * lax.rsqrt(ms + LN_EPS) * ng_ref[...]


def _gmlp(proj, p):
    m = proj.shape[0]
    const = lambda a: pl.BlockSpec(a.shape, lambda i: (0,) * a.ndim)
    small = [p["gm_ln_g"], p["gm_ln_b"], p["gm_w_s"], p["gm_b_s"], p["gm_norm_g"]]
    return pl.pallas_call(
        _gmlp_kernel,
        grid=(m // GM_ROWS,),
        in_specs=[pl.BlockSpec((GM_ROWS, GM_W), lambda i: (i, COL_GU // GM_W)),
                  pl.BlockSpec((GM_ROWS, GM_W), lambda i: (i, COL_GV // GM_W))] + [const(a) for a in small],
        out_specs=pl.BlockSpec((GM_ROWS, GM_W), lambda i: (i, 0)),
        out_shape=jax.ShapeDtypeStruct((m, GM_W), F32),
        compiler_params=_params(("parallel",)),
        name="gmlp",
    )(proj, proj, *small)


OUT_TM = 512
LOGIT_ROWS = 16


def _out_proj_kernel(att_ref, ssd_ref, gm_ref, h_ref, ag_ref, wa_ref, ws_ref, wg_ref, g_ref, b_ref, o_ref,
                     router=None):
    att = att_ref[...]
    ms = jnp.mean(att * att, axis=-1, keepdims=True)
    att = att * lax.rsqrt(ms + LN_EPS) * ag_ref[...]
    mix = jnp.dot(att.astype(BF16), wa_ref[...], preferred_element_type=F32)
    mix = mix + jnp.dot(ssd_ref[...].astype(BF16), ws_ref[...], preferred_element_type=F32)
    mix = mix + jnp.dot(gm_ref[...].astype(BF16), wg_ref[...], preferred_element_type=F32)
    h1 = _layer_norm(ALPHA * h_ref[...] + mix, g_ref[...], b_ref[...])
    o_ref[...] = h1
    if router:
        wr_ref, logit_ref = router
        logit_ref[...] = _dot_bf16x3(wr_ref[...], h1.T)


def _out_proj_kernel_routed(att_ref, ssd_ref, gm_ref, h_ref, ag_ref, wa_ref, ws_ref, wg_ref, g_ref, b_ref, wr_ref,
                            o_ref, logit_ref):
    _out_proj_kernel(att_ref, ssd_ref, gm_ref, h_ref, ag_ref, wa_ref, ws_ref, wg_ref, g_ref, b_ref, o_ref,
                     router=(wr_ref, logit_ref))


def _out_proj(att, ssd, gm, h, p, w_router=None):
    m = h.shape[0]
    row = lambda i: (i, 0)
    const = lambda a: pl.BlockSpec(a.shape, lambda i: (0, 0))
    small = [p["att_norm_g"], p["w_out_att"], p["w_out_ssd"], p["w_out_gm"], p["ln1_g"], p["ln1_b"]]
    out_specs = [pl.BlockSpec((OUT_TM, D_MODEL), row)]
    out_shape = [jax.ShapeDtypeStruct((m, D_MODEL), F32)]
    body = _out_proj_kernel
    if w_router is not None:
        small.append(jnp.pad(w_router.T, ((0, LOGIT_ROWS - N_EXPERTS), (0, 0))))
        out_specs.append(pl.BlockSpec((LOGIT_ROWS, OUT_TM), lambda i: (0, i)))
        out_shape.append(jax.ShapeDtypeStruct((LOGIT_ROWS, m), F32))
        body = _out_proj_kernel_routed
    return pl.pallas_call(
        body,
        grid=(m // OUT_TM,),
        in_specs=[pl.BlockSpec((OUT_TM, ATT_W), row), pl.BlockSpec((OUT_TM, SSD_PAD), row),
                  pl.BlockSpec((OUT_TM, GM_W), row), pl.BlockSpec((OUT_TM, D_MODEL), row)]
                 + [const(a) for a in small],
        out_specs=out_specs,
        out_shape=out_shape,
        compiler_params=_params(("parallel",)),
        name="out_proj_ln1",
    )(att, ssd, gm, h, *small)


FFN_TM = 512


def _swiglu_chunks(acc_ref, xb, w_gate, w_up, w_down, widths):
    dot = functools.partial(jnp.dot, preferred_element_type=F32)
    bounds = [sum(widths[:c]) for c in range(len(widths) + 1)]
    chunk = lambda c: slice(bounds[c], bounds[c + 1])
    up = lambda c: (dot(xb, w_gate(chunk(c))), dot(xb, w_up(chunk(c))))
    pending = up(0)
    for c in range(len(widths)):
        hg, hu = pending
        if c + 1 < len(widths):
            pending = up(c + 1)
        acc_ref[...] += dot((_silu(hg) * hu).astype(BF16), w_down(chunk(c)))


FFN_CHUNK = 512


def _ffn_kernel(x_ref, wg_ref, wu_ref, wd_ref, g_ref, b_ref, o_ref, acc_ref):
    d_ff = wg_ref.shape[-1]
    widths = (FFN_CHUNK,) * (d_ff // FFN_CHUNK) + ((d_ff % FFN_CHUNK,) if d_ff % FFN_CHUNK else ())
    x = x_ref[...]
    acc_ref[...] = ALPHA * x
    _swiglu_chunks(acc_ref, x.astype(BF16), lambda c: wg_ref[:, c], lambda c: wu_ref[:, c],
                   lambda c: wd_ref[c, :], widths)
    o_ref[...] = _layer_norm(acc_ref[...], g_ref[...], b_ref[...])


def _ffn(x, wg, wu, wd, ln_g, ln_b):
    m = x.shape[0]
    resident = lambda a: pl.BlockSpec(a.shape, lambda i: (0, 0), pipeline_mode=pl.Buffered(1))
    return pl.pallas_call(
        _ffn_kernel,
        grid=(m // FFN_TM,),
        in_specs=[pl.BlockSpec((FFN_TM, D_MODEL), lambda i: (i, 0)), resident(wg), resident(wu), resident(wd),
                  pl.BlockSpec((1, D_MODEL), lambda i: (0, 0)),
                  pl.BlockSpec((1, D_MODEL), lambda i: (0, 0))],
        out_specs=pl.BlockSpec((FFN_TM, D_MODEL), lambda i: (i, 0)),
        out_shape=jax.ShapeDtypeStruct((m, D_MODEL), F32),
        scratch_shapes=[pltpu.VMEM((FFN_TM, D_MODEL), F32)],
        compiler_params=_params(("parallel",)),
        name="ffn_ln2",
    )(x, wg, wu, wd, ln_g.reshape(1, D_MODEL), ln_b.reshape(1, D_MODEL))


MOE_TR = 256
MOE_TM = 1024
MOE_TF = 512
MOE_CHUNKS = (256, 256)
SEG_ALIGN = 16
INFO_GATE = N_EXPERTS


def _expert_cap(m):
    n_tiles = m // MOE_TR
    rows = m + (SEG_ALIGN - 1) * n_tiles + MOE_TR + MOE_TM
    return -(-rows // MOE_TM) * MOE_TM


def _dispatch_kernel(h_ref, logit_ref, info_ref, offs_ref, lens_ref, xs_hbm, run_ref, xbuf_ref, zbuf_ref,
                     sem_ref, zsem_ref, *, n_tiles, cap):
    t = pl.program_id(0)
    tr = MOE_TR
    slot = t % 2

    @pl.when(t == 0)
    def _():
        for e in range(N_EXPERTS):
            run_ref[e] = 0

    xb = h_ref[...].astype(BF16)
    lg = logit_ref[0:N_EXPERTS, :]
    row = lax.broadcasted_iota(jnp.int32, (N_EXPERTS, tr), 0)
    m1 = jnp.max(lg, axis=0, keepdims=True)
    i1 = jnp.min(jnp.where(lg == m1, row, N_EXPERTS), axis=0, keepdims=True)
    rest = jnp.where(row == i1, -jnp.inf, lg)
    m2 = jnp.max(rest, axis=0, keepdims=True)
    i2 = jnp.min(jnp.where(rest == m2, row, N_EXPERTS), axis=0, keepdims=True)
    e2 = jnp.exp(m2 - m1)
    p1 = 1.0 / (1.0 + e2)
    gate_t = jnp.where(row == i1, p1, 0.0) + jnp.where(row == i2, e2 * p1, 0.0)
    sel_t = jnp.where((row == i1) | (row == i2), 1.0, 0.0)
    before = (lax.broadcasted_iota(jnp.int32, (tr, tr), 0) < lax.broadcasted_iota(jnp.int32, (tr, tr), 1))
    rank_t = jnp.dot(sel_t.astype(BF16), jnp.where(before, 1.0, 0.0).astype(BF16),
                     preferred_element_type=F32)
    cnt = jnp.sum(sel_t, axis=1, keepdims=True).astype(jnp.int32)
    info_t = jnp.concatenate([rank_t, gate_t, jnp.zeros((LANES - 2 * N_EXPERTS, tr), F32)], axis=0)
    info_ref[...] = info_t.T

    def seg_copy(sl, e, off):
        return pltpu.make_async_copy(xbuf_ref.at[sl, e], xs_hbm.at[pl.ds(pl.multiple_of(off, SEG_ALIGN), tr)],
                                     sem_ref.at[sl, e])

    dst_row = lax.broadcasted_iota(jnp.int32, (tr, tr), 0).astype(F32)
    onehots = [jnp.where((rank_t[e:e + 1, :] == dst_row) & (sel_t[e:e + 1, :] > 0.0), 1.0, 0.0).astype(BF16)
               for e in range(N_EXPERTS)]
    onehot_all = jnp.concatenate(onehots, axis=0)
    half = D_MODEL // 2
    for c in range(2):
        packed = jnp.dot(onehot_all, xb[:, c * half:(c + 1) * half], preferred_element_type=F32)
        xbuf_ref[slot, :, :, c * half:(c + 1) * half] = packed.astype(BF16).reshape(N_EXPERTS, tr, half)

    @pl.when(t > 0)
    def _():
        for e in range(N_EXPERTS):
            seg_copy(1 - slot, e, 0).wait()

    for e in range(N_EXPERTS):
        off = e * cap + run_ref[e]
        offs_ref[t * N_EXPERTS + e] = off
        seg_copy(slot, e, off).start()
        run_ref[e] = run_ref[e] + jnp.bitwise_and(cnt[e, 0] + (SEG_ALIGN - 1), -SEG_ALIGN)

    @pl.when(t == n_tiles - 1)
    def _():
        zbuf_ref[...] = jnp.zeros_like(zbuf_ref)
        for e in range(N_EXPERTS):
            seg_copy(slot, e, 0).wait()
        tails = []
        for e in range(N_EXPERTS):
            lens_ref[e] = run_ref[e]
            tail = pl.multiple_of(e * cap + run_ref[e], SEG_ALIGN)
            tails.append(pltpu.make_async_copy(zbuf_ref, xs_hbm.at[pl.ds(tail, MOE_TR + MOE_TM)], zsem_ref.at[e]))
            tails[-1].start()
        for cp in tails:
            cp.wait()


def _dispatch(h, logits):
    m = h.shape[0]
    n_tiles = m // MOE_TR
    cap = _expert_cap(m)
    smem = pl.BlockSpec(memory_space=pltpu.SMEM)
    return pl.pallas_call(
        functools.partial(_dispatch_kernel, n_tiles=n_tiles, cap=cap),
        grid=(n_tiles,),
        in_specs=[pl.BlockSpec((MOE_TR, D_MODEL), lambda t: (t, 0)),
                  pl.BlockSpec((LOGIT_ROWS, MOE_TR), lambda t: (0, t))],
        out_specs=[pl.BlockSpec((MOE_TR, LANES), lambda t: (t, 0)), smem, smem,
                   pl.BlockSpec(memory_space=pl.ANY)],
        out_shape=[jax.ShapeDtypeStruct((m, LANES), F32),
                   jax.ShapeDtypeStruct((n_tiles * N_EXPERTS,), jnp.int32),
                   jax.ShapeDtypeStruct((N_EXPERTS,), jnp.int32),
                   jax.ShapeDtypeStruct((N_EXPERTS * cap, D_MODEL), BF16)],
        scratch_shapes=[pltpu.SMEM((N_EXPERTS,), jnp.int32),
                        pltpu.VMEM((2, N_EXPERTS, MOE_TR, D_MODEL), BF16),
                        pltpu.VMEM((MOE_TR + MOE_TM, D_MODEL), BF16),
                        pltpu.SemaphoreType.DMA((2, N_EXPERTS)),
                        pltpu.SemaphoreType.DMA((N_EXPERTS,))],
        compiler_params=_params(("arbitrary",)),
        name="moe_dispatch",
    )(h, logits)


def _expert_ffn_kernel(exp_ref, blk_ref, valid_ref, x_ref, wg_ref, wu_ref, wd_ref, y_ref, acc_ref):
    w = pl.program_id(0)
    f = pl.program_id(1)

    @pl.when(valid_ref[w] == 1)
    def _():
        @pl.when(f == 0)
        def _():
            acc_ref[...] = jnp.zeros_like(acc_ref)

        _swiglu_chunks(acc_ref, x_ref[...], lambda c: wg_ref[0, :, c].astype(BF16),
                       lambda c: wu_ref[0, :, c].astype(BF16), lambda c: wd_ref[0, c, :].astype(BF16), MOE_CHUNKS)

        @pl.when(f == pl.num_programs(1) - 1)
        def _():
            y_ref[...] = acc_ref[...].astype(BF16)


def _expert_ffn(xs, exp_w, blk_w, valid_w, w_gate, w_up, w_down):
    d_ff = w_gate.shape[-1]
    n_f = d_ff // MOE_TF
    fcol = lambda w, f, exp, blk, valid: jnp.where(valid[w] == 1, f, n_f - 1)
    grid_spec = pltpu.PrefetchScalarGridSpec(
        num_scalar_prefetch=3,
        grid=(exp_w.shape[0], n_f),
        in_specs=[pl.BlockSpec((MOE_TM, D_MODEL), lambda w, f, exp, blk, valid: (blk[w], 0)),
                  pl.BlockSpec((1, D_MODEL, MOE_TF), lambda w, f, exp, blk, valid: (exp[w], 0, fcol(w, f, exp, blk, valid))),
                  pl.BlockSpec((1, D_MODEL, MOE_TF), lambda w, f, exp, blk, valid: (exp[w], 0, fcol(w, f, exp, blk, valid))),
                  pl.BlockSpec((1, MOE_TF, D_MODEL), lambda w, f, exp, blk, valid: (exp[w], fcol(w, f, exp, blk, valid), 0))],
        out_specs=pl.BlockSpec((MOE_TM, D_MODEL), lambda w, f, exp, blk, valid: (blk[w], 0)),
        scratch_shapes=[pltpu.VMEM((MOE_TM, D_MODEL), F32)])
    return pl.pallas_call(
        _expert_ffn_kernel,
        grid_spec=grid_spec,
        out_shape=jax.ShapeDtypeStruct(xs.shape, BF16),
        compiler_params=_params(("arbitrary", "arbitrary")),
        name="moe_expert_ffn",
    )(exp_w, blk_w, valid_w, xs, w_gate, w_up, w_down)


def _combine_kernel(offs_ref, info_ref, h_ref, g_ref, b_ref, ys_hbm, o_ref, ybuf_ref, sem_ref, *, n_tiles):
    t = pl.program_id(0)
    tr = MOE_TR
    slot = t % 2

    def seg_copy(tile, sl, e):
        off = pl.multiple_of(offs_ref[tile * N_EXPERTS + e], SEG_ALIGN)
        return pltpu.make_async_copy(ys_hbm.at[pl.ds(off, tr)], ybuf_ref.at[sl, e], sem_ref.at[sl, e])

    @pl.when(t == 0)
    def _():
        for e in range(N_EXPERTS):
            seg_copy(0, 0, e).start()

    @pl.when(t + 1 < n_tiles)
    def _():
        for e in range(N_EXPERTS):
            seg_copy(t + 1, 1 - slot, e).start()

    info = info_ref[...]
    src_row = lax.broadcasted_iota(jnp.int32, (tr, tr), 1).astype(F32)
    acc = jnp.zeros((tr, D_MODEL), F32)
    for e in range(N_EXPERTS):
        seg_copy(t, slot, e).wait()
    for e in range(N_EXPERTS):
        rank_c = info[:, e:e + 1]
        gate_c = info[:, INFO_GATE + e:INFO_GATE + e + 1]
        onehot = jnp.where((rank_c == src_row) & (gate_c != 0.0), 1.0, 0.0).astype(BF16)
        acc = acc + gate_c * jnp.dot(onehot, ybuf_ref[slot, e], preferred_element_type=F32)
    o_ref[...] = _layer_norm(ALPHA * h_ref[...] + acc, g_ref[...], b_ref[...])


def _combine(offs, info, h, ys, ln_g, ln_b):
    m = h.shape[0]
    n_tiles = m // MOE_TR
    grid_spec = pltpu.PrefetchScalarGridSpec(
        num_scalar_prefetch=1,
        grid=(n_tiles,),
        in_specs=[pl.BlockSpec((MOE_TR, LANES), lambda t, offs: (t, 0)),
                  pl.BlockSpec((MOE_TR, D_MODEL), lambda t, offs: (t, 0)),
                  pl.BlockSpec((1, D_MODEL), lambda t, offs: (0, 0)),
                  pl.BlockSpec((1, D_MODEL), lambda t, offs: (0, 0)),
                  pl.BlockSpec(memory_space=pl.ANY)],
        out_specs=pl.BlockSpec((MOE_TR, D_MODEL), lambda t, offs: (t, 0)),
        scratch_shapes=[pltpu.VMEM((2, N_EXPERTS, MOE_TR, D_MODEL), BF16),
                        pltpu.SemaphoreType.DMA((2, N_EXPERTS))])
    return pl.pallas_call(
        functools.partial(_combine_kernel, n_tiles=n_tiles),
        grid_spec=grid_spec,
        out_shape=jax.ShapeDtypeStruct((m, D_MODEL), F32),
        compiler_params=_params(("arbitrary",)),
        name="moe_combine_ln2",
    )(offs, info, h, ln_g.reshape(1, D_MODEL), ln_b.reshape(1, D_MODEL), ys)


def _work_list(lens, m):
    cap = _expert_cap(m)
    n_tiles = m // MOE_TR
    max_rows = 2 * m + (SEG_ALIGN - 1) * min(N_EXPERTS * n_tiles, 2 * m) + N_EXPERTS * MOE_TR
    w_max = -(-max_rows // MOE_TM) + N_EXPERTS
    tiles_e = (lens + MOE_TR + MOE_TM - 1) // MOE_TM
    ends = jnp.cumsum(tiles_e)
    w = jnp.arange(w_max, dtype=jnp.int32)
    wc = jnp.minimum(w, ends[-1] - 1)
    exp_w = jnp.sum((wc[:, None] >= ends[None, :]).astype(jnp.int32), axis=1)
    blk_w = exp_w * (cap // MOE_TM) + wc - (ends - tiles_e)[exp_w]
    return exp_w.astype(jnp.int32), blk_w.astype(jnp.int32), (w < ends[-1]).astype(jnp.int32)


def _moe_layer(h, logits, w_gate, w_up, w_down, ln_g, ln_b):
    m = h.shape[0]
    info, offs, lens, xs = _dispatch(h, logits)
    exp_w, blk_w, valid_w = _work_list(lens, m)
    ys = _expert_ffn(xs, exp_w, blk_w, valid_w, w_gate, w_up, w_down)
    return _combine(offs, info, h, ys, ln_g, ln_b)


def _pad_groups(a):
    lead = a.shape[:-1]
    a = a.reshape(*lead, SSD_GROUPS, GROUP_W)
    a = jnp.pad(a, [(0, 0)] * len(lead) + [(0, 0), (0, GROUP_PAD - GROUP_W)])
    return a.reshape(*lead, SSD_PAD)


def _head_lanes(v):
    return jnp.pad(v, (0, LANES - SSD_HEADS)).reshape(1, LANES)


D_IN = 3 * ATT_W + SSD_W + (SSD_W + 2 * SSD_GROUPS * SSD_STATE) + SSD_HEADS + 2 * GM_W
PERM_ROWS = 256


def _permute_w_kernel(w_ref, o_ref):
    src_z = 3 * ATT_W
    src_xs = src_z + SSD_W
    src_b = src_xs + SSD_W
    src_c = src_b + SSD_GROUPS * SSD_STATE
    src_dt = src_c + SSD_GROUPS * SSD_STATE
    src_gu = src_dt + SSD_HEADS
    src_gv = src_gu + GM_W
    o_ref[...] = jnp.zeros_like(o_ref)

    def put(dst, src, width):
        o_ref[:, dst:dst + width] = w_ref[0, :, src:src + width].astype(BF16)

    for g in range(SSD_GROUPS):
        put(COL_Z + g * GROUP_PAD, src_z + g * GROUP_W, GROUP_W)
        put(COL_XS + g * GROUP_PAD, src_xs + g * GROUP_W, GROUP_W)
    put(COL_B, src_b, SSD_GROUPS * SSD_STATE)
    put(COL_C, src_c, SSD_GROUPS * SSD_STATE)
    put(COL_GU, src_gu, GM_W)
    put(COL_GV, src_gv, GM_W)
    put(COL_Q, 0, ATT_W)
    put(COL_K, ATT_W, ATT_W)
    put(COL_V, 2 * ATT_W, ATT_W)
    put(COL_DT, src_dt, SSD_HEADS)


def _permute_w_in(w_in, layer):
    return pl.pallas_call(
        _permute_w_kernel,
        grid=(D_MODEL // PERM_ROWS,),
        in_specs=[pl.BlockSpec((1, PERM_ROWS, D_IN), lambda i: (layer, i, 0))],
        out_specs=pl.BlockSpec((PERM_ROWS, D_PROJ), lambda i: (i, 0)),
        out_shape=jax.ShapeDtypeStruct((D_MODEL, D_PROJ), BF16),
        compiler_params=_params(("parallel",)),
        name="permute_w_in",
    )(w_in)


def _layer_params(i, w_in, conv_w, conv_b, dt_bias, a_log, d_skip, ssd_norm_g, att_norm_g, gm_ln_g, gm_ln_b,
                  gm_w_s, gm_b_s, gm_norm_g, w_out, ln1_g, ln1_b):
    cw, cb = conv_w[i], conv_b[i]
    nbc = SSD_GROUPS * SSD_STATE
    wo = w_out[i]
    return {
        "w_in": _permute_w_in(w_in, i),
        "conv_w_xs": _pad_groups(cw[:, :SSD_W]),
        "conv_w_b": cw[:, SSD_W:SSD_W + nbc],
        "conv_w_c": cw[:, SSD_W + nbc:],
        "conv_b_xs": _pad_groups(cb[:SSD_W]).reshape(1, SSD_PAD),
        "conv_b_b": cb[SSD_W:SSD_W + nbc].reshape(1, nbc),
        "conv_b_c": cb[SSD_W + nbc:].reshape(1, nbc),
        "dt_bias": _head_lanes(dt_bias[i]),
        "a_log": _head_lanes(a_log[i]),
        "d_skip": _pad_groups(jnp.repeat(d_skip[i], SSD_HEAD_DIM)).reshape(1, SSD_PAD),
        "ssd_norm_g": _pad_groups(ssd_norm_g[i]).reshape(1, SSD_PAD),
        "att_norm_g": att_norm_g[i].reshape(1, ATT_W),
        "gm_ln_g": gm_ln_g[i].reshape(1, GM_W),
        "gm_ln_b": gm_ln_b[i].reshape(1, GM_W),
        "gm_w_s": gm_w_s[i],
        "gm_b_s": jnp.repeat(gm_b_s[i].T, GM_W // GM_GROUPS, axis=1),
        "gm_norm_g": gm_norm_g[i].reshape(1, GM_W),
        "w_out_att": wo[:ATT_W].astype(BF16),
        "w_out_ssd": _pad_groups(wo[ATT_W:ATT_W + SSD_W].T).T.astype(BF16),
        "w_out_gm": wo[ATT_W + SSD_W:].astype(BF16),
        "ln1_g": ln1_g[i].reshape(1, D_MODEL),
        "ln1_b": ln1_b[i].reshape(1, D_MODEL),
    }


def _rope_tables(positions):
    inv = ROPE_THETA ** (-jnp.arange(0, ROPE_DIM, 2, dtype=F32) / ROPE_DIM)
    d = jnp.arange(LANES) % HEAD_DIM
    inv_lane = jnp.where(d < ROPE_DIM, inv[d % ROPE_HALF], 0.0)
    ang = positions.astype(F32)[..., None] * inv_lane
    cos, sin = jnp.cos(ang), jnp.sin(ang)
    return cos, jnp.where(d < ROPE_HALF, -sin, 0.0), jnp.where(d >= ROPE_HALF, sin, 0.0)


def kernel(x, positions, ln_in_g, ln_in_b, w_in, conv_w, conv_b, dt_bias, a_log, d_skip, ssd_norm_g, att_norm_g, gm_ln_g, gm_ln_b, gm_w_s, gm_b_s, gm_norm_g, w_out, ln1_g, ln1_b, ln2_g, ln2_b, ffn_w_gate, ffn_w_up, ffn_w_down, router_w, moe_w_gate, moe_w_up, moe_w_down):
    bsz, seq, _ = x.shape
    m = bsz * seq
    rope_c, rope_sa, rope_sb = _rope_tables(positions)
    h = x.reshape(m, D_MODEL)
    for i in range(DEPTH):
        p = _layer_params(i, w_in, conv_w, conv_b, dt_bias, a_log, d_skip, ssd_norm_g, att_norm_g, gm_ln_g,
                          gm_ln_b, gm_w_s, gm_b_s, gm_norm_g, w_out, ln1_g, ln1_b)
        if i == 0:
            proj, h = _in_proj(h, p["w_in"], ln=(ln_in_g, ln_in_b))
        else:
            (proj,) = _in_proj(h, p["w_in"])
        proj3 = proj.reshape(bsz, seq, D_PROJ)
        att = _attention(proj3, rope_c, rope_sa, rope_sb).reshape(m, ATT_W)
        ssd = _ssd(proj3, p).reshape(m, SSD_PAD)
        gm = _gmlp(proj, p)
        j = i // 2
        if i % 2 == 0:
            (h,) = _out_proj(att, ssd, gm, h, p)
            h = _ffn(h, ffn_w_gate[j].astype(BF16), ffn_w_up[j].astype(BF16), ffn_w_down[j].astype(BF16),
                     ln2_g[i], ln2_b[i])
        else:
            h, logits = _out_proj(att, ssd, gm, h, p, w_router=router_w[j])
            h = _moe_layer(h, logits, moe_w_gate[j], moe_w_up[j], moe_w_down[j], ln2_g[i], ln2_b[i])
    return h.reshape(bsz, seq, D_MODEL)
```

```python
import functools

import jax
import jax.numpy as jnp
from jax import lax
from jax.experimental import pallas as pl
from jax.experimental.pallas import tpu as pltpu

F32 = jnp.float32
BF16 = jnp.bfloat16
HIGHEST = lax.Precision.HIGHEST

D_MODEL = 1024
ATT_HEADS = 6
HEAD_DIM = 64
ATT_W = ATT_HEADS * HEAD_DIM
ROPE_DIM = HEAD_DIM // 4
ROPE_HALF = ROPE_DIM // 2
ROPE_THETA = 500000.0
MOBA_BLOCK = 256
MOBA_TOPK = 3
SSD_HEADS = 6
SSD_HEAD_DIM = 64
SSD_W = SSD_HEADS * SSD_HEAD_DIM
SSD_GROUPS = 2
SSD_STATE = 128
SSD_CONV = 4
SSD_CHUNK = 128
GM_GROUPS = 4
GM_W = 256
GM_CHUNK = 128
N_EXPERTS = 8
DEPTH = 2
ALPHA = (2.0 * DEPTH) ** 0.25
NEG = -1e30
LN_EPS = 1e-5

LANES = 128
GROUP_W = SSD_W // SSD_GROUPS
GROUP_PAD = 256
SSD_PAD = SSD_GROUPS * GROUP_PAD
HEADS_PER_GROUP = SSD_HEADS // SSD_GROUPS

COL_Z = 0
COL_XS = 512
COL_B = 1024
COL_C = 1280
COL_GU = 1536
COL_GV = 1792
COL_Q = 2048
COL_K = 2432
COL_V = 2816
COL_DT = 3200
D_PROJ = 3328

VMEM_LIMIT = 56 * 1024 * 1024


def _params(sem, vmem=VMEM_LIMIT):
    return pltpu.CompilerParams(dimension_semantics=sem, vmem_limit_bytes=vmem)


def _layer_norm(x, g, b):
    mu = jnp.mean(x, axis=-1, keepdims=True)
    xc = x - mu
    var = jnp.mean(xc * xc, axis=-1, keepdims=True)
    return xc * lax.rsqrt(var + LN_EPS) * g + b


def _silu(x):
    half = 0.5 * x
    return half + half * jnp.tanh(half)


def _split_bf16(x):
    hi = x.astype(BF16)
    return hi, (x - hi.astype(F32)).astype(BF16)


def _dot_bf16x3(a, b):
    a_hi, a_lo = _split_bf16(a)
    b_hi, b_lo = _split_bf16(b)
    dot = functools.partial(jnp.dot, preferred_element_type=F32)
    return dot(a_hi, b_hi) + (dot(a_lo, b_hi) + dot(a_hi, b_lo))


def _gelu_tanh(x):
    return 0.5 * x * (1.0 + jnp.tanh(0.7978845608028654 * (x + 0.044715 * x * x * x)))


PROJ_TM = 512
PROJ_CH = 256


def _in_proj_kernel(*refs, apply_ln):
    if apply_ln:
        x_ref, g_ref, b_ref, w_ref, proj_ref, h_ref = refs
        x = _layer_norm(x_ref[...], g_ref[...], b_ref[...])
        h_ref[...] = x
    else:
        x_ref, w_ref, proj_ref = refs
        x = x_ref[...]
    xb = x.astype(BF16)
    for j in range(D_PROJ // PROJ_CH):
        cols = slice(j * PROJ_CH, (j + 1) * PROJ_CH)
        proj_ref[:, cols] = jnp.dot(xb, w_ref[:, cols], preferred_element_type=F32)


def _in_proj(x, w, ln=None):
    m = x.shape[0]
    row = lambda i: (i, 0)
    const = lambda i: (0, 0)
    in_specs = [pl.BlockSpec((PROJ_TM, D_MODEL), row)]
    args = [x]
    out_shape = [jax.ShapeDtypeStruct((m, D_PROJ), F32)]
    out_specs = [pl.BlockSpec((PROJ_TM, D_PROJ), row)]
    if ln is not None:
        in_specs += [pl.BlockSpec((1, D_MODEL), const), pl.BlockSpec((1, D_MODEL), const)]
        args += [ln[0].reshape(1, D_MODEL), ln[1].reshape(1, D_MODEL)]
        out_shape.append(jax.ShapeDtypeStruct((m, D_MODEL), F32))
        out_specs.append(pl.BlockSpec((PROJ_TM, D_MODEL), row))
    in_specs.append(pl.BlockSpec((D_MODEL, D_PROJ), const))
    args.append(w)
    return pl.pallas_call(
        functools.partial(_in_proj_kernel, apply_ln=ln is not None),
        grid=(m // PROJ_TM,),
        in_specs=in_specs,
        out_specs=out_specs,
        out_shape=out_shape,
        compiler_params=_params(("parallel",)),
        name="in_proj",
    )(*args)


VT_ROWS = HEAD_DIM + 16
LOG2E = 1.4426950408889634
ATT_VALUE_LAG = 3


def _attn_kernel(q_ref, k_ref, v_ref, c_ref, sa_ref, sb_ref, o_ref, kb_ref, vt_ref, *, seq):
    nb = seq // MOBA_BLOCK
    tq = MOBA_BLOCK
    lane = lax.broadcasted_iota(jnp.int32, (1, LANES), 1)

    def rope(x, rows):
        return (x * c_ref[0, rows, :] + pltpu.roll(x, LANES - ROPE_HALF, 1) * sa_ref[0, rows, :]
                + pltpu.roll(x, ROPE_HALF, 1) * sb_ref[0, rows, :])

    k = rope(k_ref[0], slice(None))
    kb_ref[...] = k.astype(BF16)
    k_mean = jnp.mean(k.reshape(nb, MOBA_BLOCK, LANES), axis=1)
    v_t = v_ref[0].T
    ones_rows = jnp.where(lax.broadcasted_iota(jnp.int32, (VT_ROWS - HEAD_DIM, seq), 0) == 0, 1.0, 0.0)
    for hh in range(2):
        vt_ref[hh] = jnp.concatenate([v_t[hh * HEAD_DIM:(hh + 1) * HEAD_DIM], ones_rows], axis=0).astype(BF16)

    qscale = HEAD_DIM ** -0.5 * LOG2E
    causal_t = (lax.broadcasted_iota(jnp.int32, (tq, tq), 0) <= lax.broadcasted_iota(jnp.int32, (tq, tq), 1))
    blk_row = lax.broadcasted_iota(jnp.int32, (nb, tq), 0)

    q_tiles = {}

    def scores(j, hh):
        if hh == 0:
            rows = slice(j * tq, (j + 1) * tq)
            q_tiles[j] = rope(q_ref[0, rows, :], rows) * qscale
        q = q_tiles[j] if hh == 0 else q_tiles.pop(j)
        head_lanes = (lane < HEAD_DIM) if hh == 0 else (lane >= HEAD_DIM)
        qh_t = jnp.where(head_lanes, q, 0.0).T
        s_t = jnp.dot(kb_ref[0:(j + 1) * tq, :], qh_t.astype(BF16), preferred_element_type=F32)
        gate_t = _dot_bf16x3(k_mean, qh_t) if j > MOBA_TOPK else None
        return s_t, gate_t

    def probs(j, s_t, gate_t):
        n_past = j * tq
        own = jnp.where(causal_t, s_t[n_past:], NEG)
        mx = jnp.max(own, axis=0, keepdims=True)
        shifts = []
        if j > 0:
            blk_max = [jnp.max(s_t[n * tq:(n + 1) * tq], axis=0, keepdims=True) for n in range(j)]
            if j > MOBA_TOPK:
                rank = jnp.zeros((nb, tq), F32)
                for m in range(j):
                    g_m = gate_t[m:m + 1, :]
                    beats = (g_m > gate_t) | ((g_m == gate_t) & (blk_row > m))
                    rank = rank + jnp.where(beats & (blk_row != m), 1.0, 0.0)
                bias = jnp.where((blk_row < j) & (rank >= MOBA_TOPK), NEG, 0.0)
                bias_n = [bias[n:n + 1, :] for n in range(j)]
            else:
                bias_n = [0.0] * j
            for n in range(j):
                mx = jnp.maximum(mx, blk_max[n] + bias_n[n])
            shifts = [mx - bias_n[n] for n in range(j)]
        parts = [jnp.exp2(s_t[n * tq:(n + 1) * tq] - shifts[n]).astype(BF16) for n in range(j)]
        parts.append(jnp.exp2(own - mx).astype(BF16))
        return jnp.concatenate(parts, axis=0) if j > 0 else parts[0]

    def values(j, hh, p_t):
        o_t = jnp.dot(vt_ref[hh, :, 0:(j + 1) * tq], p_t, preferred_element_type=F32)
        return o_t[0:HEAD_DIM] / o_t[HEAD_DIM:HEAD_DIM + 1]

    chains = [(j, hh) for j in range(nb) for hh in range(2)]
    s_state = {}
    p_state = {}
    outs = {}
    for step in range(len(chains) + ATT_VALUE_LAG):
        if step >= ATT_VALUE_LAG:
            j, hh = chains[step - ATT_VALUE_LAG]
            outs[(j, hh)] = values(j, hh, p_state.pop((j, hh)))
            if hh == 1:
                o_ref[0, j * tq:(j + 1) * tq, :] = jnp.concatenate([outs.pop((j, 0)), outs.pop((j, 1))], axis=0).T
        if step < len(chains):
            j, hh = chains[step]
            s_state[(j, hh)] = scores(j, hh)
        if 1 <= step <= len(chains):
            j, hh = chains[step - 1]
            p_state[(j, hh)] = probs(j, *s_state.pop((j, hh)))


def _attention(proj3, rope_c, rope_sa, rope_sb):
    bsz, seq, _ = proj3.shape
    blk = (1, seq, LANES)
    col = lambda base: (lambda b, p: (b, 0, base // LANES + p))
    tab = lambda b, p: (b, 0, 0)
    return pl.pallas_call(
        functools.partial(_attn_kernel, seq=seq),
        grid=(bsz, ATT_W // LANES),
        in_specs=[pl.BlockSpec(blk, col(COL_Q)), pl.BlockSpec(blk, col(COL_K)), pl.BlockSpec(blk, col(COL_V)),
                  pl.BlockSpec(blk, tab), pl.BlockSpec(blk, tab), pl.BlockSpec(blk, tab)],
        out_specs=pl.BlockSpec(blk, lambda b, p: (b, 0, p)),
        out_shape=jax.ShapeDtypeStruct((bsz, seq, ATT_W), F32),
        scratch_shapes=[pltpu.VMEM((seq, LANES), BF16), pltpu.VMEM((2, VT_ROWS, seq), BF16)],
        compiler_params=_params(("parallel", "arbitrary")),
        name="moba_attention",
    )(proj3, proj3, proj3, rope_c, rope_sa, rope_sb)


assert HEADS_PER_GROUP == 3 and GROUP_PAD == 2 * LANES and SSD_HEAD_DIM * 2 == LANES


def _expand_heads(small):
    low = lax.broadcasted_iota(jnp.int32, (1, LANES), 1) < SSD_HEAD_DIM
    slabs = []
    for g in range(SSD_GROUPS):
        h0 = g * HEADS_PER_GROUP
        slabs.append(jnp.where(low, small[:, h0:h0 + 1], small[:, h0 + 1:h0 + 2]))
        slabs.append(jnp.where(low, small[:, h0 + 2:h0 + 3], 0.0))
    return jnp.concatenate(slabs, axis=1)


def _ssd_kernel(z_ref, xs_ref, b_ref, c_ref, dt_ref, cwx_ref, cwb_ref, cwc_ref, cbx_ref, cbb_ref, cbc_ref,
                dtb_ref, alog_ref, dskip_ref, ng_ref, o_ref, state_ref, *, seq):
    q = SSD_CHUNK
    n = SSD_STATE
    state_ref[...] = jnp.zeros_like(state_ref)
    row_i = lax.broadcasted_iota(jnp.int32, (q, q), 0)
    col_i = lax.broadcasted_iota(jnp.int32, (q, q), 1)
    lower = row_i >= col_i
    tril = jnp.where(lower, 1.0, 0.0)
    slab = lax.broadcasted_iota(jnp.int32, (1, GROUP_PAD), 1) // SSD_HEAD_DIM
    a_neg = -jnp.exp(alog_ref[...])

    def conv_silu(ref, w_ref, bias_ref, c):
        start = pl.multiple_of(c * q, q)
        cur = ref[0, pl.ds(start, q), :]
        prev_start = pl.multiple_of(jnp.maximum(c * q - 8, 0), 8)
        prev = jnp.where(c > 0, ref[0, pl.ds(prev_start, 8), :], 0.0)
        full = jnp.concatenate([prev, cur], axis=0)
        y = bias_ref[...] + w_ref[SSD_CONV - 1:SSD_CONV, :] * cur
        for i in range(SSD_CONV - 1):
            shift = SSD_CONV - 1 - i
            y = y + w_ref[i:i + 1, :] * pltpu.roll(full, shift, 0)[8:, :]
        return _silu(y)

    def chunk(c, carry):
        start = pl.multiple_of(c * q, q)
        xs = conv_silu(xs_ref, cwx_ref, cbx_ref, c)
        bm = conv_silu(b_ref, cwb_ref, cbb_ref, c)
        cm = conv_silu(c_ref, cwc_ref, cbc_ref, c)
        x = dt_ref[0, pl.ds(start, q), :] + dtb_ref[...]
        dt = jnp.maximum(x, 0.0) + jnp.log(1.0 + jnp.exp(-jnp.abs(x)))
        a = dt * a_neg
        a_cum = jnp.dot(tril, a, precision=HIGHEST, preferred_element_type=F32)
        a_cum_t = a_cum.T
        dt_e = _expand_heads(dt)
        acum_e = _expand_heads(a_cum)
        alast_e = acum_e[q - 1:q, :]
        xdt = xs * dt_e
        decay_e = jnp.exp(alast_e - acum_e)
        y_parts = []
        for g in range(SSD_GROUPS):
            gl = slice(g * GROUP_PAD, (g + 1) * GROUP_PAD)
            nl = slice(g * n, (g + 1) * n)
            b_g = bm[:, nl].astype(BF16)
            b_gt = bm[:, nl].T.astype(BF16)
            c_g = cm[:, nl].astype(BF16)
            cb = lax.dot_general(c_g, b_g, (((1,), (1,)), ((), ())), preferred_element_type=F32)
            xdt_g = xdt[:, gl]
            xdt_b = xdt_g.astype(BF16)
            y_g = jnp.zeros((q, GROUP_PAD), F32)
            for r in range(HEADS_PER_GROUP):
                h = g * HEADS_PER_GROUP + r
                seg = a_cum[:, h:h + 1] - a_cum_t[h:h + 1, :]
                l_h = jnp.where(lower, jnp.exp(jnp.minimum(seg, 0.0)), 0.0)
                y_h = jnp.dot((cb * l_h).astype(BF16), xdt_b, preferred_element_type=F32)
                y_g = jnp.where(slab == r, y_h, y_g)
            prev = state_ref[g]
            y_off = jnp.dot(c_g, prev.astype(BF16), preferred_element_type=F32) * jnp.exp(acum_e[:, gl])
            new = jnp.dot(b_gt, (xdt_g * decay_e[:, gl]).astype(BF16), preferred_element_type=F32)
            state_ref[g] = prev * jnp.exp(alast_e[:, gl]) + new
            y_parts.append(y_g + y_off)
        y = jnp.concatenate(y_parts, axis=1) + dskip_ref[...] * xs
        yz = y * _silu(z_ref[0, pl.ds(start, q), :])
        outs = []
        for g in range(SSD_GROUPS):
            yg = yz[:, g * GROUP_PAD:(g + 1) * GROUP_PAD]
            ms = jnp.sum(yg * yg, axis=-1, keepdims=True) * (1.0 / GROUP_W)
            outs.append(yg * lax.rsqrt(ms + LN_EPS))
        o_ref[0, pl.ds(start, q), :] = jnp.concatenate(outs, axis=1) * ng_ref[...]
        return carry

    lax.fori_loop(0, seq // q, chunk, 0)


def _ssd(proj3, p):
    bsz, seq, _ = proj3.shape
    slab = lambda width, base: pl.BlockSpec((1, seq, width), lambda b: (b, 0, base // width))
    const2 = lambda a: pl.BlockSpec(a.shape, lambda b: (0, 0))
    small = [p["conv_w_xs"], p["conv_w_b"], p["conv_w_c"], p["conv_b_xs"], p["conv_b_b"], p["conv_b_c"],
             p["dt_bias"], p["a_log"], p["d_skip"], p["ssd_norm_g"]]
    return pl.pallas_call(
        functools.partial(_ssd_kernel, seq=seq),
        grid=(bsz,),
        in_specs=[slab(SSD_PAD, COL_Z), slab(SSD_PAD, COL_XS), slab(2 * SSD_STATE, COL_B),
                  slab(2 * SSD_STATE, COL_C), slab(LANES, COL_DT)] + [const2(a) for a in small],
        out_specs=pl.BlockSpec((1, seq, SSD_PAD), lambda b: (b, 0, 0)),
        out_shape=jax.ShapeDtypeStruct((bsz, seq, SSD_PAD), F32),
        scratch_shapes=[pltpu.VMEM((SSD_GROUPS, SSD_STATE, GROUP_PAD), F32)],
        compiler_params=_params(("parallel",)),
        name="ssd",
    )(proj3, proj3, proj3, proj3, proj3, *small)


GM_ROWS = 1024


def _gmlp_kernel(u_ref, v_ref, lng_ref, lnb_ref, ws_ref, bs_ref, ng_ref, o_ref):
    q = GM_CHUNK
    row_i = lax.broadcasted_iota(jnp.int32, (q, q), 0)
    col_i = lax.broadcasted_iota(jnp.int32, (q, q), 1)
    lower = row_i >= col_i
    grp = lax.broadcasted_iota(jnp.int32, (1, GM_W), 1) // (GM_W // GM_GROUPS)
    w = [jnp.where(lower, ws_ref[g], 0.0).astype(BF16) for g in range(GM_GROUPS)]
    for c in range(GM_ROWS // q):
        rows = slice(c * q, (c + 1) * q)
        u = _gelu_tanh(u_ref[rows, :])
        v = _layer_norm(_gelu_tanh(v_ref[rows, :]), lng_ref[...], lnb_ref[...])
        vb = v.astype(BF16)
        mixed = jnp.zeros((q, GM_W), F32)
        for g in range(GM_GROUPS):
            mixed = jnp.where(grp == g, jnp.dot(w[g], vb, preferred_element_type=F32), mixed)
        gm = u * (mixed + bs_ref[...])
        ms = jnp.mean(gm * gm, axis=-1, keepdims=True)
        o_ref[rows, :] = gm * lax.rsqrt(ms + LN_EPS) * ng_ref[...]


def _gmlp(proj, p):
    m = proj.shape[0]
    const = lambda a: pl.BlockSpec(a.shape, lambda i: (0,) * a.ndim)
    small = [p["gm_ln_g"], p["gm_ln_b"], p["gm_w_s"], p["gm_b_s"], p["gm_norm_g"]]
    return pl.pallas_call(
        _gmlp_kernel,
        grid=(m // GM_ROWS,),
        in_specs=[pl.BlockSpec((GM_ROWS, GM_W), lambda i: (i, COL_GU // GM_W)),
                  pl.BlockSpec((GM_ROWS, GM_W), lambda i: (i, COL_GV // GM_W))] + [const(a) for a in small],
        out_specs=pl.BlockSpec((GM_ROWS, GM_W), lambda i: (i, 0)),
        out_shape=jax.ShapeDtypeStruct((m, GM_W), F32),
        compiler_params=_params(("parallel",)),
        name="gmlp",
    )(proj, proj, *small)


OUT_TM = 512
LOGIT_ROWS = 16


def _out_proj_kernel(att_ref, ssd_ref, gm_ref, h_ref, ag_ref, wa_ref, ws_ref, wg_ref, g_ref, b_ref, o_ref,
                     router=None):
    att = att_ref[...]
    ms = jnp.mean(att * att, axis=-1, keepdims=True)
    att = att * lax.rsqrt(ms + LN_EPS) * ag_ref[...]
    mix = jnp.dot(att.astype(BF16), wa_ref[...], preferred_element_type=F32)
    mix = mix + jnp.dot(ssd_ref[...].astype(BF16), ws_ref[...], preferred_element_type=F32)
    mix = mix + jnp.dot(gm_ref[...].astype(BF16), wg_ref[...], preferred_element_type=F32)
    h1 = _layer_norm(ALPHA * h_ref[...] + mix, g_ref[...], b_ref[...])
    o_ref[...] = h1
    if router:
        wr_ref, logit_ref = router
        logit_ref[...] = _dot_bf16x3(wr_ref[...], h1.T)


def _out_proj_kernel_routed(att_ref, ssd_ref, gm_ref, h_ref, ag_ref, wa_ref, ws_ref, wg_ref, g_ref, b_ref, wr_ref,
                            o_ref, logit_ref):
    _out_proj_kernel(att_ref, ssd_ref, gm_ref, h_ref, ag_ref, wa_ref, ws_ref, wg_ref, g_ref, b_ref, o_ref,
                     router=(wr_ref, logit_ref))


def _out_proj(att, ssd, gm, h, p, w_router=None):
    m = h.shape[0]
    row = lambda i: (i, 0)
    const = lambda a: pl.BlockSpec(a.shape, lambda i: (0, 0))
    small = [p["att_norm_g"], p["w_out_att"], p["w_out_ssd"], p["w_out_gm"], p["ln1_g"], p["ln1_b"]]
    out_specs = [pl.BlockSpec((OUT_TM, D_MODEL), row)]
    out_shape = [jax.ShapeDtypeStruct((m, D_MODEL), F32)]
    body = _out_proj_kernel
    if w_router is not None:
        small.append(jnp.pad(w_router.T, ((0, LOGIT_ROWS - N_EXPERTS), (0, 0))))
        out_specs.append(pl.BlockSpec((LOGIT_ROWS, OUT_TM), lambda i: (0, i)))
        out_shape.append(jax.ShapeDtypeStruct((LOGIT_ROWS, m), F32))
        body = _out_proj_kernel_routed
    return pl.pallas_call(
        body,
        grid=(m // OUT_TM,),
        in_specs=[pl.BlockSpec((OUT_TM, ATT_W), row), pl.BlockSpec((OUT_TM, SSD_PAD), row),
                  pl.BlockSpec((OUT_TM, GM_W), row), pl.BlockSpec((OUT_TM, D_MODEL), row)]
                 + [const(a) for a in small],
        out_specs=out_specs,
        out_shape=out_shape,
        compiler_params=_params(("parallel",)),
        name="out_proj_ln1",
    )(att, ssd, gm, h, *small)


FFN_TM = 512


def _swiglu_chunks(acc_ref, xb, w_gate, w_up, w_down, widths):
    dot = functools.partial(jnp.dot, preferred_element_type=F32)
    bounds = [sum(widths[:c]) for c in range(len(widths) + 1)]
    chunk = lambda c: slice(bounds[c], bounds[c + 1])
    up = lambda c: (dot(xb, w_gate(chunk(c))), dot(xb, w_up(chunk(c))))
    pending = up(0)
    for c in range(len(widths)):
        hg, hu = pending
        if c + 1 < len(widths):
            pending = up(c + 1)
        acc_ref[...] += dot((_silu(hg) * hu).astype(BF16), w_down(chunk(c)))


FFN_CHUNK = 512


def _ffn_kernel(x_ref, wg_ref, wu_ref, wd_ref, g_ref, b_ref, o_ref, acc_ref):
    d_ff = wg_ref.shape[-1]
    widths = (FFN_CHUNK,) * (d_ff // FFN_CHUNK) + ((d_ff % FFN_CHUNK,) if d_ff % FFN_CHUNK else ())
    x = x_ref[...]
    acc_ref[...] = ALPHA * x
    _swiglu_chunks(acc_ref, x.astype(BF16), lambda c: wg_ref[:, c], lambda c: wu_ref[:, c],
                   lambda c: wd_ref[c, :], widths)
    o_ref[...] = _layer_norm(acc_ref[...], g_ref[...], b_ref[...])


def _ffn(x, wg, wu, wd, ln_g, ln_b):
    m = x.shape[0]
    resident = lambda a: pl.BlockSpec(a.shape, lambda i: (0, 0), pipeline_mode=pl.Buffered(1))
    return pl.pallas_call(
        _ffn_kernel,
        grid=(m // FFN_TM,),
        in_specs=[pl.BlockSpec((FFN_TM, D_MODEL), lambda i: (i, 0)), resident(wg), resident(wu), resident(wd),
                  pl.BlockSpec((1, D_MODEL), lambda i: (0, 0)),
                  pl.BlockSpec((1, D_MODEL), lambda i: (0, 0))],
        out_specs=pl.BlockSpec((FFN_TM, D_MODEL), lambda i: (i, 0)),
        out_shape=jax.ShapeDtypeStruct((m, D_MODEL), F32),
        scratch_shapes=[pltpu.VMEM((FFN_TM, D_MODEL), F32)],
        compiler_params=_params(("parallel",)),
        name="ffn_ln2",
    )(x, wg, wu, wd, ln_g.reshape(1, D_MODEL), ln_b.reshape(1, D_MODEL))


MOE_TR = 256
MOE_TM = 1024
MOE_HALF = MOE_TM // 2
MOE_TF = 512
MOE_CHUNKS = (256, 256)
SEG_ALIGN = 16
MOE_WIN_SMALL = 128
INFO_GATE = N_EXPERTS


def _expert_cap(m):
    n_tiles = m // MOE_TR
    rows = m + (SEG_ALIGN - 1) * n_tiles + MOE_TR + MOE_HALF
    return -(-rows // MOE_TM) * MOE_TM


def _rows_computed(lens):
    return jnp.maximum(-(-lens // MOE_HALF), 1) * MOE_HALF


def _dispatch_kernel(h_ref, logit_ref, info_ref, offs_ref, lens_ref, small_ref, xs_hbm, run_ref, xbuf_ref, zbuf_ref,
                     sem_ref, zsem_ref, *, n_tiles, cap):
    t = pl.program_id(0)
    tr = MOE_TR
    slot = t % 2

    @pl.when(t == 0)
    def _():
        for e in range(N_EXPERTS):
            run_ref[e] = 0

    xb = h_ref[...].astype(BF16)
    lg = logit_ref[0:N_EXPERTS, :]
    row = lax.broadcasted_iota(jnp.int32, (N_EXPERTS, tr), 0)
    m1 = jnp.max(lg, axis=0, keepdims=True)
    i1 = jnp.min(jnp.where(lg == m1, row, N_EXPERTS), axis=0, keepdims=True)
    rest = jnp.where(row == i1, -jnp.inf, lg)
    m2 = jnp.max(rest, axis=0, keepdims=True)
    i2 = jnp.min(jnp.where(rest == m2, row, N_EXPERTS), axis=0, keepdims=True)
    e2 = jnp.exp(m2 - m1)
    p1 = 1.0 / (1.0 + e2)
    gate_t = jnp.where(row == i1, p1, 0.0) + jnp.where(row == i2, e2 * p1, 0.0)
    sel_t = jnp.where((row == i1) | (row == i2), 1.0, 0.0)
    before = (lax.broadcasted_iota(jnp.int32, (tr, tr), 0) < lax.broadcasted_iota(jnp.int32, (tr, tr), 1))
    rank_t = jnp.dot(sel_t.astype(BF16), jnp.where(before, 1.0, 0.0).astype(BF16),
                     preferred_element_type=F32)
    cnt = jnp.sum(sel_t, axis=1, keepdims=True).astype(jnp.int32)
    info_t = jnp.concatenate([rank_t, gate_t, jnp.zeros((LANES - 2 * N_EXPERTS, tr), F32)], axis=0)
    info_ref[...] = info_t.T

    small = jnp.max(cnt) <= MOE_WIN_SMALL
    small_ref[t] = small.astype(jnp.int32)

    def seg_copy(sl, e, off, rows):
        return pltpu.make_async_copy(xbuf_ref.at[sl, e, pl.ds(0, rows)],
                                     xs_hbm.at[pl.ds(pl.multiple_of(off, SEG_ALIGN), rows)], sem_ref.at[sl, e])

    def by_window(flag, fn):
        pl.when(flag)(functools.partial(fn, MOE_WIN_SMALL))
        pl.when(jnp.logical_not(flag))(functools.partial(fn, tr))

    def compact(rows):
        dst_row = lax.broadcasted_iota(jnp.int32, (rows, tr), 0).astype(F32)
        onehots = [jnp.where((rank_t[e:e + 1, :] == dst_row) & (sel_t[e:e + 1, :] > 0.0), 1.0, 0.0).astype(BF16)
                   for e in range(N_EXPERTS)]
        onehot_all = jnp.concatenate(onehots, axis=0)
        half = D_MODEL // 2
        for c in range(2):
            packed = jnp.dot(onehot_all, xb[:, c * half:(c + 1) * half], preferred_element_type=F32)
            xbuf_ref[slot, :, 0:rows, c * half:(c + 1) * half] = packed.astype(BF16).reshape(N_EXPERTS, rows, half)

    by_window(small, compact)

    def wait_previous(rows):
        for e in range(N_EXPERTS):
            seg_copy(1 - slot, e, 0, rows).wait()

    @pl.when(t > 0)
    def _():
        by_window(small_ref[jnp.maximum(t - 1, 0)] == 1, wait_previous)

    def start_all(rows):
        for e in range(N_EXPERTS):
            seg_copy(slot, e, e * cap + run_ref[e], rows).start()

    by_window(small, start_all)
    for e in range(N_EXPERTS):
        offs_ref[t * N_EXPERTS + e] = e * cap + run_ref[e]
        run_ref[e] = run_ref[e] + jnp.bitwise_and(cnt[e, 0] + (SEG_ALIGN - 1), -SEG_ALIGN)

    def wait_current(rows):
        for e in range(N_EXPERTS):
            seg_copy(slot, e, 0, rows).wait()

    @pl.when(t == n_tiles - 1)
    def _():
        zbuf_ref[...] = jnp.zeros_like(zbuf_ref)
        by_window(small, wait_current)
        tails = []
        for e in range(N_EXPERTS):
            lens_ref[e] = run_ref[e]
            tail = pl.multiple_of(e * cap + run_ref[e], SEG_ALIGN)
            tails.append(pltpu.make_async_copy(zbuf_ref, xs_hbm.at[pl.ds(tail, MOE_HALF)], zsem_ref.at[e]))
            tails[-1].start()
        for cp in tails:
            cp.wait()


def _dispatch(h, logits):
    m = h.shape[0]
    n_tiles = m // MOE_TR
    cap = _expert_cap(m)
    smem = pl.BlockSpec(memory_space=pltpu.SMEM)
    return pl.pallas_call(
        functools.partial(_dispatch_kernel, n_tiles=n_tiles, cap=cap),
        grid=(n_tiles,),
        in_specs=[pl.BlockSpec((MOE_TR, D_MODEL), lambda t: (t, 0)),
                  pl.BlockSpec((LOGIT_ROWS, MOE_TR), lambda t: (0, t))],
        out_specs=[pl.BlockSpec((MOE_TR, LANES), lambda t: (t, 0)), smem, smem, smem,
                   pl.BlockSpec(memory_space=pl.ANY)],
        out_shape=[jax.ShapeDtypeStruct((m, LANES), F32),
                   jax.ShapeDtypeStruct((n_tiles * N_EXPERTS,), jnp.int32),
                   jax.ShapeDtypeStruct((N_EXPERTS,), jnp.int32),
                   jax.ShapeDtypeStruct((n_tiles,), jnp.int32),
                   jax.ShapeDtypeStruct((N_EXPERTS * cap, D_MODEL), BF16)],
        scratch_shapes=[pltpu.SMEM((N_EXPERTS,), jnp.int32),
                        pltpu.VMEM((2, N_EXPERTS, MOE_TR, D_MODEL), BF16),
                        pltpu.VMEM((MOE_HALF, D_MODEL), BF16),
                        pltpu.SemaphoreType.DMA((2, N_EXPERTS)),
                        pltpu.SemaphoreType.DMA((N_EXPERTS,))],
        compiler_params=_params(("arbitrary",)),
        name="moe_dispatch",
    )(h, logits)


TILE_SKIP, TILE_HALF, TILE_FULL = 0, 1, 2


def _expert_ffn_kernel(exp_ref, blk_ref, mode_ref, x_ref, wg_ref, wu_ref, wd_ref, y_ref, acc_ref):
    w = pl.program_id(0)
    f = pl.program_id(1)

    def tile(rows):
        acc = acc_ref.at[pl.ds(0, rows)]

        @pl.when(f == 0)
        def _():
            acc[...] = jnp.zeros((rows, D_MODEL), F32)

        _swiglu_chunks(acc, x_ref[0:rows, :], lambda c: wg_ref[0, :, c].astype(BF16),
                       lambda c: wu_ref[0, :, c].astype(BF16), lambda c: wd_ref[0, c, :].astype(BF16), MOE_CHUNKS)

        @pl.when(f == pl.num_programs(1) - 1)
        def _():
            y_ref[0:rows, :] = acc[...].astype(BF16)

    pl.when(mode_ref[w] == TILE_FULL)(functools.partial(tile, MOE_TM))
    pl.when(mode_ref[w] == TILE_HALF)(functools.partial(tile, MOE_HALF))


def _expert_ffn(xs, exp_w, blk_w, mode_w, w_gate, w_up, w_down):
    d_ff = w_gate.shape[-1]
    n_f = d_ff // MOE_TF
    fcol = lambda w, f, exp, blk, mode: jnp.where(mode[w] != TILE_SKIP, f, n_f - 1)
    grid_spec = pltpu.PrefetchScalarGridSpec(
        num_scalar_prefetch=3,
        grid=(exp_w.shape[0], n_f),
        in_specs=[pl.BlockSpec((MOE_TM, D_MODEL), lambda w, f, exp, blk, valid: (blk[w], 0)),
                  pl.BlockSpec((1, D_MODEL, MOE_TF), lambda w, f, exp, blk, valid: (exp[w], 0, fcol(w, f, exp, blk, valid))),
                  pl.BlockSpec((1, D_MODEL, MOE_TF), lambda w, f, exp, blk, valid: (exp[w], 0, fcol(w, f, exp, blk, valid))),
                  pl.BlockSpec((1, MOE_TF, D_MODEL), lambda w, f, exp, blk, valid: (exp[w], fcol(w, f, exp, blk, valid), 0))],
        out_specs=pl.BlockSpec((MOE_TM, D_MODEL), lambda w, f, exp, blk, valid: (blk[w], 0)),
        scratch_shapes=[pltpu.VMEM((MOE_TM, D_MODEL), F32)])
    return pl.pallas_call(
        _expert_ffn_kernel,
        grid_spec=grid_spec,
        out_shape=jax.ShapeDtypeStruct(xs.shape, BF16),
        compiler_params=_params(("arbitrary", "arbitrary")),
        name="moe_expert_ffn",
    )(exp_w, blk_w, mode_w, xs, w_gate, w_up, w_down)


def _combine_kernel(starts_ref, shifts_ref, small_ref, info_ref, h_ref, g_ref, b_ref, ys_hbm, o_ref, ybuf_ref,
                    sem_ref, *, n_tiles):
    t = pl.program_id(0)
    tr = MOE_TR
    slot = t % 2

    def seg_copy(tile, sl, e, rows):
        off = pl.multiple_of(starts_ref[tile * N_EXPERTS + e], SEG_ALIGN)
        return pltpu.make_async_copy(ys_hbm.at[pl.ds(off, rows)], ybuf_ref.at[sl, e, pl.ds(0, rows)],
                                     sem_ref.at[sl, e])

    def by_window(tile, fn):
        flag = small_ref[tile] == 1
        pl.when(flag)(functools.partial(fn, MOE_WIN_SMALL))
        pl.when(jnp.logical_not(flag))(functools.partial(fn, tr))

    def fetch(tile, sl):
        def start(rows):
            for e in range(N_EXPERTS):
                seg_copy(tile, sl, e, rows).start()
        by_window(tile, start)

    @pl.when(t == 0)
    def _():
        fetch(0, 0)

    @pl.when(t + 1 < n_tiles)
    def _():
        fetch(jnp.minimum(t + 1, n_tiles - 1), 1 - slot)

    def gather(rows):
        for e in range(N_EXPERTS):
            seg_copy(t, slot, e, rows).wait()
        info = info_ref[...]
        src_row = lax.broadcasted_iota(jnp.int32, (tr, rows), 1).astype(F32)
        acc = ALPHA * h_ref[...]
        for e in range(N_EXPERTS):
            rank_c = info[:, e:e + 1] + shifts_ref[t * N_EXPERTS + e].astype(F32)
            gate_c = info[:, INFO_GATE + e:INFO_GATE + e + 1]
            onehot = jnp.where((rank_c == src_row) & (gate_c != 0.0), 1.0, 0.0).astype(BF16)
            acc = acc + gate_c * jnp.dot(onehot, ybuf_ref[slot, e, 0:rows, :], preferred_element_type=F32)
        o_ref[...] = _layer_norm(acc, g_ref[...], b_ref[...])

    by_window(t, gather)


def _combine(starts, shifts, small, info, h, ys, ln_g, ln_b):
    m = h.shape[0]
    n_tiles = m // MOE_TR
    grid_spec = pltpu.PrefetchScalarGridSpec(
        num_scalar_prefetch=3,
        grid=(n_tiles,),
        in_specs=[pl.BlockSpec((MOE_TR, LANES), lambda t, *_: (t, 0)),
                  pl.BlockSpec((MOE_TR, D_MODEL), lambda t, *_: (t, 0)),
                  pl.BlockSpec((1, D_MODEL), lambda t, *_: (0, 0)),
                  pl.BlockSpec((1, D_MODEL), lambda t, *_: (0, 0)),
                  pl.BlockSpec(memory_space=pl.ANY)],
        out_specs=pl.BlockSpec((MOE_TR, D_MODEL), lambda t, *_: (t, 0)),
        scratch_shapes=[pltpu.VMEM((2, N_EXPERTS, MOE_TR, D_MODEL), BF16),
                        pltpu.SemaphoreType.DMA((2, N_EXPERTS))])
    return pl.pallas_call(
        functools.partial(_combine_kernel, n_tiles=n_tiles),
        grid_spec=grid_spec,
        out_shape=jax.ShapeDtypeStruct((m, D_MODEL), F32),
        compiler_params=_params(("arbitrary",)),
        name="moe_combine_ln2",
    )(starts, shifts, small, info, h, ln_g.reshape(1, D_MODEL), ln_b.reshape(1, D_MODEL), ys)


def _work_list(lens, m):
    cap = _expert_cap(m)
    n_tiles = m // MOE_TR
    max_rows = 2 * m + (SEG_ALIGN - 1) * min(N_EXPERTS * n_tiles, 2 * m)
    w_max = -(-(max_rows + N_EXPERTS * MOE_HALF) // MOE_TM) + N_EXPERTS
    rows_e = _rows_computed(lens)
    tiles_e = -(-rows_e // MOE_TM)
    ends = jnp.cumsum(tiles_e)
    w = jnp.arange(w_max, dtype=jnp.int32)
    wc = jnp.minimum(w, ends[-1] - 1)
    exp_w = jnp.sum((wc[:, None] >= ends[None, :]).astype(jnp.int32), axis=1)
    tile_w = wc - (ends - tiles_e)[exp_w]
    blk_w = exp_w * (cap // MOE_TM) + tile_w
    full = (tile_w + 1) * MOE_TM <= rows_e[exp_w]
    mode_w = jnp.where(w < ends[-1], jnp.where(full, TILE_FULL, TILE_HALF), TILE_SKIP)
    return exp_w.astype(jnp.int32), blk_w.astype(jnp.int32), mode_w.astype(jnp.int32)


def _read_windows(offs, small, lens, m):
    cap = _expert_cap(m)
    region = jnp.arange(N_EXPERTS, dtype=jnp.int32) * cap
    rel = offs.reshape(-1, N_EXPERTS) - region
    width = jnp.where(small == 1, MOE_WIN_SMALL, MOE_TR)[:, None]
    start = jnp.minimum(rel, _rows_computed(lens)[None, :] - width)
    return (start + region).reshape(-1).astype(jnp.int32), (rel - start).reshape(-1).astype(jnp.int32)


def _moe_layer(h, logits, w_gate, w_up, w_down, ln_g, ln_b):
    m = h.shape[0]
    info, offs, lens, small, xs = _dispatch(h, logits)
    exp_w, blk_w, mode_w = _work_list(lens, m)
    ys = _expert_ffn(xs, exp_w, blk_w, mode_w, w_gate, w_up, w_down)
    starts, shifts = _read_windows(offs, small, lens, m)
    return _combine(starts, shifts, small, info, h, ys, ln_g, ln_b)


def _pad_groups(a):
    lead = a.shape[:-1]
    a = a.reshape(*lead, SSD_GROUPS, GROUP_W)
    a = jnp.pad(a, [(0, 0)] * len(lead) + [(0, 0), (0, GROUP_PAD - GROUP_W)])
    return a.reshape(*lead, SSD_PAD)


def _head_lanes(v):
    return jnp.pad(v, (0, LANES - SSD_HEADS)).reshape(1, LANES)


D_IN = 3 * ATT_W + SSD_W + (SSD_W + 2 * SSD_GROUPS * SSD_STATE) + SSD_HEADS + 2 * GM_W
PERM_ROWS = 256


def _permute_w_kernel(w_ref, o_ref):
    src_z = 3 * ATT_W
    src_xs = src_z + SSD_W
    src_b = src_xs + SSD_W
    src_c = src_b + SSD_GROUPS * SSD_STATE
    src_dt = src_c + SSD_GROUPS * SSD_STATE
    src_gu = src_dt + SSD_HEADS
    src_gv = src_gu + GM_W
    o_ref[...] = jnp.zeros_like(o_ref)

    def put(dst, src, width):
        o_ref[:, dst:dst + width] = w_ref[0, :, src:src + width].astype(BF16)

    for g in range(SSD_GROUPS):
        put(COL_Z + g * GROUP_PAD, src_z + g * GROUP_W, GROUP_W)
        put(COL_XS + g * GROUP_PAD, src_xs + g * GROUP_W, GROUP_W)
    put(COL_B, src_b, SSD_GROUPS * SSD_STATE)
    put(COL_C, src_c, SSD_GROUPS * SSD_STATE)
    put(COL_GU, src_gu, GM_W)
    put(COL_GV, src_gv, GM_W)
    put(COL_Q, 0, ATT_W)
    put(COL_K, ATT_W, ATT_W)
    put(COL_V, 2 * ATT_W, ATT_W)
    put(COL_DT, src_dt, SSD_HEADS)


def _permute_w_in(w_in, layer):
    return pl.pallas_call(
        _permute_w_kernel,
        grid=(D_MODEL // PERM_ROWS,),
        in_specs=[pl.BlockSpec((1, PERM_ROWS, D_IN), lambda i: (layer, i, 0))],
        out_specs=pl.BlockSpec((PERM_ROWS, D_PROJ), lambda i: (i, 0)),
        out_shape=jax.ShapeDtypeStruct((D_MODEL, D_PROJ), BF16),
        compiler_params=_params(("parallel",)),
        name="permute_w_in",
    )(w_in)


def _layer_params(i, w_in, conv_w, conv_b, dt_bias, a_log, d_skip, ssd_norm_g, att_norm_g, gm_ln_g, gm_ln_b,
                  gm_w_s, gm_b_s, gm_norm_g, w_out, ln1_g, ln1_b):
    cw, cb = conv_w[i], conv_b[i]
    nbc = SSD_GROUPS * SSD_STATE
    wo = w_out[i]
    return {
        "w_in": _permute_w_in(w_in, i),
        "conv_w_xs": _pad_groups(cw[:, :SSD_W]),
        "conv_w_b": cw[:, SSD_W:SSD_W + nbc],
        "conv_w_c": cw[:, SSD_W + nbc:],
        "conv_b_xs": _pad_groups(cb[:SSD_W]).reshape(1, SSD_PAD),
        "conv_b_b": cb[SSD_W:SSD_W + nbc].reshape(1, nbc),
        "conv_b_c": cb[SSD_W + nbc:].reshape(1, nbc),
        "dt_bias": _head_lanes(dt_bias[i]),
        "a_log": _head_lanes(a_log[i]),
        "d_skip": _pad_groups(jnp.repeat(d_skip[i], SSD_HEAD_DIM)).reshape(1, SSD_PAD),
        "ssd_norm_g": _pad_groups(ssd_norm_g[i]).reshape(1, SSD_PAD),
        "att_norm_g": att_norm_g[i].reshape(1, ATT_W),
        "gm_ln_g": gm_ln_g[i].reshape(1, GM_W),
        "gm_ln_b": gm_ln_b[i].reshape(1, GM_W),
        "gm_w_s": gm_w_s[i],
        "gm_b_s": jnp.repeat(gm_b_s[i].T, GM_W // GM_GROUPS, axis=1),
        "gm_norm_g": gm_norm_g[i].reshape(1, GM_W),
        "w_out_att": wo[:ATT_W].astype(BF16),
        "w_out_ssd": _pad_groups(wo[ATT_W:ATT_W + SSD_W].T).T.astype(BF16),
        "w_out_gm": wo[ATT_W + SSD_W:].astype(BF16),
        "ln1_g": ln1_g[i].reshape(1, D_MODEL),
        "ln1_b": ln1_b[i].reshape(1, D_MODEL),
    }


def _rope_tables(positions):
    inv = ROPE_THETA ** (-jnp.arange(0, ROPE_DIM, 2, dtype=F32) / ROPE_DIM)
    d = jnp.arange(LANES) % HEAD_DIM
    inv_lane = jnp.where(d < ROPE_DIM, inv[d % ROPE_HALF], 0.0)
    ang = positions.astype(F32)[..., None] * inv_lane
    cos, sin = jnp.cos(ang), jnp.sin(ang)
    return cos, jnp.where(d < ROPE_HALF, -sin, 0.0), jnp.where(d >= ROPE_HALF, sin, 0.0)


def kernel(x, positions, ln_in_g, ln_in_b, w_in, conv_w, conv_b, dt_bias, a_log, d_skip, ssd_norm_g, att_norm_g, gm_ln_g, gm_ln_b, gm_w_s, gm_b_s, gm_norm_g, w_out, ln1_g, ln1_b, ln2_g, ln2_b, ffn_w_gate, ffn_w_up, ffn_w_down, router_w, moe_w_gate, moe_w_up, moe_w_down):
    bsz, seq, _ = x.shape
    m = bsz * seq
    rope_c, rope_sa, rope_sb = _rope_tables(positions)
    h = x.reshape(m, D_MODEL)
    for i in range(DEPTH):
        p = _layer_params(i, w_in, conv_w, conv_b, dt_bias, a_log, d_skip, ssd_norm_g, att_norm_g, gm_ln_g,
                          gm_ln_b, gm_w_s, gm_b_s, gm_norm_g, w_out, ln1_g, ln1_b)
        if i == 0:
            proj, h = _in_proj(h, p["w_in"], ln=(ln_in_g, ln_in_b))
        else:
            (proj,) = _in_proj(h, p["w_in"])
        proj3 = proj.reshape(bsz, seq, D_PROJ)
        att = _attention(proj3, rope_c, rope_sa, rope_sb).reshape(m, ATT_W)
        ssd = _ssd(proj3, p).reshape(m, SSD_PAD)
        gm = _gmlp(proj, p)
        j = i // 2
        if i % 2 == 0:
            (h,) = _out_proj(att, ssd, gm, h, p)
            h = _ffn(h, ffn_w_gate[j].astype(BF16), ffn_w_up[j].astype(BF16), ffn_w_down[j].astype(BF16),
                     ln2_g[i], ln2_b[i])
        else:
            h, logits = _out_proj(att, ssd, gm, h, p, w_router=router_w[j])
            h = _moe_layer(h, logits, moe_w_gate[j], moe_w_up[j], moe_w_down[j], ln2_g[i], ln2_b[i])
    return h.reshape(bsz, seq, D_MODEL)
```

```python
import functools

import jax
import jax.numpy as jnp
from jax import lax
from jax.experimental import pallas as pl
from jax.experimental.pallas import tpu as pltpu

F32 = jnp.float32
BF16 = jnp.bfloat16
HIGHEST = lax.Precision.HIGHEST

D_MODEL = 1024
ATT_HEADS = 6
HEAD_DIM = 64
ATT_W = ATT_HEADS * HEAD_DIM
ROPE_DIM = HEAD_DIM // 4
ROPE_HALF = ROPE_DIM // 2
ROPE_THETA = 500000.0
MOBA_BLOCK = 256
MOBA_TOPK = 3
SSD_HEADS = 6
SSD_HEAD_DIM = 64
SSD_W = SSD_HEADS * SSD_HEAD_DIM
SSD_GROUPS = 2
SSD_STATE = 128
SSD_CONV = 4
SSD_CHUNK = 128
GM_GROUPS = 4
GM_W = 256
GM_CHUNK = 128
N_EXPERTS = 8
DEPTH = 2
ALPHA = (2.0 * DEPTH) ** 0.25
NEG = -1e30
LN_EPS = 1e-5

LANES = 128
GROUP_W = SSD_W // SSD_GROUPS
GROUP_PAD = 256
SSD_PAD = SSD_GROUPS * GROUP_PAD
HEADS_PER_GROUP = SSD_HEADS // SSD_GROUPS

COL_Z = 0
COL_XS = 512
COL_B = 1024
COL_C = 1280
COL_GU = 1536
COL_GV = 1792
COL_Q = 2048
COL_K = 2432
COL_V = 2816
COL_DT = 3200
D_PROJ = 3328

VMEM_LIMIT = 56 * 1024 * 1024


def _params(sem, vmem=VMEM_LIMIT):
    return pltpu.CompilerParams(dimension_semantics=sem, vmem_limit_bytes=vmem)


def _layer_norm(x, g, b):
    mu = jnp.mean(x, axis=-1, keepdims=True)
    xc = x - mu
    var = jnp.mean(xc * xc, axis=-1, keepdims=True)
    return xc * lax.rsqrt(var + LN_EPS) * g + b


def _silu(x):
    half = 0.5 * x
    return half + half * jnp.tanh(half)


def _split_bf16(x):
    hi = x.astype(BF16)
    return hi, (x - hi.astype(F32)).astype(BF16)


def _dot_bf16x3(a, b):
    a_hi, a_lo = _split_bf16(a)
    b_hi, b_lo = _split_bf16(b)
    dot = functools.partial(jnp.dot, preferred_element_type=F32)
    return dot(a_hi, b_hi) + (dot(a_lo, b_hi) + dot(a_hi, b_lo))


def _gelu_tanh(x):
    return 0.5 * x * (1.0 + jnp.tanh(0.7978845608028654 * (x + 0.044715 * x * x * x)))


PROJ_TM = 512
PROJ_CH = 256


def _in_proj_kernel(*refs, apply_ln):
    if apply_ln:
        x_ref, g_ref, b_ref, w_ref, proj_ref, h_ref = refs
        x = _layer_norm(x_ref[...], g_ref[...], b_ref[...])
        h_ref[...] = x
    else:
        x_ref, w_ref, proj_ref = refs
        x = x_ref[...]
    xb = x.astype(BF16)
    for j in range(D_PROJ // PROJ_CH):
        cols = slice(j * PROJ_CH, (j + 1) * PROJ_CH)
        proj_ref[:, cols] = jnp.dot(xb, w_ref[:, cols], preferred_element_type=F32)


def _in_proj(x, w, ln=None):
    m = x.shape[0]
    row = lambda i: (i, 0)
    const = lambda i: (0, 0)
    in_specs = [pl.BlockSpec((PROJ_TM, D_MODEL), row)]
    args = [x]
    out_shape = [jax.ShapeDtypeStruct((m, D_PROJ), F32)]
    out_specs = [pl.BlockSpec((PROJ_TM, D_PROJ), row)]
    if ln is not None:
        in_specs += [pl.BlockSpec((1, D_MODEL), const), pl.BlockSpec((1, D_MODEL), const)]
        args += [ln[0].reshape(1, D_MODEL), ln[1].reshape(1, D_MODEL)]
        out_shape.append(jax.ShapeDtypeStruct((m, D_MODEL), F32))
        out_specs.append(pl.BlockSpec((PROJ_TM, D_MODEL), row))
    in_specs.append(pl.BlockSpec((D_MODEL, D_PROJ), const))
    args.append(w)
    return pl.pallas_call(
        functools.partial(_in_proj_kernel, apply_ln=ln is not None),
        grid=(m // PROJ_TM,),
        in_specs=in_specs,
        out_specs=out_specs,
        out_shape=out_shape,
        compiler_params=_params(("parallel",)),
        name="in_proj",
    )(*args)


VT_ROWS = HEAD_DIM + 16
LOG2E = 1.4426950408889634
ATT_VALUE_LAG = 2


def _attn_kernel(q_ref, k_ref, v_ref, c_ref, sa_ref, sb_ref, o_ref, kb_ref, vt_ref, *, seq):
    nb = seq // MOBA_BLOCK
    tq = MOBA_BLOCK
    lane = lax.broadcasted_iota(jnp.int32, (1, LANES), 1)

    def rope(x, rows):
        return (x * c_ref[0, rows, :] + pltpu.roll(x, LANES - ROPE_HALF, 1) * sa_ref[0, rows, :]
                + pltpu.roll(x, ROPE_HALF, 1) * sb_ref[0, rows, :])

    k = rope(k_ref[0], slice(None))
    kb_ref[...] = k.astype(BF16)
    k_mean = jnp.mean(k.reshape(nb, MOBA_BLOCK, LANES), axis=1)
    v_t = v_ref[0].T
    ones_rows = jnp.where(lax.broadcasted_iota(jnp.int32, (VT_ROWS - HEAD_DIM, seq), 0) == 0, 1.0, 0.0)
    for hh in range(2):
        vt_ref[hh * VT_ROWS:(hh + 1) * VT_ROWS, :] = jnp.concatenate(
            [v_t[hh * HEAD_DIM:(hh + 1) * HEAD_DIM], ones_rows], axis=0).astype(BF16)

    qscale = HEAD_DIM ** -0.5 * LOG2E
    key_i = lax.broadcasted_iota(jnp.int32, (tq, 2 * tq), 0)
    qry_i = lax.broadcasted_iota(jnp.int32, (tq, 2 * tq), 1)
    causal_t = key_i <= jnp.where(qry_i < tq, qry_i, qry_i - tq)
    blk_row = lax.broadcasted_iota(jnp.int32, (nb, 2 * tq), 0)

    def scores(j):
        rows = slice(j * tq, (j + 1) * tq)
        q = rope(q_ref[0, rows, :], rows) * qscale
        qh_t = jnp.concatenate([jnp.where(lane < HEAD_DIM, q, 0.0).T, jnp.where(lane >= HEAD_DIM, q, 0.0).T],
                               axis=1)
        s_t = jnp.dot(kb_ref[0:(j + 1) * tq, :], qh_t.astype(BF16), preferred_element_type=F32)
        gate_t = _dot_bf16x3(k_mean, qh_t) if j > MOBA_TOPK else None
        return s_t, gate_t

    def probs(j, s_t, gate_t):
        n_past = j * tq
        own = jnp.where(causal_t, s_t[n_past:], NEG)
        mx = jnp.max(own, axis=0, keepdims=True)
        shifts = []
        if j > 0:
            blk_max = [jnp.max(s_t[n * tq:(n + 1) * tq], axis=0, keepdims=True) for n in range(j)]
            if j > MOBA_TOPK:
                rank = jnp.zeros((nb, 2 * tq), F32)
                for m in range(j):
                    g_m = gate_t[m:m + 1, :]
                    beats = (g_m > gate_t) | ((g_m == gate_t) & (blk_row > m))
                    rank = rank + jnp.where(beats & (blk_row != m), 1.0, 0.0)
                bias = jnp.where((blk_row < j) & (rank >= MOBA_TOPK), NEG, 0.0)
                bias_n = [bias[n:n + 1, :] for n in range(j)]
            else:
                bias_n = [0.0] * j
            for n in range(j):
                mx = jnp.maximum(mx, blk_max[n] + bias_n[n])
            shifts = [mx - bias_n[n] for n in range(j)]
        parts = [jnp.exp2(s_t[n * tq:(n + 1) * tq] - shifts[n]).astype(BF16) for n in range(j)]
        parts.append(jnp.exp2(own - mx).astype(BF16))
        return jnp.concatenate(parts, axis=0) if j > 0 else parts[0]

    def values(j, p_t):
        heads = [jnp.dot(vt_ref[hh * VT_ROWS:(hh + 1) * VT_ROWS, 0:(j + 1) * tq], p_t[:, hh * tq:(hh + 1) * tq],
                         preferred_element_type=F32) for hh in range(2)]
        out_t = jnp.concatenate([o[0:HEAD_DIM] / o[HEAD_DIM:HEAD_DIM + 1] for o in heads], axis=0)
        o_ref[0, j * tq:(j + 1) * tq, :] = out_t.T

    chains = list(range(nb - 1, -1, -1))
    s_state = {}
    p_state = {}
    for step in range(len(chains) + ATT_VALUE_LAG):
        if step >= ATT_VALUE_LAG:
            j = chains[step - ATT_VALUE_LAG]
            values(j, p_state.pop(j))
        if step < len(chains):
            j = chains[step]
            s_state[j] = scores(j)
        if 1 <= step <= len(chains):
            j = chains[step - 1]
            p_state[j] = probs(j, *s_state.pop(j))


def _attention(proj3, rope_c, rope_sa, rope_sb):
    bsz, seq, _ = proj3.shape
    blk = (1, seq, LANES)
    col = lambda base: (lambda b, p: (b, 0, base // LANES + p))
    tab = lambda b, p: (b, 0, 0)
    return pl.pallas_call(
        functools.partial(_attn_kernel, seq=seq),
        grid=(bsz, ATT_W // LANES),
        in_specs=[pl.BlockSpec(blk, col(COL_Q)), pl.BlockSpec(blk, col(COL_K)), pl.BlockSpec(blk, col(COL_V)),
                  pl.BlockSpec(blk, tab), pl.BlockSpec(blk, tab), pl.BlockSpec(blk, tab)],
        out_specs=pl.BlockSpec(blk, lambda b, p: (b, 0, p)),
        out_shape=jax.ShapeDtypeStruct((bsz, seq, ATT_W), F32),
        scratch_shapes=[pltpu.VMEM((seq, LANES), BF16), pltpu.VMEM((2 * VT_ROWS, seq), BF16)],
        compiler_params=_params(("parallel", "arbitrary")),
        name="moba_attention",
    )(proj3, proj3, proj3, rope_c, rope_sa, rope_sb)


assert HEADS_PER_GROUP == 3 and GROUP_PAD == 2 * LANES and SSD_HEAD_DIM * 2 == LANES


def _expand_heads(small):
    low = lax.broadcasted_iota(jnp.int32, (1, LANES), 1) < SSD_HEAD_DIM
    slabs = []
    for g in range(SSD_GROUPS):
        h0 = g * HEADS_PER_GROUP
        slabs.append(jnp.where(low, small[:, h0:h0 + 1], small[:, h0 + 1:h0 + 2]))
        slabs.append(jnp.where(low, small[:, h0 + 2:h0 + 3], 0.0))
    return jnp.concatenate(slabs, axis=1)


def _ssd_kernel(z_ref, xs_ref, b_ref, c_ref, dt_ref, cwx_ref, cwb_ref, cwc_ref, cbx_ref, cbb_ref, cbc_ref,
                dtb_ref, alog_ref, dskip_ref, ng_ref, o_ref, state_ref, *, seq):
    q = SSD_CHUNK
    n = SSD_STATE
    state_ref[...] = jnp.zeros_like(state_ref)
    row_i = lax.broadcasted_iota(jnp.int32, (q, q), 0)
    col_i = lax.broadcasted_iota(jnp.int32, (q, q), 1)
    lower = row_i >= col_i
    tril = jnp.where(lower, 1.0, 0.0)
    slab = lax.broadcasted_iota(jnp.int32, (1, GROUP_PAD), 1) // SSD_HEAD_DIM
    a_neg = -jnp.exp(alog_ref[...])

    def conv_silu(ref, w_ref, bias_ref, c):
        start = pl.multiple_of(c * q, q)
        cur = ref[0, pl.ds(start, q), :]
        prev_start = pl.multiple_of(jnp.maximum(c * q - 8, 0), 8)
        prev = jnp.where(c > 0, ref[0, pl.ds(prev_start, 8), :], 0.0)
        full = jnp.concatenate([prev, cur], axis=0)
        y = bias_ref[...] + w_ref[SSD_CONV - 1:SSD_CONV, :] * cur
        for i in range(SSD_CONV - 1):
            shift = SSD_CONV - 1 - i
            y = y + w_ref[i:i + 1, :] * pltpu.roll(full, shift, 0)[8:, :]
        return _silu(y)

    def chunk(c, carry):
        start = pl.multiple_of(c * q, q)
        xs = conv_silu(xs_ref, cwx_ref, cbx_ref, c)
        bm = conv_silu(b_ref, cwb_ref, cbb_ref, c)
        cm = conv_silu(c_ref, cwc_ref, cbc_ref, c)
        x = dt_ref[0, pl.ds(start, q), :] + dtb_ref[...]
        dt = jnp.maximum(x, 0.0) + jnp.log(1.0 + jnp.exp(-jnp.abs(x)))
        a = dt * a_neg
        a_cum = jnp.dot(tril, a, precision=HIGHEST, preferred_element_type=F32)
        a_cum_t = a_cum.T
        dt_e = _expand_heads(dt)
        acum_e = _expand_heads(a_cum)
        alast_e = acum_e[q - 1:q, :]
        xdt = xs * dt_e
        decay_e = jnp.exp(alast_e - acum_e)
        y_parts = []
        for g in range(SSD_GROUPS):
            gl = slice(g * GROUP_PAD, (g + 1) * GROUP_PAD)
            nl = slice(g * n, (g + 1) * n)
            b_g = bm[:, nl].astype(BF16)
            b_gt = bm[:, nl].T.astype(BF16)
            c_g = cm[:, nl].astype(BF16)
            cb = lax.dot_general(c_g, b_g, (((1,), (1,)), ((), ())), preferred_element_type=F32)
            xdt_g = xdt[:, gl]
            xdt_b = xdt_g.astype(BF16)
            y_g = jnp.zeros((q, GROUP_PAD), F32)
            for r in range(HEADS_PER_GROUP):
                h = g * HEADS_PER_GROUP + r
                seg = a_cum[:, h:h + 1] - a_cum_t[h:h + 1, :]
                l_h = jnp.where(lower, jnp.exp(jnp.minimum(seg, 0.0)), 0.0)
                y_h = jnp.dot((cb * l_h).astype(BF16), xdt_b, preferred_element_type=F32)
                y_g = jnp.where(slab == r, y_h, y_g)
            prev = state_ref[g]
            y_off = jnp.dot(c_g, prev.astype(BF16), preferred_element_type=F32) * jnp.exp(acum_e[:, gl])
            new = jnp.dot(b_gt, (xdt_g * decay_e[:, gl]).astype(BF16), preferred_element_type=F32)
            state_ref[g] = prev * jnp.exp(alast_e[:, gl]) + new
            y_parts.append(y_g + y_off)
        y = jnp.concatenate(y_parts, axis=1) + dskip_ref[...] * xs
        yz = y * _silu(z_ref[0, pl.ds(start, q), :])
        outs = []
        for g in range(SSD_GROUPS):
            yg = yz[:, g * GROUP_PAD:(g + 1) * GROUP_PAD]
            ms = jnp.sum(yg * yg, axis=-1, keepdims=True) * (1.0 / GROUP_W)
            outs.append(yg * lax.rsqrt(ms + LN_EPS))
        o_ref[0, pl.ds(start, q), :] = (jnp.concatenate(outs, axis=1) * ng_ref[...]).astype(BF16)
        return carry

    lax.fori_loop(0, seq // q, chunk, 0)


def _ssd(proj3, p):
    bsz, seq, _ = proj3.shape
    slab = lambda width, base: pl.BlockSpec((1, seq, width), lambda b: (b, 0, base // width))
    const2 = lambda a: pl.BlockSpec(a.shape, lambda b: (0, 0))
    small = [p["conv_w_xs"], p["conv_w_b"], p["conv_w_c"], p["conv_b_xs"], p["conv_b_b"], p["conv_b_c"],
             p["dt_bias"], p["a_log"], p["d_skip"], p["ssd_norm_g"]]
    return pl.pallas_call(
        functools.partial(_ssd_kernel, seq=seq),
        grid=(bsz,),
        in_specs=[slab(SSD_PAD, COL_Z), slab(SSD_PAD, COL_XS), slab(2 * SSD_STATE, COL_B),
                  slab(2 * SSD_STATE, COL_C), slab(LANES, COL_DT)] + [const2(a) for a in small],
        out_specs=pl.BlockSpec((1, seq, SSD_PAD), lambda b: (b, 0, 0)),
        out_shape=jax.ShapeDtypeStruct((bsz, seq, SSD_PAD), BF16),
        scratch_shapes=[pltpu.VMEM((SSD_GROUPS, SSD_STATE, GROUP_PAD), F32)],
        compiler_params=_params(("parallel",)),
        name="ssd",
    )(proj3, proj3, proj3, proj3, proj3, *small)


GM_ROWS = 1024


def _gmlp_kernel(u_ref, v_ref, lng_ref, lnb_ref, ws_ref, bs_ref, ng_ref, o_ref):
    q = GM_CHUNK
    row_i = lax.broadcasted_iota(jnp.int32, (q, q), 0)
    col_i = lax.broadcasted_iota(jnp.int32, (q, q), 1)
    lower = row_i >= col_i
    grp = lax.broadcasted_iota(jnp.int32, (1, GM_W), 1) // (GM_W // GM_GROUPS)
    w = [jnp.where(lower, ws_ref[g], 0.0).astype(BF16) for g in range(GM_GROUPS)]
    for c in range(GM_ROWS // q):
        rows = slice(c * q, (c + 1) * q)
        u = _gelu_tanh(u_ref[rows, :])
        v = _layer_norm(_gelu_tanh(v_ref[rows, :]), lng_ref[...], lnb_ref[...])
        vb = v.astype(BF16)
        mixed = jnp.zeros((q, GM_W), F32)
        for g in range(GM_GROUPS):
            mixed = jnp.where(grp == g, jnp.dot(w[g], vb, preferred_element_type=F32), mixed)
        gm = u * (mixed + bs_ref[...])
        ms = jnp.mean(gm * gm, axis=-1, keepdims=True)
        o_ref[rows, :] = (gm * lax.rsqrt(ms + LN_EPS) * ng_ref[...]).astype(BF16)


def _gmlp(proj, p):
    m = proj.shape[0]
    const = lambda a: pl.BlockSpec(a.shape, lambda i: (0,) * a.ndim)
    small = [p["gm_ln_g"], p["gm_ln_b"], p["gm_w_s"], p["gm_b_s"], p["gm_norm_g"]]
    return pl.pallas_call(
        _gmlp_kernel,
        grid=(m // GM_ROWS,),
        in_specs=[pl.BlockSpec((GM_ROWS, GM_W), lambda i: (i, COL_GU // GM_W)),
                  pl.BlockSpec((GM_ROWS, GM_W), lambda i: (i, COL_GV // GM_W))] + [const(a) for a in small],
        out_specs=pl.BlockSpec((GM_ROWS, GM_W), lambda i: (i, 0)),
        out_shape=jax.ShapeDtypeStruct((m, GM_W), BF16),
        compiler_params=_params(("parallel",)),
        name="gmlp",
    )(proj, proj, *small)


OUT_TM = 512
LOGIT_ROWS = 16


def _out_proj_kernel(att_ref, ssd_ref, gm_ref, h_ref, ag_ref, wa_ref, ws_ref, wg_ref, g_ref, b_ref, o_ref,
                     router=None):
    att = att_ref[...]
    ms = jnp.mean(att * att, axis=-1, keepdims=True)
    att = att * lax.rsqrt(ms + LN_EPS) * ag_ref[...]
    mix = jnp.dot(att.astype(BF16), wa_ref[...], preferred_element_type=F32)
    mix = mix + jnp.dot(ssd_ref[...], ws_ref[...], preferred_element_type=F32)
    mix = mix + jnp.dot(gm_ref[...], wg_ref[...], preferred_element_type=F32)
    h1 = _layer_norm(ALPHA * h_ref[...] + mix, g_ref[...], b_ref[...])
    o_ref[...] = h1
    if router:
        wr_ref, logit_ref = router
        logit_ref[...] = _dot_bf16x3(wr_ref[...], h1.T)


def _out_proj_kernel_routed(att_ref, ssd_ref, gm_ref, h_ref, ag_ref, wa_ref, ws_ref, wg_ref, g_ref, b_ref, wr_ref,
                            o_ref, logit_ref):
    _out_proj_kernel(att_ref, ssd_ref, gm_ref, h_ref, ag_ref, wa_ref, ws_ref, wg_ref, g_ref, b_ref, o_ref,
                     router=(wr_ref, logit_ref))


def _out_proj(att, ssd, gm, h, p, w_router=None):
    m = h.shape[0]
    row = lambda i: (i, 0)
    const = lambda a: pl.BlockSpec(a.shape, lambda i: (0, 0))
    small = [p["att_norm_g"], p["w_out_att"], p["w_out_ssd"], p["w_out_gm"], p["ln1_g"], p["ln1_b"]]
    out_specs = [pl.BlockSpec((OUT_TM, D_MODEL), row)]
    out_shape = [jax.ShapeDtypeStruct((m, D_MODEL), F32)]
    body = _out_proj_kernel
    if w_router is not None:
        small.append(jnp.pad(w_router.T, ((0, LOGIT_ROWS - N_EXPERTS), (0, 0))))
        out_specs.append(pl.BlockSpec((LOGIT_ROWS, OUT_TM), lambda i: (0, i)))
        out_shape.append(jax.ShapeDtypeStruct((LOGIT_ROWS, m), F32))
        body = _out_proj_kernel_routed
    return pl.pallas_call(
        body,
        grid=(m // OUT_TM,),
        in_specs=[pl.BlockSpec((OUT_TM, ATT_W), row), pl.BlockSpec((OUT_TM, SSD_PAD), row),
                  pl.BlockSpec((OUT_TM, GM_W), row), pl.BlockSpec((OUT_TM, D_MODEL), row)]
                 + [const(a) for a in small],
        out_specs=out_specs,
        out_shape=out_shape,
        compiler_params=_params(("parallel",)),
        name="out_proj_ln1",
    )(att, ssd, gm, h, *small)


FFN_TM = 512


def _swiglu_chunks(acc_ref, xb, w_gate, w_up, w_down, widths):
    dot = functools.partial(jnp.dot, preferred_element_type=F32)
    bounds = [sum(widths[:c]) for c in range(len(widths) + 1)]
    chunk = lambda c: slice(bounds[c], bounds[c + 1])
    up = lambda c: (dot(xb, w_gate(chunk(c))), dot(xb, w_up(chunk(c))))
    pending = up(0)
    for c in range(len(widths)):
        hg, hu = pending
        if c + 1 < len(widths):
            pending = up(c + 1)
        acc_ref[...] += dot((_silu(hg) * hu).astype(BF16), w_down(chunk(c)))


FFN_CHUNK = 512


def _ffn_kernel(x_ref, wg_ref, wu_ref, wd_ref, g_ref, b_ref, o_ref, acc_ref):
    d_ff = wg_ref.shape[-1]
    widths = (FFN_CHUNK,) * (d_ff // FFN_CHUNK) + ((d_ff % FFN_CHUNK,) if d_ff % FFN_CHUNK else ())
    x = x_ref[...]
    acc_ref[...] = ALPHA * x
    _swiglu_chunks(acc_ref, x.astype(BF16), lambda c: wg_ref[:, c], lambda c: wu_ref[:, c],
                   lambda c: wd_ref[c, :], widths)
    o_ref[...] = _layer_norm(acc_ref[...], g_ref[...], b_ref[...])


def _ffn(x, wg, wu, wd, ln_g, ln_b):
    m = x.shape[0]
    resident = lambda a: pl.BlockSpec(a.shape, lambda i: (0, 0), pipeline_mode=pl.Buffered(1))
    return pl.pallas_call(
        _ffn_kernel,
        grid=(m // FFN_TM,),
        in_specs=[pl.BlockSpec((FFN_TM, D_MODEL), lambda i: (i, 0)), resident(wg), resident(wu), resident(wd),
                  pl.BlockSpec((1, D_MODEL), lambda i: (0, 0)),
                  pl.BlockSpec((1, D_MODEL), lambda i: (0, 0))],
        out_specs=pl.BlockSpec((FFN_TM, D_MODEL), lambda i: (i, 0)),
        out_shape=jax.ShapeDtypeStruct((m, D_MODEL), F32),
        scratch_shapes=[pltpu.VMEM((FFN_TM, D_MODEL), F32)],
        compiler_params=_params(("parallel",)),
        name="ffn_ln2",
    )(x, wg, wu, wd, ln_g.reshape(1, D_MODEL), ln_b.reshape(1, D_MODEL))


MOE_TR = 256
MOE_HALF = 512
MOE_UNITS = 3
MOE_TM = MOE_UNITS * MOE_HALF
MOE_TF = 512
MOE_CHUNKS = (256, 256)
SEG_ALIGN = 16
MOE_WIN_SMALL = 128
INFO_GATE = N_EXPERTS


def _expert_cap(m):
    n_tiles = m // MOE_TR
    rows = m + (SEG_ALIGN - 1) * n_tiles + MOE_TR + MOE_HALF
    return -(-rows // MOE_TM) * MOE_TM


def _rows_computed(lens):
    return jnp.maximum(-(-lens // MOE_HALF), 1) * MOE_HALF


def _dispatch_kernel(h_ref, logit_ref, info_ref, offs_ref, lens_ref, small_ref, xs_hbm, run_ref, xbuf_ref, zbuf_ref,
                     sem_ref, zsem_ref, *, n_tiles, cap):
    t = pl.program_id(0)
    tr = MOE_TR
    slot = t % 2

    @pl.when(t == 0)
    def _():
        for e in range(N_EXPERTS):
            run_ref[e] = 0

    xb = h_ref[...].astype(BF16)
    lg = logit_ref[0:N_EXPERTS, :]
    row = lax.broadcasted_iota(jnp.int32, (N_EXPERTS, tr), 0)
    m1 = jnp.max(lg, axis=0, keepdims=True)
    i1 = jnp.min(jnp.where(lg == m1, row, N_EXPERTS), axis=0, keepdims=True)
    rest = jnp.where(row == i1, -jnp.inf, lg)
    m2 = jnp.max(rest, axis=0, keepdims=True)
    i2 = jnp.min(jnp.where(rest == m2, row, N_EXPERTS), axis=0, keepdims=True)
    e2 = jnp.exp(m2 - m1)
    p1 = 1.0 / (1.0 + e2)
    gate_t = jnp.where(row == i1, p1, 0.0) + jnp.where(row == i2, e2 * p1, 0.0)
    sel_t = jnp.where((row == i1) | (row == i2), 1.0, 0.0)
    before = (lax.broadcasted_iota(jnp.int32, (tr, tr), 0) < lax.broadcasted_iota(jnp.int32, (tr, tr), 1))
    rank_t = jnp.dot(sel_t.astype(BF16), jnp.where(before, 1.0, 0.0).astype(BF16),
                     preferred_element_type=F32)
    cnt = jnp.sum(sel_t, axis=1, keepdims=True).astype(jnp.int32)
    info_t = jnp.concatenate([rank_t, gate_t, jnp.zeros((LANES - 2 * N_EXPERTS, tr), F32)], axis=0)
    info_ref[...] = info_t.T

    small = jnp.max(cnt) <= MOE_WIN_SMALL
    small_ref[t] = small.astype(jnp.int32)

    def seg_copy(sl, e, off, rows):
        return pltpu.make_async_copy(xbuf_ref.at[sl, e, pl.ds(0, rows)],
                                     xs_hbm.at[pl.ds(pl.multiple_of(off, SEG_ALIGN), rows)], sem_ref.at[sl, e])

    def by_window(flag, fn):
        pl.when(flag)(functools.partial(fn, MOE_WIN_SMALL))
        pl.when(jnp.logical_not(flag))(functools.partial(fn, tr))

    def compact(rows):
        dst_row = lax.broadcasted_iota(jnp.int32, (rows, tr), 0).astype(F32)
        onehots = [jnp.where((rank_t[e:e + 1, :] == dst_row) & (sel_t[e:e + 1, :] > 0.0), 1.0, 0.0).astype(BF16)
                   for e in range(N_EXPERTS)]
        onehot_all = jnp.concatenate(onehots, axis=0)
        half = D_MODEL // 2
        for c in range(2):
            packed = jnp.dot(onehot_all, xb[:, c * half:(c + 1) * half], preferred_element_type=F32)
            xbuf_ref[slot, :, 0:rows, c * half:(c + 1) * half] = packed.astype(BF16).reshape(N_EXPERTS, rows, half)

    by_window(small, compact)

    def wait_previous(rows):
        for e in range(N_EXPERTS):
            seg_copy(1 - slot, e, 0, rows).wait()

    @pl.when(t > 0)
    def _():
        by_window(small_ref[jnp.maximum(t - 1, 0)] == 1, wait_previous)

    def start_all(rows):
        for e in range(N_EXPERTS):
            seg_copy(slot, e, e * cap + run_ref[e], rows).start()

    by_window(small, start_all)
    for e in range(N_EXPERTS):
        offs_ref[t * N_EXPERTS + e] = e * cap + run_ref[e]
        run_ref[e] = run_ref[e] + jnp.bitwise_and(cnt[e, 0] + (SEG_ALIGN - 1), -SEG_ALIGN)

    def wait_current(rows):
        for e in range(N_EXPERTS):
            seg_copy(slot, e, 0, rows).wait()

    @pl.when(t == n_tiles - 1)
    def _():
        zbuf_ref[...] = jnp.zeros_like(zbuf_ref)
        by_window(small, wait_current)
        tails = []
        for e in range(N_EXPERTS):
            lens_ref[e] = run_ref[e]
            tail = pl.multiple_of(e * cap + run_ref[e], SEG_ALIGN)
            tails.append(pltpu.make_async_copy(zbuf_ref, xs_hbm.at[pl.ds(tail, MOE_HALF)], zsem_ref.at[e]))
            tails[-1].start()
        for cp in tails:
            cp.wait()


def _dispatch(h, logits):
    m = h.shape[0]
    n_tiles = m // MOE_TR
    cap = _expert_cap(m)
    smem = pl.BlockSpec(memory_space=pltpu.SMEM)
    return pl.pallas_call(
        functools.partial(_dispatch_kernel, n_tiles=n_tiles, cap=cap),
        grid=(n_tiles,),
        in_specs=[pl.BlockSpec((MOE_TR, D_MODEL), lambda t: (t, 0)),
                  pl.BlockSpec((LOGIT_ROWS, MOE_TR), lambda t: (0, t))],
        out_specs=[pl.BlockSpec((MOE_TR, LANES), lambda t: (t, 0)), smem, smem, smem,
                   pl.BlockSpec(memory_space=pl.ANY)],
        out_shape=[jax.ShapeDtypeStruct((m, LANES), F32),
                   jax.ShapeDtypeStruct((n_tiles * N_EXPERTS,), jnp.int32),
                   jax.ShapeDtypeStruct((N_EXPERTS,), jnp.int32),
                   jax.ShapeDtypeStruct((n_tiles,), jnp.int32),
                   jax.ShapeDtypeStruct((N_EXPERTS * cap, D_MODEL), BF16)],
        scratch_shapes=[pltpu.SMEM((N_EXPERTS,), jnp.int32),
                        pltpu.VMEM((2, N_EXPERTS, MOE_TR, D_MODEL), BF16),
                        pltpu.VMEM((MOE_HALF, D_MODEL), BF16),
                        pltpu.SemaphoreType.DMA((2, N_EXPERTS)),
                        pltpu.SemaphoreType.DMA((N_EXPERTS,))],
        compiler_params=_params(("arbitrary",)),
        name="moe_dispatch",
    )(h, logits)


def _expert_ffn_kernel(exp_ref, blk_ref, units_ref, x_ref, wg_ref, wu_ref, wd_ref, y_ref, acc_ref):
    w = pl.program_id(0)
    f = pl.program_id(1)

    def tile(rows):
        acc = acc_ref.at[pl.ds(0, rows)]

        @pl.when(f == 0)
        def _():
            acc[...] = jnp.zeros((rows, D_MODEL), F32)

        _swiglu_chunks(acc, x_ref[0:rows, :], lambda c: wg_ref[0, :, c].astype(BF16),
                       lambda c: wu_ref[0, :, c].astype(BF16), lambda c: wd_ref[0, c, :].astype(BF16), MOE_CHUNKS)

        @pl.when(f == pl.num_programs(1) - 1)
        def _():
            y_ref[0:rows, :] = acc[...].astype(BF16)

    for units in range(1, MOE_UNITS + 1):
        pl.when(units_ref[w] == units)(functools.partial(tile, units * MOE_HALF))


def _expert_ffn(xs, exp_w, blk_w, units_w, w_gate, w_up, w_down):
    d_ff = w_gate.shape[-1]
    n_f = d_ff // MOE_TF
    fcol = lambda w, f, exp, blk, units: jnp.where(units[w] != 0, f, n_f - 1)
    grid_spec = pltpu.PrefetchScalarGridSpec(
        num_scalar_prefetch=3,
        grid=(exp_w.shape[0], n_f),
        in_specs=[pl.BlockSpec((MOE_TM, D_MODEL), lambda w, f, exp, blk, valid: (blk[w], 0)),
                  pl.BlockSpec((1, D_MODEL, MOE_TF), lambda w, f, exp, blk, valid: (exp[w], 0, fcol(w, f, exp, blk, valid))),
                  pl.BlockSpec((1, D_MODEL, MOE_TF), lambda w, f, exp, blk, valid: (exp[w], 0, fcol(w, f, exp, blk, valid))),
                  pl.BlockSpec((1, MOE_TF, D_MODEL), lambda w, f, exp, blk, valid: (exp[w], fcol(w, f, exp, blk, valid), 0))],
        out_specs=pl.BlockSpec((MOE_TM, D_MODEL), lambda w, f, exp, blk, valid: (blk[w], 0)),
        scratch_shapes=[pltpu.VMEM((MOE_TM, D_MODEL), F32)])
    return pl.pallas_call(
        _expert_ffn_kernel,
        grid_spec=grid_spec,
        out_shape=jax.ShapeDtypeStruct(xs.shape, BF16),
        compiler_params=_params(("arbitrary", "arbitrary")),
        name="moe_expert_ffn",
    )(exp_w, blk_w, units_w, xs, w_gate, w_up, w_down)


def _combine_kernel(starts_ref, shifts_ref, small_ref, info_ref, h_ref, g_ref, b_ref, ys_hbm, o_ref, ybuf_ref,
                    sem_ref, *, n_tiles):
    t = pl.program_id(0)
    tr = MOE_TR
    slot = t % 2

    def seg_copy(tile, sl, e, rows):
        off = pl.multiple_of(starts_ref[tile * N_EXPERTS + e], SEG_ALIGN)
        return pltpu.make_async_copy(ys_hbm.at[pl.ds(off, rows)], ybuf_ref.at[sl, e, pl.ds(0, rows)],
                                     sem_ref.at[sl, e])

    def by_window(tile, fn):
        flag = small_ref[tile] == 1
        pl.when(flag)(functools.partial(fn, MOE_WIN_SMALL))
        pl.when(jnp.logical_not(flag))(functools.partial(fn, tr))

    def fetch(tile, sl):
        def start(rows):
            for e in range(N_EXPERTS):
                seg_copy(tile, sl, e, rows).start()
        by_window(tile, start)

    @pl.when(t == 0)
    def _():
        fetch(0, 0)

    @pl.when(t + 1 < n_tiles)
    def _():
        fetch(jnp.minimum(t + 1, n_tiles - 1), 1 - slot)

    def gather(rows):
        for e in range(N_EXPERTS):
            seg_copy(t, slot, e, rows).wait()
        info = info_ref[...]
        src_row = lax.broadcasted_iota(jnp.int32, (tr, rows), 1).astype(F32)
        acc = ALPHA * h_ref[...]
        for e in range(N_EXPERTS):
            rank_c = info[:, e:e + 1] + shifts_ref[t * N_EXPERTS + e].astype(F32)
            gate_c = info[:, INFO_GATE + e:INFO_GATE + e + 1]
            onehot = jnp.where((rank_c == src_row) & (gate_c != 0.0), 1.0, 0.0).astype(BF16)
            acc = acc + gate_c * jnp.dot(onehot, ybuf_ref[slot, e, 0:rows, :], preferred_element_type=F32)
        o_ref[...] = _layer_norm(acc, g_ref[...], b_ref[...])

    by_window(t, gather)


def _combine(starts, shifts, small, info, h, ys, ln_g, ln_b):
    m = h.shape[0]
    n_tiles = m // MOE_TR
    grid_spec = pltpu.PrefetchScalarGridSpec(
        num_scalar_prefetch=3,
        grid=(n_tiles,),
        in_specs=[pl.BlockSpec((MOE_TR, LANES), lambda t, *_: (t, 0)),
                  pl.BlockSpec((MOE_TR, D_MODEL), lambda t, *_: (t, 0)),
                  pl.BlockSpec((1, D_MODEL), lambda t, *_: (0, 0)),
                  pl.BlockSpec((1, D_MODEL), lambda t, *_: (0, 0)),
                  pl.BlockSpec(memory_space=pl.ANY)],
        out_specs=pl.BlockSpec((MOE_TR, D_MODEL), lambda t, *_: (t, 0)),
        scratch_shapes=[pltpu.VMEM((2, N_EXPERTS, MOE_TR, D_MODEL), BF16),
                        pltpu.SemaphoreType.DMA((2, N_EXPERTS))])
    return pl.pallas_call(
        functools.partial(_combine_kernel, n_tiles=n_tiles),
        grid_spec=grid_spec,
        out_shape=jax.ShapeDtypeStruct((m, D_MODEL), F32),
        compiler_params=_params(("arbitrary",)),
        name="moe_combine_ln2",
    )(starts, shifts, small, info, h, ln_g.reshape(1, D_MODEL), ln_b.reshape(1, D_MODEL), ys)


def _work_list(lens, m):
    cap = _expert_cap(m)
    n_tiles = m // MOE_TR
    max_rows = 2 * m + (SEG_ALIGN - 1) * min(N_EXPERTS * n_tiles, 2 * m)
    w_max = -(-(max_rows + N_EXPERTS * MOE_HALF) // MOE_TM) + N_EXPERTS
    rows_e = _rows_computed(lens)
    tiles_e = -(-rows_e // MOE_TM)
    ends = jnp.cumsum(tiles_e)
    w = jnp.arange(w_max, dtype=jnp.int32)
    wc = jnp.minimum(w, ends[-1] - 1)
    exp_w = jnp.sum((wc[:, None] >= ends[None, :]).astype(jnp.int32), axis=1)
    tile_w = wc - (ends - tiles_e)[exp_w]
    blk_w = exp_w * (cap // MOE_TM) + tile_w
    units_w = jnp.minimum((rows_e[exp_w] - tile_w * MOE_TM) // MOE_HALF, MOE_UNITS)
    units_w = jnp.where(w < ends[-1], units_w, 0)
    return exp_w.astype(jnp.int32), blk_w.astype(jnp.int32), units_w.astype(jnp.int32)


def _read_windows(offs, small, lens, m):
    cap = _expert_cap(m)
    region = jnp.arange(N_EXPERTS, dtype=jnp.int32) * cap
    rel = offs.reshape(-1, N_EXPERTS) - region
    width = jnp.where(small == 1, MOE_WIN_SMALL, MOE_TR)[:, None]
    start = jnp.minimum(rel, _rows_computed(lens)[None, :] - width)
    return (start + region).reshape(-1).astype(jnp.int32), (rel - start).reshape(-1).astype(jnp.int32)


def _moe_layer(h, logits, w_gate, w_up, w_down, ln_g, ln_b):
    m = h.shape[0]
    info, offs, lens, small, xs = _dispatch(h, logits)
    exp_w, blk_w, units_w = _work_list(lens, m)
    ys = _expert_ffn(xs, exp_w, blk_w, units_w, w_gate, w_up, w_down)
    starts, shifts = _read_windows(offs, small, lens, m)
    return _combine(starts, shifts, small, info, h, ys, ln_g, ln_b)


def _pad_groups(a):
    lead = a.shape[:-1]
    a = a.reshape(*lead, SSD_GROUPS, GROUP_W)
    a = jnp.pad(a, [(0, 0)] * len(lead) + [(0, 0), (0, GROUP_PAD - GROUP_W)])
    return a.reshape(*lead, SSD_PAD)


def _head_lanes(v):
    return jnp.pad(v, (0, LANES - SSD_HEADS)).reshape(1, LANES)


D_IN = 3 * ATT_W + SSD_W + (SSD_W + 2 * SSD_GROUPS * SSD_STATE) + SSD_HEADS + 2 * GM_W
PERM_ROWS = 256


def _permute_w_kernel(w_ref, o_ref):
    src_z = 3 * ATT_W
    src_xs = src_z + SSD_W
    src_b = src_xs + SSD_W
    src_c = src_b + SSD_GROUPS * SSD_STATE
    src_dt = src_c + SSD_GROUPS * SSD_STATE
    src_gu = src_dt + SSD_HEADS
    src_gv = src_gu + GM_W
    o_ref[...] = jnp.zeros_like(o_ref)

    def put(dst, src, width):
        o_ref[:, dst:dst + width] = w_ref[0, :, src:src + width].astype(BF16)

    for g in range(SSD_GROUPS):
        put(COL_Z + g * GROUP_PAD, src_z + g * GROUP_W, GROUP_W)
        put(COL_XS + g * GROUP_PAD, src_xs + g * GROUP_W, GROUP_W)
    put(COL_B, src_b, SSD_GROUPS * SSD_STATE)
    put(COL_C, src_c, SSD_GROUPS * SSD_STATE)
    put(COL_GU, src_gu, GM_W)
    put(COL_GV, src_gv, GM_W)
    put(COL_Q, 0, ATT_W)
    put(COL_K, ATT_W, ATT_W)
    put(COL_V, 2 * ATT_W, ATT_W)
    put(COL_DT, src_dt, SSD_HEADS)


def _permute_w_in(w_in, layer):
    return pl.pallas_call(
        _permute_w_kernel,
        grid=(D_MODEL // PERM_ROWS,),
        in_specs=[pl.BlockSpec((1, PERM_ROWS, D_IN), lambda i: (layer, i, 0))],
        out_specs=pl.BlockSpec((PERM_ROWS, D_PROJ), lambda i: (i, 0)),
        out_shape=jax.ShapeDtypeStruct((D_MODEL, D_PROJ), BF16),
        compiler_params=_params(("parallel",)),
        name="permute_w_in",
    )(w_in)


def _layer_params(i, w_in, conv_w, conv_b, dt_bias, a_log, d_skip, ssd_norm_g, att_norm_g, gm_ln_g, gm_ln_b,
                  gm_w_s, gm_b_s, gm_norm_g, w_out, ln1_g, ln1_b):
    cw, cb = conv_w[i], conv_b[i]
    nbc = SSD_GROUPS * SSD_STATE
    wo = w_out[i]
    return {
        "w_in": _permute_w_in(w_in, i),
        "conv_w_xs": _pad_groups(cw[:, :SSD_W]),
        "conv_w_b": cw[:, SSD_W:SSD_W + nbc],
        "conv_w_c": cw[:, SSD_W + nbc:],
        "conv_b_xs": _pad_groups(cb[:SSD_W]).reshape(1, SSD_PAD),
        "conv_b_b": cb[SSD_W:SSD_W + nbc].reshape(1, nbc),
        "conv_b_c": cb[SSD_W + nbc:].reshape(1, nbc),
        "dt_bias": _head_lanes(dt_bias[i]),
        "a_log": _head_lanes(a_log[i]),
        "d_skip": _pad_groups(jnp.repeat(d_skip[i], SSD_HEAD_DIM)).reshape(1, SSD_PAD),
        "ssd_norm_g": _pad_groups(ssd_norm_g[i]).reshape(1, SSD_PAD),
        "att_norm_g": att_norm_g[i].reshape(1, ATT_W),
        "gm_ln_g": gm_ln_g[i].reshape(1, GM_W),
        "gm_ln_b": gm_ln_b[i].reshape(1, GM_W),
        "gm_w_s": gm_w_s[i],
        "gm_b_s": jnp.repeat(gm_b_s[i].T, GM_W // GM_GROUPS, axis=1),
        "gm_norm_g": gm_norm_g[i].reshape(1, GM_W),
        "w_out_att": wo[:ATT_W].astype(BF16),
        "w_out_ssd": _pad_groups(wo[ATT_W:ATT_W + SSD_W].T).T.astype(BF16),
        "w_out_gm": wo[ATT_W + SSD_W:].astype(BF16),
        "ln1_g": ln1_g[i].reshape(1, D_MODEL),
        "ln1_b": ln1_b[i].reshape(1, D_MODEL),
    }


def _rope_tables(positions):
    inv = ROPE_THETA ** (-jnp.arange(0, ROPE_DIM, 2, dtype=F32) / ROPE_DIM)
    d = jnp.arange(LANES) % HEAD_DIM
    inv_lane = jnp.where(d < ROPE_DIM, inv[d % ROPE_HALF], 0.0)
    ang = positions.astype(F32)[..., None] * inv_lane
    cos, sin = jnp.cos(ang), jnp.sin(ang)
    return cos, jnp.where(d < ROPE_HALF, -sin, 0.0), jnp.where(d >= ROPE_HALF, sin, 0.0)


def kernel(x, positions, ln_in_g, ln_in_b, w_in, conv_w, conv_b, dt_bias, a_log, d_skip, ssd_norm_g, att_norm_g, gm_ln_g, gm_ln_b, gm_w_s, gm_b_s, gm_norm_g, w_out, ln1_g, ln1_b, ln2_g, ln2_b, ffn_w_gate, ffn_w_up, ffn_w_down, router_w, moe_w_gate, moe_w_up, moe_w_down):
    bsz, seq, _ = x.shape
    m = bsz * seq
    rope_c, rope_sa, rope_sb = _rope_tables(positions)
    h = x.reshape(m, D_MODEL)
    for i in range(DEPTH):
        p = _layer_params(i, w_in, conv_w, conv_b, dt_bias, a_log, d_skip, ssd_norm_g, att_norm_g, gm_ln_g,
                          gm_ln_b, gm_w_s, gm_b_s, gm_norm_g, w_out, ln1_g, ln1_b)
        if i == 0:
            proj, h = _in_proj(h, p["w_in"], ln=(ln_in_g, ln_in_b))
        else:
            (proj,) = _in_proj(h, p["w_in"])
        proj3 = proj.reshape(bsz, seq, D_PROJ)
        att = _attention(proj3, rope_c, rope_sa, rope_sb).reshape(m, ATT_W)
        ssd = _ssd(proj3, p).reshape(m, SSD_PAD)
        gm = _gmlp(proj, p)
        j = i // 2
        if i % 2 == 0:
            (h,) = _out_proj(att, ssd, gm, h, p)
            h = _ffn(h, ffn_w_gate[j].astype(BF16), ffn_w_up[j].astype(BF16), ffn_w_down[j].astype(BF16),
                     ln2_g[i], ln2_b[i])
        else:
            h, logits = _out_proj(att, ssd, gm, h, p, w_router=router_w[j])
            h = _moe_layer(h, logits, moe_w_gate[j], moe_w_up[j], moe_w_down[j], ln2_g[i], ln2_b[i])
    return h.reshape(bsz, seq, D_MODEL)
```

```python
import functools

import jax
import jax.numpy as jnp
from jax import lax
from jax.experimental import pallas as pl
from jax.experimental.pallas import tpu as pltpu

F32 = jnp.float32
BF16 = jnp.bfloat16
HIGHEST = lax.Precision.HIGHEST

D_MODEL = 1024
ATT_HEADS = 6
HEAD_DIM = 64
ATT_W = ATT_HEADS * HEAD_DIM
ROPE_DIM = HEAD_DIM // 4
ROPE_HALF = ROPE_DIM // 2
ROPE_THETA = 500000.0
MOBA_BLOCK = 256
MOBA_TOPK = 3
SSD_HEADS = 6
SSD_HEAD_DIM = 64
SSD_W = SSD_HEADS * SSD_HEAD_DIM
SSD_GROUPS = 2
SSD_STATE = 128
SSD_CONV = 4
SSD_CHUNK = 128
GM_GROUPS = 4
GM_W = 256
GM_CHUNK = 128
N_EXPERTS = 8
DEPTH = 2
ALPHA = (2.0 * DEPTH) ** 0.25
NEG = -1e30
LN_EPS = 1e-5

LANES = 128
GROUP_W = SSD_W // SSD_GROUPS
GROUP_PAD = 256
SSD_PAD = SSD_GROUPS * GROUP_PAD
HEADS_PER_GROUP = SSD_HEADS // SSD_GROUPS

COL_Z = 0
COL_XS = 512
COL_B = 1024
COL_C = 1280
COL_GU = 1536
COL_GV = 1792
COL_Q = 2048
COL_K = 2432
COL_V = 2816
COL_DT = 3200
D_PROJ = 3328

VMEM_LIMIT = 56 * 1024 * 1024


def _params(sem, vmem=VMEM_LIMIT):
    return pltpu.CompilerParams(dimension_semantics=sem, vmem_limit_bytes=vmem)


def _layer_norm(x, g, b):
    mu = jnp.mean(x, axis=-1, keepdims=True)
    xc = x - mu
    var = jnp.mean(xc * xc, axis=-1, keepdims=True)
    return xc * lax.rsqrt(var + LN_EPS) * g + b


def _silu(x):
    half = 0.5 * x
    return half + half * jnp.tanh(half)


def _split_bf16(x):
    hi = x.astype(BF16)
    return hi, (x - hi.astype(F32)).astype(BF16)


def _dot_bf16x3(a, b):
    a_hi, a_lo = _split_bf16(a)
    b_hi, b_lo = _split_bf16(b)
    dot = functools.partial(jnp.dot, preferred_element_type=F32)
    return dot(a_hi, b_hi) + (dot(a_lo, b_hi) + dot(a_hi, b_lo))


def _gelu_tanh(x):
    return 0.5 * x * (1.0 + jnp.tanh(0.7978845608028654 * (x + 0.044715 * x * x * x)))


PROJ_TM = 512
PROJ_CH = 256


def _in_proj_kernel(*refs, apply_ln):
    if apply_ln:
        x_ref, g_ref, b_ref, w_ref, proj_ref, h_ref = refs
        x = _layer_norm(x_ref[...], g_ref[...], b_ref[...])
        h_ref[...] = x
    else:
        x_ref, w_ref, proj_ref = refs
        x = x_ref[...]
    xb = x.astype(BF16)
    for j in range(D_PROJ // PROJ_CH):
        cols = slice(j * PROJ_CH, (j + 1) * PROJ_CH)
        proj_ref[:, cols] = jnp.dot(xb, w_ref[:, cols], preferred_element_type=F32)


def _in_proj(x, w, ln=None):
    m = x.shape[0]
    row = lambda i: (i, 0)
    const = lambda i: (0, 0)
    in_specs = [pl.BlockSpec((PROJ_TM, D_MODEL), row)]
    args = [x]
    out_shape = [jax.ShapeDtypeStruct((m, D_PROJ), F32)]
    out_specs = [pl.BlockSpec((PROJ_TM, D_PROJ), row)]
    if ln is not None:
        in_specs += [pl.BlockSpec((1, D_MODEL), const), pl.BlockSpec((1, D_MODEL), const)]
        args += [ln[0].reshape(1, D_MODEL), ln[1].reshape(1, D_MODEL)]
        out_shape.append(jax.ShapeDtypeStruct((m, D_MODEL), F32))
        out_specs.append(pl.BlockSpec((PROJ_TM, D_MODEL), row))
    in_specs.append(pl.BlockSpec((D_MODEL, D_PROJ), const))
    args.append(w)
    return pl.pallas_call(
        functools.partial(_in_proj_kernel, apply_ln=ln is not None),
        grid=(m // PROJ_TM,),
        in_specs=in_specs,
        out_specs=out_specs,
        out_shape=out_shape,
        compiler_params=_params(("parallel",)),
        name="in_proj",
    )(*args)


VT_ROWS = HEAD_DIM + 16
LOG2E = 1.4426950408889634
ATT_VALUE_LAG = 6


def _attn_kernel(q_ref, k_ref, v_ref, c_ref, sa_ref, sb_ref, o_ref, kb_ref, vt_ref, *, seq):
    nb = seq // MOBA_BLOCK
    tq = MOBA_BLOCK
    lane = lax.broadcasted_iota(jnp.int32, (1, LANES), 1)

    def rope(x, rows):
        return (x * c_ref[0, rows, :] + pltpu.roll(x, LANES - ROPE_HALF, 1) * sa_ref[0, rows, :]
                + pltpu.roll(x, ROPE_HALF, 1) * sb_ref[0, rows, :])

    k = rope(k_ref[0], slice(None))
    kb_ref[...] = k.astype(BF16)
    k_mean = jnp.mean(k.reshape(nb, MOBA_BLOCK, LANES), axis=1)
    v_t = v_ref[0].T
    ones_rows = jnp.where(lax.broadcasted_iota(jnp.int32, (VT_ROWS - HEAD_DIM, seq), 0) == 0, 1.0, 0.0)
    for hh in range(2):
        vt_ref[hh] = jnp.concatenate([v_t[hh * HEAD_DIM:(hh + 1) * HEAD_DIM], ones_rows], axis=0).astype(BF16)

    qscale = HEAD_DIM ** -0.5 * LOG2E
    causal_t = (lax.broadcasted_iota(jnp.int32, (tq, tq), 0) <= lax.broadcasted_iota(jnp.int32, (tq, tq), 1))
    blk_row = lax.broadcasted_iota(jnp.int32, (nb, tq), 0)

    chain_q = {}
    chain_bias = {}
    chain_m = {}
    chain_o = {}
    q_tiles = {}

    def start_chain(j, hh):
        if hh == 0:
            rows = slice(j * tq, (j + 1) * tq)
            q_tiles[j] = rope(q_ref[0, rows, :], rows) * qscale
        q = q_tiles[j] if hh == 0 else q_tiles.pop(j)
        head_lanes = (lane < HEAD_DIM) if hh == 0 else (lane >= HEAD_DIM)
        qh_t = jnp.where(head_lanes, q, 0.0).T
        chain_q[(j, hh)] = qh_t.astype(BF16)
        chain_bias[(j, hh)] = None
        if j > MOBA_TOPK:
            gate_t = _dot_bf16x3(k_mean, qh_t)
            rank = jnp.zeros((nb, tq), F32)
            for m in range(j):
                g_m = gate_t[m:m + 1, :]
                beats = (g_m > gate_t) | ((g_m == gate_t) & (blk_row > m))
                rank = rank + jnp.where(beats & (blk_row != m), 1.0, 0.0)
            bias = jnp.where((blk_row < j) & (rank >= MOBA_TOPK), NEG, 0.0)
            chain_bias[(j, hh)] = [bias[n:n + 1, :] for n in range(j)]

    def scores(j, hh, n):
        return jnp.dot(kb_ref[n * tq:(n + 1) * tq, :], chain_q[(j, hh)], preferred_element_type=F32)

    def probs(j, hh, n, s_t):
        key = (j, hh)
        if n == j:
            s_t = jnp.where(causal_t, s_t, NEG)
            m_new = jnp.max(s_t, axis=0, keepdims=True)
            alpha = None
            shift = m_new
        else:
            bias_n = 0.0 if chain_bias[key] is None else chain_bias[key][n]
            m_new = jnp.maximum(chain_m[key], jnp.max(s_t, axis=0, keepdims=True) + bias_n)
            alpha = jnp.exp2(chain_m[key] - m_new)
            shift = m_new - bias_n
        chain_m[key] = m_new
        return jnp.exp2(s_t - shift).astype(BF16), alpha

    def values(j, hh, n, p_t, alpha):
        key = (j, hh)
        part = jnp.dot(vt_ref[hh, :, n * tq:(n + 1) * tq], p_t, preferred_element_type=F32)
        chain_o[key] = part if alpha is None else chain_o[key] * alpha + part

    def finish(j):
        halves = [chain_o.pop((j, hh)) for hh in range(2)]
        out_t = jnp.concatenate([o[0:HEAD_DIM] / o[HEAD_DIM:HEAD_DIM + 1] for o in halves], axis=0)
        o_ref[0, j * tq:(j + 1) * tq, :] = out_t.T

    units = []
    for j in range(nb):
        for n in [j] + list(range(j)):
            units += [(j, 0, n), (j, 1, n)]
    s_state = {}
    p_state = {}
    for step in range(len(units) + ATT_VALUE_LAG):
        if step >= ATT_VALUE_LAG:
            j, hh, n = unit = units[step - ATT_VALUE_LAG]
            values(j, hh, n, *p_state.pop(unit))
            last_block = n == (j - 1 if j > 0 else 0)
            if hh == 1 and last_block:
                finish(j)
        if step < len(units):
            j, hh, n = unit = units[step]
            if n == j:
                start_chain(j, hh)
            s_state[unit] = scores(j, hh, n)
        if 1 <= step <= len(units):
            j, hh, n = unit = units[step - 1]
            p_state[unit] = probs(j, hh, n, s_state.pop(unit))


TWO_PASS_LAG = 3


def _attn_kernel_two_pass(q_ref, k_ref, v_ref, c_ref, sa_ref, sb_ref, o_ref, kb_ref, vt_ref, *, seq):
    nb = seq // MOBA_BLOCK
    tq = MOBA_BLOCK
    lane = lax.broadcasted_iota(jnp.int32, (1, LANES), 1)

    def rope(x, rows):
        return (x * c_ref[0, rows, :] + pltpu.roll(x, LANES - ROPE_HALF, 1) * sa_ref[0, rows, :]
                + pltpu.roll(x, ROPE_HALF, 1) * sb_ref[0, rows, :])

    k = rope(k_ref[0], slice(None))
    kb_ref[...] = k.astype(BF16)
    k_mean = jnp.mean(k.reshape(nb, MOBA_BLOCK, LANES), axis=1)
    v_t = v_ref[0].T
    ones_rows = jnp.where(lax.broadcasted_iota(jnp.int32, (VT_ROWS - HEAD_DIM, seq), 0) == 0, 1.0, 0.0)
    for hh in range(2):
        vt_ref[hh] = jnp.concatenate([v_t[hh * HEAD_DIM:(hh + 1) * HEAD_DIM], ones_rows], axis=0).astype(BF16)

    qscale = HEAD_DIM ** -0.5 * LOG2E
    causal_t = (lax.broadcasted_iota(jnp.int32, (tq, tq), 0) <= lax.broadcasted_iota(jnp.int32, (tq, tq), 1))
    blk_row = lax.broadcasted_iota(jnp.int32, (nb, tq), 0)

    q_tiles = {}

    def scores(j, hh):
        if hh == 0:
            rows = slice(j * tq, (j + 1) * tq)
            q_tiles[j] = rope(q_ref[0, rows, :], rows) * qscale
        q = q_tiles[j] if hh == 0 else q_tiles.pop(j)
        head_lanes = (lane < HEAD_DIM) if hh == 0 else (lane >= HEAD_DIM)
        qh_t = jnp.where(head_lanes, q, 0.0).T
        s_t = jnp.dot(kb_ref[0:(j + 1) * tq, :], qh_t.astype(BF16), preferred_element_type=F32)
        gate_t = _dot_bf16x3(k_mean, qh_t) if j > MOBA_TOPK else None
        return s_t, gate_t

    def probs(j, s_t, gate_t):
        n_past = j * tq
        own = jnp.where(causal_t, s_t[n_past:], NEG)
        mx = jnp.max(own, axis=0, keepdims=True)
        shifts = []
        if j > 0:
            blk_max = [jnp.max(s_t[n * tq:(n + 1) * tq], axis=0, keepdims=True) for n in range(j)]
            if j > MOBA_TOPK:
                rank = jnp.zeros((nb, tq), F32)
                for m in range(j):
                    g_m = gate_t[m:m + 1, :]
                    beats = (g_m > gate_t) | ((g_m == gate_t) & (blk_row > m))
                    rank = rank + jnp.where(beats & (blk_row != m), 1.0, 0.0)
                bias = jnp.where((blk_row < j) & (rank >= MOBA_TOPK), NEG, 0.0)
                bias_n = [bias[n:n + 1, :] for n in range(j)]
            else:
                bias_n = [0.0] * j
            for n in range(j):
                mx = jnp.maximum(mx, blk_max[n] + bias_n[n])
            shifts = [mx - bias_n[n] for n in range(j)]
        parts = [jnp.exp2(s_t[n * tq:(n + 1) * tq] - shifts[n]).astype(BF16) for n in range(j)]
        parts.append(jnp.exp2(own - mx).astype(BF16))
        return jnp.concatenate(parts, axis=0) if j > 0 else parts[0]

    def values(j, hh, p_t):
        o_t = jnp.dot(vt_ref[hh, :, 0:(j + 1) * tq], p_t, preferred_element_type=F32)
        return o_t[0:HEAD_DIM] / o_t[HEAD_DIM:HEAD_DIM + 1]

    chains = [(j, hh) for j in range(nb) for hh in range(2)]
    s_state = {}
    p_state = {}
    outs = {}
    for step in range(len(chains) + TWO_PASS_LAG):
        if step >= TWO_PASS_LAG:
            j, hh = chains[step - TWO_PASS_LAG]
            outs[(j, hh)] = values(j, hh, p_state.pop((j, hh)))
            if hh == 1:
                o_ref[0, j * tq:(j + 1) * tq, :] = jnp.concatenate([outs.pop((j, 0)), outs.pop((j, 1))], axis=0).T
        if step < len(chains):
            j, hh = chains[step]
            s_state[(j, hh)] = scores(j, hh)
        if 1 <= step <= len(chains):
            j, hh = chains[step - 1]
            p_state[(j, hh)] = probs(j, *s_state.pop((j, hh)))


def _attention(proj3, rope_c, rope_sa, rope_sb, body=_attn_kernel):
    bsz, seq, _ = proj3.shape
    blk = (1, seq, LANES)
    col = lambda base: (lambda b, p: (b, 0, base // LANES + p))
    tab = lambda b, p: (b, 0, 0)
    return pl.pallas_call(
        functools.partial(body, seq=seq),
        grid=(bsz, ATT_W // LANES),
        in_specs=[pl.BlockSpec(blk, col(COL_Q)), pl.BlockSpec(blk, col(COL_K)), pl.BlockSpec(blk, col(COL_V)),
                  pl.BlockSpec(blk, tab), pl.BlockSpec(blk, tab), pl.BlockSpec(blk, tab)],
        out_specs=pl.BlockSpec(blk, lambda b, p: (b, 0, p)),
        out_shape=jax.ShapeDtypeStruct((bsz, seq, ATT_W), F32),
        scratch_shapes=[pltpu.VMEM((seq, LANES), BF16), pltpu.VMEM((2, VT_ROWS, seq), BF16)],
        compiler_params=_params(("parallel", "arbitrary")),
        name="moba_attention",
    )(proj3, proj3, proj3, rope_c, rope_sa, rope_sb)


assert HEADS_PER_GROUP == 3 and GROUP_PAD == 2 * LANES and SSD_HEAD_DIM * 2 == LANES


def _expand_heads(small):
    low = lax.broadcasted_iota(jnp.int32, (1, LANES), 1) < SSD_HEAD_DIM
    slabs = []
    for g in range(SSD_GROUPS):
        h0 = g * HEADS_PER_GROUP
        slabs.append(jnp.where(low, small[:, h0:h0 + 1], small[:, h0 + 1:h0 + 2]))
        slabs.append(jnp.where(low, small[:, h0 + 2:h0 + 3], 0.0))
    return jnp.concatenate(slabs, axis=1)


def _ssd_kernel(z_ref, xs_ref, b_ref, c_ref, dt_ref, cwx_ref, cwb_ref, cwc_ref, cbx_ref, cbb_ref, cbc_ref,
                dtb_ref, alog_ref, dskip_ref, ng_ref, o_ref, state_ref, *, seq):
    q = SSD_CHUNK
    n = SSD_STATE
    state_ref[...] = jnp.zeros_like(state_ref)
    row_i = lax.broadcasted_iota(jnp.int32, (q, q), 0)
    col_i = lax.broadcasted_iota(jnp.int32, (q, q), 1)
    lower = row_i >= col_i
    tril = jnp.where(lower, 1.0, 0.0)
    slab = lax.broadcasted_iota(jnp.int32, (1, GROUP_PAD), 1) // SSD_HEAD_DIM
    a_neg = -jnp.exp(alog_ref[...])

    def conv_silu(ref, w_ref, bias_ref, c):
        start = pl.multiple_of(c * q, q)
        cur = ref[0, pl.ds(start, q), :]
        prev_start = pl.multiple_of(jnp.maximum(c * q - 8, 0), 8)
        prev = jnp.where(c > 0, ref[0, pl.ds(prev_start, 8), :], 0.0)
        full = jnp.concatenate([prev, cur], axis=0)
        y = bias_ref[...] + w_ref[SSD_CONV - 1:SSD_CONV, :] * cur
        for i in range(SSD_CONV - 1):
            shift = SSD_CONV - 1 - i
            y = y + w_ref[i:i + 1, :] * pltpu.roll(full, shift, 0)[8:, :]
        return _silu(y)

    def chunk(c, carry):
        start = pl.multiple_of(c * q, q)
        xs = conv_silu(xs_ref, cwx_ref, cbx_ref, c)
        bm = conv_silu(b_ref, cwb_ref, cbb_ref, c)
        cm = conv_silu(c_ref, cwc_ref, cbc_ref, c)
        x = dt_ref[0, pl.ds(start, q), :] + dtb_ref[...]
        dt = jnp.maximum(x, 0.0) + jnp.log(1.0 + jnp.exp(-jnp.abs(x)))
        a = dt * a_neg
        a_cum = jnp.dot(tril, a, precision=HIGHEST, preferred_element_type=F32)
        a_cum_t = a_cum.T
        dt_e = _expand_heads(dt)
        acum_e = _expand_heads(a_cum)
        alast_e = acum_e[q - 1:q, :]
        xdt = xs * dt_e
        decay_e = jnp.exp(alast_e - acum_e)
        y_parts = []
        for g in range(SSD_GROUPS):
            gl = slice(g * GROUP_PAD, (g + 1) * GROUP_PAD)
            nl = slice(g * n, (g + 1) * n)
            b_g = bm[:, nl].astype(BF16)
            b_gt = bm[:, nl].T.astype(BF16)
            c_g = cm[:, nl].astype(BF16)
            cb = lax.dot_general(c_g, b_g, (((1,), (1,)), ((), ())), preferred_element_type=F32)
            xdt_g = xdt[:, gl]
            xdt_b = xdt_g.astype(BF16)
            y_g = jnp.zeros((q, GROUP_PAD), F32)
            for r in range(HEADS_PER_GROUP):
                h = g * HEADS_PER_GROUP + r
                seg = a_cum[:, h:h + 1] - a_cum_t[h:h + 1, :]
                l_h = jnp.where(lower, jnp.exp(jnp.minimum(seg, 0.0)), 0.0)
                y_h = jnp.dot((cb * l_h).astype(BF16), xdt_b, preferred_element_type=F32)
                y_g = jnp.where(slab == r, y_h, y_g)
            prev = state_ref[g]
            y_off = jnp.dot(c_g, prev.astype(BF16), preferred_element_type=F32) * jnp.exp(acum_e[:, gl])
            new = jnp.dot(b_gt, (xdt_g * decay_e[:, gl]).astype(BF16), preferred_element_type=F32)
            state_ref[g] = prev * jnp.exp(alast_e[:, gl]) + new
            y_parts.append(y_g + y_off)
        y = jnp.concatenate(y_parts, axis=1) + dskip_ref[...] * xs
        yz = y * _silu(z_ref[0, pl.ds(start, q), :])
        outs = []
        for g in range(SSD_GROUPS):
            yg = yz[:, g * GROUP_PAD:(g + 1) * GROUP_PAD]
            ms = jnp.sum(yg * yg, axis=-1, keepdims=True) * (1.0 / GROUP_W)
            outs.append(yg * lax.rsqrt(ms + LN_EPS))
        o_ref[0, pl.ds(start, q), :] = (jnp.concatenate(outs, axis=1) * ng_ref[...]).astype(BF16)
        return carry

    lax.fori_loop(0, seq // q, chunk, 0)


def _ssd(proj3, p):
    bsz, seq, _ = proj3.shape
    slab = lambda width, base: pl.BlockSpec((1, seq, width), lambda b: (b, 0, base // width))
    const2 = lambda a: pl.BlockSpec(a.shape, lambda b: (0, 0))
    small = [p["conv_w_xs"], p["conv_w_b"], p["conv_w_c"], p["conv_b_xs"], p["conv_b_b"], p["conv_b_c"],
             p["dt_bias"], p["a_log"], p["d_skip"], p["ssd_norm_g"]]
    return pl.pallas_call(
        functools.partial(_ssd_kernel, seq=seq),
        grid=(bsz,),
        in_specs=[slab(SSD_PAD, COL_Z), slab(SSD_PAD, COL_XS), slab(2 * SSD_STATE, COL_B),
                  slab(2 * SSD_STATE, COL_C), slab(LANES, COL_DT)] + [const2(a) for a in small],
        out_specs=pl.BlockSpec((1, seq, SSD_PAD), lambda b: (b, 0, 0)),
        out_shape=jax.ShapeDtypeStruct((bsz, seq, SSD_PAD), BF16),
        scratch_shapes=[pltpu.VMEM((SSD_GROUPS, SSD_STATE, GROUP_PAD), F32)],
        compiler_params=_params(("parallel",)),
        name="ssd",
    )(proj3, proj3, proj3, proj3, proj3, *small)


GM_ROWS = 1024


def _gmlp_kernel(u_ref, v_ref, lng_ref, lnb_ref, ws_ref, bs_ref, ng_ref, o_ref):
    q = GM_CHUNK
    row_i = lax.broadcasted_iota(jnp.int32, (q, q), 0)
    col_i = lax.broadcasted_iota(jnp.int32, (q, q), 1)
    lower = row_i >= col_i
    grp = lax.broadcasted_iota(jnp.int32, (1, GM_W), 1) // (GM_W // GM_GROUPS)
    w = [jnp.where(lower, ws_ref[g], 0.0).astype(BF16) for g in range(GM_GROUPS)]
    for c in range(GM_ROWS // q):
        rows = slice(c * q, (c + 1) * q)
        u = _gelu_tanh(u_ref[rows, :])
        v = _layer_norm(_gelu_tanh(v_ref[rows, :]), lng_ref[...], lnb_ref[...])
        vb = v.astype(BF16)
        mixed = jnp.zeros((q, GM_W), F32)
        for g in range(GM_GROUPS):
            mixed = jnp.where(grp == g, jnp.dot(w[g], vb, preferred_element_type=F32), mixed)
        gm = u * (mixed + bs_ref[...])
        ms = jnp.mean(gm * gm, axis=-1, keepdims=True)
        o_ref[rows, :] = (gm * lax.rsqrt(ms + LN_EPS) * ng_ref[...]).astype(BF16)


def _gmlp(proj, p):
    m = proj.shape[0]
    const = lambda a: pl.BlockSpec(a.shape, lambda i: (0,) * a.ndim)
    small = [p["gm_ln_g"], p["gm_ln_b"], p["gm_w_s"], p["gm_b_s"], p["gm_norm_g"]]
    return pl.pallas_call(
        _gmlp_kernel,
        grid=(m // GM_ROWS,),
        in_specs=[pl.BlockSpec((GM_ROWS, GM_W), lambda i: (i, COL_GU // GM_W)),
                  pl.BlockSpec((GM_ROWS, GM_W), lambda i: (i, COL_GV // GM_W))] + [const(a) for a in small],
        out_specs=pl.BlockSpec((GM_ROWS, GM_W), lambda i: (i, 0)),
        out_shape=jax.ShapeDtypeStruct((m, GM_W), BF16),
        compiler_params=_params(("parallel",)),
        name="gmlp",
    )(proj, proj, *small)


OUT_TM = 512
LOGIT_ROWS = 16


def _out_proj_kernel(att_ref, ssd_ref, gm_ref, h_ref, ag_ref, wa_ref, ws_ref, wg_ref, g_ref, b_ref, o_ref,
                     router=None):
    att = att_ref[...]
    ms = jnp.mean(att * att, axis=-1, keepdims=True)
    att = att * lax.rsqrt(ms + LN_EPS) * ag_ref[...]
    mix = jnp.dot(att.astype(BF16), wa_ref[...], preferred_element_type=F32)
    mix = mix + jnp.dot(ssd_ref[...], ws_ref[...], preferred_element_type=F32)
    mix = mix + jnp.dot(gm_ref[...], wg_ref[...], preferred_element_type=F32)
    h1 = _layer_norm(ALPHA * h_ref[...] + mix, g_ref[...], b_ref[...])
    o_ref[...] = h1
    if router:
        wr_ref, logit_ref = router
        logit_ref[...] = _dot_bf16x3(wr_ref[...], h1.T)


def _out_proj_kernel_routed(att_ref, ssd_ref, gm_ref, h_ref, ag_ref, wa_ref, ws_ref, wg_ref, g_ref, b_ref, wr_ref,
                            o_ref, logit_ref):
    _out_proj_kernel(att_ref, ssd_ref, gm_ref, h_ref, ag_ref, wa_ref, ws_ref, wg_ref, g_ref, b_ref, o_ref,
                     router=(wr_ref, logit_ref))


def _out_proj(att, ssd, gm, h, p, w_router=None):
    m = h.shape[0]
    row = lambda i: (i, 0)
    const = lambda a: pl.BlockSpec(a.shape, lambda i: (0, 0))
    small = [p["att_norm_g"], p["w_out_att"], p["w_out_ssd"], p["w_out_gm"], p["ln1_g"], p["ln1_b"]]
    out_specs = [pl.BlockSpec((OUT_TM, D_MODEL), row)]
    out_shape = [jax.ShapeDtypeStruct((m, D_MODEL), F32)]
    body = _out_proj_kernel
    if w_router is not None:
        small.append(jnp.pad(w_router.T, ((0, LOGIT_ROWS - N_EXPERTS), (0, 0))))
        out_specs.append(pl.BlockSpec((LOGIT_ROWS, OUT_TM), lambda i: (0, i)))
        out_shape.append(jax.ShapeDtypeStruct((LOGIT_ROWS, m), F32))
        body = _out_proj_kernel_routed
    return pl.pallas_call(
        body,
        grid=(m // OUT_TM,),
        in_specs=[pl.BlockSpec((OUT_TM, ATT_W), row), pl.BlockSpec((OUT_TM, SSD_PAD), row),
                  pl.BlockSpec((OUT_TM, GM_W), row), pl.BlockSpec((OUT_TM, D_MODEL), row)]
                 + [const(a) for a in small],
        out_specs=out_specs,
        out_shape=out_shape,
        compiler_params=_params(("parallel",)),
        name="out_proj_ln1",
    )(att, ssd, gm, h, *small)


FFN_TM = 512


DOWN_PIECE = 512


def _swiglu_chunks(acc_ref, xb, w_gate, w_up, w_down, widths):
    dot = functools.partial(jnp.dot, preferred_element_type=F32)
    bounds = [sum(widths[:c]) for c in range(len(widths) + 1)]
    chunk = lambda c: slice(bounds[c], bounds[c + 1])
    up = lambda c: (dot(xb, w_gate(chunk(c))), dot(xb, w_up(chunk(c))))
    pending = up(0)
    n_out = acc_ref.shape[-1]
    for c in range(len(widths)):
        hg, hu = pending
        if c + 1 < len(widths):
            pending = up(c + 1)
        hid = (_silu(hg) * hu).astype(BF16)
        w_d = w_down(chunk(c))
        for lo in range(0, n_out, DOWN_PIECE):
            acc_ref[:, lo:lo + DOWN_PIECE] += dot(hid, w_d[:, lo:lo + DOWN_PIECE])


FFN_CHUNK = 512


def _ffn_kernel(x_ref, wg_ref, wu_ref, wd_ref, g_ref, b_ref, o_ref, acc_ref):
    d_ff = wg_ref.shape[-1]
    widths = (FFN_CHUNK,) * (d_ff // FFN_CHUNK) + ((d_ff % FFN_CHUNK,) if d_ff % FFN_CHUNK else ())
    x = x_ref[...]
    acc_ref[...] = ALPHA * x
    _swiglu_chunks(acc_ref, x.astype(BF16), lambda c: wg_ref[:, c], lambda c: wu_ref[:, c],
                   lambda c: wd_ref[c, :], widths)
    o_ref[...] = _layer_norm(acc_ref[...], g_ref[...], b_ref[...])


def _ffn(x, wg, wu, wd, ln_g, ln_b):
    m = x.shape[0]
    resident = lambda a: pl.BlockSpec(a.shape, lambda i: (0, 0), pipeline_mode=pl.Buffered(1))
    return pl.pallas_call(
        _ffn_kernel,
        grid=(m // FFN_TM,),
        in_specs=[pl.BlockSpec((FFN_TM, D_MODEL), lambda i: (i, 0)), resident(wg), resident(wu), resident(wd),
                  pl.BlockSpec((1, D_MODEL), lambda i: (0, 0)),
                  pl.BlockSpec((1, D_MODEL), lambda i: (0, 0))],
        out_specs=pl.BlockSpec((FFN_TM, D_MODEL), lambda i: (i, 0)),
        out_shape=jax.ShapeDtypeStruct((m, D_MODEL), F32),
        scratch_shapes=[pltpu.VMEM((FFN_TM, D_MODEL), F32)],
        compiler_params=_params(("parallel",)),
        name="ffn_ln2",
    )(x, wg, wu, wd, ln_g.reshape(1, D_MODEL), ln_b.reshape(1, D_MODEL))


MOE_TR = 256
MOE_HALF = 512
MOE_UNITS = 3
MOE_TM = MOE_UNITS * MOE_HALF
MOE_TF = 512
MOE_CHUNKS = (256, 256)
SEG_ALIGN = 16
MOE_WIN_SMALL = 128
INFO_GATE = N_EXPERTS


def _expert_cap(m):
    n_tiles = m // MOE_TR
    rows = m + (SEG_ALIGN - 1) * n_tiles + MOE_TR + MOE_HALF
    return -(-rows // MOE_TM) * MOE_TM


def _rows_computed(lens):
    return jnp.maximum(-(-lens // MOE_HALF), 1) * MOE_HALF


def _dispatch_kernel(h_ref, logit_ref, info_ref, offs_ref, lens_ref, small_ref, xs_hbm, run_ref, xbuf_ref, zbuf_ref,
                     sem_ref, zsem_ref, *, n_tiles, cap):
    t = pl.program_id(0)
    tr = MOE_TR
    slot = t % 2

    @pl.when(t == 0)
    def _():
        for e in range(N_EXPERTS):
            run_ref[e] = 0

    xb = h_ref[...].astype(BF16)
    lg = logit_ref[0:N_EXPERTS, :]
    row = lax.broadcasted_iota(jnp.int32, (N_EXPERTS, tr), 0)
    m1 = jnp.max(lg, axis=0, keepdims=True)
    i1 = jnp.min(jnp.where(lg == m1, row, N_EXPERTS), axis=0, keepdims=True)
    rest = jnp.where(row == i1, -jnp.inf, lg)
    m2 = jnp.max(rest, axis=0, keepdims=True)
    i2 = jnp.min(jnp.where(rest == m2, row, N_EXPERTS), axis=0, keepdims=True)
    e2 = jnp.exp(m2 - m1)
    p1 = 1.0 / (1.0 + e2)
    gate_t = jnp.where(row == i1, p1, 0.0) + jnp.where(row == i2, e2 * p1, 0.0)
    sel_t = jnp.where((row == i1) | (row == i2), 1.0, 0.0)
    before = (lax.broadcasted_iota(jnp.int32, (tr, tr), 0) < lax.broadcasted_iota(jnp.int32, (tr, tr), 1))
    rank_t = jnp.dot(sel_t.astype(BF16), jnp.where(before, 1.0, 0.0).astype(BF16),
                     preferred_element_type=F32)
    cnt = jnp.sum(sel_t, axis=1, keepdims=True).astype(jnp.int32)
    info_t = jnp.concatenate([rank_t, gate_t, jnp.zeros((LANES - 2 * N_EXPERTS, tr), F32)], axis=0)
    info_ref[...] = info_t.T

    small = jnp.max(cnt) <= MOE_WIN_SMALL
    small_ref[t] = small.astype(jnp.int32)

    def seg_copy(sl, e, off, rows):
        return pltpu.make_async_copy(xbuf_ref.at[sl, e, pl.ds(0, rows)],
                                     xs_hbm.at[pl.ds(pl.multiple_of(off, SEG_ALIGN), rows)], sem_ref.at[sl, e])

    def by_window(flag, fn):
        pl.when(flag)(functools.partial(fn, MOE_WIN_SMALL))
        pl.when(jnp.logical_not(flag))(functools.partial(fn, tr))

    def compact(rows):
        dst_row = lax.broadcasted_iota(jnp.int32, (rows, tr), 0).astype(F32)
        onehots = [jnp.where((rank_t[e:e + 1, :] == dst_row) & (sel_t[e:e + 1, :] > 0.0), 1.0, 0.0).astype(BF16)
                   for e in range(N_EXPERTS)]
        onehot_all = jnp.concatenate(onehots, axis=0)
        half = D_MODEL // 2
        for c in range(2):
            packed = jnp.dot(onehot_all, xb[:, c * half:(c + 1) * half], preferred_element_type=F32)
            xbuf_ref[slot, :, 0:rows, c * half:(c + 1) * half] = packed.astype(BF16).reshape(N_EXPERTS, rows, half)

    by_window(small, compact)

    def wait_previous(rows):
        for e in range(N_EXPERTS):
            seg_copy(1 - slot, e, 0, rows).wait()

    @pl.when(t > 0)
    def _():
        by_window(small_ref[jnp.maximum(t - 1, 0)] == 1, wait_previous)

    def start_all(rows):
        for e in range(N_EXPERTS):
            seg_copy(slot, e, e * cap + run_ref[e], rows).start()

    by_window(small, start_all)
    for e in range(N_EXPERTS):
        offs_ref[t * N_EXPERTS + e] = e * cap + run_ref[e]
        run_ref[e] = run_ref[e] + jnp.bitwise_and(cnt[e, 0] + (SEG_ALIGN - 1), -SEG_ALIGN)

    def wait_current(rows):
        for e in range(N_EXPERTS):
            seg_copy(slot, e, 0, rows).wait()

    @pl.when(t == n_tiles - 1)
    def _():
        zbuf_ref[...] = jnp.zeros_like(zbuf_ref)
        by_window(small, wait_current)
        tails = []
        for e in range(N_EXPERTS):
            lens_ref[e] = run_ref[e]
            tail = pl.multiple_of(e * cap + run_ref[e], SEG_ALIGN)
            tails.append(pltpu.make_async_copy(zbuf_ref, xs_hbm.at[pl.ds(tail, MOE_HALF)], zsem_ref.at[e]))
            tails[-1].start()
        for cp in tails:
            cp.wait()


def _dispatch(h, logits):
    m = h.shape[0]
    n_tiles = m // MOE_TR
    cap = _expert_cap(m)
    smem = pl.BlockSpec(memory_space=pltpu.SMEM)
    return pl.pallas_call(
        functools.partial(_dispatch_kernel, n_tiles=n_tiles, cap=cap),
        grid=(n_tiles,),
        in_specs=[pl.BlockSpec((MOE_TR, D_MODEL), lambda t: (t, 0)),
                  pl.BlockSpec((LOGIT_ROWS, MOE_TR), lambda t: (0, t))],
        out_specs=[pl.BlockSpec((MOE_TR, LANES), lambda t: (t, 0)), smem, smem, smem,
                   pl.BlockSpec(memory_space=pl.ANY)],
        out_shape=[jax.ShapeDtypeStruct((m, LANES), F32),
                   jax.ShapeDtypeStruct((n_tiles * N_EXPERTS,), jnp.int32),
                   jax.ShapeDtypeStruct((N_EXPERTS,), jnp.int32),
                   jax.ShapeDtypeStruct((n_tiles,), jnp.int32),
                   jax.ShapeDtypeStruct((N_EXPERTS * cap, D_MODEL), BF16)],
        scratch_shapes=[pltpu.SMEM((N_EXPERTS,), jnp.int32),
                        pltpu.VMEM((2, N_EXPERTS, MOE_TR, D_MODEL), BF16),
                        pltpu.VMEM((MOE_HALF, D_MODEL), BF16),
                        pltpu.SemaphoreType.DMA((2, N_EXPERTS)),
                        pltpu.SemaphoreType.DMA((N_EXPERTS,))],
        compiler_params=_params(("arbitrary",)),
        name="moe_dispatch",
    )(h, logits)


def _expert_ffn_kernel(exp_ref, blk_ref, units_ref, x_ref, wg_ref, wu_ref, wd_ref, y_ref, acc_ref):
    w = pl.program_id(0)
    f = pl.program_id(1)

    def tile(rows):
        acc = acc_ref.at[pl.ds(0, rows)]

        @pl.when(f == 0)
        def _():
            acc[...] = jnp.zeros((rows, D_MODEL), F32)

        _swiglu_chunks(acc, x_ref[0:rows, :], lambda c: wg_ref[0, :, c].astype(BF16),
                       lambda c: wu_ref[0, :, c].astype(BF16), lambda c: wd_ref[0, c, :].astype(BF16), MOE_CHUNKS)

        @pl.when(f == pl.num_programs(1) - 1)
        def _():
            y_ref[0:rows, :] = acc[...].astype(BF16)

    for units in range(1, MOE_UNITS + 1):
        pl.when(units_ref[w] == units)(functools.partial(tile, units * MOE_HALF))


def _expert_ffn(xs, exp_w, blk_w, units_w, w_gate, w_up, w_down):
    d_ff = w_gate.shape[-1]
    n_f = d_ff // MOE_TF
    fcol = lambda w, f, exp, blk, units: jnp.where(units[w] != 0, f, n_f - 1)
    grid_spec = pltpu.PrefetchScalarGridSpec(
        num_scalar_prefetch=3,
        grid=(exp_w.shape[0], n_f),
        in_specs=[pl.BlockSpec((MOE_TM, D_MODEL), lambda w, f, exp, blk, valid: (blk[w], 0)),
                  pl.BlockSpec((1, D_MODEL, MOE_TF), lambda w, f, exp, blk, valid: (exp[w], 0, fcol(w, f, exp, blk, valid))),
                  pl.BlockSpec((1, D_MODEL, MOE_TF), lambda w, f, exp, blk, valid: (exp[w], 0, fcol(w, f, exp, blk, valid))),
                  pl.BlockSpec((1, MOE_TF, D_MODEL), lambda w, f, exp, blk, valid: (exp[w], fcol(w, f, exp, blk, valid), 0))],
        out_specs=pl.BlockSpec((MOE_TM, D_MODEL), lambda w, f, exp, blk, valid: (blk[w], 0)),
        scratch_shapes=[pltpu.VMEM((MOE_TM, D_MODEL), F32)])
    return pl.pallas_call(
        _expert_ffn_kernel,
        grid_spec=grid_spec,
        out_shape=jax.ShapeDtypeStruct(xs.shape, BF16),
        compiler_params=_params(("arbitrary", "arbitrary")),
        name="moe_expert_ffn",
    )(exp_w, blk_w, units_w, xs, w_gate, w_up, w_down)


def _combine_kernel(starts_ref, shifts_ref, small_ref, info_ref, h_ref, g_ref, b_ref, ys_hbm, o_ref, ybuf_ref,
                    sem_ref, *, n_tiles):
    t = pl.program_id(0)
    tr = MOE_TR
    slot = t % 2

    def seg_copy(tile, sl, e, rows):
        off = pl.multiple_of(starts_ref[tile * N_EXPERTS + e], SEG_ALIGN)
        return pltpu.make_async_copy(ys_hbm.at[pl.ds(off, rows)], ybuf_ref.at[sl, e, pl.ds(0, rows)],
                                     sem_ref.at[sl, e])

    def by_window(tile, fn):
        flag = small_ref[tile] == 1
        pl.when(flag)(functools.partial(fn, MOE_WIN_SMALL))
        pl.when(jnp.logical_not(flag))(functools.partial(fn, tr))

    def fetch(tile, sl):
        def start(rows):
            for e in range(N_EXPERTS):
                seg_copy(tile, sl, e, rows).start()
        by_window(tile, start)

    @pl.when(t == 0)
    def _():
        fetch(0, 0)

    @pl.when(t + 1 < n_tiles)
    def _():
        fetch(jnp.minimum(t + 1, n_tiles - 1), 1 - slot)

    def gather(rows):
        for e in range(N_EXPERTS):
            seg_copy(t, slot, e, rows).wait()
        info = info_ref[...]
        src_row = lax.broadcasted_iota(jnp.int32, (tr, rows), 1).astype(F32)
        acc = ALPHA * h_ref[...]
        for e in range(N_EXPERTS):
            rank_c = info[:, e:e + 1] + shifts_ref[t * N_EXPERTS + e].astype(F32)
            gate_c = info[:, INFO_GATE + e:INFO_GATE + e + 1]
            onehot = jnp.where((rank_c == src_row) & (gate_c != 0.0), 1.0, 0.0).astype(BF16)
            acc = acc + gate_c * jnp.dot(onehot, ybuf_ref[slot, e, 0:rows, :], preferred_element_type=F32)
        o_ref[...] = _layer_norm(acc, g_ref[...], b_ref[...])

    by_window(t, gather)


def _combine(starts, shifts, small, info, h, ys, ln_g, ln_b):
    m = h.shape[0]
    n_tiles = m // MOE_TR
    grid_spec = pltpu.PrefetchScalarGridSpec(
        num_scalar_prefetch=3,
        grid=(n_tiles,),
        in_specs=[pl.BlockSpec((MOE_TR, LANES), lambda t, *_: (t, 0)),
                  pl.BlockSpec((MOE_TR, D_MODEL), lambda t, *_: (t, 0)),
                  pl.BlockSpec((1, D_MODEL), lambda t, *_: (0, 0)),
                  pl.BlockSpec((1, D_MODEL), lambda t, *_: (0, 0)),
                  pl.BlockSpec(memory_space=pl.ANY)],
        out_specs=pl.BlockSpec((MOE_TR, D_MODEL), lambda t, *_: (t, 0)),
        scratch_shapes=[pltpu.VMEM((2, N_EXPERTS, MOE_TR, D_MODEL), BF16),
                        pltpu.SemaphoreType.DMA((2, N_EXPERTS))])
    return pl.pallas_call(
        functools.partial(_combine_kernel, n_tiles=n_tiles),
        grid_spec=grid_spec,
        out_shape=jax.ShapeDtypeStruct((m, D_MODEL), F32),
        compiler_params=_params(("arbitrary",)),
        name="moe_combine_ln2",
    )(starts, shifts, small, info, h, ln_g.reshape(1, D_MODEL), ln_b.reshape(1, D_MODEL), ys)


def _work_list(lens, m):
    cap = _expert_cap(m)
    n_tiles = m // MOE_TR
    max_rows = 2 * m + (SEG_ALIGN - 1) * min(N_EXPERTS * n_tiles, 2 * m)
    w_max = -(-(max_rows + N_EXPERTS * MOE_HALF) // MOE_TM) + N_EXPERTS
    rows_e = _rows_computed(lens)
    tiles_e = -(-rows_e // MOE_TM)
    ends = jnp.cumsum(tiles_e)
    w = jnp.arange(w_max, dtype=jnp.int32)
    wc = jnp.minimum(w, ends[-1] - 1)
    exp_w = jnp.sum((wc[:, None] >= ends[None, :]).astype(jnp.int32), axis=1)
    tile_w = wc - (ends - tiles_e)[exp_w]
    blk_w = exp_w * (cap // MOE_TM) + tile_w
    units_w = jnp.minimum((rows_e[exp_w] - tile_w * MOE_TM) // MOE_HALF, MOE_UNITS)
    units_w = jnp.where(w < ends[-1], units_w, 0)
    return exp_w.astype(jnp.int32), blk_w.astype(jnp.int32), units_w.astype(jnp.int32)


def _read_windows(offs, small, lens, m):
    cap = _expert_cap(m)
    region = jnp.arange(N_EXPERTS, dtype=jnp.int32) * cap
    rel = offs.reshape(-1, N_EXPERTS) - region
    width = jnp.where(small == 1, MOE_WIN_SMALL, MOE_TR)[:, None]
    start = jnp.minimum(rel, _rows_computed(lens)[None, :] - width)
    return (start + region).reshape(-1).astype(jnp.int32), (rel - start).reshape(-1).astype(jnp.int32)


def _moe_layer(h, logits, w_gate, w_up, w_down, ln_g, ln_b):
    m = h.shape[0]
    info, offs, lens, small, xs = _dispatch(h, logits)
    exp_w, blk_w, units_w = _work_list(lens, m)
    ys = _expert_ffn(xs, exp_w, blk_w, units_w, w_gate, w_up, w_down)
    starts, shifts = _read_windows(offs, small, lens, m)
    return _combine(starts, shifts, small, info, h, ys, ln_g, ln_b)


def _pad_groups(a):
    lead = a.shape[:-1]
    a = a.reshape(*lead, SSD_GROUPS, GROUP_W)
    a = jnp.pad(a, [(0, 0)] * len(lead) + [(0, 0), (0, GROUP_PAD - GROUP_W)])
    return a.reshape(*lead, SSD_PAD)


def _head_lanes(v):
    return jnp.pad(v, (0, LANES - SSD_HEADS)).reshape(1, LANES)


D_IN = 3 * ATT_W + SSD_W + (SSD_W + 2 * SSD_GROUPS * SSD_STATE) + SSD_HEADS + 2 * GM_W
PERM_ROWS = 256


def _permute_w_kernel(w_ref, o_ref):
    src_z = 3 * ATT_W
    src_xs = src_z + SSD_W
    src_b = src_xs + SSD_W
    src_c = src_b + SSD_GROUPS * SSD_STATE
    src_dt = src_c + SSD_GROUPS * SSD_STATE
    src_gu = src_dt + SSD_HEADS
    src_gv = src_gu + GM_W
    o_ref[...] = jnp.zeros_like(o_ref)

    def put(dst, src, width):
        o_ref[:, dst:dst + width] = w_ref[0, :, src:src + width].astype(BF16)

    for g in range(SSD_GROUPS):
        put(COL_Z + g * GROUP_PAD, src_z + g * GROUP_W, GROUP_W)
        put(COL_XS + g * GROUP_PAD, src_xs + g * GROUP_W, GROUP_W)
    put(COL_B, src_b, SSD_GROUPS * SSD_STATE)
    put(COL_C, src_c, SSD_GROUPS * SSD_STATE)
    put(COL_GU, src_gu, GM_W)
    put(COL_GV, src_gv, GM_W)
    put(COL_Q, 0, ATT_W)
    put(COL_K, ATT_W, ATT_W)
    put(COL_V, 2 * ATT_W, ATT_W)
    put(COL_DT, src_dt, SSD_HEADS)


def _permute_w_in(w_in, layer):
    return pl.pallas_call(
        _permute_w_kernel,
        grid=(D_MODEL // PERM_ROWS,),
        in_specs=[pl.BlockSpec((1, PERM_ROWS, D_IN), lambda i: (layer, i, 0))],
        out_specs=pl.BlockSpec((PERM_ROWS, D_PROJ), lambda i: (i, 0)),
        out_shape=jax.ShapeDtypeStruct((D_MODEL, D_PROJ), BF16),
        compiler_params=_params(("parallel",)),
        name="permute_w_in",
    )(w_in)


def _layer_params(i, w_in, conv_w, conv_b, dt_bias, a_log, d_skip, ssd_norm_g, att_norm_g, gm_ln_g, gm_ln_b,
                  gm_w_s, gm_b_s, gm_norm_g, w_out, ln1_g, ln1_b):
    cw, cb = conv_w[i], conv_b[i]
    nbc = SSD_GROUPS * SSD_STATE
    wo = w_out[i]
    return {
        "w_in": _permute_w_in(w_in, i),
        "conv_w_xs": _pad_groups(cw[:, :SSD_W]),
        "conv_w_b": cw[:, SSD_W:SSD_W + nbc],
        "conv_w_c": cw[:, SSD_W + nbc:],
        "conv_b_xs": _pad_groups(cb[:SSD_W]).reshape(1, SSD_PAD),
        "conv_b_b": cb[SSD_W:SSD_W + nbc].reshape(1, nbc),
        "conv_b_c": cb[SSD_W + nbc:].reshape(1, nbc),
        "dt_bias": _head_lanes(dt_bias[i]),
        "a_log": _head_lanes(a_log[i]),
        "d_skip": _pad_groups(jnp.repeat(d_skip[i], SSD_HEAD_DIM)).reshape(1, SSD_PAD),
        "ssd_norm_g": _pad_groups(ssd_norm_g[i]).reshape(1, SSD_PAD),
        "att_norm_g": att_norm_g[i].reshape(1, ATT_W),
        "gm_ln_g": gm_ln_g[i].reshape(1, GM_W),
        "gm_ln_b": gm_ln_b[i].reshape(1, GM_W),
        "gm_w_s": gm_w_s[i],
        "gm_b_s": jnp.repeat(gm_b_s[i].T, GM_W // GM_GROUPS, axis=1),
        "gm_norm_g": gm_norm_g[i].reshape(1, GM_W),
        "w_out_att": wo[:ATT_W].astype(BF16),
        "w_out_ssd": _pad_groups(wo[ATT_W:ATT_W + SSD_W].T).T.astype(BF16),
        "w_out_gm": wo[ATT_W + SSD_W:].astype(BF16),
        "ln1_g": ln1_g[i].reshape(1, D_MODEL),
        "ln1_b": ln1_b[i].reshape(1, D_MODEL),
    }


def _rope_tables(positions):
    inv = ROPE_THETA ** (-jnp.arange(0, ROPE_DIM, 2, dtype=F32) / ROPE_DIM)
    d = jnp.arange(LANES) % HEAD_DIM
    inv_lane = jnp.where(d < ROPE_DIM, inv[d % ROPE_HALF], 0.0)
    ang = positions.astype(F32)[..., None] * inv_lane
    cos, sin = jnp.cos(ang), jnp.sin(ang)
    return cos, jnp.where(d < ROPE_HALF, -sin, 0.0), jnp.where(d >= ROPE_HALF, sin, 0.0)


def kernel(x, positions, ln_in_g, ln_in_b, w_in, conv_w, conv_b, dt_bias, a_log, d_skip, ssd_norm_g, att_norm_g, gm_ln_g, gm_ln_b, gm_w_s, gm_b_s, gm_norm_g, w_out, ln1_g, ln1_b, ln2_g, ln2_b, ffn_w_gate, ffn_w_up, ffn_w_down, router_w, moe_w_gate, moe_w_up, moe_w_down):
    bsz, seq, _ = x.shape
    m = bsz * seq
    rope_c, rope_sa, rope_sb = _rope_tables(positions)
    h = x.reshape(m, D_MODEL)
    for i in range(DEPTH):
        p = _layer_params(i, w_in, conv_w, conv_b, dt_bias, a_log, d_skip, ssd_norm_g, att_norm_g, gm_ln_g,
                          gm_ln_b, gm_w_s, gm_b_s, gm_norm_g, w_out, ln1_g, ln1_b)
        if i == 0:
            proj, h = _in_proj(h, p["w_in"], ln=(ln_in_g, ln_in_b))
        else:
            (proj,) = _in_proj(h, p["w_in"])
        proj3 = proj.reshape(bsz, seq, D_PROJ)
        att = _attention(proj3, rope_c, rope_sa, rope_sb,
                         body=_attn_kernel if i == 0 else _attn_kernel_two_pass).reshape(m, ATT_W)
        ssd = _ssd(proj3, p).reshape(m, SSD_PAD)
        gm = _gmlp(proj, p)
        j = i // 2
        if i % 2 == 0:
            (h,) = _out_proj(att, ssd, gm, h, p)
            h = _ffn(h, ffn_w_gate[j].astype(BF16), ffn_w_up[j].astype(BF16), ffn_w_down[j].astype(BF16),
                     ln2_g[i], ln2_b[i])
        else:
            h, logits = _out_proj(att, ssd, gm, h, p, w_router=router_w[j])
            h = _moe_layer(h, logits, moe_w_gate[j], moe_w_up[j], moe_w_down[j], ln2_g[i], ln2_b[i])
    return h.reshape(bsz, seq, D_MODEL)
```

```python
import functools

import jax
import jax.numpy as jnp
from jax import lax
from jax.experimental import pallas as pl
from jax.experimental.pallas import tpu as pltpu

F32 = jnp.float32
BF16 = jnp.bfloat16
HIGHEST = lax.Precision.HIGHEST

D_MODEL = 1024
ATT_HEADS = 6
HEAD_DIM = 64
ATT_W = ATT_HEADS * HEAD_DIM
ROPE_DIM = HEAD_DIM // 4
ROPE_HALF = ROPE_DIM // 2
ROPE_THETA = 500000.0
MOBA_BLOCK = 256
MOBA_TOPK = 3
SSD_HEADS = 6
SSD_HEAD_DIM = 64
SSD_W = SSD_HEADS * SSD_HEAD_DIM
SSD_GROUPS = 2
SSD_STATE = 128
SSD_CONV = 4
SSD_CHUNK = 128
GM_GROUPS = 4
GM_W = 256
GM_CHUNK = 128
N_EXPERTS = 8
DEPTH = 2
ALPHA = (2.0 * DEPTH) ** 0.25
NEG = -1e30
LN_EPS = 1e-5

LANES = 128
GROUP_W = SSD_W // SSD_GROUPS
GROUP_PAD = 256
SSD_PAD = SSD_GROUPS * GROUP_PAD
HEADS_PER_GROUP = SSD_HEADS // SSD_GROUPS

COL_Z = 0
COL_XS = 512
COL_B = 1024
COL_C = 1280
COL_GU = 1536
COL_GV = 1792
COL_Q = 2048
COL_K = 2432
COL_V = 2816
COL_DT = 3200
D_PROJ = 3328

VMEM_LIMIT = 56 * 1024 * 1024


def _params(sem, vmem=VMEM_LIMIT):
    return pltpu.CompilerParams(dimension_semantics=sem, vmem_limit_bytes=vmem)


def _layer_norm(x, g, b):
    mu = jnp.mean(x, axis=-1, keepdims=True)
    xc = x - mu
    var = jnp.mean(xc * xc, axis=-1, keepdims=True)
    return xc * lax.rsqrt(var + LN_EPS) * g + b


def _silu(x):
    half = 0.5 * x
    return half + half * jnp.tanh(half)


def _split_bf16(x):
    hi = x.astype(BF16)
    return hi, (x - hi.astype(F32)).astype(BF16)


def _dot_bf16x3(a, b):
    a_hi, a_lo = _split_bf16(a)
    b_hi, b_lo = _split_bf16(b)
    dot = functools.partial(jnp.dot, preferred_element_type=F32)
    return dot(a_hi, b_hi) + (dot(a_lo, b_hi) + dot(a_hi, b_lo))


def _gelu_tanh(x):
    return 0.5 * x * (1.0 + jnp.tanh(0.7978845608028654 * (x + 0.044715 * x * x * x)))


PROJ_TM = 512
PROJ_CH = 256


def _in_proj_kernel(*refs, apply_ln):
    if apply_ln:
        x_ref, g_ref, b_ref, w_ref, proj_ref, h_ref = refs
        x = _layer_norm(x_ref[...], g_ref[...], b_ref[...])
        h_ref[...] = x
    else:
        x_ref, w_ref, proj_ref = refs
        x = x_ref[...]
    xb = x.astype(BF16)
    for j in range(D_PROJ // PROJ_CH):
        cols = slice(j * PROJ_CH, (j + 1) * PROJ_CH)
        proj_ref[:, cols] = jnp.dot(xb, w_ref[:, cols], preferred_element_type=F32)


def _in_proj(x, w, ln=None):
    m = x.shape[0]
    row = lambda i: (i, 0)
    const = lambda i: (0, 0)
    in_specs = [pl.BlockSpec((PROJ_TM, D_MODEL), row)]
    args = [x]
    out_shape = [jax.ShapeDtypeStruct((m, D_PROJ), F32)]
    out_specs = [pl.BlockSpec((PROJ_TM, D_PROJ), row)]
    if ln is not None:
        in_specs += [pl.BlockSpec((1, D_MODEL), const), pl.BlockSpec((1, D_MODEL), const)]
        args += [ln[0].reshape(1, D_MODEL), ln[1].reshape(1, D_MODEL)]
        out_shape.append(jax.ShapeDtypeStruct((m, D_MODEL), F32))
        out_specs.append(pl.BlockSpec((PROJ_TM, D_MODEL), row))
    in_specs.append(pl.BlockSpec((D_MODEL, D_PROJ), const))
    args.append(w)
    return pl.pallas_call(
        functools.partial(_in_proj_kernel, apply_ln=ln is not None),
        grid=(m // PROJ_TM,),
        in_specs=in_specs,
        out_specs=out_specs,
        out_shape=out_shape,
        compiler_params=_params(("parallel",)),
        name="in_proj",
    )(*args)


VT_ROWS = HEAD_DIM + 16
LOG2E = 1.4426950408889634


def _attn_kernel(q_ref, k_ref, v_ref, c_ref, sa_ref, sb_ref, o_ref, kb_ref, vt_ref, *, seq, value_lag):
    nb = seq // MOBA_BLOCK
    tq = MOBA_BLOCK
    lane = lax.broadcasted_iota(jnp.int32, (1, LANES), 1)

    def rope(x, rows):
        return (x * c_ref[0, rows, :] + pltpu.roll(x, LANES - ROPE_HALF, 1) * sa_ref[0, rows, :]
                + pltpu.roll(x, ROPE_HALF, 1) * sb_ref[0, rows, :])

    k = rope(k_ref[0], slice(None))
    kb_ref[...] = k.astype(BF16)
    k_mean = jnp.mean(k.reshape(nb, MOBA_BLOCK, LANES), axis=1)
    v_t = v_ref[0].T
    ones_rows = jnp.where(lax.broadcasted_iota(jnp.int32, (VT_ROWS - HEAD_DIM, seq), 0) == 0, 1.0, 0.0)
    for hh in range(2):
        vt_ref[hh] = jnp.concatenate([v_t[hh * HEAD_DIM:(hh + 1) * HEAD_DIM], ones_rows], axis=0).astype(BF16)

    qscale = HEAD_DIM ** -0.5 * LOG2E
    causal_t = (lax.broadcasted_iota(jnp.int32, (tq, tq), 0) <= lax.broadcasted_iota(jnp.int32, (tq, tq), 1))
    blk_row = lax.broadcasted_iota(jnp.int32, (nb, tq), 0)

    chain_q = {}
    chain_bias = {}
    chain_m = {}
    chain_o = {}
    q_tiles = {}

    def start_chain(j, hh):
        if hh == 0:
            rows = slice(j * tq, (j + 1) * tq)
            q_tiles[j] = rope(q_ref[0, rows, :], rows) * qscale
        q = q_tiles[j] if hh == 0 else q_tiles.pop(j)
        head_lanes = (lane < HEAD_DIM) if hh == 0 else (lane >= HEAD_DIM)
        qh_t = jnp.where(head_lanes, q, 0.0).T
        chain_q[(j, hh)] = qh_t.astype(BF16)
        chain_bias[(j, hh)] = None
        if j > MOBA_TOPK:
            gate_t = _dot_bf16x3(k_mean, qh_t)
            rank = jnp.zeros((nb, tq), F32)
            for m in range(j):
                g_m = gate_t[m:m + 1, :]
                beats = (g_m > gate_t) | ((g_m == gate_t) & (blk_row > m))
                rank = rank + jnp.where(beats & (blk_row != m), 1.0, 0.0)
            bias = jnp.where((blk_row < j) & (rank >= MOBA_TOPK), NEG, 0.0)
            chain_bias[(j, hh)] = [bias[n:n + 1, :] for n in range(j)]

    def scores(j, hh, n):
        return jnp.dot(kb_ref[n * tq:(n + 1) * tq, :], chain_q[(j, hh)], preferred_element_type=F32)

    def probs(j, hh, n, s_t):
        key = (j, hh)
        if n == j:
            s_t = jnp.where(causal_t, s_t, NEG)
            m_new = jnp.max(s_t, axis=0, keepdims=True)
            alpha = None
            shift = m_new
        else:
            bias_n = 0.0 if chain_bias[key] is None else chain_bias[key][n]
            m_new = jnp.maximum(chain_m[key], jnp.max(s_t, axis=0, keepdims=True) + bias_n)
            alpha = jnp.exp2(chain_m[key] - m_new)
            shift = m_new - bias_n
        chain_m[key] = m_new
        return jnp.exp2(s_t - shift).astype(BF16), alpha

    def values(j, hh, n, p_t, alpha):
        key = (j, hh)
        part = jnp.dot(vt_ref[hh, :, n * tq:(n + 1) * tq], p_t, preferred_element_type=F32)
        chain_o[key] = part if alpha is None else chain_o[key] * alpha + part

    def finish(j):
        halves = [chain_o.pop((j, hh)) for hh in range(2)]
        out_t = jnp.concatenate([o[0:HEAD_DIM] / o[HEAD_DIM:HEAD_DIM + 1] for o in halves], axis=0)
        o_ref[0, j * tq:(j + 1) * tq, :] = out_t.T

    units = []
    for j in range(nb):
        for n in [j] + list(range(j)):
            units += [(j, 0, n), (j, 1, n)]
    s_state = {}
    p_state = {}
    for step in range(len(units) + value_lag):
        if step >= value_lag:
            j, hh, n = unit = units[step - value_lag]
            values(j, hh, n, *p_state.pop(unit))
            last_block = n == (j - 1 if j > 0 else 0)
            if hh == 1 and last_block:
                finish(j)
        if step < len(units):
            j, hh, n = unit = units[step]
            if n == j:
                start_chain(j, hh)
            s_state[unit] = scores(j, hh, n)
        if 1 <= step <= len(units):
            j, hh, n = unit = units[step - 1]
            p_state[unit] = probs(j, hh, n, s_state.pop(unit))


def _attention(proj3, rope_c, rope_sa, rope_sb, value_lag):
    bsz, seq, _ = proj3.shape
    blk = (1, seq, LANES)
    col = lambda base: (lambda b, p: (b, 0, base // LANES + p))
    tab = lambda b, p: (b, 0, 0)
    return pl.pallas_call(
        functools.partial(_attn_kernel, seq=seq, value_lag=value_lag),
        grid=(bsz, ATT_W // LANES),
        in_specs=[pl.BlockSpec(blk, col(COL_Q)), pl.BlockSpec(blk, col(COL_K)), pl.BlockSpec(blk, col(COL_V)),
                  pl.BlockSpec(blk, tab), pl.BlockSpec(blk, tab), pl.BlockSpec(blk, tab)],
        out_specs=pl.BlockSpec(blk, lambda b, p: (b, 0, p)),
        out_shape=jax.ShapeDtypeStruct((bsz, seq, ATT_W), F32),
        scratch_shapes=[pltpu.VMEM((seq, LANES), BF16), pltpu.VMEM((2, VT_ROWS, seq), BF16)],
        compiler_params=_params(("parallel", "arbitrary")),
        name="moba_attention",
    )(proj3, proj3, proj3, rope_c, rope_sa, rope_sb)


assert HEADS_PER_GROUP == 3 and GROUP_PAD == 2 * LANES and SSD_HEAD_DIM * 2 == LANES


def _expand_heads(small):
    low = lax.broadcasted_iota(jnp.int32, (1, LANES), 1) < SSD_HEAD_DIM
    slabs = []
    for g in range(SSD_GROUPS):
        h0 = g * HEADS_PER_GROUP
        slabs.append(jnp.where(low, small[:, h0:h0 + 1], small[:, h0 + 1:h0 + 2]))
        slabs.append(jnp.where(low, small[:, h0 + 2:h0 + 3], 0.0))
    return jnp.concatenate(slabs, axis=1)


def _ssd_kernel(z_ref, xs_ref, b_ref, c_ref, dt_ref, cwx_ref, cwb_ref, cwc_ref, cbx_ref, cbb_ref, cbc_ref,
                dtb_ref, alog_ref, dskip_ref, ng_ref, o_ref, state_ref, *, seq):
    q = SSD_CHUNK
    n = SSD_STATE
    state_ref[...] = jnp.zeros_like(state_ref)
    row_i = lax.broadcasted_iota(jnp.int32, (q, q), 0)
    col_i = lax.broadcasted_iota(jnp.int32, (q, q), 1)
    lower = row_i >= col_i
    tril = jnp.where(lower, 1.0, 0.0)
    slab = lax.broadcasted_iota(jnp.int32, (1, GROUP_PAD), 1) // SSD_HEAD_DIM
    a_neg = -jnp.exp(alog_ref[...])

    def conv_silu(ref, w_ref, bias_ref, c):
        start = pl.multiple_of(c * q, q)
        cur = ref[0, pl.ds(start, q), :]
        prev_start = pl.multiple_of(jnp.maximum(c * q - 8, 0), 8)
        prev = jnp.where(c > 0, ref[0, pl.ds(prev_start, 8), :], 0.0)
        full = jnp.concatenate([prev, cur], axis=0)
        y = bias_ref[...] + w_ref[SSD_CONV - 1:SSD_CONV, :] * cur
        for i in range(SSD_CONV - 1):
            shift = SSD_CONV - 1 - i
            y = y + w_ref[i:i + 1, :] * pltpu.roll(full, shift, 0)[8:, :]
        return _silu(y)

    def chunk(c, carry):
        start = pl.multiple_of(c * q, q)
        xs = conv_silu(xs_ref, cwx_ref, cbx_ref, c)
        bm = conv_silu(b_ref, cwb_ref, cbb_ref, c)
        cm = conv_silu(c_ref, cwc_ref, cbc_ref, c)
        x = dt_ref[0, pl.ds(start, q), :] + dtb_ref[...]
        dt = jnp.maximum(x, 0.0) + jnp.log(1.0 + jnp.exp(-jnp.abs(x)))
        a = dt * a_neg
        a_cum = jnp.dot(tril, a, precision=HIGHEST, preferred_element_type=F32)
        a_cum_t = a_cum.T
        dt_e = _expand_heads(dt)
        acum_e = _expand_heads(a_cum)
        alast_e = acum_e[q - 1:q, :]
        xdt = xs * dt_e
        decay_e = jnp.exp(alast_e - acum_e)
        y_parts = []
        for g in range(SSD_GROUPS):
            gl = slice(g * GROUP_PAD, (g + 1) * GROUP_PAD)
            nl = slice(g * n, (g + 1) * n)
            b_g = bm[:, nl].astype(BF16)
            b_gt = bm[:, nl].T.astype(BF16)
            c_g = cm[:, nl].astype(BF16)
            cb = lax.dot_general(c_g, b_g, (((1,), (1,)), ((), ())), preferred_element_type=F32)
            xdt_g = xdt[:, gl]
            xdt_b = xdt_g.astype(BF16)
            y_g = jnp.zeros((q, GROUP_PAD), F32)
            for r in range(HEADS_PER_GROUP):
                h = g * HEADS_PER_GROUP + r
                seg = a_cum[:, h:h + 1] - a_cum_t[h:h + 1, :]
                l_h = jnp.where(lower, jnp.exp(jnp.minimum(seg, 0.0)), 0.0)
                y_h = jnp.dot((cb * l_h).astype(BF16), xdt_b, preferred_element_type=F32)
                y_g = jnp.where(slab == r, y_h, y_g)
            prev = state_ref[g]
            y_off = jnp.dot(c_g, prev.astype(BF16), preferred_element_type=F32) * jnp.exp(acum_e[:, gl])
            new = jnp.dot(b_gt, (xdt_g * decay_e[:, gl]).astype(BF16), preferred_element_type=F32)
            state_ref[g] = prev * jnp.exp(alast_e[:, gl]) + new
            y_parts.append(y_g + y_off)
        y = jnp.concatenate(y_parts, axis=1) + dskip_ref[...] * xs
        yz = y * _silu(z_ref[0, pl.ds(start, q), :])
        outs = []
        for g in range(SSD_GROUPS):
            yg = yz[:, g * GROUP_PAD:(g + 1) * GROUP_PAD]
            ms = jnp.sum(yg * yg, axis=-1, keepdims=True) * (1.0 / GROUP_W)
            outs.append(yg * lax.rsqrt(ms + LN_EPS))
        o_ref[0, pl.ds(start, q), :] = (jnp.concatenate(outs, axis=1) * ng_ref[...]).astype(BF16)
        return carry

    lax.fori_loop(0, seq // q, chunk, 0)


def _ssd(proj3, p):
    bsz, seq, _ = proj3.shape
    slab = lambda width, base: pl.BlockSpec((1, seq, width), lambda b: (b, 0, base // width))
    const2 = lambda a: pl.BlockSpec(a.shape, lambda b: (0, 0))
    small = [p["conv_w_xs"], p["conv_w_b"], p["conv_w_c"], p["conv_b_xs"], p["conv_b_b"], p["conv_b_c"],
             p["dt_bias"], p["a_log"], p["d_skip"], p["ssd_norm_g"]]
    return pl.pallas_call(
        functools.partial(_ssd_kernel, seq=seq),
        grid=(bsz,),
        in_specs=[slab(SSD_PAD, COL_Z), slab(SSD_PAD, COL_XS), slab(2 * SSD_STATE, COL_B),
                  slab(2 * SSD_STATE, COL_C), slab(LANES, COL_DT)] + [const2(a) for a in small],
        out_specs=pl.BlockSpec((1, seq, SSD_PAD), lambda b: (b, 0, 0)),
        out_shape=jax.ShapeDtypeStruct((bsz, seq, SSD_PAD), BF16),
        scratch_shapes=[pltpu.VMEM((SSD_GROUPS, SSD_STATE, GROUP_PAD), F32)],
        compiler_params=_params(("parallel",)),
        name="ssd",
    )(proj3, proj3, proj3, proj3, proj3, *small)


GM_ROWS = 1024


def _gmlp_kernel(u_ref, v_ref, lng_ref, lnb_ref, ws_ref, bs_ref, ng_ref, o_ref):
    q = GM_CHUNK
    row_i = lax.broadcasted_iota(jnp.int32, (q, q), 0)
    col_i = lax.broadcasted_iota(jnp.int32, (q, q), 1)
    lower = row_i >= col_i
    grp = lax.broadcasted_iota(jnp.int32, (1, GM_W), 1) // (GM_W // GM_GROUPS)
    w = [jnp.where(lower, ws_ref[g], 0.0).astype(BF16) for g in range(GM_GROUPS)]
    for c in range(GM_ROWS // q):
        rows = slice(c * q, (c + 1) * q)
        u = _gelu_tanh(u_ref[rows, :])
        v = _layer_norm(_gelu_tanh(v_ref[rows, :]), lng_ref[...], lnb_ref[...])
        vb = v.astype(BF16)
        mixed = jnp.zeros((q, GM_W), F32)
        for g in range(GM_GROUPS):
            mixed = jnp.where(grp == g, jnp.dot(w[g], vb, preferred_element_type=F32), mixed)
        gm = u * (mixed + bs_ref[...])
        ms = jnp.mean(gm * gm, axis=-1, keepdims=True)
        o_ref[rows, :] = (gm * lax.rsqrt(ms + LN_EPS) * ng_ref[...]).astype(BF16)


def _gmlp(proj, p):
    m = proj.shape[0]
    const = lambda a: pl.BlockSpec(a.shape, lambda i: (0,) * a.ndim)
    small = [p["gm_ln_g"], p["gm_ln_b"], p["gm_w_s"], p["gm_b_s"], p["gm_norm_g"]]
    return pl.pallas_call(
        _gmlp_kernel,
        grid=(m // GM_ROWS,),
        in_specs=[pl.BlockSpec((GM_ROWS, GM_W), lambda i: (i, COL_GU // GM_W)),
                  pl.BlockSpec((GM_ROWS, GM_W), lambda i: (i, COL_GV // GM_W))] + [const(a) for a in small],
        out_specs=pl.BlockSpec((GM_ROWS, GM_W), lambda i: (i, 0)),
        out_shape=jax.ShapeDtypeStruct((m, GM_W), BF16),
        compiler_params=_params(("parallel",)),
        name="gmlp",
    )(proj, proj, *small)


OUT_TM = 1024
LOGIT_ROWS = 16


def _out_proj_kernel(att_ref, ssd_ref, gm_ref, h_ref, ag_ref, wa_ref, ws_ref, wg_ref, g_ref, b_ref, o_ref,
                     router=None):
    att = att_ref[...]
    ms = jnp.mean(att * att, axis=-1, keepdims=True)
    att = att * lax.rsqrt(ms + LN_EPS) * ag_ref[...]
    mix = jnp.dot(att.astype(BF16), wa_ref[...], preferred_element_type=F32)
    mix = mix + jnp.dot(ssd_ref[...], ws_ref[...], preferred_element_type=F32)
    mix = mix + jnp.dot(gm_ref[...], wg_ref[...], preferred_element_type=F32)
    h1 = _layer_norm(ALPHA * h_ref[...] + mix, g_ref[...], b_ref[...])
    o_ref[...] = h1
    if router:
        wr_ref, logit_ref = router
        logit_ref[...] = _dot_bf16x3(wr_ref[...], h1.T)


def _out_proj_kernel_routed(att_ref, ssd_ref, gm_ref, h_ref, ag_ref, wa_ref, ws_ref, wg_ref, g_ref, b_ref, wr_ref,
                            o_ref, logit_ref):
    _out_proj_kernel(att_ref, ssd_ref, gm_ref, h_ref, ag_ref, wa_ref, ws_ref, wg_ref, g_ref, b_ref, o_ref,
                     router=(wr_ref, logit_ref))


def _out_proj(att, ssd, gm, h, p, w_router=None):
    m = h.shape[0]
    row = lambda i: (i, 0)
    const = lambda a: pl.BlockSpec(a.shape, lambda i: (0, 0))
    small = [p["att_norm_g"], p["w_out_att"], p["w_out_ssd"], p["w_out_gm"], p["ln1_g"], p["ln1_b"]]
    out_specs = [pl.BlockSpec((OUT_TM, D_MODEL), row)]
    out_shape = [jax.ShapeDtypeStruct((m, D_MODEL), F32)]
    body = _out_proj_kernel
    if w_router is not None:
        small.append(jnp.pad(w_router.T, ((0, LOGIT_ROWS - N_EXPERTS), (0, 0))))
        out_specs.append(pl.BlockSpec((LOGIT_ROWS, OUT_TM), lambda i: (0, i)))
        out_shape.append(jax.ShapeDtypeStruct((LOGIT_ROWS, m), F32))
        body = _out_proj_kernel_routed
    return pl.pallas_call(
        body,
        grid=(m // OUT_TM,),
        in_specs=[pl.BlockSpec((OUT_TM, ATT_W), row), pl.BlockSpec((OUT_TM, SSD_PAD), row),
                  pl.BlockSpec((OUT_TM, GM_W), row), pl.BlockSpec((OUT_TM, D_MODEL), row)]
                 + [const(a) for a in small],
        out_specs=out_specs,
        out_shape=out_shape,
        compiler_params=_params(("parallel",)),
        name="out_proj_ln1",
    )(att, ssd, gm, h, *small)


FFN_TM = 1024


DOWN_PIECE = 512


def _swiglu_chunks(acc_ref, xb, w_gate, w_up, w_down, widths):
    dot = functools.partial(jnp.dot, preferred_element_type=F32)
    bounds = [sum(widths[:c]) for c in range(len(widths) + 1)]
    chunk = lambda c: slice(bounds[c], bounds[c + 1])
    up = lambda c: (dot(xb, w_gate(chunk(c))), dot(xb, w_up(chunk(c))))
    pending = up(0)
    n_out = acc_ref.shape[-1]
    for c in range(len(widths)):
        hg, hu = pending
        if c + 1 < len(widths):
            pending = up(c + 1)
        hid = (_silu(hg) * hu).astype(BF16)
        w_d = w_down(chunk(c))
        for lo in range(0, n_out, DOWN_PIECE):
            acc_ref[:, lo:lo + DOWN_PIECE] += dot(hid, w_d[:, lo:lo + DOWN_PIECE])


FFN_CHUNK = 512


def _ffn_kernel(x_ref, wg_ref, wu_ref, wd_ref, g_ref, b_ref, o_ref, acc_ref):
    d_ff = wg_ref.shape[-1]
    widths = (FFN_CHUNK,) * (d_ff // FFN_CHUNK) + ((d_ff % FFN_CHUNK,) if d_ff % FFN_CHUNK else ())
    x = x_ref[...]
    acc_ref[...] = ALPHA * x
    _swiglu_chunks(acc_ref, x.astype(BF16), lambda c: wg_ref[:, c], lambda c: wu_ref[:, c],
                   lambda c: wd_ref[c, :], widths)
    o_ref[...] = _layer_norm(acc_ref[...], g_ref[...], b_ref[...])


def _ffn(x, wg, wu, wd, ln_g, ln_b):
    m = x.shape[0]
    resident = lambda a: pl.BlockSpec(a.shape, lambda i: (0, 0), pipeline_mode=pl.Buffered(1))
    return pl.pallas_call(
        _ffn_kernel,
        grid=(m // FFN_TM,),
        in_specs=[pl.BlockSpec((FFN_TM, D_MODEL), lambda i: (i, 0)), resident(wg), resident(wu), resident(wd),
                  pl.BlockSpec((1, D_MODEL), lambda i: (0, 0)),
                  pl.BlockSpec((1, D_MODEL), lambda i: (0, 0))],
        out_specs=pl.BlockSpec((FFN_TM, D_MODEL), lambda i: (i, 0)),
        out_shape=jax.ShapeDtypeStruct((m, D_MODEL), F32),
        scratch_shapes=[pltpu.VMEM((FFN_TM, D_MODEL), F32)],
        compiler_params=_params(("parallel",)),
        name="ffn_ln2",
    )(x, wg, wu, wd, ln_g.reshape(1, D_MODEL), ln_b.reshape(1, D_MODEL))


MOE_TR = 256
MOE_HALF = 512
MOE_UNITS = 3
MOE_TM = MOE_UNITS * MOE_HALF
MOE_TF = 512
MOE_CHUNKS = (256, 256)
SEG_ALIGN = 16
MOE_WIN_SMALL = 128
INFO_GATE = N_EXPERTS


def _expert_cap(m):
    n_tiles = m // MOE_TR
    rows = m + (SEG_ALIGN - 1) * n_tiles + MOE_TR + MOE_HALF
    return -(-rows // MOE_TM) * MOE_TM


def _rows_computed(lens):
    return jnp.maximum(-(-lens // MOE_HALF), 1) * MOE_HALF


def _dispatch_kernel(h_ref, logit_ref, info_ref, offs_ref, lens_ref, small_ref, xs_hbm, run_ref, xbuf_ref, zbuf_ref,
                     sem_ref, zsem_ref, *, n_tiles, cap):
    t = pl.program_id(0)
    tr = MOE_TR
    slot = t % 2

    @pl.when(t == 0)
    def _():
        for e in range(N_EXPERTS):
            run_ref[e] = 0

    xb = h_ref[...].astype(BF16)
    lg = logit_ref[0:N_EXPERTS, :]
    row = lax.broadcasted_iota(jnp.int32, (N_EXPERTS, tr), 0)
    m1 = jnp.max(lg, axis=0, keepdims=True)
    i1 = jnp.min(jnp.where(lg == m1, row, N_EXPERTS), axis=0, keepdims=True)
    rest = jnp.where(row == i1, -jnp.inf, lg)
    m2 = jnp.max(rest, axis=0, keepdims=True)
    i2 = jnp.min(jnp.where(rest == m2, row, N_EXPERTS), axis=0, keepdims=True)
    e2 = jnp.exp(m2 - m1)
    p1 = 1.0 / (1.0 + e2)
    gate_t = jnp.where(row == i1, p1, 0.0) + jnp.where(row == i2, e2 * p1, 0.0)
    sel_t = jnp.where((row == i1) | (row == i2), 1.0, 0.0)
    before = (lax.broadcasted_iota(jnp.int32, (tr, tr), 0) < lax.broadcasted_iota(jnp.int32, (tr, tr), 1))
    rank_t = jnp.dot(sel_t.astype(BF16), jnp.where(before, 1.0, 0.0).astype(BF16),
                     preferred_element_type=F32)
    cnt = jnp.sum(sel_t, axis=1, keepdims=True).astype(jnp.int32)
    info_t = jnp.concatenate([rank_t, gate_t, jnp.zeros((LANES - 2 * N_EXPERTS, tr), F32)], axis=0)
    info_ref[...] = info_t.T

    small = jnp.max(cnt) <= MOE_WIN_SMALL
    small_ref[t] = small.astype(jnp.int32)

    def seg_copy(sl, e, off, rows):
        return pltpu.make_async_copy(xbuf_ref.at[sl, e, pl.ds(0, rows)],
                                     xs_hbm.at[pl.ds(pl.multiple_of(off, SEG_ALIGN), rows)], sem_ref.at[sl, e])

    def by_window(flag, fn):
        pl.when(flag)(functools.partial(fn, MOE_WIN_SMALL))
        pl.when(jnp.logical_not(flag))(functools.partial(fn, tr))

    def compact(rows):
        dst_row = lax.broadcasted_iota(jnp.int32, (rows, tr), 0).astype(F32)
        onehots = [jnp.where((rank_t[e:e + 1, :] == dst_row) & (sel_t[e:e + 1, :] > 0.0), 1.0, 0.0).astype(BF16)
                   for e in range(N_EXPERTS)]
        onehot_all = jnp.concatenate(onehots, axis=0)
        half = D_MODEL // 2
        for c in range(2):
            packed = jnp.dot(onehot_all, xb[:, c * half:(c + 1) * half], preferred_element_type=F32)
            xbuf_ref[slot, :, 0:rows, c * half:(c + 1) * half] = packed.astype(BF16).reshape(N_EXPERTS, rows, half)

    by_window(small, compact)

    def wait_previous(rows):
        for e in range(N_EXPERTS):
            seg_copy(1 - slot, e, 0, rows).wait()

    @pl.when(t > 0)
    def _():
        by_window(small_ref[jnp.maximum(t - 1, 0)] == 1, wait_previous)

    def start_all(rows):
        for e in range(N_EXPERTS):
            seg_copy(slot, e, e * cap + run_ref[e], rows).start()

    by_window(small, start_all)
    for e in range(N_EXPERTS):
        offs_ref[t * N_EXPERTS + e] = e * cap + run_ref[e]
        run_ref[e] = run_ref[e] + jnp.bitwise_and(cnt[e, 0] + (SEG_ALIGN - 1), -SEG_ALIGN)

    def wait_current(rows):
        for e in range(N_EXPERTS):
            seg_copy(slot, e, 0, rows).wait()

    @pl.when(t == n_tiles - 1)
    def _():
        zbuf_ref[...] = jnp.zeros_like(zbuf_ref)
        by_window(small, wait_current)
        tails = []
        for e in range(N_EXPERTS):
            lens_ref[e] = run_ref[e]
            tail = pl.multiple_of(e * cap + run_ref[e], SEG_ALIGN)
            tails.append(pltpu.make_async_copy(zbuf_ref, xs_hbm.at[pl.ds(tail, MOE_HALF)], zsem_ref.at[e]))
            tails[-1].start()
        for cp in tails:
            cp.wait()


def _dispatch(h, logits):
    m = h.shape[0]
    n_tiles = m // MOE_TR
    cap = _expert_cap(m)
    smem = pl.BlockSpec(memory_space=pltpu.SMEM)
    return pl.pallas_call(
        functools.partial(_dispatch_kernel, n_tiles=n_tiles, cap=cap),
        grid=(n_tiles,),
        in_specs=[pl.BlockSpec((MOE_TR, D_MODEL), lambda t: (t, 0)),
                  pl.BlockSpec((LOGIT_ROWS, MOE_TR), lambda t: (0, t))],
        out_specs=[pl.BlockSpec((MOE_TR, LANES), lambda t: (t, 0)), smem, smem, smem,
                   pl.BlockSpec(memory_space=pl.ANY)],
        out_shape=[jax.ShapeDtypeStruct((m, LANES), F32),
                   jax.ShapeDtypeStruct((n_tiles * N_EXPERTS,), jnp.int32),
                   jax.ShapeDtypeStruct((N_EXPERTS,), jnp.int32),
                   jax.ShapeDtypeStruct((n_tiles,), jnp.int32),
                   jax.ShapeDtypeStruct((N_EXPERTS * cap, D_MODEL), BF16)],
        scratch_shapes=[pltpu.SMEM((N_EXPERTS,), jnp.int32),
                        pltpu.VMEM((2, N_EXPERTS, MOE_TR, D_MODEL), BF16),
                        pltpu.VMEM((MOE_HALF, D_MODEL), BF16),
                        pltpu.SemaphoreType.DMA((2, N_EXPERTS)),
                        pltpu.SemaphoreType.DMA((N_EXPERTS,))],
        compiler_params=_params(("arbitrary",)),
        name="moe_dispatch",
    )(h, logits)


def _expert_ffn_kernel(exp_ref, blk_ref, units_ref, x_ref, wg_ref, wu_ref, wd_ref, y_ref, acc_ref):
    w = pl.program_id(0)
    f = pl.program_id(1)

    def tile(rows):
        acc = acc_ref.at[pl.ds(0, rows)]

        @pl.when(f == 0)
        def _():
            acc[...] = jnp.zeros((rows, D_MODEL), F32)

        _swiglu_chunks(acc, x_ref[0:rows, :], lambda c: wg_ref[0, :, c].astype(BF16),
                       lambda c: wu_ref[0, :, c].astype(BF16), lambda c: wd_ref[0, c, :].astype(BF16), MOE_CHUNKS)

        @pl.when(f == pl.num_programs(1) - 1)
        def _():
            y_ref[0:rows, :] = acc[...].astype(BF16)

    for units in range(1, MOE_UNITS + 1):
        pl.when(units_ref[w] == units)(functools.partial(tile, units * MOE_HALF))


def _expert_ffn(xs, exp_w, blk_w, units_w, w_gate, w_up, w_down):
    d_ff = w_gate.shape[-1]
    n_f = d_ff // MOE_TF
    fcol = lambda w, f, exp, blk, units: jnp.where(units[w] != 0, f, n_f - 1)
    grid_spec = pltpu.PrefetchScalarGridSpec(
        num_scalar_prefetch=3,
        grid=(exp_w.shape[0], n_f),
        in_specs=[pl.BlockSpec((MOE_TM, D_MODEL), lambda w, f, exp, blk, valid: (blk[w], 0)),
                  pl.BlockSpec((1, D_MODEL, MOE_TF), lambda w, f, exp, blk, valid: (exp[w], 0, fcol(w, f, exp, blk, valid))),
                  pl.BlockSpec((1, D_MODEL, MOE_TF), lambda w, f, exp, blk, valid: (exp[w], 0, fcol(w, f, exp, blk, valid))),
                  pl.BlockSpec((1, MOE_TF, D_MODEL), lambda w, f, exp, blk, valid: (exp[w], fcol(w, f, exp, blk, valid), 0))],
        out_specs=pl.BlockSpec((MOE_TM, D_MODEL), lambda w, f, exp, blk, valid: (blk[w], 0)),
        scratch_shapes=[pltpu.VMEM((MOE_TM, D_MODEL), F32)])
    return pl.pallas_call(
        _expert_ffn_kernel,
        grid_spec=grid_spec,
        out_shape=jax.ShapeDtypeStruct(xs.shape, BF16),
        compiler_params=_params(("arbitrary", "arbitrary")),
        name="moe_expert_ffn",
    )(exp_w, blk_w, units_w, xs, w_gate, w_up, w_down)


def _combine_kernel(starts_ref, shifts_ref, small_ref, info_ref, h_ref, g_ref, b_ref, ys_hbm, o_ref, ybuf_ref,
                    sem_ref, *, n_tiles):
    t = pl.program_id(0)
    tr = MOE_TR
    slot = t % 2

    def seg_copy(tile, sl, e, rows):
        off = pl.multiple_of(starts_ref[tile * N_EXPERTS + e], SEG_ALIGN)
        return pltpu.make_async_copy(ys_hbm.at[pl.ds(off, rows)], ybuf_ref.at[sl, e, pl.ds(0, rows)],
                                     sem_ref.at[sl, e])

    def by_window(tile, fn):
        flag = small_ref[tile] == 1
        pl.when(flag)(functools.partial(fn, MOE_WIN_SMALL))
        pl.when(jnp.logical_not(flag))(functools.partial(fn, tr))

    def fetch(tile, sl):
        def start(rows):
            for e in range(N_EXPERTS):
                seg_copy(tile, sl, e, rows).start()
        by_window(tile, start)

    @pl.when(t == 0)
    def _():
        fetch(0, 0)

    @pl.when(t + 1 < n_tiles)
    def _():
        fetch(jnp.minimum(t + 1, n_tiles - 1), 1 - slot)

    def gather(rows):
        for e in range(N_EXPERTS):
            seg_copy(t, slot, e, rows).wait()
        info = info_ref[...]
        src_row = lax.broadcasted_iota(jnp.int32, (tr, rows), 1).astype(F32)
        acc = ALPHA * h_ref[...]
        for e in range(N_EXPERTS):
            rank_c = info[:, e:e + 1] + shifts_ref[t * N_EXPERTS + e].astype(F32)
            gate_c = info[:, INFO_GATE + e:INFO_GATE + e + 1]
            onehot = jnp.where((rank_c == src_row) & (gate_c != 0.0), 1.0, 0.0).astype(BF16)
            acc = acc + gate_c * jnp.dot(onehot, ybuf_ref[slot, e, 0:rows, :], preferred_element_type=F32)
        o_ref[...] = _layer_norm(acc, g_ref[...], b_ref[...])

    by_window(t, gather)


def _combine(starts, shifts, small, info, h, ys, ln_g, ln_b):
    m = h.shape[0]
    n_tiles = m // MOE_TR
    grid_spec = pltpu.PrefetchScalarGridSpec(
        num_scalar_prefetch=3,
        grid=(n_tiles,),
        in_specs=[pl.BlockSpec((MOE_TR, LANES), lambda t, *_: (t, 0)),
                  pl.BlockSpec((MOE_TR, D_MODEL), lambda t, *_: (t, 0)),
                  pl.BlockSpec((1, D_MODEL), lambda t, *_: (0, 0)),
                  pl.BlockSpec((1, D_MODEL), lambda t, *_: (0, 0)),
                  pl.BlockSpec(memory_space=pl.ANY)],
        out_specs=pl.BlockSpec((MOE_TR, D_MODEL), lambda t, *_: (t, 0)),
        scratch_shapes=[pltpu.VMEM((2, N_EXPERTS, MOE_TR, D_MODEL), BF16),
                        pltpu.SemaphoreType.DMA((2, N_EXPERTS))])
    return pl.pallas_call(
        functools.partial(_combine_kernel, n_tiles=n_tiles),
        grid_spec=grid_spec,
        out_shape=jax.ShapeDtypeStruct((m, D_MODEL), F32),
        compiler_params=_params(("arbitrary",)),
        name="moe_combine_ln2",
    )(starts, shifts, small, info, h, ln_g.reshape(1, D_MODEL), ln_b.reshape(1, D_MODEL), ys)


def _work_list(lens, m):
    cap = _expert_cap(m)
    n_tiles = m // MOE_TR
    max_rows = 2 * m + (SEG_ALIGN - 1) * min(N_EXPERTS * n_tiles, 2 * m)
    w_max = -(-(max_rows + N_EXPERTS * MOE_HALF) // MOE_TM) + N_EXPERTS
    rows_e = _rows_computed(lens)
    tiles_e = -(-rows_e // MOE_TM)
    ends = jnp.cumsum(tiles_e)
    w = jnp.arange(w_max, dtype=jnp.int32)
    wc = jnp.minimum(w, ends[-1] - 1)
    exp_w = jnp.sum((wc[:, None] >= ends[None, :]).astype(jnp.int32), axis=1)
    tile_w = wc - (ends - tiles_e)[exp_w]
    blk_w = exp_w * (cap // MOE_TM) + tile_w
    units_w = jnp.minimum((rows_e[exp_w] - tile_w * MOE_TM) // MOE_HALF, MOE_UNITS)
    units_w = jnp.where(w < ends[-1], units_w, 0)
    return exp_w.astype(jnp.int32), blk_w.astype(jnp.int32), units_w.astype(jnp.int32)


def _read_windows(offs, small, lens, m):
    cap = _expert_cap(m)
    region = jnp.arange(N_EXPERTS, dtype=jnp.int32) * cap
    rel = offs.reshape(-1, N_EXPERTS) - region
    width = jnp.where(small == 1, MOE_WIN_SMALL, MOE_TR)[:, None]
    start = jnp.minimum(rel, _rows_computed(lens)[None, :] - width)
    return (start + region).reshape(-1).astype(jnp.int32), (rel - start).reshape(-1).astype(jnp.int32)


def _moe_layer(h, logits, w_gate, w_up, w_down, ln_g, ln_b):
    m = h.shape[0]
    info, offs, lens, small, xs = _dispatch(h, logits)
    exp_w, blk_w, units_w = _work_list(lens, m)
    ys = _expert_ffn(xs, exp_w, blk_w, units_w, w_gate, w_up, w_down)
    starts, shifts = _read_windows(offs, small, lens, m)
    return _combine(starts, shifts, small, info, h, ys, ln_g, ln_b)


def _pad_groups(a):
    lead = a.shape[:-1]
    a = a.reshape(*lead, SSD_GROUPS, GROUP_W)
    a = jnp.pad(a, [(0, 0)] * len(lead) + [(0, 0), (0, GROUP_PAD - GROUP_W)])
    return a.reshape(*lead, SSD_PAD)


def _head_lanes(v):
    return jnp.pad(v, (0, LANES - SSD_HEADS)).reshape(1, LANES)


D_IN = 3 * ATT_W + SSD_W + (SSD_W + 2 * SSD_GROUPS * SSD_STATE) + SSD_HEADS + 2 * GM_W
PERM_ROWS = 256


def _permute_w_kernel(w_ref, o_ref):
    src_z = 3 * ATT_W
    src_xs = src_z + SSD_W
    src_b = src_xs + SSD_W
    src_c = src_b + SSD_GROUPS * SSD_STATE
    src_dt = src_c + SSD_GROUPS * SSD_STATE
    src_gu = src_dt + SSD_HEADS
    src_gv = src_gu + GM_W
    o_ref[...] = jnp.zeros_like(o_ref)

    def put(dst, src, width):
        o_ref[:, dst:dst + width] = w_ref[0, :, src:src + width].astype(BF16)

    for g in range(SSD_GROUPS):
        put(COL_Z + g * GROUP_PAD, src_z + g * GROUP_W, GROUP_W)
        put(COL_XS + g * GROUP_PAD, src_xs + g * GROUP_W, GROUP_W)
    put(COL_B, src_b, SSD_GROUPS * SSD_STATE)
    put(COL_C, src_c, SSD_GROUPS * SSD_STATE)
    put(COL_GU, src_gu, GM_W)
    put(COL_GV, src_gv, GM_W)
    put(COL_Q, 0, ATT_W)
    put(COL_K, ATT_W, ATT_W)
    put(COL_V, 2 * ATT_W, ATT_W)
    put(COL_DT, src_dt, SSD_HEADS)


def _permute_w_in(w_in, layer):
    return pl.pallas_call(
        _permute_w_kernel,
        grid=(D_MODEL // PERM_ROWS,),
        in_specs=[pl.BlockSpec((1, PERM_ROWS, D_IN), lambda i: (layer, i, 0))],
        out_specs=pl.BlockSpec((PERM_ROWS, D_PROJ), lambda i: (i, 0)),
        out_shape=jax.ShapeDtypeStruct((D_MODEL, D_PROJ), BF16),
        compiler_params=_params(("parallel",)),
        name="permute_w_in",
    )(w_in)


def _layer_params(i, w_in, conv_w, conv_b, dt_bias, a_log, d_skip, ssd_norm_g, att_norm_g, gm_ln_g, gm_ln_b,
                  gm_w_s, gm_b_s, gm_norm_g, w_out, ln1_g, ln1_b):
    cw, cb = conv_w[i], conv_b[i]
    nbc = SSD_GROUPS * SSD_STATE
    wo = w_out[i]
    return {
        "w_in": _permute_w_in(w_in, i),
        "conv_w_xs": _pad_groups(cw[:, :SSD_W]),
        "conv_w_b": cw[:, SSD_W:SSD_W + nbc],
        "conv_w_c": cw[:, SSD_W + nbc:],
        "conv_b_xs": _pad_groups(cb[:SSD_W]).reshape(1, SSD_PAD),
        "conv_b_b": cb[SSD_W:SSD_W + nbc].reshape(1, nbc),
        "conv_b_c": cb[SSD_W + nbc:].reshape(1, nbc),
        "dt_bias": _head_lanes(dt_bias[i]),
        "a_log": _head_lanes(a_log[i]),
        "d_skip": _pad_groups(jnp.repeat(d_skip[i], SSD_HEAD_DIM)).reshape(1, SSD_PAD),
        "ssd_norm_g": _pad_groups(ssd_norm_g[i]).reshape(1, SSD_PAD),
        "att_norm_g": att_norm_g[i].reshape(1, ATT_W),
        "gm_ln_g": gm_ln_g[i].reshape(1, GM_W),
        "gm_ln_b": gm_ln_b[i].reshape(1, GM_W),
        "gm_w_s": gm_w_s[i],
        "gm_b_s": jnp.repeat(gm_b_s[i].T, GM_W // GM_GROUPS, axis=1),
        "gm_norm_g": gm_norm_g[i].reshape(1, GM_W),
        "w_out_att": wo[:ATT_W].astype(BF16),
        "w_out_ssd": _pad_groups(wo[ATT_W:ATT_W + SSD_W].T).T.astype(BF16),
        "w_out_gm": wo[ATT_W + SSD_W:].astype(BF16),
        "ln1_g": ln1_g[i].reshape(1, D_MODEL),
        "ln1_b": ln1_b[i].reshape(1, D_MODEL),
    }


def _rope_tables(positions):
    inv = ROPE_THETA ** (-jnp.arange(0, ROPE_DIM, 2, dtype=F32) / ROPE_DIM)
    d = jnp.arange(LANES) % HEAD_DIM
    inv_lane = jnp.where(d < ROPE_DIM, inv[d % ROPE_HALF], 0.0)
    ang = positions.astype(F32)[..., None] * inv_lane
    cos, sin = jnp.cos(ang), jnp.sin(ang)
    return cos, jnp.where(d < ROPE_HALF, -sin, 0.0), jnp.where(d >= ROPE_HALF, sin, 0.0)


def kernel(x, positions, ln_in_g, ln_in_b, w_in, conv_w, conv_b, dt_bias, a_log, d_skip, ssd_norm_g, att_norm_g, gm_ln_g, gm_ln_b, gm_w_s, gm_b_s, gm_norm_g, w_out, ln1_g, ln1_b, ln2_g, ln2_b, ffn_w_gate, ffn_w_up, ffn_w_down, router_w, moe_w_gate, moe_w_up, moe_w_down):
    bsz, seq, _ = x.shape
    m = bsz * seq
    rope_c, rope_sa, rope_sb = _rope_tables(positions)
    h = x.reshape(m, D_MODEL)
    for i in range(DEPTH):
        p = _layer_params(i, w_in, conv_w, conv_b, dt_bias, a_log, d_skip, ssd_norm_g, att_norm_g, gm_ln_g,
                          gm_ln_b, gm_w_s, gm_b_s, gm_norm_g, w_out, ln1_g, ln1_b)
        if i == 0:
            proj, h = _in_proj(h, p["w_in"], ln=(ln_in_g, ln_in_b))
        else:
            (proj,) = _in_proj(h, p["w_in"])
        proj3 = proj.reshape(bsz, seq, D_PROJ)
        att = _attention(proj3, rope_c, rope_sa, rope_sb, value_lag=6 if i == 0 else 8).reshape(m, ATT_W)
        ssd = _ssd(proj3, p).reshape(m, SSD_PAD)
        gm = _gmlp(proj, p)
        j = i // 2
        if i % 2 == 0:
            (h,) = _out_proj(att, ssd, gm, h, p)
            h = _ffn(h, ffn_w_gate[j].astype(BF16), ffn_w_up[j].astype(BF16), ffn_w_down[j].astype(BF16),
                     ln2_g[i], ln2_b[i])
        else:
            h, logits = _out_proj(att, ssd, gm, h, p, w_router=router_w[j])
            h = _moe_layer(h, logits, moe_w_gate[j], moe_w_up[j], moe_w_down[j], ln2_g[i], ln2_b[i])
    return h.reshape(bsz, seq, D_MODEL)
```

```python
import functools

import jax
import jax.numpy as jnp
from jax import lax
from jax.experimental import pallas as pl
from jax.experimental.pallas import tpu as pltpu

F32 = jnp.float32
BF16 = jnp.bfloat16
HIGHEST = lax.Precision.HIGHEST

D_MODEL = 1024
ATT_HEADS = 6
HEAD_DIM = 64
ATT_W = ATT_HEADS * HEAD_DIM
ROPE_DIM = HEAD_DIM // 4
ROPE_HALF = ROPE_DIM // 2
ROPE_THETA = 500000.0
MOBA_BLOCK = 256
MOBA_TOPK = 3
SSD_HEADS = 6
SSD_HEAD_DIM = 64
SSD_W = SSD_HEADS * SSD_HEAD_DIM
SSD_GROUPS = 2
SSD_STATE = 128
SSD_CONV = 4
SSD_CHUNK = 128
GM_GROUPS = 4
GM_W = 256
GM_CHUNK = 128
N_EXPERTS = 8
DEPTH = 2
ALPHA = (2.0 * DEPTH) ** 0.25
NEG = -1e30
LN_EPS = 1e-5

LANES = 128
GROUP_W = SSD_W // SSD_GROUPS
GROUP_PAD = 256
SSD_PAD = SSD_GROUPS * GROUP_PAD
HEADS_PER_GROUP = SSD_HEADS // SSD_GROUPS

COL_Z = 0
COL_XS = 512
COL_B = 1024
COL_C = 1280
COL_Q = 1536
COL_K = 1920
COL_V = 2304
COL_DT = 2688
D_PROJ = 2816
COL_GU = 2816
COL_GV = 3072
D_W = 3328

VMEM_LIMIT = 56 * 1024 * 1024


def _params(sem, vmem=VMEM_LIMIT):
    return pltpu.CompilerParams(dimension_semantics=sem, vmem_limit_bytes=vmem)


def _layer_norm(x, g, b):
    mu = jnp.mean(x, axis=-1, keepdims=True)
    xc = x - mu
    var = jnp.mean(xc * xc, axis=-1, keepdims=True)
    return xc * lax.rsqrt(var + LN_EPS) * g + b


def _silu(x):
    half = 0.5 * x
    return half + half * jnp.tanh(half)


def _split_bf16(x):
    hi = x.astype(BF16)
    return hi, (x - hi.astype(F32)).astype(BF16)


def _dot_bf16x3(a, b):
    a_hi, a_lo = _split_bf16(a)
    b_hi, b_lo = _split_bf16(b)
    dot = functools.partial(jnp.dot, preferred_element_type=F32)
    return dot(a_hi, b_hi) + (dot(a_lo, b_hi) + dot(a_hi, b_lo))


def _gelu_tanh(x):
    return 0.5 * x * (1.0 + jnp.tanh(0.7978845608028654 * (x + 0.044715 * x * x * x)))


PROJ_TM = 512
PROJ_CH = 256
PROJ_TAIL = 4


def _in_proj_kernel(*refs, apply_ln):
    if apply_ln:
        x_ref, g_ref, b_ref, w_ref, lng_ref, lnb_ref, ws_ref, bs_ref, ng_ref, proj_ref, gm_ref, h_ref = refs
        x = _layer_norm(x_ref[...], g_ref[...], b_ref[...])
        h_ref[...] = x
    else:
        x_ref, w_ref, lng_ref, lnb_ref, ws_ref, bs_ref, ng_ref, proj_ref, gm_ref = refs
        x = x_ref[...]
    xb = x.astype(BF16)

    def project(j):
        cols = slice(j * PROJ_CH, (j + 1) * PROJ_CH)
        proj_ref[:, cols] = jnp.dot(xb, w_ref[:, cols], preferred_element_type=F32)

    gu = jnp.dot(xb, w_ref[:, COL_GU:COL_GU + GM_W], preferred_element_type=F32)
    gv = jnp.dot(xb, w_ref[:, COL_GV:COL_GV + GM_W], preferred_element_type=F32)
    u, vb = _gmlp_activate(gu, gv, lng_ref, lnb_ref)
    n_chunks = D_PROJ // PROJ_CH
    for j in range(n_chunks - PROJ_TAIL):
        project(j)
    _gmlp_mix(u, vb, ws_ref, bs_ref, ng_ref, gm_ref)
    for j in range(n_chunks - PROJ_TAIL, n_chunks):
        project(j)


def _in_proj(x, w, p, ln=None):
    m = x.shape[0]
    row = lambda i: (i, 0)
    const = lambda i: (0, 0)
    whole = lambda a: pl.BlockSpec(a.shape, lambda i: (0,) * a.ndim)
    in_specs = [pl.BlockSpec((PROJ_TM, D_MODEL), row)]
    args = [x]
    out_shape = [jax.ShapeDtypeStruct((m, D_PROJ), F32), jax.ShapeDtypeStruct((m, GM_W), BF16)]
    out_specs = [pl.BlockSpec((PROJ_TM, D_PROJ), row), pl.BlockSpec((PROJ_TM, GM_W), row)]
    if ln is not None:
        in_specs += [pl.BlockSpec((1, D_MODEL), const), pl.BlockSpec((1, D_MODEL), const)]
        args += [ln[0].reshape(1, D_MODEL), ln[1].reshape(1, D_MODEL)]
        out_shape.append(jax.ShapeDtypeStruct((m, D_MODEL), F32))
        out_specs.append(pl.BlockSpec((PROJ_TM, D_MODEL), row))
    in_specs.append(pl.BlockSpec((D_MODEL, D_W), const))
    args.append(w)
    gm_small = [p["gm_ln_g"], p["gm_ln_b"], p["gm_w_s"], p["gm_b_s"], p["gm_norm_g"]]
    in_specs += [whole(a) for a in gm_small]
    args += gm_small
    return pl.pallas_call(
        functools.partial(_in_proj_kernel, apply_ln=ln is not None),
        grid=(m // PROJ_TM,),
        in_specs=in_specs,
        out_specs=out_specs,
        out_shape=out_shape,
        compiler_params=_params(("parallel",)),
        name="in_proj",
    )(*args)


VT_ROWS = HEAD_DIM + 16
LOG2E = 1.4426950408889634


def _attn_kernel(q_ref, k_ref, v_ref, c_ref, sa_ref, sb_ref, o_ref, kb_ref, vt_ref, *, seq, value_lag):
    nb = seq // MOBA_BLOCK
    tq = MOBA_BLOCK
    lane = lax.broadcasted_iota(jnp.int32, (1, LANES), 1)

    def rope(x, rows):
        return (x * c_ref[0, rows, :] + pltpu.roll(x, LANES - ROPE_HALF, 1) * sa_ref[0, rows, :]
                + pltpu.roll(x, ROPE_HALF, 1) * sb_ref[0, rows, :])

    k = rope(k_ref[0], slice(None))
    kb_ref[...] = k.astype(BF16)
    k_mean = jnp.mean(k.reshape(nb, MOBA_BLOCK, LANES), axis=1)
    v_t = v_ref[0].T
    ones_rows = jnp.where(lax.broadcasted_iota(jnp.int32, (VT_ROWS - HEAD_DIM, seq), 0) == 0, 1.0, 0.0)
    for hh in range(2):
        vt_ref[hh] = jnp.concatenate([v_t[hh * HEAD_DIM:(hh + 1) * HEAD_DIM], ones_rows], axis=0).astype(BF16)

    qscale = HEAD_DIM ** -0.5 * LOG2E
    causal_t = (lax.broadcasted_iota(jnp.int32, (tq, tq), 0) <= lax.broadcasted_iota(jnp.int32, (tq, tq), 1))
    blk_row = lax.broadcasted_iota(jnp.int32, (nb, tq), 0)

    chain_q = {}
    chain_bias = {}
    chain_m = {}
    chain_o = {}
    q_tiles = {}

    def start_chain(j, hh):
        if hh == 0:
            rows = slice(j * tq, (j + 1) * tq)
            q_tiles[j] = rope(q_ref[0, rows, :], rows) * qscale
        q = q_tiles[j] if hh == 0 else q_tiles.pop(j)
        head_lanes = (lane < HEAD_DIM) if hh == 0 else (lane >= HEAD_DIM)
        qh_t = jnp.where(head_lanes, q, 0.0).T
        chain_q[(j, hh)] = qh_t.astype(BF16)
        chain_bias[(j, hh)] = None
        if j > MOBA_TOPK:
            gate_t = _dot_bf16x3(k_mean, qh_t)
            rank = jnp.zeros((nb, tq), F32)
            for m in range(j):
                g_m = gate_t[m:m + 1, :]
                beats = (g_m > gate_t) | ((g_m == gate_t) & (blk_row > m))
                rank = rank + jnp.where(beats & (blk_row != m), 1.0, 0.0)
            bias = jnp.where((blk_row < j) & (rank >= MOBA_TOPK), NEG, 0.0)
            chain_bias[(j, hh)] = [bias[n:n + 1, :] for n in range(j)]

    def scores(j, hh, n):
        return jnp.dot(kb_ref[n * tq:(n + 1) * tq, :], chain_q[(j, hh)], preferred_element_type=F32)

    def probs(j, hh, n, s_t):
        key = (j, hh)
        if n == j:
            s_t = jnp.where(causal_t, s_t, NEG)
            m_new = jnp.max(s_t, axis=0, keepdims=True)
            alpha = None
            shift = m_new
        else:
            bias_n = 0.0 if chain_bias[key] is None else chain_bias[key][n]
            m_new = jnp.maximum(chain_m[key], jnp.max(s_t, axis=0, keepdims=True) + bias_n)
            alpha = jnp.exp2(chain_m[key] - m_new)
            shift = m_new - bias_n
        chain_m[key] = m_new
        return jnp.exp2(s_t - shift).astype(BF16), alpha

    def values(j, hh, n, p_t, alpha):
        key = (j, hh)
        part = jnp.dot(vt_ref[hh, :, n * tq:(n + 1) * tq], p_t, preferred_element_type=F32)
        chain_o[key] = part if alpha is None else chain_o[key] * alpha + part

    def finish(j):
        halves = [chain_o.pop((j, hh)) for hh in range(2)]
        out_t = jnp.concatenate([o[0:HEAD_DIM] / o[HEAD_DIM:HEAD_DIM + 1] for o in halves], axis=0)
        o_ref[0, j * tq:(j + 1) * tq, :] = out_t.T

    units = []
    for j in range(nb):
        for n in [j] + list(range(j)):
            units += [(j, 0, n), (j, 1, n)]
    s_state = {}
    p_state = {}
    for step in range(len(units) + value_lag):
        if step >= value_lag:
            j, hh, n = unit = units[step - value_lag]
            values(j, hh, n, *p_state.pop(unit))
            last_block = n == (j - 1 if j > 0 else 0)
            if hh == 1 and last_block:
                finish(j)
        if step < len(units):
            j, hh, n = unit = units[step]
            if n == j:
                start_chain(j, hh)
            s_state[unit] = scores(j, hh, n)
        if 1 <= step <= len(units):
            j, hh, n = unit = units[step - 1]
            p_state[unit] = probs(j, hh, n, s_state.pop(unit))


def _attention(proj3, rope_c, rope_sa, rope_sb, value_lag):
    bsz, seq, _ = proj3.shape
    blk = (1, seq, LANES)
    col = lambda base: (lambda b, p: (b, 0, base // LANES + p))
    tab = lambda b, p: (b, 0, 0)
    return pl.pallas_call(
        functools.partial(_attn_kernel, seq=seq, value_lag=value_lag),
        grid=(bsz, ATT_W // LANES),
        in_specs=[pl.BlockSpec(blk, col(COL_Q)), pl.BlockSpec(blk, col(COL_K)), pl.BlockSpec(blk, col(COL_V)),
                  pl.BlockSpec(blk, tab), pl.BlockSpec(blk, tab), pl.BlockSpec(blk, tab)],
        out_specs=pl.BlockSpec(blk, lambda b, p: (b, 0, p)),
        out_shape=jax.ShapeDtypeStruct((bsz, seq, ATT_W), F32),
        scratch_shapes=[pltpu.VMEM((seq, LANES), BF16), pltpu.VMEM((2, VT_ROWS, seq), BF16)],
        compiler_params=_params(("parallel", "arbitrary")),
        name="moba_attention",
    )(proj3, proj3, proj3, rope_c, rope_sa, rope_sb)


assert HEADS_PER_GROUP == 3 and GROUP_PAD == 2 * LANES and SSD_HEAD_DIM * 2 == LANES


def _expand_heads(small):
    low = lax.broadcasted_iota(jnp.int32, (1, LANES), 1) < SSD_HEAD_DIM
    slabs = []
    for g in range(SSD_GROUPS):
        h0 = g * HEADS_PER_GROUP
        slabs.append(jnp.where(low, small[:, h0:h0 + 1], small[:, h0 + 1:h0 + 2]))
        slabs.append(jnp.where(low, small[:, h0 + 2:h0 + 3], 0.0))
    return jnp.concatenate(slabs, axis=1)


def _ssd_kernel(z_ref, xs_ref, b_ref, c_ref, dt_ref, cwx_ref, cwb_ref, cwc_ref, cbx_ref, cbb_ref, cbc_ref,
                dtb_ref, alog_ref, dskip_ref, ng_ref, o_ref, state_ref, *, seq):
    q = SSD_CHUNK
    n = SSD_STATE
    state_ref[...] = jnp.zeros_like(state_ref)
    row_i = lax.broadcasted_iota(jnp.int32, (q, q), 0)
    col_i = lax.broadcasted_iota(jnp.int32, (q, q), 1)
    lower = row_i >= col_i
    tril = jnp.where(lower, 1.0, 0.0)
    slab = lax.broadcasted_iota(jnp.int32, (1, GROUP_PAD), 1) // SSD_HEAD_DIM
    a_neg = -jnp.exp(alog_ref[...])

    def conv_silu(ref, w_ref, bias_ref, c):
        start = pl.multiple_of(c * q, q)
        cur = ref[0, pl.ds(start, q), :]
        prev_start = pl.multiple_of(jnp.maximum(c * q - 8, 0), 8)
        prev = jnp.where(c > 0, ref[0, pl.ds(prev_start, 8), :], 0.0)
        full = jnp.concatenate([prev, cur], axis=0)
        y = bias_ref[...] + w_ref[SSD_CONV - 1:SSD_CONV, :] * cur
        for i in range(SSD_CONV - 1):
            shift = SSD_CONV - 1 - i
            y = y + w_ref[i:i + 1, :] * pltpu.roll(full, shift, 0)[8:, :]
        return _silu(y)

    def chunk(c, carry):
        start = pl.multiple_of(c * q, q)
        xs = conv_silu(xs_ref, cwx_ref, cbx_ref, c)
        bm = conv_silu(b_ref, cwb_ref, cbb_ref, c)
        cm = conv_silu(c_ref, cwc_ref, cbc_ref, c)
        x = dt_ref[0, pl.ds(start, q), :] + dtb_ref[...]
        dt = jnp.maximum(x, 0.0) + jnp.log(1.0 + jnp.exp(-jnp.abs(x)))
        a = dt * a_neg
        a_cum = jnp.dot(tril, a, precision=HIGHEST, preferred_element_type=F32)
        a_cum_t = a_cum.T
        dt_e = _expand_heads(dt)
        acum_e = _expand_heads(a_cum)
        alast_e = acum_e[q - 1:q, :]
        xdt = xs * dt_e
        decay_e = jnp.exp(alast_e - acum_e)
        y_parts = []
        for g in range(SSD_GROUPS):
            gl = slice(g * GROUP_PAD, (g + 1) * GROUP_PAD)
            nl = slice(g * n, (g + 1) * n)
            b_g = bm[:, nl].astype(BF16)
            b_gt = bm[:, nl].T.astype(BF16)
            c_g = cm[:, nl].astype(BF16)
            cb = lax.dot_general(c_g, b_g, (((1,), (1,)), ((), ())), preferred_element_type=F32)
            xdt_g = xdt[:, gl]
            xdt_b = xdt_g.astype(BF16)
            y_g = jnp.zeros((q, GROUP_PAD), F32)
            for r in range(HEADS_PER_GROUP):
                h = g * HEADS_PER_GROUP + r
                seg = a_cum[:, h:h + 1] - a_cum_t[h:h + 1, :]
                l_h = jnp.where(lower, jnp.exp(jnp.minimum(seg, 0.0)), 0.0)
                y_h = jnp.dot((cb * l_h).astype(BF16), xdt_b, preferred_element_type=F32)
                y_g = jnp.where(slab == r, y_h, y_g)
            prev = state_ref[g]
            y_off = jnp.dot(c_g, prev.astype(BF16), preferred_element_type=F32) * jnp.exp(acum_e[:, gl])
            new = jnp.dot(b_gt, (xdt_g * decay_e[:, gl]).astype(BF16), preferred_element_type=F32)
            state_ref[g] = prev * jnp.exp(alast_e[:, gl]) + new
            y_parts.append(y_g + y_off)
        y = jnp.concatenate(y_parts, axis=1) + dskip_ref[...] * xs
        yz = y * _silu(z_ref[0, pl.ds(start, q), :])
        outs = []
        for g in range(SSD_GROUPS):
            yg = yz[:, g * GROUP_PAD:(g + 1) * GROUP_PAD]
            ms = jnp.sum(yg * yg, axis=-1, keepdims=True) * (1.0 / GROUP_W)
            outs.append(yg * lax.rsqrt(ms + LN_EPS))
        o_ref[0, pl.ds(start, q), :] = (jnp.concatenate(outs, axis=1) * ng_ref[...]).astype(BF16)
        return carry

    lax.fori_loop(0, seq // q, chunk, 0)


def _ssd(proj3, p):
    bsz, seq, _ = proj3.shape
    slab = lambda width, base: pl.BlockSpec((1, seq, width), lambda b: (b, 0, base // width))
    const2 = lambda a: pl.BlockSpec(a.shape, lambda b: (0, 0))
    small = [p["conv_w_xs"], p["conv_w_b"], p["conv_w_c"], p["conv_b_xs"], p["conv_b_b"], p["conv_b_c"],
             p["dt_bias"], p["a_log"], p["d_skip"], p["ssd_norm_g"]]
    return pl.pallas_call(
        functools.partial(_ssd_kernel, seq=seq),
        grid=(bsz,),
        in_specs=[slab(SSD_PAD, COL_Z), slab(SSD_PAD, COL_XS), slab(2 * SSD_STATE, COL_B),
                  slab(2 * SSD_STATE, COL_C), slab(LANES, COL_DT)] + [const2(a) for a in small],
        out_specs=pl.BlockSpec((1, seq, SSD_PAD), lambda b: (b, 0, 0)),
        out_shape=jax.ShapeDtypeStruct((bsz, seq, SSD_PAD), BF16),
        scratch_shapes=[pltpu.VMEM((SSD_GROUPS, SSD_STATE, GROUP_PAD), F32)],
        compiler_params=_params(("parallel",)),
        name="ssd",
    )(proj3, proj3, proj3, proj3, proj3, *small)


def _gmlp_activate(gu, gv, lng_ref, lnb_ref):
    return _gelu_tanh(gu), _layer_norm(_gelu_tanh(gv), lng_ref[...], lnb_ref[...]).astype(BF16)


def _gmlp_mix(u, vb, ws_ref, bs_ref, ng_ref, o_ref):
    q = GM_CHUNK
    row_i = lax.broadcasted_iota(jnp.int32, (q, q), 0)
    col_i = lax.broadcasted_iota(jnp.int32, (q, q), 1)
    lower = row_i >= col_i
    grp = lax.broadcasted_iota(jnp.int32, (1, GM_W), 1) // (GM_W // GM_GROUPS)
    w = [jnp.where(lower, ws_ref[g], 0.0).astype(BF16) for g in range(GM_GROUPS)]
    for c in range(u.shape[0] // q):
        rows = slice(c * q, (c + 1) * q)
        mixed = jnp.zeros((q, GM_W), F32)
        for g in range(GM_GROUPS):
            mixed = jnp.where(grp == g, jnp.dot(w[g], vb[rows, :], preferred_element_type=F32), mixed)
        gm = u[rows, :] * (mixed + bs_ref[...])
        ms = jnp.mean(gm * gm, axis=-1, keepdims=True)
        o_ref[rows, :] = (gm * lax.rsqrt(ms + LN_EPS) * ng_ref[...]).astype(BF16)


OUT_TM = 1024
LOGIT_ROWS = 16


def _out_proj_kernel(att_ref, ssd_ref, gm_ref, h_ref, ag_ref, wa_ref, ws_ref, wg_ref, g_ref, b_ref, o_ref,
                     router=None):
    att = att_ref[...]
    ms = jnp.mean(att * att, axis=-1, keepdims=True)
    att = att * lax.rsqrt(ms + LN_EPS) * ag_ref[...]
    mix = jnp.dot(att.astype(BF16), wa_ref[...], preferred_element_type=F32)
    mix = mix + jnp.dot(ssd_ref[...], ws_ref[...], preferred_element_type=F32)
    mix = mix + jnp.dot(gm_ref[...], wg_ref[...], preferred_element_type=F32)
    h1 = _layer_norm(ALPHA * h_ref[...] + mix, g_ref[...], b_ref[...])
    o_ref[...] = h1
    if router:
        wr_ref, logit_ref = router
        logit_ref[...] = _dot_bf16x3(wr_ref[...], h1.T)


def _out_proj_kernel_routed(att_ref, ssd_ref, gm_ref, h_ref, ag_ref, wa_ref, ws_ref, wg_ref, g_ref, b_ref, wr_ref,
                            o_ref, logit_ref):
    _out_proj_kernel(att_ref, ssd_ref, gm_ref, h_ref, ag_ref, wa_ref, ws_ref, wg_ref, g_ref, b_ref, o_ref,
                     router=(wr_ref, logit_ref))


def _out_proj(att, ssd, gm, h, p, w_router=None):
    m = h.shape[0]
    row = lambda i: (i, 0)
    const = lambda a: pl.BlockSpec(a.shape, lambda i: (0, 0))
    small = [p["att_norm_g"], p["w_out_att"], p["w_out_ssd"], p["w_out_gm"], p["ln1_g"], p["ln1_b"]]
    out_specs = [pl.BlockSpec((OUT_TM, D_MODEL), row)]
    out_shape = [jax.ShapeDtypeStruct((m, D_MODEL), F32)]
    body = _out_proj_kernel
    if w_router is not None:
        small.append(jnp.pad(w_router.T, ((0, LOGIT_ROWS - N_EXPERTS), (0, 0))))
        out_specs.append(pl.BlockSpec((LOGIT_ROWS, OUT_TM), lambda i: (0, i)))
        out_shape.append(jax.ShapeDtypeStruct((LOGIT_ROWS, m), F32))
        body = _out_proj_kernel_routed
    return pl.pallas_call(
        body,
        grid=(m // OUT_TM,),
        in_specs=[pl.BlockSpec((OUT_TM, ATT_W), row), pl.BlockSpec((OUT_TM, SSD_PAD), row),
                  pl.BlockSpec((OUT_TM, GM_W), row), pl.BlockSpec((OUT_TM, D_MODEL), row)]
                 + [const(a) for a in small],
        out_specs=out_specs,
        out_shape=out_shape,
        compiler_params=_params(("parallel",)),
        name="out_proj_ln1",
    )(att, ssd, gm, h, *small)


FFN_TM = 1024


DOWN_PIECE = 512


def _swiglu_chunks(acc_ref, xb, w_gate, w_up, w_down, widths):
    dot = functools.partial(jnp.dot, preferred_element_type=F32)
    bounds = [sum(widths[:c]) for c in range(len(widths) + 1)]
    chunk = lambda c: slice(bounds[c], bounds[c + 1])
    up = lambda c: (dot(xb, w_gate(chunk(c))), dot(xb, w_up(chunk(c))))
    pending = up(0)
    n_out = acc_ref.shape[-1]
    for c in range(len(widths)):
        hg, hu = pending
        if c + 1 < len(widths):
            pending = up(c + 1)
        hid = (_silu(hg) * hu).astype(BF16)
        w_d = w_down(chunk(c))
        for lo in range(0, n_out, DOWN_PIECE):
            acc_ref[:, lo:lo + DOWN_PIECE] += dot(hid, w_d[:, lo:lo + DOWN_PIECE])


FFN_CHUNK = 512


def _ffn_kernel(x_ref, wg_ref, wu_ref, wd_ref, g_ref, b_ref, o_ref, acc_ref):
    d_ff = wg_ref.shape[-1]
    widths = (FFN_CHUNK,) * (d_ff // FFN_CHUNK) + ((d_ff % FFN_CHUNK,) if d_ff % FFN_CHUNK else ())
    x = x_ref[...]
    acc_ref[...] = ALPHA * x
    _swiglu_chunks(acc_ref, x.astype(BF16), lambda c: wg_ref[:, c], lambda c: wu_ref[:, c],
                   lambda c: wd_ref[c, :], widths)
    o_ref[...] = _layer_norm(acc_ref[...], g_ref[...], b_ref[...])


def _ffn(x, wg, wu, wd, ln_g, ln_b):
    m = x.shape[0]
    resident = lambda a: pl.BlockSpec(a.shape, lambda i: (0, 0), pipeline_mode=pl.Buffered(1))
    return pl.pallas_call(
        _ffn_kernel,
        grid=(m // FFN_TM,),
        in_specs=[pl.BlockSpec((FFN_TM, D_MODEL), lambda i: (i, 0)), resident(wg), resident(wu), resident(wd),
                  pl.BlockSpec((1, D_MODEL), lambda i: (0, 0)),
                  pl.BlockSpec((1, D_MODEL), lambda i: (0, 0))],
        out_specs=pl.BlockSpec((FFN_TM, D_MODEL), lambda i: (i, 0)),
        out_shape=jax.ShapeDtypeStruct((m, D_MODEL), F32),
        scratch_shapes=[pltpu.VMEM((FFN_TM, D_MODEL), F32)],
        compiler_params=_params(("parallel",)),
        name="ffn_ln2",
    )(x, wg, wu, wd, ln_g.reshape(1, D_MODEL), ln_b.reshape(1, D_MODEL))


MOE_TR = 256
MOE_HALF = 512
MOE_UNITS = 3
MOE_TM = MOE_UNITS * MOE_HALF
MOE_TF = 512
MOE_CHUNKS = (256, 256)
SEG_ALIGN = 16
MOE_WIN_SMALL = 128
INFO_GATE = N_EXPERTS


def _expert_cap(m):
    n_tiles = m // MOE_TR
    rows = m + (SEG_ALIGN - 1) * n_tiles + MOE_TR + MOE_HALF
    return -(-rows // MOE_TM) * MOE_TM


def _rows_computed(lens):
    return jnp.maximum(-(-lens // MOE_HALF), 1) * MOE_HALF


def _dispatch_kernel(h_ref, logit_ref, info_ref, offs_ref, lens_ref, small_ref, xs_hbm, run_ref, xbuf_ref, zbuf_ref,
                     sem_ref, zsem_ref, *, n_tiles, cap):
    t = pl.program_id(0)
    tr = MOE_TR
    slot = t % 2

    @pl.when(t == 0)
    def _():
        for e in range(N_EXPERTS):
            run_ref[e] = 0

    xb = h_ref[...].astype(BF16)
    lg = logit_ref[0:N_EXPERTS, :]
    row = lax.broadcasted_iota(jnp.int32, (N_EXPERTS, tr), 0)
    m1 = jnp.max(lg, axis=0, keepdims=True)
    i1 = jnp.min(jnp.where(lg == m1, row, N_EXPERTS), axis=0, keepdims=True)
    rest = jnp.where(row == i1, -jnp.inf, lg)
    m2 = jnp.max(rest, axis=0, keepdims=True)
    i2 = jnp.min(jnp.where(rest == m2, row, N_EXPERTS), axis=0, keepdims=True)
    e2 = jnp.exp(m2 - m1)
    p1 = 1.0 / (1.0 + e2)
    gate_t = jnp.where(row == i1, p1, 0.0) + jnp.where(row == i2, e2 * p1, 0.0)
    sel_t = jnp.where((row == i1) | (row == i2), 1.0, 0.0)
    before = (lax.broadcasted_iota(jnp.int32, (tr, tr), 0) < lax.broadcasted_iota(jnp.int32, (tr, tr), 1))
    rank_t = jnp.dot(sel_t.astype(BF16), jnp.where(before, 1.0, 0.0).astype(BF16),
                     preferred_element_type=F32)
    cnt = jnp.sum(sel_t, axis=1, keepdims=True).astype(jnp.int32)
    info_t = jnp.concatenate([rank_t, gate_t, jnp.zeros((LANES - 2 * N_EXPERTS, tr), F32)], axis=0)
    info_ref[...] = info_t.T

    small = jnp.max(cnt) <= MOE_WIN_SMALL
    small_ref[t] = small.astype(jnp.int32)

    def seg_copy(sl, e, off, rows):
        return pltpu.make_async_copy(xbuf_ref.at[sl, e, pl.ds(0, rows)],
                                     xs_hbm.at[pl.ds(pl.multiple_of(off, SEG_ALIGN), rows)], sem_ref.at[sl, e])

    def by_window(flag, fn):
        pl.when(flag)(functools.partial(fn, MOE_WIN_SMALL))
        pl.when(jnp.logical_not(flag))(functools.partial(fn, tr))

    def compact(rows):
        dst_row = lax.broadcasted_iota(jnp.int32, (rows, tr), 0).astype(F32)
        onehots = [jnp.where((rank_t[e:e + 1, :] == dst_row) & (sel_t[e:e + 1, :] > 0.0), 1.0, 0.0).astype(BF16)
                   for e in range(N_EXPERTS)]
        onehot_all = jnp.concatenate(onehots, axis=0)
        half = D_MODEL // 2
        for c in range(2):
            packed = jnp.dot(onehot_all, xb[:, c * half:(c + 1) * half], preferred_element_type=F32)
            xbuf_ref[slot, :, 0:rows, c * half:(c + 1) * half] = packed.astype(BF16).reshape(N_EXPERTS, rows, half)

    by_window(small, compact)

    def wait_previous(rows):
        for e in range(N_EXPERTS):
            seg_copy(1 - slot, e, 0, rows).wait()

    @pl.when(t > 0)
    def _():
        by_window(small_ref[jnp.maximum(t - 1, 0)] == 1, wait_previous)

    def start_all(rows):
        for e in range(N_EXPERTS):
            seg_copy(slot, e, e * cap + run_ref[e], rows).start()

    by_window(small, start_all)
    for e in range(N_EXPERTS):
        offs_ref[t * N_EXPERTS + e] = e * cap + run_ref[e]
        run_ref[e] = run_ref[e] + jnp.bitwise_and(cnt[e, 0] + (SEG_ALIGN - 1), -SEG_ALIGN)

    def wait_current(rows):
        for e in range(N_EXPERTS):
            seg_copy(slot, e, 0, rows).wait()

    @pl.when(t == n_tiles - 1)
    def _():
        zbuf_ref[...] = jnp.zeros_like(zbuf_ref)
        by_window(small, wait_current)
        tails = []
        for e in range(N_EXPERTS):
            lens_ref[e] = run_ref[e]
            tail = pl.multiple_of(e * cap + run_ref[e], SEG_ALIGN)
            tails.append(pltpu.make_async_copy(zbuf_ref, xs_hbm.at[pl.ds(tail, MOE_HALF)], zsem_ref.at[e]))
            tails[-1].start()
        for cp in tails:
            cp.wait()


def _dispatch(h, logits):
    m = h.shape[0]
    n_tiles = m // MOE_TR
    cap = _expert_cap(m)
    smem = pl.BlockSpec(memory_space=pltpu.SMEM)
    return pl.pallas_call(
        functools.partial(_dispatch_kernel, n_tiles=n_tiles, cap=cap),
        grid=(n_tiles,),
        in_specs=[pl.BlockSpec((MOE_TR, D_MODEL), lambda t: (t, 0)),
                  pl.BlockSpec((LOGIT_ROWS, MOE_TR), lambda t: (0, t))],
        out_specs=[pl.BlockSpec((MOE_TR, LANES), lambda t: (t, 0)), smem, smem, smem,
                   pl.BlockSpec(memory_space=pl.ANY)],
        out_shape=[jax.ShapeDtypeStruct((m, LANES), F32),
                   jax.ShapeDtypeStruct((n_tiles * N_EXPERTS,), jnp.int32),
                   jax.ShapeDtypeStruct((N_EXPERTS,), jnp.int32),
                   jax.ShapeDtypeStruct((n_tiles,), jnp.int32),
                   jax.ShapeDtypeStruct((N_EXPERTS * cap, D_MODEL), BF16)],
        scratch_shapes=[pltpu.SMEM((N_EXPERTS,), jnp.int32),
                        pltpu.VMEM((2, N_EXPERTS, MOE_TR, D_MODEL), BF16),
                        pltpu.VMEM((MOE_HALF, D_MODEL), BF16),
                        pltpu.SemaphoreType.DMA((2, N_EXPERTS)),
                        pltpu.SemaphoreType.DMA((N_EXPERTS,))],
        compiler_params=_params(("arbitrary",)),
        name="moe_dispatch",
    )(h, logits)


def _expert_ffn_kernel(exp_ref, blk_ref, units_ref, x_ref, wg_ref, wu_ref, wd_ref, y_ref, acc_ref):
    w = pl.program_id(0)
    f = pl.program_id(1)

    def tile(rows):
        acc = acc_ref.at[pl.ds(0, rows)]

        @pl.when(f == 0)
        def _():
            acc[...] = jnp.zeros((rows, D_MODEL), F32)

        _swiglu_chunks(acc, x_ref[0:rows, :], lambda c: wg_ref[0, :, c].astype(BF16),
                       lambda c: wu_ref[0, :, c].astype(BF16), lambda c: wd_ref[0, c, :].astype(BF16), MOE_CHUNKS)

        @pl.when(f == pl.num_programs(1) - 1)
        def _():
            y_ref[0:rows, :] = acc[...].astype(BF16)

    for units in range(1, MOE_UNITS + 1):
        pl.when(units_ref[w] == units)(functools.partial(tile, units * MOE_HALF))


def _expert_ffn(xs, exp_w, blk_w, units_w, w_gate, w_up, w_down):
    d_ff = w_gate.shape[-1]
    n_f = d_ff // MOE_TF
    fcol = lambda w, f, exp, blk, units: jnp.where(units[w] != 0, f, n_f - 1)
    grid_spec = pltpu.PrefetchScalarGridSpec(
        num_scalar_prefetch=3,
        grid=(exp_w.shape[0], n_f),
        in_specs=[pl.BlockSpec((MOE_TM, D_MODEL), lambda w, f, exp, blk, valid: (blk[w], 0)),
                  pl.BlockSpec((1, D_MODEL, MOE_TF), lambda w, f, exp, blk, valid: (exp[w], 0, fcol(w, f, exp, blk, valid))),
                  pl.BlockSpec((1, D_MODEL, MOE_TF), lambda w, f, exp, blk, valid: (exp[w], 0, fcol(w, f, exp, blk, valid))),
                  pl.BlockSpec((1, MOE_TF, D_MODEL), lambda w, f, exp, blk, valid: (exp[w], fcol(w, f, exp, blk, valid), 0))],
        out_specs=pl.BlockSpec((MOE_TM, D_MODEL), lambda w, f, exp, blk, valid: (blk[w], 0)),
        scratch_shapes=[pltpu.VMEM((MOE_TM, D_MODEL), F32)])
    return pl.pallas_call(
        _expert_ffn_kernel,
        grid_spec=grid_spec,
        out_shape=jax.ShapeDtypeStruct(xs.shape, BF16),
        compiler_params=_params(("arbitrary", "arbitrary")),
        name="moe_expert_ffn",
    )(exp_w, blk_w, units_w, xs, w_gate, w_up, w_down)


def _combine_kernel(starts_ref, shifts_ref, small_ref, info_ref, h_ref, g_ref, b_ref, ys_hbm, o_ref, ybuf_ref,
                    sem_ref, *, n_tiles):
    t = pl.program_id(0)
    tr = MOE_TR
    slot = t % 2

    def seg_copy(tile, sl, e, rows):
        off = pl.multiple_of(starts_ref[tile * N_EXPERTS + e], SEG_ALIGN)
        return pltpu.make_async_copy(ys_hbm.at[pl.ds(off, rows)], ybuf_ref.at[sl, e, pl.ds(0, rows)],
                                     sem_ref.at[sl, e])

    def by_window(tile, fn):
        flag = small_ref[tile] == 1
        pl.when(flag)(functools.partial(fn, MOE_WIN_SMALL))
        pl.when(jnp.logical_not(flag))(functools.partial(fn, tr))

    def fetch(tile, sl):
        def start(rows):
            for e in range(N_EXPERTS):
                seg_copy(tile, sl, e, rows).start()
        by_window(tile, start)

    @pl.when(t == 0)
    def _():
        fetch(0, 0)

    @pl.when(t + 1 < n_tiles)
    def _():
        fetch(jnp.minimum(t + 1, n_tiles - 1), 1 - slot)

    def gather(rows):
        for e in range(N_EXPERTS):
            seg_copy(t, slot, e, rows).wait()
        info = info_ref[...]
        src_row = lax.broadcasted_iota(jnp.int32, (tr, rows), 1).astype(F32)
        acc = ALPHA * h_ref[...]
        for e in range(N_EXPERTS):
            rank_c = info[:, e:e + 1] + shifts_ref[t * N_EXPERTS + e].astype(F32)
            gate_c = info[:, INFO_GATE + e:INFO_GATE + e + 1]
            onehot = jnp.where((rank_c == src_row) & (gate_c != 0.0), 1.0, 0.0).astype(BF16)
            acc = acc + gate_c * jnp.dot(onehot, ybuf_ref[slot, e, 0:rows, :], preferred_element_type=F32)
        o_ref[...] = _layer_norm(acc, g_ref[...], b_ref[...])

    by_window(t, gather)


def _combine(starts, shifts, small, info, h, ys, ln_g, ln_b):
    m = h.shape[0]
    n_tiles = m // MOE_TR
    grid_spec = pltpu.PrefetchScalarGridSpec(
        num_scalar_prefetch=3,
        grid=(n_tiles,),
        in_specs=[pl.BlockSpec((MOE_TR, LANES), lambda t, *_: (t, 0)),
                  pl.BlockSpec((MOE_TR, D_MODEL), lambda t, *_: (t, 0)),
                  pl.BlockSpec((1, D_MODEL), lambda t, *_: (0, 0)),
                  pl.BlockSpec((1, D_MODEL), lambda t, *_: (0, 0)),
                  pl.BlockSpec(memory_space=pl.ANY)],
        out_specs=pl.BlockSpec((MOE_TR, D_MODEL), lambda t, *_: (t, 0)),
        scratch_shapes=[pltpu.VMEM((2, N_EXPERTS, MOE_TR, D_MODEL), BF16),
                        pltpu.SemaphoreType.DMA((2, N_EXPERTS))])
    return pl.pallas_call(
        functools.partial(_combine_kernel, n_tiles=n_tiles),
        grid_spec=grid_spec,
        out_shape=jax.ShapeDtypeStruct((m, D_MODEL), F32),
        compiler_params=_params(("arbitrary",)),
        name="moe_combine_ln2",
    )(starts, shifts, small, info, h, ln_g.reshape(1, D_MODEL), ln_b.reshape(1, D_MODEL), ys)


def _work_list(lens, m):
    cap = _expert_cap(m)
    n_tiles = m // MOE_TR
    max_rows = 2 * m + (SEG_ALIGN - 1) * min(N_EXPERTS * n_tiles, 2 * m)
    w_max = -(-(max_rows + N_EXPERTS * MOE_HALF) // MOE_TM) + N_EXPERTS
    rows_e = _rows_computed(lens)
    tiles_e = -(-rows_e // MOE_TM)
    ends = jnp.cumsum(tiles_e)
    w = jnp.arange(w_max, dtype=jnp.int32)
    wc = jnp.minimum(w, ends[-1] - 1)
    exp_w = jnp.sum((wc[:, None] >= ends[None, :]).astype(jnp.int32), axis=1)
    tile_w = wc - (ends - tiles_e)[exp_w]
    blk_w = exp_w * (cap // MOE_TM) + tile_w
    units_w = jnp.minimum((rows_e[exp_w] - tile_w * MOE_TM) // MOE_HALF, MOE_UNITS)
    units_w = jnp.where(w < ends[-1], units_w, 0)
    return exp_w.astype(jnp.int32), blk_w.astype(jnp.int32), units_w.astype(jnp.int32)


def _read_windows(offs, small, lens, m):
    cap = _expert_cap(m)
    region = jnp.arange(N_EXPERTS, dtype=jnp.int32) * cap
    rel = offs.reshape(-1, N_EXPERTS) - region
    width = jnp.where(small == 1, MOE_WIN_SMALL, MOE_TR)[:, None]
    start = jnp.minimum(rel, _rows_computed(lens)[None, :] - width)
    return (start + region).reshape(-1).astype(jnp.int32), (rel - start).reshape(-1).astype(jnp.int32)


def _moe_layer(h, logits, w_gate, w_up, w_down, ln_g, ln_b):
    m = h.shape[0]
    info, offs, lens, small, xs = _dispatch(h, logits)
    exp_w, blk_w, units_w = _work_list(lens, m)
    ys = _expert_ffn(xs, exp_w, blk_w, units_w, w_gate, w_up, w_down)
    starts, shifts = _read_windows(offs, small, lens, m)
    return _combine(starts, shifts, small, info, h, ys, ln_g, ln_b)


def _pad_groups(a):
    lead = a.shape[:-1]
    a = a.reshape(*lead, SSD_GROUPS, GROUP_W)
    a = jnp.pad(a, [(0, 0)] * len(lead) + [(0, 0), (0, GROUP_PAD - GROUP_W)])
    return a.reshape(*lead, SSD_PAD)


def _head_lanes(v):
    return jnp.pad(v, (0, LANES - SSD_HEADS)).reshape(1, LANES)


D_IN = 3 * ATT_W + SSD_W + (SSD_W + 2 * SSD_GROUPS * SSD_STATE) + SSD_HEADS + 2 * GM_W
PERM_ROWS = 256


def _permute_w_kernel(w_ref, o_ref):
    src_z = 3 * ATT_W
    src_xs = src_z + SSD_W
    src_b = src_xs + SSD_W
    src_c = src_b + SSD_GROUPS * SSD_STATE
    src_dt = src_c + SSD_GROUPS * SSD_STATE
    src_gu = src_dt + SSD_HEADS
    src_gv = src_gu + GM_W
    o_ref[...] = jnp.zeros_like(o_ref)

    def put(dst, src, width):
        o_ref[:, dst:dst + width] = w_ref[0, :, src:src + width].astype(BF16)

    for g in range(SSD_GROUPS):
        put(COL_Z + g * GROUP_PAD, src_z + g * GROUP_W, GROUP_W)
        put(COL_XS + g * GROUP_PAD, src_xs + g * GROUP_W, GROUP_W)
    put(COL_B, src_b, SSD_GROUPS * SSD_STATE)
    put(COL_C, src_c, SSD_GROUPS * SSD_STATE)
    put(COL_GU, src_gu, GM_W)
    put(COL_GV, src_gv, GM_W)
    put(COL_Q, 0, ATT_W)
    put(COL_K, ATT_W, ATT_W)
    put(COL_V, 2 * ATT_W, ATT_W)
    put(COL_DT, src_dt, SSD_HEADS)


def _permute_w_in(w_in, layer):
    return pl.pallas_call(
        _permute_w_kernel,
        grid=(D_MODEL // PERM_ROWS,),
        in_specs=[pl.BlockSpec((1, PERM_ROWS, D_IN), lambda i: (layer, i, 0))],
        out_specs=pl.BlockSpec((PERM_ROWS, D_W), lambda i: (i, 0)),
        out_shape=jax.ShapeDtypeStruct((D_MODEL, D_W), BF16),
        compiler_params=_params(("parallel",)),
        name="permute_w_in",
    )(w_in)


def _layer_params(i, w_in, conv_w, conv_b, dt_bias, a_log, d_skip, ssd_norm_g, att_norm_g, gm_ln_g, gm_ln_b,
                  gm_w_s, gm_b_s, gm_norm_g, w_out, ln1_g, ln1_b):
    cw, cb = conv_w[i], conv_b[i]
    nbc = SSD_GROUPS * SSD_STATE
    wo = w_out[i]
    return {
        "w_in": _permute_w_in(w_in, i),
        "conv_w_xs": _pad_groups(cw[:, :SSD_W]),
        "conv_w_b": cw[:, SSD_W:SSD_W + nbc],
        "conv_w_c": cw[:, SSD_W + nbc:],
        "conv_b_xs": _pad_groups(cb[:SSD_W]).reshape(1, SSD_PAD),
        "conv_b_b": cb[SSD_W:SSD_W + nbc].reshape(1, nbc),
        "conv_b_c": cb[SSD_W + nbc:].reshape(1, nbc),
        "dt_bias": _head_lanes(dt_bias[i]),
        "a_log": _head_lanes(a_log[i]),
        "d_skip": _pad_groups(jnp.repeat(d_skip[i], SSD_HEAD_DIM)).reshape(1, SSD_PAD),
        "ssd_norm_g": _pad_groups(ssd_norm_g[i]).reshape(1, SSD_PAD),
        "att_norm_g": att_norm_g[i].reshape(1, ATT_W),
        "gm_ln_g": gm_ln_g[i].reshape(1, GM_W),
        "gm_ln_b": gm_ln_b[i].reshape(1, GM_W),
        "gm_w_s": gm_w_s[i],
        "gm_b_s": jnp.repeat(gm_b_s[i].T, GM_W // GM_GROUPS, axis=1),
        "gm_norm_g": gm_norm_g[i].reshape(1, GM_W),
        "w_out_att": wo[:ATT_W].astype(BF16),
        "w_out_ssd": _pad_groups(wo[ATT_W:ATT_W + SSD_W].T).T.astype(BF16),
        "w_out_gm": wo[ATT_W + SSD_W:].astype(BF16),
        "ln1_g": ln1_g[i].reshape(1, D_MODEL),
        "ln1_b": ln1_b[i].reshape(1, D_MODEL),
    }


def _rope_tables(positions):
    inv = ROPE_THETA ** (-jnp.arange(0, ROPE_DIM, 2, dtype=F32) / ROPE_DIM)
    d = jnp.arange(LANES) % HEAD_DIM
    inv_lane = jnp.where(d < ROPE_DIM, inv[d % ROPE_HALF], 0.0)
    ang = positions.astype(F32)[..., None] * inv_lane
    cos, sin = jnp.cos(ang), jnp.sin(ang)
    return cos, jnp.where(d < ROPE_HALF, -sin, 0.0), jnp.where(d >= ROPE_HALF, sin, 0.0)


def kernel(x, positions, ln_in_g, ln_in_b, w_in, conv_w, conv_b, dt_bias, a_log, d_skip, ssd_norm_g, att_norm_g, gm_ln_g, gm_ln_b, gm_w_s, gm_b_s, gm_norm_g, w_out, ln1_g, ln1_b, ln2_g, ln2_b, ffn_w_gate, ffn_w_up, ffn_w_down, router_w, moe_w_gate, moe_w_up, moe_w_down):
    bsz, seq, _ = x.shape
    m = bsz * seq
    rope_c, rope_sa, rope_sb = _rope_tables(positions)
    h = x.reshape(m, D_MODEL)
    for i in range(DEPTH):
        p = _layer_params(i, w_in, conv_w, conv_b, dt_bias, a_log, d_skip, ssd_norm_g, att_norm_g, gm_ln_g,
                          gm_ln_b, gm_w_s, gm_b_s, gm_norm_g, w_out, ln1_g, ln1_b)
        if i == 0:
            proj, gm, h = _in_proj(h, p["w_in"], p, ln=(ln_in_g, ln_in_b))
        else:
            proj, gm = _in_proj(h, p["w_in"], p)
        proj3 = proj.reshape(bsz, seq, D_PROJ)
        att = _attention(proj3, rope_c, rope_sa, rope_sb, value_lag=5 if i == 0 else 6).reshape(m, ATT_W)
        ssd = _ssd(proj3, p).reshape(m, SSD_PAD)
        j = i // 2
        if i % 2 == 0:
            (h,) = _out_proj(att, ssd, gm, h, p)
            h = _ffn(h, ffn_w_gate[j].astype(BF16), ffn_w_up[j].astype(BF16), ffn_w_down[j].astype(BF16),
                     ln2_g[i], ln2_b[i])
        else:
            h, logits = _out_proj(att, ssd, gm, h, p, w_router=router_w[j])
            h = _moe_layer(h, logits, moe_w_gate[j], moe_w_up[j], moe_w_down[j], ln2_g[i], ln2_b[i])
    return h.reshape(bsz, seq, D_MODEL)
```

```python
import functools

import jax
import jax.numpy as jnp
from jax import lax
from jax.experimental import pallas as pl
from jax.experimental.pallas import tpu as pltpu

F32 = jnp.float32
BF16 = jnp.bfloat16
HIGHEST = lax.Precision.HIGHEST

D_MODEL = 1024
ATT_HEADS = 6
HEAD_DIM = 64
ATT_W = ATT_HEADS * HEAD_DIM
ROPE_DIM = HEAD_DIM // 4
ROPE_HALF = ROPE_DIM // 2
ROPE_THETA = 500000.0
MOBA_BLOCK = 256
MOBA_TOPK = 3
SSD_HEADS = 6
SSD_HEAD_DIM = 64
SSD_W = SSD_HEADS * SSD_HEAD_DIM
SSD_GROUPS = 2
SSD_STATE = 128
SSD_CONV = 4
SSD_CHUNK = 128
GM_GROUPS = 4
GM_W = 256
GM_CHUNK = 128
N_EXPERTS = 8
DEPTH = 2
ALPHA = (2.0 * DEPTH) ** 0.25
NEG = -1e30
LN_EPS = 1e-5

LANES = 128
GROUP_W = SSD_W // SSD_GROUPS
GROUP_PAD = 256
SSD_PAD = SSD_GROUPS * GROUP_PAD
HEADS_PER_GROUP = SSD_HEADS // SSD_GROUPS

COL_Z = 0
COL_XS = 512
COL_B = 1024
COL_C = 1280
COL_Q = 1536
COL_K = 1920
COL_V = 2304
COL_DT = 2688
D_PROJ = 2816
COL_GU = 2816
COL_GV = 3072
D_W = 3328

VMEM_LIMIT = 56 * 1024 * 1024


def _params(sem, vmem=VMEM_LIMIT):
    return pltpu.CompilerParams(dimension_semantics=sem, vmem_limit_bytes=vmem)


def _layer_norm(x, g, b):
    mu = jnp.mean(x, axis=-1, keepdims=True)
    xc = x - mu
    var = jnp.mean(xc * xc, axis=-1, keepdims=True)
    return xc * lax.rsqrt(var + LN_EPS) * g + b


def _silu(x):
    half = 0.5 * x
    return half + half * jnp.tanh(half)


def _split_bf16(x):
    hi = x.astype(BF16)
    return hi, (x - hi.astype(F32)).astype(BF16)


def _dot_bf16x3(a, b):
    a_hi, a_lo = _split_bf16(a)
    b_hi, b_lo = _split_bf16(b)
    dot = functools.partial(jnp.dot, preferred_element_type=F32)
    return dot(a_hi, b_hi) + (dot(a_lo, b_hi) + dot(a_hi, b_lo))


def _gelu_tanh(x):
    return 0.5 * x * (1.0 + jnp.tanh(0.7978845608028654 * (x + 0.044715 * x * x * x)))


PROJ_TM = 512
PROJ_CH = 256
PROJ_TAIL = 4


def _in_proj_kernel(*refs, apply_ln):
    if apply_ln:
        x_ref, g_ref, b_ref, w_ref, lng_ref, lnb_ref, ws_ref, bs_ref, ng_ref, proj_ref, gm_ref, h_ref = refs
        x = _layer_norm(x_ref[...], g_ref[...], b_ref[...])
        h_ref[...] = x
    else:
        x_ref, w_ref, lng_ref, lnb_ref, ws_ref, bs_ref, ng_ref, proj_ref, gm_ref = refs
        x = x_ref[...]
    xb = x.astype(BF16)

    def project(j):
        cols = slice(j * PROJ_CH, (j + 1) * PROJ_CH)
        proj_ref[:, cols] = jnp.dot(xb, w_ref[:, cols], preferred_element_type=F32)

    gu = jnp.dot(xb, w_ref[:, COL_GU:COL_GU + GM_W], preferred_element_type=F32)
    gv = jnp.dot(xb, w_ref[:, COL_GV:COL_GV + GM_W], preferred_element_type=F32)
    u, vb = _gmlp_activate(gu, gv, lng_ref, lnb_ref)
    n_chunks = D_PROJ // PROJ_CH
    for j in range(n_chunks - PROJ_TAIL):
        project(j)
    _gmlp_mix(u, vb, ws_ref, bs_ref, ng_ref, gm_ref)
    for j in range(n_chunks - PROJ_TAIL, n_chunks):
        project(j)


def _in_proj(x, w, p, ln=None):
    m = x.shape[0]
    row = lambda i: (i, 0)
    const = lambda i: (0, 0)
    whole = lambda a: pl.BlockSpec(a.shape, lambda i: (0,) * a.ndim)
    in_specs = [pl.BlockSpec((PROJ_TM, D_MODEL), row)]
    args = [x]
    out_shape = [jax.ShapeDtypeStruct((m, D_PROJ), F32), jax.ShapeDtypeStruct((m, GM_W), BF16)]
    out_specs = [pl.BlockSpec((PROJ_TM, D_PROJ), row), pl.BlockSpec((PROJ_TM, GM_W), row)]
    if ln is not None:
        in_specs += [pl.BlockSpec((1, D_MODEL), const), pl.BlockSpec((1, D_MODEL), const)]
        args += [ln[0].reshape(1, D_MODEL), ln[1].reshape(1, D_MODEL)]
        out_shape.append(jax.ShapeDtypeStruct((m, D_MODEL), F32))
        out_specs.append(pl.BlockSpec((PROJ_TM, D_MODEL), row))
    in_specs.append(pl.BlockSpec((D_MODEL, D_W), const))
    args.append(w)
    gm_small = [p["gm_ln_g"], p["gm_ln_b"], p["gm_w_s"], p["gm_b_s"], p["gm_norm_g"]]
    in_specs += [whole(a) for a in gm_small]
    args += gm_small
    return pl.pallas_call(
        functools.partial(_in_proj_kernel, apply_ln=ln is not None),
        grid=(m // PROJ_TM,),
        in_specs=in_specs,
        out_specs=out_specs,
        out_shape=out_shape,
        compiler_params=_params(("parallel",)),
        name="in_proj",
    )(*args)


VT_ROWS = HEAD_DIM + 16
LOG2E = 1.4426950408889634


ATT_VALUE_LAG = 6


def _attn_kernel(q_ref, k_ref, v_ref, c_ref, sa_ref, sb_ref, o_ref, kb_ref, vt_ref, *, seq):
    nb = seq // MOBA_BLOCK
    tq = MOBA_BLOCK
    lane = lax.broadcasted_iota(jnp.int32, (1, LANES), 1)

    def rope(x, rows):
        return (x * c_ref[0, rows, :] + pltpu.roll(x, LANES - ROPE_HALF, 1) * sa_ref[0, rows, :]
                + pltpu.roll(x, ROPE_HALF, 1) * sb_ref[0, rows, :])

    k = rope(k_ref[0], slice(None))
    kb_ref[...] = k.astype(BF16)
    k_mean = jnp.mean(k.reshape(nb, MOBA_BLOCK, LANES), axis=1)
    v_t = v_ref[0].T
    ones_rows = jnp.where(lax.broadcasted_iota(jnp.int32, (VT_ROWS - HEAD_DIM, seq), 0) == 0, 1.0, 0.0)
    for hh in range(2):
        vt_ref[hh] = jnp.concatenate([v_t[hh * HEAD_DIM:(hh + 1) * HEAD_DIM], ones_rows], axis=0).astype(BF16)

    qscale = HEAD_DIM ** -0.5 * LOG2E
    causal_t = (lax.broadcasted_iota(jnp.int32, (tq, tq), 0) <= lax.broadcasted_iota(jnp.int32, (tq, tq), 1))
    blk_row = lax.broadcasted_iota(jnp.int32, (nb, tq), 0)

    chain_q = {}
    chain_bias = {}
    chain_m = {}
    chain_o = {}
    q_tiles = {}

    def start_chain(j, hh):
        if hh == 0:
            rows = slice(j * tq, (j + 1) * tq)
            q_tiles[j] = rope(q_ref[0, rows, :], rows) * qscale
        q = q_tiles[j] if hh == 0 else q_tiles.pop(j)
        head_lanes = (lane < HEAD_DIM) if hh == 0 else (lane >= HEAD_DIM)
        qh_t = jnp.where(head_lanes, q, 0.0).T
        chain_q[(j, hh)] = qh_t.astype(BF16)
        chain_bias[(j, hh)] = None
        if j > MOBA_TOPK:
            gate_t = _dot_bf16x3(k_mean, qh_t)
            rank = jnp.zeros((nb, tq), F32)
            for m in range(j):
                g_m = gate_t[m:m + 1, :]
                beats = (g_m > gate_t) | ((g_m == gate_t) & (blk_row > m))
                rank = rank + jnp.where(beats & (blk_row != m), 1.0, 0.0)
            bias = jnp.where((blk_row < j) & (rank >= MOBA_TOPK), NEG, 0.0)
            chain_bias[(j, hh)] = [bias[n:n + 1, :] for n in range(j)]

    def scores(j, hh, n):
        return jnp.dot(kb_ref[n * tq:(n + 1) * tq, :], chain_q[(j, hh)], preferred_element_type=F32)

    def probs(j, hh, n, s_t):
        key = (j, hh)
        if n == j:
            s_t = jnp.where(causal_t, s_t, NEG)
            m_new = jnp.max(s_t, axis=0, keepdims=True)
            alpha = None
            shift = m_new
        else:
            bias_n = 0.0 if chain_bias[key] is None else chain_bias[key][n]
            m_new = jnp.maximum(chain_m[key], jnp.max(s_t, axis=0, keepdims=True) + bias_n)
            alpha = jnp.exp2(chain_m[key] - m_new)
            shift = m_new - bias_n
        chain_m[key] = m_new
        return jnp.exp2(s_t - shift).astype(BF16), alpha

    def values(j, hh, n, p_t, alpha):
        key = (j, hh)
        part = jnp.dot(vt_ref[hh, :, n * tq:(n + 1) * tq], p_t, preferred_element_type=F32)
        chain_o[key] = part if alpha is None else chain_o[key] * alpha + part

    def finish(j):
        halves = [chain_o.pop((j, hh)) for hh in range(2)]
        out_t = jnp.concatenate([o[0:HEAD_DIM] / o[HEAD_DIM:HEAD_DIM + 1] for o in halves], axis=0)
        o_ref[0, j * tq:(j + 1) * tq, :] = out_t.T

    value_lag = ATT_VALUE_LAG
    units = []
    for j in range(nb):
        for n in [j] + list(range(j)):
            units += [(j, 0, n), (j, 1, n)]
    s_state = {}
    p_state = {}
    for step in range(len(units) + value_lag):
        if step >= value_lag:
            j, hh, n = unit = units[step - value_lag]
            values(j, hh, n, *p_state.pop(unit))
            last_block = n == (j - 1 if j > 0 else 0)
            if hh == 1 and last_block:
                finish(j)
        if step < len(units):
            j, hh, n = unit = units[step]
            if n == j:
                start_chain(j, hh)
            s_state[unit] = scores(j, hh, n)
        if 1 <= step <= len(units):
            j, hh, n = unit = units[step - 1]
            p_state[unit] = probs(j, hh, n, s_state.pop(unit))


def _attention(proj3, rope_c, rope_sa, rope_sb):
    bsz, seq, _ = proj3.shape
    blk = (1, seq, LANES)
    col = lambda base: (lambda b, p: (b, 0, base // LANES + p))
    tab = lambda b, p: (b, 0, 0)
    return pl.pallas_call(
        functools.partial(_attn_kernel, seq=seq),
        grid=(bsz, ATT_W // LANES),
        in_specs=[pl.BlockSpec(blk, col(COL_Q)), pl.BlockSpec(blk, col(COL_K)), pl.BlockSpec(blk, col(COL_V)),
                  pl.BlockSpec(blk, tab), pl.BlockSpec(blk, tab), pl.BlockSpec(blk, tab)],
        out_specs=pl.BlockSpec(blk, lambda b, p: (b, 0, p)),
        out_shape=jax.ShapeDtypeStruct((bsz, seq, ATT_W), F32),
        scratch_shapes=[pltpu.VMEM((seq, LANES), BF16), pltpu.VMEM((2, VT_ROWS, seq), BF16)],
        compiler_params=_params(("parallel", "arbitrary")),
        name="moba_attention",
    )(proj3, proj3, proj3, rope_c, rope_sa, rope_sb)


assert HEADS_PER_GROUP == 3 and GROUP_PAD == 2 * LANES and SSD_HEAD_DIM * 2 == LANES


def _expand_heads(small):
    low = lax.broadcasted_iota(jnp.int32, (1, LANES), 1) < SSD_HEAD_DIM
    slabs = []
    for g in range(SSD_GROUPS):
        h0 = g * HEADS_PER_GROUP
        slabs.append(jnp.where(low, small[:, h0:h0 + 1], small[:, h0 + 1:h0 + 2]))
        slabs.append(jnp.where(low, small[:, h0 + 2:h0 + 3], 0.0))
    return jnp.concatenate(slabs, axis=1)


def _ssd_kernel(z_ref, xs_ref, b_ref, c_ref, dt_ref, cwx_ref, cwb_ref, cwc_ref, cbx_ref, cbb_ref, cbc_ref,
                dtb_ref, alog_ref, dskip_ref, ng_ref, o_ref, state_ref, *, seq, unroll):
    q = SSD_CHUNK
    n = SSD_STATE
    state_ref[...] = jnp.zeros_like(state_ref)
    row_i = lax.broadcasted_iota(jnp.int32, (q, q), 0)
    col_i = lax.broadcasted_iota(jnp.int32, (q, q), 1)
    lower = row_i >= col_i
    tril = jnp.where(lower, 1.0, 0.0)
    slab = lax.broadcasted_iota(jnp.int32, (1, GROUP_PAD), 1) // SSD_HEAD_DIM
    a_neg = -jnp.exp(alog_ref[...])

    def conv_silu(ref, w_ref, bias_ref, c):
        start = pl.multiple_of(c * q, q)
        cur = ref[0, pl.ds(start, q), :]
        prev_start = pl.multiple_of(jnp.maximum(c * q - 8, 0), 8)
        prev = jnp.where(c > 0, ref[0, pl.ds(prev_start, 8), :], 0.0)
        full = jnp.concatenate([prev, cur], axis=0)
        y = bias_ref[...] + w_ref[SSD_CONV - 1:SSD_CONV, :] * cur
        for i in range(SSD_CONV - 1):
            shift = SSD_CONV - 1 - i
            y = y + w_ref[i:i + 1, :] * pltpu.roll(full, shift, 0)[8:, :]
        return _silu(y)

    def chunk(c, carry):
        start = pl.multiple_of(c * q, q)
        xs = conv_silu(xs_ref, cwx_ref, cbx_ref, c)
        bm = conv_silu(b_ref, cwb_ref, cbb_ref, c)
        cm = conv_silu(c_ref, cwc_ref, cbc_ref, c)
        x = dt_ref[0, pl.ds(start, q), :] + dtb_ref[...]
        dt = jnp.maximum(x, 0.0) + jnp.log(1.0 + jnp.exp(-jnp.abs(x)))
        a = dt * a_neg
        a_cum = jnp.dot(tril, a, precision=HIGHEST, preferred_element_type=F32)
        a_cum_t = a_cum.T
        dt_e = _expand_heads(dt)
        acum_e = _expand_heads(a_cum)
        alast_e = acum_e[q - 1:q, :]
        xdt = xs * dt_e
        decay_e = jnp.exp(alast_e - acum_e)
        y_parts = []
        for g in range(SSD_GROUPS):
            gl = slice(g * GROUP_PAD, (g + 1) * GROUP_PAD)
            nl = slice(g * n, (g + 1) * n)
            b_g = bm[:, nl].astype(BF16)
            b_gt = bm[:, nl].T.astype(BF16)
            c_g = cm[:, nl].astype(BF16)
            cb = lax.dot_general(c_g, b_g, (((1,), (1,)), ((), ())), preferred_element_type=F32)
            xdt_g = xdt[:, gl]
            xdt_b = xdt_g.astype(BF16)
            y_g = jnp.zeros((q, GROUP_PAD), F32)
            for r in range(HEADS_PER_GROUP):
                h = g * HEADS_PER_GROUP + r
                seg = a_cum[:, h:h + 1] - a_cum_t[h:h + 1, :]
                l_h = jnp.where(lower, jnp.exp(jnp.minimum(seg, 0.0)), 0.0)
                y_h = jnp.dot((cb * l_h).astype(BF16), xdt_b, preferred_element_type=F32)
                y_g = jnp.where(slab == r, y_h, y_g)
            prev = state_ref[g]
            y_off = jnp.dot(c_g, prev.astype(BF16), preferred_element_type=F32) * jnp.exp(acum_e[:, gl])
            new = jnp.dot(b_gt, (xdt_g * decay_e[:, gl]).astype(BF16), preferred_element_type=F32)
            state_ref[g] = prev * jnp.exp(alast_e[:, gl]) + new
            y_parts.append(y_g + y_off)
        y = jnp.concatenate(y_parts, axis=1) + dskip_ref[...] * xs
        yz = y * _silu(z_ref[0, pl.ds(start, q), :])
        outs = []
        for g in range(SSD_GROUPS):
            yg = yz[:, g * GROUP_PAD:(g + 1) * GROUP_PAD]
            ms = jnp.sum(yg * yg, axis=-1, keepdims=True) * (1.0 / GROUP_W)
            outs.append(yg * lax.rsqrt(ms + LN_EPS))
        o_ref[0, pl.ds(start, q), :] = (jnp.concatenate(outs, axis=1) * ng_ref[...]).astype(BF16)
        return carry

    lax.fori_loop(0, seq // q, chunk, 0, unroll=unroll)


def _ssd(proj3, p, unroll):
    bsz, seq, _ = proj3.shape
    slab = lambda width, base: pl.BlockSpec((1, seq, width), lambda b: (b, 0, base // width))
    const2 = lambda a: pl.BlockSpec(a.shape, lambda b: (0, 0))
    small = [p["conv_w_xs"], p["conv_w_b"], p["conv_w_c"], p["conv_b_xs"], p["conv_b_b"], p["conv_b_c"],
             p["dt_bias"], p["a_log"], p["d_skip"], p["ssd_norm_g"]]
    return pl.pallas_call(
        functools.partial(_ssd_kernel, seq=seq, unroll=unroll),
        grid=(bsz,),
        in_specs=[slab(SSD_PAD, COL_Z), slab(SSD_PAD, COL_XS), slab(2 * SSD_STATE, COL_B),
                  slab(2 * SSD_STATE, COL_C), slab(LANES, COL_DT)] + [const2(a) for a in small],
        out_specs=pl.BlockSpec((1, seq, SSD_PAD), lambda b: (b, 0, 0)),
        out_shape=jax.ShapeDtypeStruct((bsz, seq, SSD_PAD), BF16),
        scratch_shapes=[pltpu.VMEM((SSD_GROUPS, SSD_STATE, GROUP_PAD), F32)],
        compiler_params=_params(("parallel",)),
        name="ssd",
    )(proj3, proj3, proj3, proj3, proj3, *small)


def _gmlp_activate(gu, gv, lng_ref, lnb_ref):
    return _gelu_tanh(gu), _layer_norm(_gelu_tanh(gv), lng_ref[...], lnb_ref[...]).astype(BF16)


def _gmlp_mix(u, vb, ws_ref, bs_ref, ng_ref, o_ref):
    q = GM_CHUNK
    row_i = lax.broadcasted_iota(jnp.int32, (q, q), 0)
    col_i = lax.broadcasted_iota(jnp.int32, (q, q), 1)
    lower = row_i >= col_i
    grp = lax.broadcasted_iota(jnp.int32, (1, GM_W), 1) // (GM_W // GM_GROUPS)
    w = [jnp.where(lower, ws_ref[g], 0.0).astype(BF16) for g in range(GM_GROUPS)]
    for c in range(u.shape[0] // q):
        rows = slice(c * q, (c + 1) * q)
        mixed = jnp.zeros((q, GM_W), F32)
        for g in range(GM_GROUPS):
            mixed = jnp.where(grp == g, jnp.dot(w[g], vb[rows, :], preferred_element_type=F32), mixed)
        gm = u[rows, :] * (mixed + bs_ref[...])
        ms = jnp.mean(gm * gm, axis=-1, keepdims=True)
        o_ref[rows, :] = (gm * lax.rsqrt(ms + LN_EPS) * ng_ref[...]).astype(BF16)


OUT_TM = 1024
LOGIT_ROWS = 16


def _out_proj_kernel(att_ref, ssd_ref, gm_ref, h_ref, ag_ref, wa_ref, ws_ref, wg_ref, g_ref, b_ref, o_ref,
                     router=None):
    att = att_ref[...]
    ms = jnp.mean(att * att, axis=-1, keepdims=True)
    att = att * lax.rsqrt(ms + LN_EPS) * ag_ref[...]
    mix = jnp.dot(att.astype(BF16), wa_ref[...], preferred_element_type=F32)
    mix = mix + jnp.dot(ssd_ref[...], ws_ref[...], preferred_element_type=F32)
    mix = mix + jnp.dot(gm_ref[...], wg_ref[...], preferred_element_type=F32)
    h1 = _layer_norm(ALPHA * h_ref[...] + mix, g_ref[...], b_ref[...])
    o_ref[...] = h1
    if router:
        wr_ref, logit_ref = router
        logit_ref[...] = _dot_bf16x3(wr_ref[...], h1.T)


def _out_proj_kernel_routed(att_ref, ssd_ref, gm_ref, h_ref, ag_ref, wa_ref, ws_ref, wg_ref, g_ref, b_ref, wr_ref,
                            o_ref, logit_ref):
    _out_proj_kernel(att_ref, ssd_ref, gm_ref, h_ref, ag_ref, wa_ref, ws_ref, wg_ref, g_ref, b_ref, o_ref,
                     router=(wr_ref, logit_ref))


def _out_proj(att, ssd, gm, h, p, w_router=None):
    m = h.shape[0]
    row = lambda i: (i, 0)
    const = lambda a: pl.BlockSpec(a.shape, lambda i: (0, 0))
    small = [p["att_norm_g"], p["w_out_att"], p["w_out_ssd"], p["w_out_gm"], p["ln1_g"], p["ln1_b"]]
    out_specs = [pl.BlockSpec((OUT_TM, D_MODEL), row)]
    out_shape = [jax.ShapeDtypeStruct((m, D_MODEL), F32)]
    body = _out_proj_kernel
    if w_router is not None:
        small.append(jnp.pad(w_router.T, ((0, LOGIT_ROWS - N_EXPERTS), (0, 0))))
        out_specs.append(pl.BlockSpec((LOGIT_ROWS, OUT_TM), lambda i: (0, i)))
        out_shape.append(jax.ShapeDtypeStruct((LOGIT_ROWS, m), F32))
        body = _out_proj_kernel_routed
    return pl.pallas_call(
        body,
        grid=(m // OUT_TM,),
        in_specs=[pl.BlockSpec((OUT_TM, ATT_W), row), pl.BlockSpec((OUT_TM, SSD_PAD), row),
                  pl.BlockSpec((OUT_TM, GM_W), row), pl.BlockSpec((OUT_TM, D_MODEL), row)]
                 + [const(a) for a in small],
        out_specs=out_specs,
        out_shape=out_shape,
        compiler_params=_params(("parallel",)),
        name="out_proj_ln1",
    )(att, ssd, gm, h, *small)


FFN_TM = 1024


DOWN_PIECE = 512


def _swiglu_chunks(acc_ref, xb, w_gate, w_up, w_down, widths):
    dot = functools.partial(jnp.dot, preferred_element_type=F32)
    bounds = [sum(widths[:c]) for c in range(len(widths) + 1)]
    chunk = lambda c: slice(bounds[c], bounds[c + 1])
    up = lambda c: (dot(xb, w_gate(chunk(c))), dot(xb, w_up(chunk(c))))
    pending = up(0)
    n_out = acc_ref.shape[-1]
    for c in range(len(widths)):
        hg, hu = pending
        if c + 1 < len(widths):
            pending = up(c + 1)
        hid = (_silu(hg) * hu).astype(BF16)
        w_d = w_down(chunk(c))
        for lo in range(0, n_out, DOWN_PIECE):
            acc_ref[:, lo:lo + DOWN_PIECE] += dot(hid, w_d[:, lo:lo + DOWN_PIECE])


FFN_CHUNK = 512


def _ffn_kernel(x_ref, wg_ref, wu_ref, wd_ref, g_ref, b_ref, o_ref, acc_ref):
    d_ff = wg_ref.shape[-1]
    widths = (FFN_CHUNK,) * (d_ff // FFN_CHUNK) + ((d_ff % FFN_CHUNK,) if d_ff % FFN_CHUNK else ())
    x = x_ref[...]
    acc_ref[...] = ALPHA * x
    _swiglu_chunks(acc_ref, x.astype(BF16), lambda c: wg_ref[:, c], lambda c: wu_ref[:, c],
                   lambda c: wd_ref[c, :], widths)
    o_ref[...] = _layer_norm(acc_ref[...], g_ref[...], b_ref[...])


def _ffn(x, wg, wu, wd, ln_g, ln_b):
    m = x.shape[0]
    resident = lambda a: pl.BlockSpec(a.shape, lambda i: (0, 0), pipeline_mode=pl.Buffered(1))
    return pl.pallas_call(
        _ffn_kernel,
        grid=(m // FFN_TM,),
        in_specs=[pl.BlockSpec((FFN_TM, D_MODEL), lambda i: (i, 0)), resident(wg), resident(wu), resident(wd),
                  pl.BlockSpec((1, D_MODEL), lambda i: (0, 0)),
                  pl.BlockSpec((1, D_MODEL), lambda i: (0, 0))],
        out_specs=pl.BlockSpec((FFN_TM, D_MODEL), lambda i: (i, 0)),
        out_shape=jax.ShapeDtypeStruct((m, D_MODEL), F32),
        scratch_shapes=[pltpu.VMEM((FFN_TM, D_MODEL), F32)],
        compiler_params=_params(("parallel",)),
        name="ffn_ln2",
    )(x, wg, wu, wd, ln_g.reshape(1, D_MODEL), ln_b.reshape(1, D_MODEL))


MOE_TR = 256
MOE_HALF = 512
MOE_UNITS = 3
MOE_TM = MOE_UNITS * MOE_HALF
MOE_TF = 512
MOE_CHUNKS = (256, 256)
SEG_ALIGN = 16
MOE_WIN_SMALL = 128
INFO_GATE = N_EXPERTS


def _expert_cap(m):
    n_tiles = m // MOE_TR
    rows = m + (SEG_ALIGN - 1) * n_tiles + MOE_TR + MOE_HALF
    return -(-rows // MOE_TM) * MOE_TM


def _rows_computed(lens):
    return jnp.maximum(-(-lens // MOE_HALF), 1) * MOE_HALF


def _dispatch_kernel(h_ref, logit_ref, info_ref, offs_ref, lens_ref, small_ref, xs_hbm, run_ref, xbuf_ref, zbuf_ref,
                     sem_ref, zsem_ref, *, n_tiles, cap):
    t = pl.program_id(0)
    tr = MOE_TR
    slot = t % 2

    @pl.when(t == 0)
    def _():
        for e in range(N_EXPERTS):
            run_ref[e] = 0

    xb = h_ref[...].astype(BF16)
    lg = logit_ref[0:N_EXPERTS, :]
    row = lax.broadcasted_iota(jnp.int32, (N_EXPERTS, tr), 0)
    m1 = jnp.max(lg, axis=0, keepdims=True)
    i1 = jnp.min(jnp.where(lg == m1, row, N_EXPERTS), axis=0, keepdims=True)
    rest = jnp.where(row == i1, -jnp.inf, lg)
    m2 = jnp.max(rest, axis=0, keepdims=True)
    i2 = jnp.min(jnp.where(rest == m2, row, N_EXPERTS), axis=0, keepdims=True)
    e2 = jnp.exp(m2 - m1)
    p1 = 1.0 / (1.0 + e2)
    gate_t = jnp.where(row == i1, p1, 0.0) + jnp.where(row == i2, e2 * p1, 0.0)
    sel_t = jnp.where((row == i1) | (row == i2), 1.0, 0.0)
    before = (lax.broadcasted_iota(jnp.int32, (tr, tr), 0) < lax.broadcasted_iota(jnp.int32, (tr, tr), 1))
    rank_t = jnp.dot(sel_t.astype(BF16), jnp.where(before, 1.0, 0.0).astype(BF16),
                     preferred_element_type=F32)
    cnt = jnp.sum(sel_t, axis=1, keepdims=True).astype(jnp.int32)
    info_t = jnp.concatenate([rank_t, gate_t, jnp.zeros((LANES - 2 * N_EXPERTS, tr), F32)], axis=0)
    info_ref[...] = info_t.T

    small = jnp.max(cnt) <= MOE_WIN_SMALL
    small_ref[t] = small.astype(jnp.int32)

    def seg_copy(sl, e, off, rows):
        return pltpu.make_async_copy(xbuf_ref.at[sl, e, pl.ds(0, rows)],
                                     xs_hbm.at[pl.ds(pl.multiple_of(off, SEG_ALIGN), rows)], sem_ref.at[sl, e])

    def by_window(flag, fn):
        pl.when(flag)(functools.partial(fn, MOE_WIN_SMALL))
        pl.when(jnp.logical_not(flag))(functools.partial(fn, tr))

    def compact(rows):
        dst_row = lax.broadcasted_iota(jnp.int32, (rows, tr), 0).astype(F32)
        onehots = [jnp.where((rank_t[e:e + 1, :] == dst_row) & (sel_t[e:e + 1, :] > 0.0), 1.0, 0.0).astype(BF16)
                   for e in range(N_EXPERTS)]
        onehot_all = jnp.concatenate(onehots, axis=0)
        half = D_MODEL // 2
        for c in range(2):
            packed = jnp.dot(onehot_all, xb[:, c * half:(c + 1) * half], preferred_element_type=F32)
            xbuf_ref[slot, :, 0:rows, c * half:(c + 1) * half] = packed.astype(BF16).reshape(N_EXPERTS, rows, half)

    by_window(small, compact)

    def wait_previous(rows):
        for e in range(N_EXPERTS):
            seg_copy(1 - slot, e, 0, rows).wait()

    @pl.when(t > 0)
    def _():
        by_window(small_ref[jnp.maximum(t - 1, 0)] == 1, wait_previous)

    def start_all(rows):
        for e in range(N_EXPERTS):
            seg_copy(slot, e, e * cap + run_ref[e], rows).start()

    by_window(small, start_all)
    for e in range(N_EXPERTS):
        offs_ref[t * N_EXPERTS + e] = e * cap + run_ref[e]
        run_ref[e] = run_ref[e] + jnp.bitwise_and(cnt[e, 0] + (SEG_ALIGN - 1), -SEG_ALIGN)

    def wait_current(rows):
        for e in range(N_EXPERTS):
            seg_copy(slot, e, 0, rows).wait()

    @pl.when(t == n_tiles - 1)
    def _():
        zbuf_ref[...] = jnp.zeros_like(zbuf_ref)
        by_window(small, wait_current)
        tails = []
        for e in range(N_EXPERTS):
            lens_ref[e] = run_ref[e]
            tail = pl.multiple_of(e * cap + run_ref[e], SEG_ALIGN)
            tails.append(pltpu.make_async_copy(zbuf_ref, xs_hbm.at[pl.ds(tail, MOE_HALF)], zsem_ref.at[e]))
            tails[-1].start()
        for cp in tails:
            cp.wait()


def _dispatch(h, logits):
    m = h.shape[0]
    n_tiles = m // MOE_TR
    cap = _expert_cap(m)
    smem = pl.BlockSpec(memory_space=pltpu.SMEM)
    return pl.pallas_call(
        functools.partial(_dispatch_kernel, n_tiles=n_tiles, cap=cap),
        grid=(n_tiles,),
        in_specs=[pl.BlockSpec((MOE_TR, D_MODEL), lambda t: (t, 0)),
                  pl.BlockSpec((LOGIT_ROWS, MOE_TR), lambda t: (0, t))],
        out_specs=[pl.BlockSpec((MOE_TR, LANES), lambda t: (t, 0)), smem, smem, smem,
                   pl.BlockSpec(memory_space=pl.ANY)],
        out_shape=[jax.ShapeDtypeStruct((m, LANES), F32),
                   jax.ShapeDtypeStruct((n_tiles * N_EXPERTS,), jnp.int32),
                   jax.ShapeDtypeStruct((N_EXPERTS,), jnp.int32),
                   jax.ShapeDtypeStruct((n_tiles,), jnp.int32),
                   jax.ShapeDtypeStruct((N_EXPERTS * cap, D_MODEL), BF16)],
        scratch_shapes=[pltpu.SMEM((N_EXPERTS,), jnp.int32),
                        pltpu.VMEM((2, N_EXPERTS, MOE_TR, D_MODEL), BF16),
                        pltpu.VMEM((MOE_HALF, D_MODEL), BF16),
                        pltpu.SemaphoreType.DMA((2, N_EXPERTS)),
                        pltpu.SemaphoreType.DMA((N_EXPERTS,))],
        compiler_params=_params(("arbitrary",)),
        name="moe_dispatch",
    )(h, logits)


def _expert_ffn_kernel(exp_ref, blk_ref, units_ref, x_ref, wg_ref, wu_ref, wd_ref, y_ref, acc_ref):
    w = pl.program_id(0)
    f = pl.program_id(1)

    def tile(rows):
        acc = acc_ref.at[pl.ds(0, rows)]

        @pl.when(f == 0)
        def _():
            acc[...] = jnp.zeros((rows, D_MODEL), F32)

        _swiglu_chunks(acc, x_ref[0:rows, :], lambda c: wg_ref[0, :, c].astype(BF16),
                       lambda c: wu_ref[0, :, c].astype(BF16), lambda c: wd_ref[0, c, :].astype(BF16), MOE_CHUNKS)

        @pl.when(f == pl.num_programs(1) - 1)
        def _():
            y_ref[0:rows, :] = acc[...].astype(BF16)

    for units in range(1, MOE_UNITS + 1):
        pl.when(units_ref[w] == units)(functools.partial(tile, units * MOE_HALF))


def _expert_ffn(xs, exp_w, blk_w, units_w, w_gate, w_up, w_down):
    d_ff = w_gate.shape[-1]
    n_f = d_ff // MOE_TF
    fcol = lambda w, f, exp, blk, units: jnp.where(units[w] != 0, f, n_f - 1)
    grid_spec = pltpu.PrefetchScalarGridSpec(
        num_scalar_prefetch=3,
        grid=(exp_w.shape[0], n_f),
        in_specs=[pl.BlockSpec((MOE_TM, D_MODEL), lambda w, f, exp, blk, valid: (blk[w], 0)),
                  pl.BlockSpec((1, D_MODEL, MOE_TF), lambda w, f, exp, blk, valid: (exp[w], 0, fcol(w, f, exp, blk, valid))),
                  pl.BlockSpec((1, D_MODEL, MOE_TF), lambda w, f, exp, blk, valid: (exp[w], 0, fcol(w, f, exp, blk, valid))),
                  pl.BlockSpec((1, MOE_TF, D_MODEL), lambda w, f, exp, blk, valid: (exp[w], fcol(w, f, exp, blk, valid), 0))],
        out_specs=pl.BlockSpec((MOE_TM, D_MODEL), lambda w, f, exp, blk, valid: (blk[w], 0)),
        scratch_shapes=[pltpu.VMEM((MOE_TM, D_MODEL), F32)])
    return pl.pallas_call(
        _expert_ffn_kernel,
        grid_spec=grid_spec,
        out_shape=jax.ShapeDtypeStruct(xs.shape, BF16),
        compiler_params=_params(("arbitrary", "arbitrary")),
        name="moe_expert_ffn",
    )(exp_w, blk_w, units_w, xs, w_gate, w_up, w_down)


def _combine_kernel(starts_ref, shifts_ref, small_ref, info_ref, h_ref, g_ref, b_ref, ys_hbm, o_ref, ybuf_ref,
                    sem_ref, *, n_tiles):
    t = pl.program_id(0)
    tr = MOE_TR
    slot = t % 2

    def seg_copy(tile, sl, e, rows):
        off = pl.multiple_of(starts_ref[tile * N_EXPERTS + e], SEG_ALIGN)
        return pltpu.make_async_copy(ys_hbm.at[pl.ds(off, rows)], ybuf_ref.at[sl, e, pl.ds(0, rows)],
                                     sem_ref.at[sl, e])

    def by_window(tile, fn):
        flag = small_ref[tile] == 1
        pl.when(flag)(functools.partial(fn, MOE_WIN_SMALL))
        pl.when(jnp.logical_not(flag))(functools.partial(fn, tr))

    def fetch(tile, sl):
        def start(rows):
            for e in range(N_EXPERTS):
                seg_copy(tile, sl, e, rows).start()
        by_window(tile, start)

    @pl.when(t == 0)
    def _():
        fetch(0, 0)

    @pl.when(t + 1 < n_tiles)
    def _():
        fetch(jnp.minimum(t + 1, n_tiles - 1), 1 - slot)

    def gather(rows):
        for e in range(N_EXPERTS):
            seg_copy(t, slot, e, rows).wait()
        info = info_ref[...]
        src_row = lax.broadcasted_iota(jnp.int32, (tr, rows), 1).astype(F32)
        acc = ALPHA * h_ref[...]
        for e in range(N_EXPERTS):
            rank_c = info[:, e:e + 1] + shifts_ref[t * N_EXPERTS + e].astype(F32)
            gate_c = info[:, INFO_GATE + e:INFO_GATE + e + 1]
            onehot = jnp.where((rank_c == src_row) & (gate_c != 0.0), 1.0, 0.0).astype(BF16)
            acc = acc + gate_c * jnp.dot(onehot, ybuf_ref[slot, e, 0:rows, :], preferred_element_type=F32)
        o_ref[...] = _layer_norm(acc, g_ref[...], b_ref[...])

    by_window(t, gather)


def _combine(starts, shifts, small, info, h, ys, ln_g, ln_b):
    m = h.shape[0]
    n_tiles = m // MOE_TR
    grid_spec = pltpu.PrefetchScalarGridSpec(
        num_scalar_prefetch=3,
        grid=(n_tiles,),
        in_specs=[pl.BlockSpec((MOE_TR, LANES), lambda t, *_: (t, 0)),
                  pl.BlockSpec((MOE_TR, D_MODEL), lambda t, *_: (t, 0)),
                  pl.BlockSpec((1, D_MODEL), lambda t, *_: (0, 0)),
                  pl.BlockSpec((1, D_MODEL), lambda t, *_: (0, 0)),
                  pl.BlockSpec(memory_space=pl.ANY)],
        out_specs=pl.BlockSpec((MOE_TR, D_MODEL), lambda t, *_: (t, 0)),
        scratch_shapes=[pltpu.VMEM((2, N_EXPERTS, MOE_TR, D_MODEL), BF16),
                        pltpu.SemaphoreType.DMA((2, N_EXPERTS))])
    return pl.pallas_call(
        functools.partial(_combine_kernel, n_tiles=n_tiles),
        grid_spec=grid_spec,
        out_shape=jax.ShapeDtypeStruct((m, D_MODEL), F32),
        compiler_params=_params(("arbitrary",)),
        name="moe_combine_ln2",
    )(starts, shifts, small, info, h, ln_g.reshape(1, D_MODEL), ln_b.reshape(1, D_MODEL), ys)


def _work_list(lens, m):
    cap = _expert_cap(m)
    n_tiles = m // MOE_TR
    max_rows = 2 * m + (SEG_ALIGN - 1) * min(N_EXPERTS * n_tiles, 2 * m)
    w_max = -(-(max_rows + N_EXPERTS * MOE_HALF) // MOE_TM) + N_EXPERTS
    rows_e = _rows_computed(lens)
    tiles_e = -(-rows_e // MOE_TM)
    ends = jnp.cumsum(tiles_e)
    w = jnp.arange(w_max, dtype=jnp.int32)
    wc = jnp.minimum(w, ends[-1] - 1)
    exp_w = jnp.sum((wc[:, None] >= ends[None, :]).astype(jnp.int32), axis=1)
    tile_w = wc - (ends - tiles_e)[exp_w]
    blk_w = exp_w * (cap // MOE_TM) + tile_w
    units_w = jnp.minimum((rows_e[exp_w] - tile_w * MOE_TM) // MOE_HALF, MOE_UNITS)
    units_w = jnp.where(w < ends[-1], units_w, 0)
    return exp_w.astype(jnp.int32), blk_w.astype(jnp.int32), units_w.astype(jnp.int32)


def _read_windows(offs, small, lens, m):
    cap = _expert_cap(m)
    region = jnp.arange(N_EXPERTS, dtype=jnp.int32) * cap
    rel = offs.reshape(-1, N_EXPERTS) - region
    width = jnp.where(small == 1, MOE_WIN_SMALL, MOE_TR)[:, None]
    start = jnp.minimum(rel, _rows_computed(lens)[None, :] - width)
    return (start + region).reshape(-1).astype(jnp.int32), (rel - start).reshape(-1).astype(jnp.int32)


def _moe_layer(h, logits, w_gate, w_up, w_down, ln_g, ln_b):
    m = h.shape[0]
    info, offs, lens, small, xs = _dispatch(h, logits)
    exp_w, blk_w, units_w = _work_list(lens, m)
    ys = _expert_ffn(xs, exp_w, blk_w, units_w, w_gate, w_up, w_down)
    starts, shifts = _read_windows(offs, small, lens, m)
    return _combine(starts, shifts, small, info, h, ys, ln_g, ln_b)


def _pad_groups(a):
    lead = a.shape[:-1]
    a = a.reshape(*lead, SSD_GROUPS, GROUP_W)
    a = jnp.pad(a, [(0, 0)] * len(lead) + [(0, 0), (0, GROUP_PAD - GROUP_W)])
    return a.reshape(*lead, SSD_PAD)


def _head_lanes(v):
    return jnp.pad(v, (0, LANES - SSD_HEADS)).reshape(1, LANES)


D_IN = 3 * ATT_W + SSD_W + (SSD_W + 2 * SSD_GROUPS * SSD_STATE) + SSD_HEADS + 2 * GM_W
PERM_ROWS = 256


def _permute_w_kernel(w_ref, o_ref):
    src_z = 3 * ATT_W
    src_xs = src_z + SSD_W
    src_b = src_xs + SSD_W
    src_c = src_b + SSD_GROUPS * SSD_STATE
    src_dt = src_c + SSD_GROUPS * SSD_STATE
    src_gu = src_dt + SSD_HEADS
    src_gv = src_gu + GM_W
    o_ref[...] = jnp.zeros_like(o_ref)

    def put(dst, src, width):
        o_ref[:, dst:dst + width] = w_ref[0, :, src:src + width].astype(BF16)

    for g in range(SSD_GROUPS):
        put(COL_Z + g * GROUP_PAD, src_z + g * GROUP_W, GROUP_W)
        put(COL_XS + g * GROUP_PAD, src_xs + g * GROUP_W, GROUP_W)
    put(COL_B, src_b, SSD_GROUPS * SSD_STATE)
    put(COL_C, src_c, SSD_GROUPS * SSD_STATE)
    put(COL_GU, src_gu, GM_W)
    put(COL_GV, src_gv, GM_W)
    put(COL_Q, 0, ATT_W)
    put(COL_K, ATT_W, ATT_W)
    put(COL_V, 2 * ATT_W, ATT_W)
    put(COL_DT, src_dt, SSD_HEADS)


def _permute_w_in(w_in, layer):
    return pl.pallas_call(
        _permute_w_kernel,
        grid=(D_MODEL // PERM_ROWS,),
        in_specs=[pl.BlockSpec((1, PERM_ROWS, D_IN), lambda i: (layer, i, 0))],
        out_specs=pl.BlockSpec((PERM_ROWS, D_W), lambda i: (i, 0)),
        out_shape=jax.ShapeDtypeStruct((D_MODEL, D_W), BF16),
        compiler_params=_params(("parallel",)),
        name="permute_w_in",
    )(w_in)


def _layer_params(i, w_in, conv_w, conv_b, dt_bias, a_log, d_skip, ssd_norm_g, att_norm_g, gm_ln_g, gm_ln_b,
                  gm_w_s, gm_b_s, gm_norm_g, w_out, ln1_g, ln1_b):
    cw, cb = conv_w[i], conv_b[i]
    nbc = SSD_GROUPS * SSD_STATE
    wo = w_out[i]
    return {
        "w_in": _permute_w_in(w_in, i),
        "conv_w_xs": _pad_groups(cw[:, :SSD_W]),
        "conv_w_b": cw[:, SSD_W:SSD_W + nbc],
        "conv_w_c": cw[:, SSD_W + nbc:],
        "conv_b_xs": _pad_groups(cb[:SSD_W]).reshape(1, SSD_PAD),
        "conv_b_b": cb[SSD_W:SSD_W + nbc].reshape(1, nbc),
        "conv_b_c": cb[SSD_W + nbc:].reshape(1, nbc),
        "dt_bias": _head_lanes(dt_bias[i]),
        "a_log": _head_lanes(a_log[i]),
        "d_skip": _pad_groups(jnp.repeat(d_skip[i], SSD_HEAD_DIM)).reshape(1, SSD_PAD),
        "ssd_norm_g": _pad_groups(ssd_norm_g[i]).reshape(1, SSD_PAD),
        "att_norm_g": att_norm_g[i].reshape(1, ATT_W),
        "gm_ln_g": gm_ln_g[i].reshape(1, GM_W),
        "gm_ln_b": gm_ln_b[i].reshape(1, GM_W),
        "gm_w_s": gm_w_s[i],
        "gm_b_s": jnp.repeat(gm_b_s[i].T, GM_W // GM_GROUPS, axis=1),
        "gm_norm_g": gm_norm_g[i].reshape(1, GM_W),
        "w_out_att": wo[:ATT_W].astype(BF16),
        "w_out_ssd": _pad_groups(wo[ATT_W:ATT_W + SSD_W].T).T.astype(BF16),
        "w_out_gm": wo[ATT_W + SSD_W:].astype(BF16),
        "ln1_g": ln1_g[i].reshape(1, D_MODEL),
        "ln1_b": ln1_b[i].reshape(1, D_MODEL),
    }


def _rope_tables(positions):
    inv = ROPE_THETA ** (-jnp.arange(0, ROPE_DIM, 2, dtype=F32) / ROPE_DIM)
    d = jnp.arange(LANES) % HEAD_DIM
    inv_lane = jnp.where(d < ROPE_DIM, inv[d % ROPE_HALF], 0.0)
    ang = positions.astype(F32)[..., None] * inv_lane
    cos, sin = jnp.cos(ang), jnp.sin(ang)
    return cos, jnp.where(d < ROPE_HALF, -sin, 0.0), jnp.where(d >= ROPE_HALF, sin, 0.0)


def kernel(x, positions, ln_in_g, ln_in_b, w_in, conv_w, conv_b, dt_bias, a_log, d_skip, ssd_norm_g, att_norm_g, gm_ln_g, gm_ln_b, gm_w_s, gm_b_s, gm_norm_g, w_out, ln1_g, ln1_b, ln2_g, ln2_b, ffn_w_gate, ffn_w_up, ffn_w_down, router_w, moe_w_gate, moe_w_up, moe_w_down):
    bsz, seq, _ = x.shape
    m = bsz * seq
    rope_c, rope_sa, rope_sb = _rope_tables(positions)
    h = x.reshape(m, D_MODEL)
    for i in range(DEPTH):
        p = _layer_params(i, w_in, conv_w, conv_b, dt_bias, a_log, d_skip, ssd_norm_g, att_norm_g, gm_ln_g,
                          gm_ln_b, gm_w_s, gm_b_s, gm_norm_g, w_out, ln1_g, ln1_b)
        if i == 0:
            proj, gm, h = _in_proj(h, p["w_in"], p, ln=(ln_in_g, ln_in_b))
        else:
            proj, gm = _in_proj(h, p["w_in"], p)
        proj3 = proj.reshape(bsz, seq, D_PROJ)
        att = _attention(proj3, rope_c, rope_sa, rope_sb).reshape(m, ATT_W)
        ssd = _ssd(proj3, p, unroll=4 if i == 0 else 2).reshape(m, SSD_PAD)
        j = i // 2
        if i % 2 == 0:
            (h,) = _out_proj(att, ssd, gm, h, p)
            h = _ffn(h, ffn_w_gate[j].astype(BF16), ffn_w_up[j].astype(BF16), ffn_w_down[j].astype(BF16),
                     ln2_g[i], ln2_b[i])
        else:
            h, logits = _out_proj(att, ssd, gm, h, p, w_router=router_w[j])
            h = _moe_layer(h, logits, moe_w_gate[j], moe_w_up[j], moe_w_down[j], ln2_g[i], ln2_b[i])
    return h.reshape(bsz, seq, D_MODEL)
```

```python
import functools

import jax
import jax.numpy as jnp
from jax import lax
from jax.experimental import pallas as pl
from jax.experimental.pallas import tpu as pltpu

F32 = jnp.float32
BF16 = jnp.bfloat16
HIGHEST = lax.Precision.HIGHEST

D_MODEL = 1024
ATT_HEADS = 6
HEAD_DIM = 64
ATT_W = ATT_HEADS * HEAD_DIM
ROPE_DIM = HEAD_DIM // 4
ROPE_HALF = ROPE_DIM // 2
ROPE_THETA = 500000.0
MOBA_BLOCK = 256
MOBA_TOPK = 3
SSD_HEADS = 6
SSD_HEAD_DIM = 64
SSD_W = SSD_HEADS * SSD_HEAD_DIM
SSD_GROUPS = 2
SSD_STATE = 128
SSD_CONV = 4
SSD_CHUNK = 128
GM_GROUPS = 4
GM_W = 256
GM_CHUNK = 128
N_EXPERTS = 8
DEPTH = 2
ALPHA = (2.0 * DEPTH) ** 0.25
NEG = -1e30
LN_EPS = 1e-5

LANES = 128
GROUP_W = SSD_W // SSD_GROUPS
GROUP_PAD = 256
SSD_PAD = SSD_GROUPS * GROUP_PAD
HEADS_PER_GROUP = SSD_HEADS // SSD_GROUPS

COL_Z = 0
COL_XS = 512
COL_B = 1024
COL_C = 1280
COL_Q = 1536
COL_K = 1920
COL_V = 2304
COL_DT = 2688
D_PROJ = 2816
COL_GU = 2816
COL_GV = 3072
D_W = 3328

VMEM_LIMIT = 56 * 1024 * 1024


def _params(sem, vmem=VMEM_LIMIT):
    return pltpu.CompilerParams(dimension_semantics=sem, vmem_limit_bytes=vmem)


def _layer_norm(x, g, b):
    mu = jnp.mean(x, axis=-1, keepdims=True)
    xc = x - mu
    var = jnp.mean(xc * xc, axis=-1, keepdims=True)
    return xc * lax.rsqrt(var + LN_EPS) * g + b


def _silu(x):
    half = 0.5 * x
    return half + half * jnp.tanh(half)


def _split_bf16(x):
    hi = x.astype(BF16)
    return hi, (x - hi.astype(F32)).astype(BF16)


def _dot_bf16x3(a, b):
    a_hi, a_lo = _split_bf16(a)
    b_hi, b_lo = _split_bf16(b)
    dot = functools.partial(jnp.dot, preferred_element_type=F32)
    return dot(a_hi, b_hi) + (dot(a_lo, b_hi) + dot(a_hi, b_lo))


def _gelu_tanh(x):
    return 0.5 * x * (1.0 + jnp.tanh(0.7978845608028654 * (x + 0.044715 * x * x * x)))


PROJ_TM = 512
PROJ_CH = 256
PROJ_TAIL = 4


def _in_proj_kernel(*refs, apply_ln):
    if apply_ln:
        x_ref, g_ref, b_ref, w_ref, lng_ref, lnb_ref, ws_ref, bs_ref, ng_ref, proj_ref, gm_ref, h_ref = refs
        x = _layer_norm(x_ref[...], g_ref[...], b_ref[...])
        h_ref[...] = x
    else:
        x_ref, w_ref, lng_ref, lnb_ref, ws_ref, bs_ref, ng_ref, proj_ref, gm_ref = refs
        x = x_ref[...]
    xb = x.astype(BF16)

    def project(j):
        cols = slice(j * PROJ_CH, (j + 1) * PROJ_CH)
        proj_ref[:, cols] = jnp.dot(xb, w_ref[:, cols], preferred_element_type=F32)

    gu = jnp.dot(xb, w_ref[:, COL_GU:COL_GU + GM_W], preferred_element_type=F32)
    gv = jnp.dot(xb, w_ref[:, COL_GV:COL_GV + GM_W], preferred_element_type=F32)
    u, vb = _gmlp_activate(gu, gv, lng_ref, lnb_ref)
    n_chunks = D_PROJ // PROJ_CH
    for j in range(n_chunks - PROJ_TAIL):
        project(j)
    _gmlp_mix(u, vb, ws_ref, bs_ref, ng_ref, gm_ref)
    for j in range(n_chunks - PROJ_TAIL, n_chunks):
        project(j)


def _in_proj(x, w, p, ln=None):
    m = x.shape[0]
    row = lambda i: (i, 0)
    const = lambda i: (0, 0)
    whole = lambda a: pl.BlockSpec(a.shape, lambda i: (0,) * a.ndim)
    in_specs = [pl.BlockSpec((PROJ_TM, D_MODEL), row)]
    args = [x]
    out_shape = [jax.ShapeDtypeStruct((m, D_PROJ), F32), jax.ShapeDtypeStruct((m, GM_W), BF16)]
    out_specs = [pl.BlockSpec((PROJ_TM, D_PROJ), row), pl.BlockSpec((PROJ_TM, GM_W), row)]
    if ln is not None:
        in_specs += [pl.BlockSpec((1, D_MODEL), const), pl.BlockSpec((1, D_MODEL), const)]
        args += [ln[0].reshape(1, D_MODEL), ln[1].reshape(1, D_MODEL)]
        out_shape.append(jax.ShapeDtypeStruct((m, D_MODEL), F32))
        out_specs.append(pl.BlockSpec((PROJ_TM, D_MODEL), row))
    in_specs.append(pl.BlockSpec((D_MODEL, D_W), const))
    args.append(w)
    gm_small = [p["gm_ln_g"], p["gm_ln_b"], p["gm_w_s"], p["gm_b_s"], p["gm_norm_g"]]
    in_specs += [whole(a) for a in gm_small]
    args += gm_small
    return pl.pallas_call(
        functools.partial(_in_proj_kernel, apply_ln=ln is not None),
        grid=(m // PROJ_TM,),
        in_specs=in_specs,
        out_specs=out_specs,
        out_shape=out_shape,
        compiler_params=_params(("parallel",)),
        name="in_proj",
    )(*args)


VT_ROWS = HEAD_DIM + 16
LOG2E = 1.4426950408889634


ATT_VALUE_LAG = 6


def _attn_kernel(q_ref, k_ref, v_ref, c_ref, sa_ref, sb_ref, o_ref, kb_ref, vt_ref, *, seq):
    nb = seq // MOBA_BLOCK
    tq = MOBA_BLOCK
    lane = lax.broadcasted_iota(jnp.int32, (1, LANES), 1)

    def rope(x, rows):
        return (x * c_ref[0, rows, :] + pltpu.roll(x, LANES - ROPE_HALF, 1) * sa_ref[0, rows, :]
                + pltpu.roll(x, ROPE_HALF, 1) * sb_ref[0, rows, :])

    k = rope(k_ref[0], slice(None))
    kb_ref[...] = k.astype(BF16)
    k_mean = jnp.mean(k.reshape(nb, MOBA_BLOCK, LANES), axis=1)
    v_t = v_ref[0].T
    ones_rows = jnp.where(lax.broadcasted_iota(jnp.int32, (VT_ROWS - HEAD_DIM, seq), 0) == 0, 1.0, 0.0)
    for hh in range(2):
        vt_ref[hh] = jnp.concatenate([v_t[hh * HEAD_DIM:(hh + 1) * HEAD_DIM], ones_rows], axis=0).astype(BF16)

    qscale = HEAD_DIM ** -0.5 * LOG2E
    causal_t = (lax.broadcasted_iota(jnp.int32, (tq, tq), 0) <= lax.broadcasted_iota(jnp.int32, (tq, tq), 1))
    blk_row = lax.broadcasted_iota(jnp.int32, (nb, tq), 0)

    chain_q = {}
    chain_bias = {}
    chain_m = {}
    chain_o = {}
    q_tiles = {}

    def start_chain(j, hh):
        if hh == 0:
            rows = slice(j * tq, (j + 1) * tq)
            q_tiles[j] = rope(q_ref[0, rows, :], rows) * qscale
        q = q_tiles[j] if hh == 0 else q_tiles.pop(j)
        head_lanes = (lane < HEAD_DIM) if hh == 0 else (lane >= HEAD_DIM)
        qh_t = jnp.where(head_lanes, q, 0.0).T
        chain_q[(j, hh)] = qh_t.astype(BF16)
        chain_bias[(j, hh)] = None
        if j > MOBA_TOPK:
            gate_t = _dot_bf16x3(k_mean, qh_t)
            rank = jnp.zeros((nb, tq), F32)
            for m in range(j):
                g_m = gate_t[m:m + 1, :]
                beats = (g_m > gate_t) | ((g_m == gate_t) & (blk_row > m))
                rank = rank + jnp.where(beats & (blk_row != m), 1.0, 0.0)
            bias = jnp.where((blk_row < j) & (rank >= MOBA_TOPK), NEG, 0.0)
            chain_bias[(j, hh)] = [bias[n:n + 1, :] for n in range(j)]

    def scores(j, hh, n):
        return jnp.dot(kb_ref[n * tq:(n + 1) * tq, :], chain_q[(j, hh)], preferred_element_type=F32)

    def probs(j, hh, n, s_t):
        key = (j, hh)
        if n == j:
            s_t = jnp.where(causal_t, s_t, NEG)
            m_new = jnp.max(s_t, axis=0, keepdims=True)
            alpha = None
            shift = m_new
        else:
            bias_n = 0.0 if chain_bias[key] is None else chain_bias[key][n]
            m_new = jnp.maximum(chain_m[key], jnp.max(s_t, axis=0, keepdims=True) + bias_n)
            alpha = jnp.exp2(chain_m[key] - m_new)
            shift = m_new - bias_n
        chain_m[key] = m_new
        return jnp.exp2(s_t - shift).astype(BF16), alpha

    def values(j, hh, n, p_t, alpha):
        key = (j, hh)
        part = jnp.dot(vt_ref[hh, :, n * tq:(n + 1) * tq], p_t, preferred_element_type=F32)
        chain_o[key] = part if alpha is None else chain_o[key] * alpha + part

    def finish(j):
        halves = [chain_o.pop((j, hh)) for hh in range(2)]
        out_t = jnp.concatenate([o[0:HEAD_DIM] / o[HEAD_DIM:HEAD_DIM + 1] for o in halves], axis=0)
        o_ref[0, j * tq:(j + 1) * tq, :] = out_t.T

    value_lag = ATT_VALUE_LAG
    units = []
    for j in range(nb):
        for n in [j] + list(range(j)):
            units += [(j, 0, n), (j, 1, n)]
    s_state = {}
    p_state = {}
    for step in range(len(units) + value_lag):
        if step >= value_lag:
            j, hh, n = unit = units[step - value_lag]
            values(j, hh, n, *p_state.pop(unit))
            last_block = n == (j - 1 if j > 0 else 0)
            if hh == 1 and last_block:
                finish(j)
        if step < len(units):
            j, hh, n = unit = units[step]
            if n == j:
                start_chain(j, hh)
            s_state[unit] = scores(j, hh, n)
        if 1 <= step <= len(units):
            j, hh, n = unit = units[step - 1]
            p_state[unit] = probs(j, hh, n, s_state.pop(unit))


def _attention(proj3, rope_c, rope_sa, rope_sb):
    bsz, seq, _ = proj3.shape
    blk = (1, seq, LANES)
    col = lambda base: (lambda b, p: (b, 0, base // LANES + p))
    tab = lambda b, p: (b, 0, 0)
    return pl.pallas_call(
        functools.partial(_attn_kernel, seq=seq),
        grid=(bsz, ATT_W // LANES),
        in_specs=[pl.BlockSpec(blk, col(COL_Q)), pl.BlockSpec(blk, col(COL_K)), pl.BlockSpec(blk, col(COL_V)),
                  pl.BlockSpec(blk, tab), pl.BlockSpec(blk, tab), pl.BlockSpec(blk, tab)],
        out_specs=pl.BlockSpec(blk, lambda b, p: (b, 0, p)),
        out_shape=jax.ShapeDtypeStruct((bsz, seq, ATT_W), F32),
        scratch_shapes=[pltpu.VMEM((seq, LANES), BF16), pltpu.VMEM((2, VT_ROWS, seq), BF16)],
        compiler_params=_params(("parallel", "arbitrary")),
        name="moba_attention",
    )(proj3, proj3, proj3, rope_c, rope_sa, rope_sb)


SSD_UNROLL = 4
assert HEADS_PER_GROUP == 3 and GROUP_PAD == 2 * LANES and SSD_HEAD_DIM * 2 == LANES


def _expand_heads(small):
    low = lax.broadcasted_iota(jnp.int32, (1, LANES), 1) < SSD_HEAD_DIM
    slabs = []
    for g in range(SSD_GROUPS):
        h0 = g * HEADS_PER_GROUP
        slabs.append(jnp.where(low, small[:, h0:h0 + 1], small[:, h0 + 1:h0 + 2]))
        slabs.append(jnp.where(low, small[:, h0 + 2:h0 + 3], 0.0))
    return jnp.concatenate(slabs, axis=1)


def _ssd_kernel(z_ref, xs_ref, b_ref, c_ref, dt_ref, cwx_ref, cwb_ref, cwc_ref, cbx_ref, cbb_ref, cbc_ref,
                dtb_ref, alog_ref, dskip_ref, ng_ref, o_ref, state_ref, *, seq):
    q = SSD_CHUNK
    n = SSD_STATE
    state_ref[...] = jnp.zeros_like(state_ref)
    row_i = lax.broadcasted_iota(jnp.int32, (q, q), 0)
    col_i = lax.broadcasted_iota(jnp.int32, (q, q), 1)
    lower = row_i >= col_i
    tril = jnp.where(lower, 1.0, 0.0)
    slab = lax.broadcasted_iota(jnp.int32, (1, GROUP_PAD), 1) // SSD_HEAD_DIM
    a_neg = -jnp.exp(alog_ref[...])

    def conv_silu(ref, w_ref, bias_ref, c):
        start = pl.multiple_of(c * q, q)
        cur = ref[0, pl.ds(start, q), :]
        prev_start = pl.multiple_of(jnp.maximum(c * q - 8, 0), 8)
        prev = jnp.where(c > 0, ref[0, pl.ds(prev_start, 8), :], 0.0)
        full = jnp.concatenate([prev, cur], axis=0)
        y = bias_ref[...] + w_ref[SSD_CONV - 1:SSD_CONV, :] * cur
        for i in range(SSD_CONV - 1):
            shift = SSD_CONV - 1 - i
            y = y + w_ref[i:i + 1, :] * pltpu.roll(full, shift, 0)[8:, :]
        return _silu(y)

    def chunk(c, carry):
        start = pl.multiple_of(c * q, q)
        xs = conv_silu(xs_ref, cwx_ref, cbx_ref, c)
        bm = conv_silu(b_ref, cwb_ref, cbb_ref, c)
        cm = conv_silu(c_ref, cwc_ref, cbc_ref, c)
        x = dt_ref[0, pl.ds(start, q), :] + dtb_ref[...]
        dt = jnp.maximum(x, 0.0) + jnp.log(1.0 + jnp.exp(-jnp.abs(x)))
        a = dt * a_neg
        a_cum = jnp.dot(tril, a, precision=HIGHEST, preferred_element_type=F32)
        a_cum_t = a_cum.T
        dt_e = _expand_heads(dt)
        acum_e = _expand_heads(a_cum)
        alast_e = acum_e[q - 1:q, :]
        xdt = xs * dt_e
        decay_e = jnp.exp(alast_e - acum_e)
        y_parts = []
        for g in range(SSD_GROUPS):
            gl = slice(g * GROUP_PAD, (g + 1) * GROUP_PAD)
            nl = slice(g * n, (g + 1) * n)
            b_g = bm[:, nl].astype(BF16)
            b_gt = bm[:, nl].T.astype(BF16)
            c_g = cm[:, nl].astype(BF16)
            cb = lax.dot_general(c_g, b_g, (((1,), (1,)), ((), ())), preferred_element_type=F32)
            xdt_g = xdt[:, gl]
            xdt_b = xdt_g.astype(BF16)
            y_g = jnp.zeros((q, GROUP_PAD), F32)
            for r in range(HEADS_PER_GROUP):
                h = g * HEADS_PER_GROUP + r
                seg = a_cum[:, h:h + 1] - a_cum_t[h:h + 1, :]
                l_h = jnp.where(lower, jnp.exp(jnp.minimum(seg, 0.0)), 0.0)
                y_h = jnp.dot((cb * l_h).astype(BF16), xdt_b, preferred_element_type=F32)
                y_g = jnp.where(slab == r, y_h, y_g)
            prev = state_ref[g]
            y_off = jnp.dot(c_g, prev.astype(BF16), preferred_element_type=F32) * jnp.exp(acum_e[:, gl])
            new = jnp.dot(b_gt, (xdt_g * decay_e[:, gl]).astype(BF16), preferred_element_type=F32)
            state_ref[g] = prev * jnp.exp(alast_e[:, gl]) + new
            y_parts.append(y_g + y_off)
        y = jnp.concatenate(y_parts, axis=1) + dskip_ref[...] * xs
        yz = y * _silu(z_ref[0, pl.ds(start, q), :])
        outs = []
        for g in range(SSD_GROUPS):
            yg = yz[:, g * GROUP_PAD:(g + 1) * GROUP_PAD]
            ms = jnp.sum(yg * yg, axis=-1, keepdims=True) * (1.0 / GROUP_W)
            outs.append(yg * lax.rsqrt(ms + LN_EPS))
        o_ref[0, pl.ds(start, q), :] = (jnp.concatenate(outs, axis=1) * ng_ref[...]).astype(BF16)
        return carry

    lax.fori_loop(0, seq // q, chunk, 0, unroll=SSD_UNROLL)


def _ssd(proj3, p):
    bsz, seq, _ = proj3.shape
    slab = lambda width, base: pl.BlockSpec((1, seq, width), lambda b: (b, 0, base // width))
    const2 = lambda a: pl.BlockSpec(a.shape, lambda b: (0, 0))
    small = [p["conv_w_xs"], p["conv_w_b"], p["conv_w_c"], p["conv_b_xs"], p["conv_b_b"], p["conv_b_c"],
             p["dt_bias"], p["a_log"], p["d_skip"], p["ssd_norm_g"]]
    return pl.pallas_call(
        functools.partial(_ssd_kernel, seq=seq),
        grid=(bsz,),
        in_specs=[slab(SSD_PAD, COL_Z), slab(SSD_PAD, COL_XS), slab(2 * SSD_STATE, COL_B),
                  slab(2 * SSD_STATE, COL_C), slab(LANES, COL_DT)] + [const2(a) for a in small],
        out_specs=pl.BlockSpec((1, seq, SSD_PAD), lambda b: (b, 0, 0)),
        out_shape=jax.ShapeDtypeStruct((bsz, seq, SSD_PAD), BF16),
        scratch_shapes=[pltpu.VMEM((SSD_GROUPS, SSD_STATE, GROUP_PAD), F32)],
        compiler_params=_params(("parallel",)),
        name="ssd",
    )(proj3, proj3, proj3, proj3, proj3, *small)


def _gmlp_activate(gu, gv, lng_ref, lnb_ref):
    return _gelu_tanh(gu), _layer_norm(_gelu_tanh(gv), lng_ref[...], lnb_ref[...]).astype(BF16)


def _gmlp_mix(u, vb, ws_ref, bs_ref, ng_ref, o_ref):
    q = GM_CHUNK
    row_i = lax.broadcasted_iota(jnp.int32, (q, q), 0)
    col_i = lax.broadcasted_iota(jnp.int32, (q, q), 1)
    lower = row_i >= col_i
    grp = lax.broadcasted_iota(jnp.int32, (1, GM_W), 1) // (GM_W // GM_GROUPS)
    w = [jnp.where(lower, ws_ref[g], 0.0).astype(BF16) for g in range(GM_GROUPS)]
    for c in range(u.shape[0] // q):
        rows = slice(c * q, (c + 1) * q)
        mixed = jnp.zeros((q, GM_W), F32)
        for g in range(GM_GROUPS):
            mixed = jnp.where(grp == g, jnp.dot(w[g], vb[rows, :], preferred_element_type=F32), mixed)
        gm = u[rows, :] * (mixed + bs_ref[...])
        ms = jnp.mean(gm * gm, axis=-1, keepdims=True)
        o_ref[rows, :] = (gm * lax.rsqrt(ms + LN_EPS) * ng_ref[...]).astype(BF16)


OUT_TM = 1024
LOGIT_ROWS = 16


def _out_proj_kernel(att_ref, ssd_ref, gm_ref, h_ref, ag_ref, wa_ref, ws_ref, wg_ref, g_ref, b_ref, o_ref,
                     router=None):
    att = att_ref[...]
    ms = jnp.mean(att * att, axis=-1, keepdims=True)
    att = att * lax.rsqrt(ms + LN_EPS) * ag_ref[...]
    mix = jnp.dot(att.astype(BF16), wa_ref[...], preferred_element_type=F32)
    mix = mix + jnp.dot(ssd_ref[...], ws_ref[...], preferred_element_type=F32)
    mix = mix + jnp.dot(gm_ref[...], wg_ref[...], preferred_element_type=F32)
    h1 = _layer_norm(ALPHA * h_ref[...] + mix, g_ref[...], b_ref[...])
    o_ref[...] = h1
    if router:
        wr_ref, logit_ref = router
        logit_ref[...] = _dot_bf16x3(wr_ref[...], h1.T)


def _out_proj_kernel_routed(att_ref, ssd_ref, gm_ref, h_ref, ag_ref, wa_ref, ws_ref, wg_ref, g_ref, b_ref, wr_ref,
                            o_ref, logit_ref):
    _out_proj_kernel(att_ref, ssd_ref, gm_ref, h_ref, ag_ref, wa_ref, ws_ref, wg_ref, g_ref, b_ref, o_ref,
                     router=(wr_ref, logit_ref))


def _out_proj(att, ssd, gm, h, p, w_router=None):
    m = h.shape[0]
    row = lambda i: (i, 0)
    const = lambda a: pl.BlockSpec(a.shape, lambda i: (0, 0))
    small = [p["att_norm_g"], p["w_out_att"], p["w_out_ssd"], p["w_out_gm"], p["ln1_g"], p["ln1_b"]]
    out_specs = [pl.BlockSpec((OUT_TM, D_MODEL), row)]
    out_shape = [jax.ShapeDtypeStruct((m, D_MODEL), F32)]
    body = _out_proj_kernel
    if w_router is not None:
        small.append(jnp.pad(w_router.T, ((0, LOGIT_ROWS - N_EXPERTS), (0, 0))))
        out_specs.append(pl.BlockSpec((LOGIT_ROWS, OUT_TM), lambda i: (0, i)))
        out_shape.append(jax.ShapeDtypeStruct((LOGIT_ROWS, m), F32))
        body = _out_proj_kernel_routed
    return pl.pallas_call(
        body,
        grid=(m // OUT_TM,),
        in_specs=[pl.BlockSpec((OUT_TM, ATT_W), row), pl.BlockSpec((OUT_TM, SSD_PAD), row),
                  pl.BlockSpec((OUT_TM, GM_W), row), pl.BlockSpec((OUT_TM, D_MODEL), row)]
                 + [const(a) for a in small],
        out_specs=out_specs,
        out_shape=out_shape,
        compiler_params=_params(("parallel",)),
        name="out_proj_ln1",
    )(att, ssd, gm, h, *small)


FFN_TM = 1024


def _swiglu_chunks(acc_ref, xb, w_gate, w_up, w_down, widths):
    dot = functools.partial(jnp.dot, preferred_element_type=F32)
    bounds = [sum(widths[:c]) for c in range(len(widths) + 1)]
    chunk = lambda c: slice(bounds[c], bounds[c + 1])
    up = lambda c: (dot(xb, w_gate(chunk(c))), dot(xb, w_up(chunk(c))))
    pending = up(0)
    for c in range(len(widths)):
        hg, hu = pending
        if c + 1 < len(widths):
            pending = up(c + 1)
        acc_ref[...] += dot((_silu(hg) * hu).astype(BF16), w_down(chunk(c)))


FFN_CHUNK = 512


def _ffn_kernel(x_ref, wg_ref, wu_ref, wd_ref, g_ref, b_ref, o_ref, acc_ref):
    d_ff = wg_ref.shape[-1]
    widths = (FFN_CHUNK,) * (d_ff // FFN_CHUNK) + ((d_ff % FFN_CHUNK,) if d_ff % FFN_CHUNK else ())
    x = x_ref[...]
    acc_ref[...] = ALPHA * x
    _swiglu_chunks(acc_ref, x.astype(BF16), lambda c: wg_ref[:, c], lambda c: wu_ref[:, c],
                   lambda c: wd_ref[c, :], widths)
    o_ref[...] = _layer_norm(acc_ref[...], g_ref[...], b_ref[...])


def _ffn(x, wg, wu, wd, ln_g, ln_b):
    m = x.shape[0]
    resident = lambda a: pl.BlockSpec(a.shape, lambda i: (0, 0), pipeline_mode=pl.Buffered(1))
    return pl.pallas_call(
        _ffn_kernel,
        grid=(m // FFN_TM,),
        in_specs=[pl.BlockSpec((FFN_TM, D_MODEL), lambda i: (i, 0)), resident(wg), resident(wu), resident(wd),
                  pl.BlockSpec((1, D_MODEL), lambda i: (0, 0)),
                  pl.BlockSpec((1, D_MODEL), lambda i: (0, 0))],
        out_specs=pl.BlockSpec((FFN_TM, D_MODEL), lambda i: (i, 0)),
        out_shape=jax.ShapeDtypeStruct((m, D_MODEL), F32),
        scratch_shapes=[pltpu.VMEM((FFN_TM, D_MODEL), F32)],
        compiler_params=_params(("parallel",)),
        name="ffn_ln2",
    )(x, wg, wu, wd, ln_g.reshape(1, D_MODEL), ln_b.reshape(1, D_MODEL))


MOE_TR = 256
MOE_HALF = 512
MOE_UNITS = 3
MOE_TM = MOE_UNITS * MOE_HALF
MOE_TF = 512
MOE_CHUNKS = (256, 256)
SEG_ALIGN = 16
MOE_WIN_SMALL = 128
INFO_GATE = N_EXPERTS


def _expert_cap(m):
    n_tiles = m // MOE_TR
    rows = m + (SEG_ALIGN - 1) * n_tiles + MOE_TR + MOE_HALF
    return -(-rows // MOE_TM) * MOE_TM


def _rows_computed(lens):
    return jnp.maximum(-(-lens // MOE_HALF), 1) * MOE_HALF


def _dispatch_kernel(h_ref, logit_ref, info_ref, offs_ref, lens_ref, small_ref, xs_hbm, run_ref, xbuf_ref, zbuf_ref,
                     sem_ref, zsem_ref, *, n_tiles, cap):
    t = pl.program_id(0)
    tr = MOE_TR
    slot = t % 2

    @pl.when(t == 0)
    def _():
        for e in range(N_EXPERTS):
            run_ref[e] = 0

    xb = h_ref[...].astype(BF16)
    lg = logit_ref[0:N_EXPERTS, :]
    row = lax.broadcasted_iota(jnp.int32, (N_EXPERTS, tr), 0)
    m1 = jnp.max(lg, axis=0, keepdims=True)
    i1 = jnp.min(jnp.where(lg == m1, row, N_EXPERTS), axis=0, keepdims=True)
    rest = jnp.where(row == i1, -jnp.inf, lg)
    m2 = jnp.max(rest, axis=0, keepdims=True)
    i2 = jnp.min(jnp.where(rest == m2, row, N_EXPERTS), axis=0, keepdims=True)
    e2 = jnp.exp(m2 - m1)
    p1 = 1.0 / (1.0 + e2)
    gate_t = jnp.where(row == i1, p1, 0.0) + jnp.where(row == i2, e2 * p1, 0.0)
    sel_t = jnp.where((row == i1) | (row == i2), 1.0, 0.0)
    before = (lax.broadcasted_iota(jnp.int32, (tr, tr), 0) < lax.broadcasted_iota(jnp.int32, (tr, tr), 1))
    rank_t = jnp.dot(sel_t.astype(BF16), jnp.where(before, 1.0, 0.0).astype(BF16),
                     preferred_element_type=F32)
    cnt = jnp.sum(sel_t, axis=1, keepdims=True).astype(jnp.int32)
    info_t = jnp.concatenate([rank_t, gate_t, jnp.zeros((LANES - 2 * N_EXPERTS, tr), F32)], axis=0)
    info_ref[...] = info_t.T

    small = jnp.max(cnt) <= MOE_WIN_SMALL
    small_ref[t] = small.astype(jnp.int32)

    def seg_copy(sl, e, off, rows):
        return pltpu.make_async_copy(xbuf_ref.at[sl, e, pl.ds(0, rows)],
                                     xs_hbm.at[pl.ds(pl.multiple_of(off, SEG_ALIGN), rows)], sem_ref.at[sl, e])

    def by_window(flag, fn):
        pl.when(flag)(functools.partial(fn, MOE_WIN_SMALL))
        pl.when(jnp.logical_not(flag))(functools.partial(fn, tr))

    def compact(rows):
        dst_row = lax.broadcasted_iota(jnp.int32, (rows, tr), 0).astype(F32)
        onehots = [jnp.where((rank_t[e:e + 1, :] == dst_row) & (sel_t[e:e + 1, :] > 0.0), 1.0, 0.0).astype(BF16)
                   for e in range(N_EXPERTS)]
        onehot_all = jnp.concatenate(onehots, axis=0)
        half = D_MODEL // 2
        for c in range(2):
            packed = jnp.dot(onehot_all, xb[:, c * half:(c + 1) * half], preferred_element_type=F32)
            xbuf_ref[slot, :, 0:rows, c * half:(c + 1) * half] = packed.astype(BF16).reshape(N_EXPERTS, rows, half)

    by_window(small, compact)

    def wait_previous(rows):
        for e in range(N_EXPERTS):
            seg_copy(1 - slot, e, 0, rows).wait()

    @pl.when(t > 0)
    def _():
        by_window(small_ref[jnp.maximum(t - 1, 0)] == 1, wait_previous)

    def start_all(rows):
        for e in range(N_EXPERTS):
            seg_copy(slot, e, e * cap + run_ref[e], rows).start()

    by_window(small, start_all)
    for e in range(N_EXPERTS):
        offs_ref[t * N_EXPERTS + e] = e * cap + run_ref[e]
        run_ref[e] = run_ref[e] + jnp.bitwise_and(cnt[e, 0] + (SEG_ALIGN - 1), -SEG_ALIGN)

    def wait_current(rows):
        for e in range(N_EXPERTS):
            seg_copy(slot, e, 0, rows).wait()

    @pl.when(t == n_tiles - 1)
    def _():
        zbuf_ref[...] = jnp.zeros_like(zbuf_ref)
        by_window(small, wait_current)
        tails = []
        for e in range(N_EXPERTS):
            lens_ref[e] = run_ref[e]
            tail = pl.multiple_of(e * cap + run_ref[e], SEG_ALIGN)
            tails.append(pltpu.make_async_copy(zbuf_ref, xs_hbm.at[pl.ds(tail, MOE_HALF)], zsem_ref.at[e]))
            tails[-1].start()
        for cp in tails:
            cp.wait()


def _dispatch(h, logits):
    m = h.shape[0]
    n_tiles = m // MOE_TR
    cap = _expert_cap(m)
    smem = pl.BlockSpec(memory_space=pltpu.SMEM)
    return pl.pallas_call(
        functools.partial(_dispatch_kernel, n_tiles=n_tiles, cap=cap),
        grid=(n_tiles,),
        in_specs=[pl.BlockSpec((MOE_TR, D_MODEL), lambda t: (t, 0)),
                  pl.BlockSpec((LOGIT_ROWS, MOE_TR), lambda t: (0, t))],
        out_specs=[pl.BlockSpec((MOE_TR, LANES), lambda t: (t, 0)), smem, smem, smem,
                   pl.BlockSpec(memory_space=pl.ANY)],
        out_shape=[jax.ShapeDtypeStruct((m, LANES), F32),
                   jax.ShapeDtypeStruct((n_tiles * N_EXPERTS,), jnp.int32),
                   jax.ShapeDtypeStruct((N_EXPERTS,), jnp.int32),
                   jax.ShapeDtypeStruct((n_tiles,), jnp.int32),
                   jax.ShapeDtypeStruct((N_EXPERTS * cap, D_MODEL), BF16)],
        scratch_shapes=[pltpu.SMEM((N_EXPERTS,), jnp.int32),
                        pltpu.VMEM((2, N_EXPERTS, MOE_TR, D_MODEL), BF16),
                        pltpu.VMEM((MOE_HALF, D_MODEL), BF16),
                        pltpu.SemaphoreType.DMA((2, N_EXPERTS)),
                        pltpu.SemaphoreType.DMA((N_EXPERTS,))],
        compiler_params=_params(("arbitrary",)),
        name="moe_dispatch",
    )(h, logits)


def _expert_ffn_kernel(exp_ref, blk_ref, units_ref, x_ref, wg_ref, wu_ref, wd_ref, y_ref, acc_ref):
    w = pl.program_id(0)
    f = pl.program_id(1)

    def tile(rows):
        acc = acc_ref.at[pl.ds(0, rows)]

        @pl.when(f == 0)
        def _():
            acc[...] = jnp.zeros((rows, D_MODEL), F32)

        _swiglu_chunks(acc, x_ref[0:rows, :], lambda c: wg_ref[0, :, c].astype(BF16),
                       lambda c: wu_ref[0, :, c].astype(BF16), lambda c: wd_ref[0, c, :].astype(BF16), MOE_CHUNKS)

        @pl.when(f == pl.num_programs(1) - 1)
        def _():
            y_ref[0:rows, :] = acc[...].astype(BF16)

    for units in range(1, MOE_UNITS + 1):
        pl.when(units_ref[w] == units)(functools.partial(tile, units * MOE_HALF))


def _expert_ffn(xs, exp_w, blk_w, units_w, w_gate, w_up, w_down):
    d_ff = w_gate.shape[-1]
    n_f = d_ff // MOE_TF
    fcol = lambda w, f, exp, blk, units: jnp.where(units[w] != 0, f, n_f - 1)
    grid_spec = pltpu.PrefetchScalarGridSpec(
        num_scalar_prefetch=3,
        grid=(exp_w.shape[0], n_f),
        in_specs=[pl.BlockSpec((MOE_TM, D_MODEL), lambda w, f, exp, blk, valid: (blk[w], 0)),
                  pl.BlockSpec((1, D_MODEL, MOE_TF), lambda w, f, exp, blk, valid: (exp[w], 0, fcol(w, f, exp, blk, valid))),
                  pl.BlockSpec((1, D_MODEL, MOE_TF), lambda w, f, exp, blk, valid: (exp[w], 0, fcol(w, f, exp, blk, valid))),
                  pl.BlockSpec((1, MOE_TF, D_MODEL), lambda w, f, exp, blk, valid: (exp[w], fcol(w, f, exp, blk, valid), 0))],
        out_specs=pl.BlockSpec((MOE_TM, D_MODEL), lambda w, f, exp, blk, valid: (blk[w], 0)),
        scratch_shapes=[pltpu.VMEM((MOE_TM, D_MODEL), F32)])
    return pl.pallas_call(
        _expert_ffn_kernel,
        grid_spec=grid_spec,
        out_shape=jax.ShapeDtypeStruct(xs.shape, BF16),
        compiler_params=_params(("arbitrary", "arbitrary")),
        name="moe_expert_ffn",
    )(exp_w, blk_w, units_w, xs, w_gate, w_up, w_down)


def _combine_kernel(starts_ref, shifts_ref, small_ref, info_ref, h_ref, g_ref, b_ref, ys_hbm, o_ref, ybuf_ref,
                    sem_ref, *, n_tiles):
    t = pl.program_id(0)
    tr = MOE_TR
    slot = t % 2

    def seg_copy(tile, sl, e, rows):
        off = pl.multiple_of(starts_ref[tile * N_EXPERTS + e], SEG_ALIGN)
        return pltpu.make_async_copy(ys_hbm.at[pl.ds(off, rows)], ybuf_ref.at[sl, e, pl.ds(0, rows)],
                                     sem_ref.at[sl, e])

    def by_window(tile, fn):
        flag = small_ref[tile] == 1
        pl.when(flag)(functools.partial(fn, MOE_WIN_SMALL))
        pl.when(jnp.logical_not(flag))(functools.partial(fn, tr))

    def fetch(tile, sl):
        def start(rows):
            for e in range(N_EXPERTS):
                seg_copy(tile, sl, e, rows).start()
        by_window(tile, start)

    @pl.when(t == 0)
    def _():
        fetch(0, 0)

    @pl.when(t + 1 < n_tiles)
    def _():
        fetch(jnp.minimum(t + 1, n_tiles - 1), 1 - slot)

    def gather(rows):
        for e in range(N_EXPERTS):
            seg_copy(t, slot, e, rows).wait()
        info = info_ref[...]
        src_row = lax.broadcasted_iota(jnp.int32, (tr, rows), 1).astype(F32)
        acc = ALPHA * h_ref[...]
        for e in range(N_EXPERTS):
            rank_c = info[:, e:e + 1] + shifts_ref[t * N_EXPERTS + e].astype(F32)
            gate_c = info[:, INFO_GATE + e:INFO_GATE + e + 1]
            onehot = jnp.where((rank_c == src_row) & (gate_c != 0.0), 1.0, 0.0).astype(BF16)
            acc = acc + gate_c * jnp.dot(onehot, ybuf_ref[slot, e, 0:rows, :], preferred_element_type=F32)
        o_ref[...] = _layer_norm(acc, g_ref[...], b_ref[...])

    by_window(t, gather)


def _combine(starts, shifts, small, info, h, ys, ln_g, ln_b):
    m = h.shape[0]
    n_tiles = m // MOE_TR
    grid_spec = pltpu.PrefetchScalarGridSpec(
        num_scalar_prefetch=3,
        grid=(n_tiles,),
        in_specs=[pl.BlockSpec((MOE_TR, LANES), lambda t, *_: (t, 0)),
                  pl.BlockSpec((MOE_TR, D_MODEL), lambda t, *_: (t, 0)),
                  pl.BlockSpec((1, D_MODEL), lambda t, *_: (0, 0)),
                  pl.BlockSpec((1, D_MODEL), lambda t, *_: (0, 0)),
                  pl.BlockSpec(memory_space=pl.ANY)],
        out_specs=pl.BlockSpec((MOE_TR, D_MODEL), lambda t, *_: (t, 0)),
        scratch_shapes=[pltpu.VMEM((2, N_EXPERTS, MOE_TR, D_MODEL), BF16),
                        pltpu.SemaphoreType.DMA((2, N_EXPERTS))])
    return pl.pallas_call(
        functools.partial(_combine_kernel, n_tiles=n_tiles),
        grid_spec=grid_spec,
        out_shape=jax.ShapeDtypeStruct((m, D_MODEL), F32),
        compiler_params=_params(("arbitrary",)),
        name="moe_combine_ln2",
    )(starts, shifts, small, info, h, ln_g.reshape(1, D_MODEL), ln_b.reshape(1, D_MODEL), ys)


def _work_list(lens, m):
    cap = _expert_cap(m)
    n_tiles = m // MOE_TR
    max_rows = 2 * m + (SEG_ALIGN - 1) * min(N_EXPERTS * n_tiles, 2 * m)
    w_max = -(-(max_rows + N_EXPERTS * MOE_HALF) // MOE_TM) + N_EXPERTS
    rows_e = _rows_computed(lens)
    tiles_e = -(-rows_e // MOE_TM)
    ends = jnp.cumsum(tiles_e)
    w = jnp.arange(w_max, dtype=jnp.int32)
    wc = jnp.minimum(w, ends[-1] - 1)
    exp_w = jnp.sum((wc[:, None] >= ends[None, :]).astype(jnp.int32), axis=1)
    tile_w = wc - (ends - tiles_e)[exp_w]
    blk_w = exp_w * (cap // MOE_TM) + tile_w
    units_w = jnp.minimum((rows_e[exp_w] - tile_w * MOE_TM) // MOE_HALF, MOE_UNITS)
    units_w = jnp.where(w < ends[-1], units_w, 0)
    return exp_w.astype(jnp.int32), blk_w.astype(jnp.int32), units_w.astype(jnp.int32)


def _read_windows(offs, small, lens, m):
    cap = _expert_cap(m)
    region = jnp.arange(N_EXPERTS, dtype=jnp.int32) * cap
    rel = offs.reshape(-1, N_EXPERTS) - region
    width = jnp.where(small == 1, MOE_WIN_SMALL, MOE_TR)[:, None]
    start = jnp.minimum(rel, _rows_computed(lens)[None, :] - width)
    return (start + region).reshape(-1).astype(jnp.int32), (rel - start).reshape(-1).astype(jnp.int32)


def _moe_layer(h, logits, w_gate, w_up, w_down, ln_g, ln_b):
    m = h.shape[0]
    info, offs, lens, small, xs = _dispatch(h, logits)
    exp_w, blk_w, units_w = _work_list(lens, m)
    ys = _expert_ffn(xs, exp_w, blk_w, units_w, w_gate, w_up, w_down)
    starts, shifts = _read_windows(offs, small, lens, m)
    return _combine(starts, shifts, small, info, h, ys, ln_g, ln_b)


def _pad_groups(a):
    lead = a.shape[:-1]
    a = a.reshape(*lead, SSD_GROUPS, GROUP_W)
    a = jnp.pad(a, [(0, 0)] * len(lead) + [(0, 0), (0, GROUP_PAD - GROUP_W)])
    return a.reshape(*lead, SSD_PAD)


def _head_lanes(v):
    return jnp.pad(v, (0, LANES - SSD_HEADS)).reshape(1, LANES)


D_IN = 3 * ATT_W + SSD_W + (SSD_W + 2 * SSD_GROUPS * SSD_STATE) + SSD_HEADS + 2 * GM_W
PERM_ROWS = 256


def _permute_w_kernel(w_ref, o_ref):
    src_z = 3 * ATT_W
    src_xs = src_z + SSD_W
    src_b = src_xs + SSD_W
    src_c = src_b + SSD_GROUPS * SSD_STATE
    src_dt = src_c + SSD_GROUPS * SSD_STATE
    src_gu = src_dt + SSD_HEADS
    src_gv = src_gu + GM_W
    o_ref[...] = jnp.zeros_like(o_ref)

    def put(dst, src, width):
        o_ref[:, dst:dst + width] = w_ref[0, :, src:src + width].astype(BF16)

    for g in range(SSD_GROUPS):
        put(COL_Z + g * GROUP_PAD, src_z + g * GROUP_W, GROUP_W)
        put(COL_XS + g * GROUP_PAD, src_xs + g * GROUP_W, GROUP_W)
    put(COL_B, src_b, SSD_GROUPS * SSD_STATE)
    put(COL_C, src_c, SSD_GROUPS * SSD_STATE)
    put(COL_GU, src_gu, GM_W)
    put(COL_GV, src_gv, GM_W)
    put(COL_Q, 0, ATT_W)
    put(COL_K, ATT_W, ATT_W)
    put(COL_V, 2 * ATT_W, ATT_W)
    put(COL_DT, src_dt, SSD_HEADS)


def _permute_w_in(w_in, layer):
    return pl.pallas_call(
        _permute_w_kernel,
        grid=(D_MODEL // PERM_ROWS,),
        in_specs=[pl.BlockSpec((1, PERM_ROWS, D_IN), lambda i: (layer, i, 0))],
        out_specs=pl.BlockSpec((PERM_ROWS, D_W), lambda i: (i, 0)),
        out_shape=jax.ShapeDtypeStruct((D_MODEL, D_W), BF16),
        compiler_params=_params(("parallel",)),
        name="permute_w_in",
    )(w_in)


def _layer_params(i, w_in, conv_w, conv_b, dt_bias, a_log, d_skip, ssd_norm_g, att_norm_g, gm_ln_g, gm_ln_b,
                  gm_w_s, gm_b_s, gm_norm_g, w_out, ln1_g, ln1_b):
    cw, cb = conv_w[i], conv_b[i]
    nbc = SSD_GROUPS * SSD_STATE
    wo = w_out[i]
    return {
        "w_in": _permute_w_in(w_in, i),
        "conv_w_xs": _pad_groups(cw[:, :SSD_W]),
        "conv_w_b": cw[:, SSD_W:SSD_W + nbc],
        "conv_w_c": cw[:, SSD_W + nbc:],
        "conv_b_xs": _pad_groups(cb[:SSD_W]).reshape(1, SSD_PAD),
        "conv_b_b": cb[SSD_W:SSD_W + nbc].reshape(1, nbc),
        "conv_b_c": cb[SSD_W + nbc:].reshape(1, nbc),
        "dt_bias": _head_lanes(dt_bias[i]),
        "a_log": _head_lanes(a_log[i]),
        "d_skip": _pad_groups(jnp.repeat(d_skip[i], SSD_HEAD_DIM)).reshape(1, SSD_PAD),
        "ssd_norm_g": _pad_groups(ssd_norm_g[i]).reshape(1, SSD_PAD),
        "att_norm_g": att_norm_g[i].reshape(1, ATT_W),
        "gm_ln_g": gm_ln_g[i].reshape(1, GM_W),
        "gm_ln_b": gm_ln_b[i].reshape(1, GM_W),
        "gm_w_s": gm_w_s[i],
        "gm_b_s": jnp.repeat(gm_b_s[i].T, GM_W // GM_GROUPS, axis=1),
        "gm_norm_g": gm_norm_g[i].reshape(1, GM_W),
        "w_out_att": wo[:ATT_W].astype(BF16),
        "w_out_ssd": _pad_groups(wo[ATT_W:ATT_W + SSD_W].T).T.astype(BF16),
        "w_out_gm": wo[ATT_W + SSD_W:].astype(BF16),
        "ln1_g": ln1_g[i].reshape(1, D_MODEL),
        "ln1_b": ln1_b[i].reshape(1, D_MODEL),
    }


def _rope_tables(positions):
    inv = ROPE_THETA ** (-jnp.arange(0, ROPE_DIM, 2, dtype=F32) / ROPE_DIM)
    d = jnp.arange(LANES) % HEAD_DIM
    inv_lane = jnp.where(d < ROPE_DIM, inv[d % ROPE_HALF], 0.0)
    ang = positions.astype(F32)[..., None] * inv_lane
    cos, sin = jnp.cos(ang), jnp.sin(ang)
    return cos, jnp.where(d < ROPE_HALF, -sin, 0.0), jnp.where(d >= ROPE_HALF, sin, 0.0)


def kernel(x, positions, ln_in_g, ln_in_b, w_in, conv_w, conv_b, dt_bias, a_log, d_skip, ssd_norm_g, att_norm_g, gm_ln_g, gm_ln_b, gm_w_s, gm_b_s, gm_norm_g, w_out, ln1_g, ln1_b, ln2_g, ln2_b, ffn_w_gate, ffn_w_up, ffn_w_down, router_w, moe_w_gate, moe_w_up, moe_w_down):
    bsz, seq, _ = x.shape
    m = bsz * seq
    rope_c, rope_sa, rope_sb = _rope_tables(positions)
    h = x.reshape(m, D_MODEL)
    for i in range(DEPTH):
        p = _layer_params(i, w_in, conv_w, conv_b, dt_bias, a_log, d_skip, ssd_norm_g, att_norm_g, gm_ln_g,
                          gm_ln_b, gm_w_s, gm_b_s, gm_norm_g, w_out, ln1_g, ln1_b)
        if i == 0:
            proj, gm, h = _in_proj(h, p["w_in"], p, ln=(ln_in_g, ln_in_b))
        else:
            proj, gm = _in_proj(h, p["w_in"], p)
        proj3 = proj.reshape(bsz, seq, D_PROJ)
        att = _attention(proj3, rope_c, rope_sa, rope_sb).reshape(m, ATT_W)
        ssd = _ssd(proj3, p).reshape(m, SSD_PAD)
        j = i // 2
        if i % 2 == 0:
            (h,) = _out_proj(att, ssd, gm, h, p)
            h = _ffn(h, ffn_w_gate[j].astype(BF16), ffn_w_up[j].astype(BF16), ffn_w_down[j].astype(BF16),
                     ln2_g[i], ln2_b[i])
        else:
            h, logits = _out_proj(att, ssd, gm, h, p, w_router=router_w[j])
            h = _moe_layer(h, logits, moe_w_gate[j], moe_w_up[j], moe_w_down[j], ln2_g[i], ln2_b[i])
    return h.reshape(bsz, seq, D_MODEL)
```

```python
import functools

import jax
import jax.numpy as jnp
from jax import lax
from jax.experimental import pallas as pl
from jax.experimental.pallas import tpu as pltpu

F32 = jnp.float32
BF16 = jnp.bfloat16
HIGHEST = lax.Precision.HIGHEST

D_MODEL = 1024
ATT_HEADS = 6
HEAD_DIM = 64
ATT_W = ATT_HEADS * HEAD_DIM
ROPE_DIM = HEAD_DIM // 4
ROPE_HALF = ROPE_DIM // 2
ROPE_THETA = 500000.0
MOBA_BLOCK = 256
MOBA_TOPK = 3
SSD_HEADS = 6
SSD_HEAD_DIM = 64
SSD_W = SSD_HEADS * SSD_HEAD_DIM
SSD_GROUPS = 2
SSD_STATE = 128
SSD_CONV = 4
SSD_CHUNK = 128
GM_GROUPS = 4
GM_W = 256
GM_CHUNK = 128
N_EXPERTS = 8
DEPTH = 2
ALPHA = (2.0 * DEPTH) ** 0.25
NEG = -1e30
LN_EPS = 1e-5

LANES = 128
GROUP_W = SSD_W // SSD_GROUPS
GROUP_PAD = 256
SSD_PAD = SSD_GROUPS * GROUP_PAD
HEADS_PER_GROUP = SSD_HEADS // SSD_GROUPS

COL_Z = 0
COL_XS = 512
COL_B = 1024
COL_C = 1280
COL_Q = 1536
COL_K = 1920
COL_V = 2304
COL_DT = 2688
D_PROJ = 2816
COL_GU = 2816
COL_GV = 3072
D_W = 3328

VMEM_LIMIT = 56 * 1024 * 1024


def _params(sem, vmem=VMEM_LIMIT):
    return pltpu.CompilerParams(dimension_semantics=sem, vmem_limit_bytes=vmem)


def _layer_norm(x, g, b):
    mu = jnp.mean(x, axis=-1, keepdims=True)
    xc = x - mu
    var = jnp.mean(xc * xc, axis=-1, keepdims=True)
    return xc * lax.rsqrt(var + LN_EPS) * g + b


def _silu(x):
    half = 0.5 * x
    return half + half * jnp.tanh(half)


def _split_bf16(x):
    hi = x.astype(BF16)
    return hi, (x - hi.astype(F32)).astype(BF16)


def _dot_bf16x3(a, b):
    a_hi, a_lo = _split_bf16(a)
    b_hi, b_lo = _split_bf16(b)
    dot = functools.partial(jnp.dot, preferred_element_type=F32)
    return dot(a_hi, b_hi) + (dot(a_lo, b_hi) + dot(a_hi, b_lo))


def _gelu_tanh(x):
    return 0.5 * x * (1.0 + jnp.tanh(0.7978845608028654 * (x + 0.044715 * x * x * x)))


PROJ_TM = 512
PROJ_CH = 256
PROJ_TAIL = 4


def _in_proj_kernel(*refs, apply_ln):
    if apply_ln:
        x_ref, g_ref, b_ref, w_ref, lng_ref, lnb_ref, ws_ref, bs_ref, ng_ref, proj_ref, gm_ref, h_ref = refs
        x = _layer_norm(x_ref[...], g_ref[...], b_ref[...])
        h_ref[...] = x
    else:
        x_ref, w_ref, lng_ref, lnb_ref, ws_ref, bs_ref, ng_ref, proj_ref, gm_ref = refs
        x = x_ref[...]
    xb = x.astype(BF16)

    def project(j):
        cols = slice(j * PROJ_CH, (j + 1) * PROJ_CH)
        proj_ref[:, cols] = jnp.dot(xb, w_ref[:, cols], preferred_element_type=F32)

    gu = jnp.dot(xb, w_ref[:, COL_GU:COL_GU + GM_W], preferred_element_type=F32)
    gv = jnp.dot(xb, w_ref[:, COL_GV:COL_GV + GM_W], preferred_element_type=F32)
    u, vb = _gmlp_activate(gu, gv, lng_ref, lnb_ref)
    n_chunks = D_PROJ // PROJ_CH
    for j in range(n_chunks - PROJ_TAIL):
        project(j)
    _gmlp_mix(u, vb, ws_ref, bs_ref, ng_ref, gm_ref)
    for j in range(n_chunks - PROJ_TAIL, n_chunks):
        project(j)


def _in_proj(x, w, p, ln=None):
    m = x.shape[0]
    row = lambda i: (i, 0)
    const = lambda i: (0, 0)
    whole = lambda a: pl.BlockSpec(a.shape, lambda i: (0,) * a.ndim)
    in_specs = [pl.BlockSpec((PROJ_TM, D_MODEL), row)]
    args = [x]
    out_shape = [jax.ShapeDtypeStruct((m, D_PROJ), F32), jax.ShapeDtypeStruct((m, GM_W), BF16)]
    out_specs = [pl.BlockSpec((PROJ_TM, D_PROJ), row), pl.BlockSpec((PROJ_TM, GM_W), row)]
    if ln is not None:
        in_specs += [pl.BlockSpec((1, D_MODEL), const), pl.BlockSpec((1, D_MODEL), const)]
        args += [ln[0].reshape(1, D_MODEL), ln[1].reshape(1, D_MODEL)]
        out_shape.append(jax.ShapeDtypeStruct((m, D_MODEL), F32))
        out_specs.append(pl.BlockSpec((PROJ_TM, D_MODEL), row))
    in_specs.append(pl.BlockSpec((D_MODEL, D_W), const))
    args.append(w)
    gm_small = [p["gm_ln_g"], p["gm_ln_b"], p["gm_w_s"], p["gm_b_s"], p["gm_norm_g"]]
    in_specs += [whole(a) for a in gm_small]
    args += gm_small
    return pl.pallas_call(
        functools.partial(_in_proj_kernel, apply_ln=ln is not None),
        grid=(m // PROJ_TM,),
        in_specs=in_specs,
        out_specs=out_specs,
        out_shape=out_shape,
        compiler_params=_params(("parallel",)),
        name="in_proj",
    )(*args)


VT_ROWS = HEAD_DIM + 16
LOG2E = 1.4426950408889634


ATT_VALUE_LAG = 6


def _attn_kernel(q_ref, k_ref, v_ref, c_ref, sa_ref, sb_ref, o_ref, kb_ref, vt_ref, *, seq):
    nb = seq // MOBA_BLOCK
    tq = MOBA_BLOCK
    lane = lax.broadcasted_iota(jnp.int32, (1, LANES), 1)

    def rope(x, rows):
        return (x * c_ref[0, rows, :] + pltpu.roll(x, LANES - ROPE_HALF, 1) * sa_ref[0, rows, :]
                + pltpu.roll(x, ROPE_HALF, 1) * sb_ref[0, rows, :])

    k = rope(k_ref[0], slice(None))
    kb_ref[...] = k.astype(BF16)
    k_mean = jnp.mean(k.reshape(nb, MOBA_BLOCK, LANES), axis=1)
    v_t = v_ref[0].T
    ones_rows = jnp.where(lax.broadcasted_iota(jnp.int32, (VT_ROWS - HEAD_DIM, seq), 0) == 0, 1.0, 0.0)
    for hh in range(2):
        vt_ref[hh] = jnp.concatenate([v_t[hh * HEAD_DIM:(hh + 1) * HEAD_DIM], ones_rows], axis=0).astype(BF16)

    qscale = HEAD_DIM ** -0.5 * LOG2E
    causal_t = (lax.broadcasted_iota(jnp.int32, (tq, tq), 0) <= lax.broadcasted_iota(jnp.int32, (tq, tq), 1))
    blk_row = lax.broadcasted_iota(jnp.int32, (nb, tq), 0)

    chain_q = {}
    chain_bias = {}
    chain_m = {}
    chain_o = {}
    q_tiles = {}

    def start_chain(j, hh):
        if hh == 0:
            rows = slice(j * tq, (j + 1) * tq)
            q_tiles[j] = rope(q_ref[0, rows, :], rows) * qscale
        q = q_tiles[j] if hh == 0 else q_tiles.pop(j)
        head_lanes = (lane < HEAD_DIM) if hh == 0 else (lane >= HEAD_DIM)
        qh_t = jnp.where(head_lanes, q, 0.0).T
        chain_q[(j, hh)] = qh_t.astype(BF16)
        chain_bias[(j, hh)] = None
        if j > MOBA_TOPK:
            gate_t = _dot_bf16x3(k_mean, qh_t)
            rank = jnp.zeros((nb, tq), F32)
            for m in range(j):
                g_m = gate_t[m:m + 1, :]
                beats = (g_m > gate_t) | ((g_m == gate_t) & (blk_row > m))
                rank = rank + jnp.where(beats & (blk_row != m), 1.0, 0.0)
            bias = jnp.where((blk_row < j) & (rank >= MOBA_TOPK), NEG, 0.0)
            chain_bias[(j, hh)] = [bias[n:n + 1, :] for n in range(j)]

    def scores(j, hh, n):
        return jnp.dot(kb_ref[n * tq:(n + 1) * tq, :], chain_q[(j, hh)], preferred_element_type=F32)

    def probs(j, hh, n, s_t):
        key = (j, hh)
        if n == j:
            s_t = jnp.where(causal_t, s_t, NEG)
            m_new = jnp.max(s_t, axis=0, keepdims=True)
            alpha = None
            shift = m_new
        else:
            bias_n = 0.0 if chain_bias[key] is None else chain_bias[key][n]
            m_new = jnp.maximum(chain_m[key], jnp.max(s_t, axis=0, keepdims=True) + bias_n)
            alpha = jnp.exp2(chain_m[key] - m_new)
            shift = m_new - bias_n
        chain_m[key] = m_new
        return jnp.exp2(s_t - shift).astype(BF16), alpha

    def values(j, hh, n, p_t, alpha):
        key = (j, hh)
        part = jnp.dot(vt_ref[hh, :, n * tq:(n + 1) * tq], p_t, preferred_element_type=F32)
        chain_o[key] = part if alpha is None else chain_o[key] * alpha + part

    def finish(j):
        halves = [chain_o.pop((j, hh)) for hh in range(2)]
        out_t = jnp.concatenate([o[0:HEAD_DIM] / o[HEAD_DIM:HEAD_DIM + 1] for o in halves], axis=0)
        o_ref[0, j * tq:(j + 1) * tq, :] = out_t.T

    value_lag = ATT_VALUE_LAG
    units = []
    for j in range(nb):
        for n in [j] + list(range(j)):
            units += [(j, 0, n), (j, 1, n)]
    s_state = {}
    p_state = {}
    for step in range(len(units) + value_lag):
        if step >= value_lag:
            j, hh, n = unit = units[step - value_lag]
            values(j, hh, n, *p_state.pop(unit))
            last_block = n == (j - 1 if j > 0 else 0)
            if hh == 1 and last_block:
                finish(j)
        if step < len(units):
            j, hh, n = unit = units[step]
            if n == j:
                start_chain(j, hh)
            s_state[unit] = scores(j, hh, n)
        if 1 <= step <= len(units):
            j, hh, n = unit = units[step - 1]
            p_state[unit] = probs(j, hh, n, s_state.pop(unit))


def _attention(proj3, rope_c, rope_sa, rope_sb):
    bsz, seq, _ = proj3.shape
    blk = (1, seq, LANES)
    col = lambda base: (lambda b, p: (b, 0, base // LANES + p))
    tab = lambda b, p: (b, 0, 0)
    return pl.pallas_call(
        functools.partial(_attn_kernel, seq=seq),
        grid=(bsz, ATT_W // LANES),
        in_specs=[pl.BlockSpec(blk, col(COL_Q)), pl.BlockSpec(blk, col(COL_K)), pl.BlockSpec(blk, col(COL_V)),
                  pl.BlockSpec(blk, tab), pl.BlockSpec(blk, tab), pl.BlockSpec(blk, tab)],
        out_specs=pl.BlockSpec(blk, lambda b, p: (b, 0, p)),
        out_shape=jax.ShapeDtypeStruct((bsz, seq, ATT_W), F32),
        scratch_shapes=[pltpu.VMEM((seq, LANES), BF16), pltpu.VMEM((2, VT_ROWS, seq), BF16)],
        compiler_params=_params(("parallel", "arbitrary")),
        name="moba_attention",
    )(proj3, proj3, proj3, rope_c, rope_sa, rope_sb)


SSD_UNROLL = 4
assert HEADS_PER_GROUP == 3 and GROUP_PAD == 2 * LANES and SSD_HEAD_DIM * 2 == LANES


def _expand_heads(small):
    low = lax.broadcasted_iota(jnp.int32, (1, LANES), 1) < SSD_HEAD_DIM
    slabs = []
    for g in range(SSD_GROUPS):
        h0 = g * HEADS_PER_GROUP
        slabs.append(jnp.where(low, small[:, h0:h0 + 1], small[:, h0 + 1:h0 + 2]))
        slabs.append(jnp.where(low, small[:, h0 + 2:h0 + 3], 0.0))
    return jnp.concatenate(slabs, axis=1)


def _ssd_kernel(z_ref, xs_ref, b_ref, c_ref, dt_ref, cwx_ref, cwb_ref, cwc_ref, cbx_ref, cbb_ref, cbc_ref,
                dtb_ref, alog_ref, dskip_ref, ng_ref, o_ref, state_ref, *, seq):
    q = SSD_CHUNK
    n = SSD_STATE
    state_ref[...] = jnp.zeros_like(state_ref)
    row_i = lax.broadcasted_iota(jnp.int32, (q, q), 0)
    col_i = lax.broadcasted_iota(jnp.int32, (q, q), 1)
    lower = row_i >= col_i
    tril = jnp.where(lower, 1.0, 0.0)
    slab = lax.broadcasted_iota(jnp.int32, (1, GROUP_PAD), 1) // SSD_HEAD_DIM
    a_neg = -jnp.exp(alog_ref[...])

    def conv_silu(ref, w_ref, bias_ref, c):
        start = pl.multiple_of(c * q, q)
        cur = ref[0, pl.ds(start, q), :]
        prev_start = pl.multiple_of(jnp.maximum(c * q - 8, 0), 8)
        prev = jnp.where(c > 0, ref[0, pl.ds(prev_start, 8), :], 0.0)
        full = jnp.concatenate([prev, cur], axis=0)
        y = bias_ref[...] + w_ref[SSD_CONV - 1:SSD_CONV, :] * cur
        for i in range(SSD_CONV - 1):
            shift = SSD_CONV - 1 - i
            y = y + w_ref[i:i + 1, :] * pltpu.roll(full, shift, 0)[8:, :]
        return _silu(y)

    def chunk(c, carry):
        start = pl.multiple_of(c * q, q)
        xs = conv_silu(xs_ref, cwx_ref, cbx_ref, c)
        bm = conv_silu(b_ref, cwb_ref, cbb_ref, c)
        cm = conv_silu(c_ref, cwc_ref, cbc_ref, c)
        x = dt_ref[0, pl.ds(start, q), :] + dtb_ref[...]
        dt = jnp.maximum(x, 0.0) + jnp.log(1.0 + jnp.exp(-jnp.abs(x)))
        a = dt * a_neg
        a_cum = jnp.dot(tril, a, precision=HIGHEST, preferred_element_type=F32)
        a_cum_t = a_cum.T
        dt_e = _expand_heads(dt)
        acum_e = _expand_heads(a_cum)
        alast_e = acum_e[q - 1:q, :]
        xdt = xs * dt_e
        decay_e = jnp.exp(alast_e - acum_e)
        y_parts = []
        for g in range(SSD_GROUPS):
            gl = slice(g * GROUP_PAD, (g + 1) * GROUP_PAD)
            nl = slice(g * n, (g + 1) * n)
            b_g = bm[:, nl].astype(BF16)
            b_gt = bm[:, nl].T.astype(BF16)
            c_g = cm[:, nl].astype(BF16)
            cb = lax.dot_general(c_g, b_g, (((1,), (1,)), ((), ())), preferred_element_type=F32)
            xdt_g = xdt[:, gl]
            xdt_b = xdt_g.astype(BF16)
            y_g = jnp.zeros((q, GROUP_PAD), F32)
            for r in range(HEADS_PER_GROUP):
                h = g * HEADS_PER_GROUP + r
                seg = a_cum[:, h:h + 1] - a_cum_t[h:h + 1, :]
                l_h = jnp.where(lower, jnp.exp(jnp.minimum(seg, 0.0)), 0.0)
                y_h = jnp.dot((cb * l_h).astype(BF16), xdt_b, preferred_element_type=F32)
                y_g = jnp.where(slab == r, y_h, y_g)
            prev = state_ref[g]
            y_off = jnp.dot(c_g, prev.astype(BF16), preferred_element_type=F32) * jnp.exp(acum_e[:, gl])
            new = jnp.dot(b_gt, (xdt_g * decay_e[:, gl]).astype(BF16), preferred_element_type=F32)
            state_ref[g] = prev * jnp.exp(alast_e[:, gl]) + new
            y_parts.append(y_g + y_off)
        y = jnp.concatenate(y_parts, axis=1) + dskip_ref[...] * xs
        yz = y * _silu(z_ref[0, pl.ds(start, q), :])
        outs = []
        for g in range(SSD_GROUPS):
            yg = yz[:, g * GROUP_PAD:(g + 1) * GROUP_PAD]
            ms = jnp.sum(yg * yg, axis=-1, keepdims=True) * (1.0 / GROUP_W)
            outs.append(yg * lax.rsqrt(ms + LN_EPS))
        o_ref[0, pl.ds(start, q), :] = (jnp.concatenate(outs, axis=1) * ng_ref[...]).astype(BF16)
        return carry

    lax.fori_loop(0, seq // q, chunk, 0, unroll=SSD_UNROLL)


def _ssd(proj3, p):
    bsz, seq, _ = proj3.shape
    slab = lambda width, base: pl.BlockSpec((1, seq, width), lambda b: (b, 0, base // width))
    const2 = lambda a: pl.BlockSpec(a.shape, lambda b: (0, 0))
    small = [p["conv_w_xs"], p["conv_w_b"], p["conv_w_c"], p["conv_b_xs"], p["conv_b_b"], p["conv_b_c"],
             p["dt_bias"], p["a_log"], p["d_skip"], p["ssd_norm_g"]]
    return pl.pallas_call(
        functools.partial(_ssd_kernel, seq=seq),
        grid=(bsz,),
        in_specs=[slab(SSD_PAD, COL_Z), slab(SSD_PAD, COL_XS), slab(2 * SSD_STATE, COL_B),
                  slab(2 * SSD_STATE, COL_C), slab(LANES, COL_DT)] + [const2(a) for a in small],
        out_specs=pl.BlockSpec((1, seq, SSD_PAD), lambda b: (b, 0, 0)),
        out_shape=jax.ShapeDtypeStruct((bsz, seq, SSD_PAD), BF16),
        scratch_shapes=[pltpu.VMEM((SSD_GROUPS, SSD_STATE, GROUP_PAD), F32)],
        compiler_params=_params(("parallel",)),
        name="ssd",
    )(proj3, proj3, proj3, proj3, proj3, *small)


def _gmlp_activate(gu, gv, lng_ref, lnb_ref):
    return _gelu_tanh(gu), _layer_norm(_gelu_tanh(gv), lng_ref[...], lnb_ref[...]).astype(BF16)


def _gmlp_mix(u, vb, ws_ref, bs_ref, ng_ref, o_ref):
    q = GM_CHUNK
    row_i = lax.broadcasted_iota(jnp.int32, (q, q), 0)
    col_i = lax.broadcasted_iota(jnp.int32, (q, q), 1)
    lower = row_i >= col_i
    grp = lax.broadcasted_iota(jnp.int32, (1, GM_W), 1) // (GM_W // GM_GROUPS)
    w = [jnp.where(lower, ws_ref[g], 0.0).astype(BF16) for g in range(GM_GROUPS)]
    for c in range(u.shape[0] // q):
        rows = slice(c * q, (c + 1) * q)
        mixed = jnp.zeros((q, GM_W), F32)
        for g in range(GM_GROUPS):
            mixed = jnp.where(grp == g, jnp.dot(w[g], vb[rows, :], preferred_element_type=F32), mixed)
        gm = u[rows, :] * (mixed + bs_ref[...])
        ms = jnp.mean(gm * gm, axis=-1, keepdims=True)
        o_ref[rows, :] = (gm * lax.rsqrt(ms + LN_EPS) * ng_ref[...]).astype(BF16)


OUT_TM = 1024
LOGIT_ROWS = 16


def _mix_ln1(att_ref, ssd_ref, gm_ref, h_ref, ag_ref, wa_ref, ws_ref, wg_ref, g_ref, b_ref):
    att = att_ref[...]
    ms = jnp.mean(att * att, axis=-1, keepdims=True)
    att = att * lax.rsqrt(ms + LN_EPS) * ag_ref[...]
    mix = jnp.dot(att.astype(BF16), wa_ref[...], preferred_element_type=F32)
    mix = mix + jnp.dot(ssd_ref[...], ws_ref[...], preferred_element_type=F32)
    mix = mix + jnp.dot(gm_ref[...], wg_ref[...], preferred_element_type=F32)
    return _layer_norm(ALPHA * h_ref[...] + mix, g_ref[...], b_ref[...])


def _out_proj_kernel(att_ref, ssd_ref, gm_ref, h_ref, ag_ref, wa_ref, ws_ref, wg_ref, g_ref, b_ref, o_ref,
                     router=None):
    h1 = _mix_ln1(att_ref, ssd_ref, gm_ref, h_ref, ag_ref, wa_ref, ws_ref, wg_ref, g_ref, b_ref)
    o_ref[...] = h1
    if router:
        wr_ref, logit_ref = router
        logit_ref[...] = _dot_bf16x3(wr_ref[...], h1.T)


def _out_proj_kernel_routed(att_ref, ssd_ref, gm_ref, h_ref, ag_ref, wa_ref, ws_ref, wg_ref, g_ref, b_ref, wr_ref,
                            o_ref, logit_ref):
    _out_proj_kernel(att_ref, ssd_ref, gm_ref, h_ref, ag_ref, wa_ref, ws_ref, wg_ref, g_ref, b_ref, o_ref,
                     router=(wr_ref, logit_ref))


def _out_proj(att, ssd, gm, h, p, w_router=None):
    m = h.shape[0]
    row = lambda i: (i, 0)
    const = lambda a: pl.BlockSpec(a.shape, lambda i: (0, 0))
    small = [p["att_norm_g"], p["w_out_att"], p["w_out_ssd"], p["w_out_gm"], p["ln1_g"], p["ln1_b"]]
    out_specs = [pl.BlockSpec((OUT_TM, D_MODEL), row)]
    out_shape = [jax.ShapeDtypeStruct((m, D_MODEL), F32)]
    body = _out_proj_kernel
    if w_router is not None:
        small.append(jnp.pad(w_router.T, ((0, LOGIT_ROWS - N_EXPERTS), (0, 0))))
        out_specs.append(pl.BlockSpec((LOGIT_ROWS, OUT_TM), lambda i: (0, i)))
        out_shape.append(jax.ShapeDtypeStruct((LOGIT_ROWS, m), F32))
        body = _out_proj_kernel_routed
    return pl.pallas_call(
        body,
        grid=(m // OUT_TM,),
        in_specs=[pl.BlockSpec((OUT_TM, ATT_W), row), pl.BlockSpec((OUT_TM, SSD_PAD), row),
                  pl.BlockSpec((OUT_TM, GM_W), row), pl.BlockSpec((OUT_TM, D_MODEL), row)]
                 + [const(a) for a in small],
        out_specs=out_specs,
        out_shape=out_shape,
        compiler_params=_params(("parallel",)),
        name="out_proj_ln1",
    )(att, ssd, gm, h, *small)


FFN_TM = 512


def _swiglu_chunks(acc_ref, xb, w_gate, w_up, w_down, widths):
    dot = functools.partial(jnp.dot, preferred_element_type=F32)
    bounds = [sum(widths[:c]) for c in range(len(widths) + 1)]
    chunk = lambda c: slice(bounds[c], bounds[c + 1])
    up = lambda c: (dot(xb, w_gate(chunk(c))), dot(xb, w_up(chunk(c))))
    pending = up(0)
    for c in range(len(widths)):
        hg, hu = pending
        if c + 1 < len(widths):
            pending = up(c + 1)
        acc_ref[...] += dot((_silu(hg) * hu).astype(BF16), w_down(chunk(c)))


FFN_CHUNK = 512


def _mix_ffn_kernel(att_ref, ssd_ref, gm_ref, h_ref, ag_ref, wa_ref, ws_ref, wgm_ref, g1_ref, b1_ref,
                    wg_ref, wu_ref, wd_ref, g2_ref, b2_ref, o_ref, acc_ref):
    d_ff = wg_ref.shape[-1]
    widths = (FFN_CHUNK,) * (d_ff // FFN_CHUNK) + ((d_ff % FFN_CHUNK,) if d_ff % FFN_CHUNK else ())
    x = _mix_ln1(att_ref, ssd_ref, gm_ref, h_ref, ag_ref, wa_ref, ws_ref, wgm_ref, g1_ref, b1_ref)
    acc_ref[...] = ALPHA * x
    _swiglu_chunks(acc_ref, x.astype(BF16), lambda c: wg_ref[:, c], lambda c: wu_ref[:, c],
                   lambda c: wd_ref[c, :], widths)
    o_ref[...] = _layer_norm(acc_ref[...], g2_ref[...], b2_ref[...])


def _mix_ffn(att, ssd, gm, h, p, wg, wu, wd, ln_g, ln_b):
    m = h.shape[0]
    row = lambda i: (i, 0)
    const = lambda a: pl.BlockSpec(a.shape, lambda i: (0, 0))
    resident = lambda a: pl.BlockSpec(a.shape, lambda i: (0, 0), pipeline_mode=pl.Buffered(1))
    small = [p["att_norm_g"], p["w_out_att"], p["w_out_ssd"], p["w_out_gm"], p["ln1_g"], p["ln1_b"]]
    return pl.pallas_call(
        _mix_ffn_kernel,
        grid=(m // FFN_TM,),
        in_specs=[pl.BlockSpec((FFN_TM, ATT_W), row), pl.BlockSpec((FFN_TM, SSD_PAD), row),
                  pl.BlockSpec((FFN_TM, GM_W), row), pl.BlockSpec((FFN_TM, D_MODEL), row)]
                 + [const(a) for a in small] + [resident(wg), resident(wu), resident(wd),
                                                pl.BlockSpec((1, D_MODEL), lambda i: (0, 0)),
                                                pl.BlockSpec((1, D_MODEL), lambda i: (0, 0))],
        out_specs=pl.BlockSpec((FFN_TM, D_MODEL), row),
        out_shape=jax.ShapeDtypeStruct((m, D_MODEL), F32),
        scratch_shapes=[pltpu.VMEM((FFN_TM, D_MODEL), F32)],
        compiler_params=_params(("parallel",)),
        name="out_proj_ffn",
    )(att, ssd, gm, h, *small, wg, wu, wd, ln_g.reshape(1, D_MODEL), ln_b.reshape(1, D_MODEL))


MOE_TR = 256
MOE_HALF = 512
MOE_UNITS = 3
MOE_TM = MOE_UNITS * MOE_HALF
MOE_TF = 512
MOE_CHUNKS = (256, 256)
SEG_ALIGN = 16
MOE_WIN_SMALL = 128
INFO_GATE = N_EXPERTS


def _expert_cap(m):
    n_tiles = m // MOE_TR
    rows = m + (SEG_ALIGN - 1) * n_tiles + MOE_TR + MOE_HALF
    return -(-rows // MOE_TM) * MOE_TM


def _rows_computed(lens):
    return jnp.maximum(-(-lens // MOE_HALF), 1) * MOE_HALF


def _dispatch_kernel(h_ref, logit_ref, info_ref, offs_ref, lens_ref, small_ref, xs_hbm, run_ref, xbuf_ref, zbuf_ref,
                     sem_ref, zsem_ref, *, n_tiles, cap):
    t = pl.program_id(0)
    tr = MOE_TR
    slot = t % 2

    @pl.when(t == 0)
    def _():
        for e in range(N_EXPERTS):
            run_ref[e] = 0

    xb = h_ref[...].astype(BF16)
    lg = logit_ref[0:N_EXPERTS, :]
    row = lax.broadcasted_iota(jnp.int32, (N_EXPERTS, tr), 0)
    m1 = jnp.max(lg, axis=0, keepdims=True)
    i1 = jnp.min(jnp.where(lg == m1, row, N_EXPERTS), axis=0, keepdims=True)
    rest = jnp.where(row == i1, -jnp.inf, lg)
    m2 = jnp.max(rest, axis=0, keepdims=True)
    i2 = jnp.min(jnp.where(rest == m2, row, N_EXPERTS), axis=0, keepdims=True)
    e2 = jnp.exp(m2 - m1)
    p1 = 1.0 / (1.0 + e2)
    gate_t = jnp.where(row == i1, p1, 0.0) + jnp.where(row == i2, e2 * p1, 0.0)
    sel_t = jnp.where((row == i1) | (row == i2), 1.0, 0.0)
    before = (lax.broadcasted_iota(jnp.int32, (tr, tr), 0) < lax.broadcasted_iota(jnp.int32, (tr, tr), 1))
    rank_t = jnp.dot(sel_t.astype(BF16), jnp.where(before, 1.0, 0.0).astype(BF16),
                     preferred_element_type=F32)
    cnt = jnp.sum(sel_t, axis=1, keepdims=True).astype(jnp.int32)
    info_t = jnp.concatenate([rank_t, gate_t, jnp.zeros((LANES - 2 * N_EXPERTS, tr), F32)], axis=0)
    info_ref[...] = info_t.T

    small = jnp.max(cnt) <= MOE_WIN_SMALL
    small_ref[t] = small.astype(jnp.int32)

    def seg_copy(sl, e, off, rows):
        return pltpu.make_async_copy(xbuf_ref.at[sl, e, pl.ds(0, rows)],
                                     xs_hbm.at[pl.ds(pl.multiple_of(off, SEG_ALIGN), rows)], sem_ref.at[sl, e])

    def by_window(flag, fn):
        pl.when(flag)(functools.partial(fn, MOE_WIN_SMALL))
        pl.when(jnp.logical_not(flag))(functools.partial(fn, tr))

    def compact(rows):
        dst_row = lax.broadcasted_iota(jnp.int32, (rows, tr), 0).astype(F32)
        onehots = [jnp.where((rank_t[e:e + 1, :] == dst_row) & (sel_t[e:e + 1, :] > 0.0), 1.0, 0.0).astype(BF16)
                   for e in range(N_EXPERTS)]
        onehot_all = jnp.concatenate(onehots, axis=0)
        half = D_MODEL // 2
        for c in range(2):
            packed = jnp.dot(onehot_all, xb[:, c * half:(c + 1) * half], preferred_element_type=F32)
            xbuf_ref[slot, :, 0:rows, c * half:(c + 1) * half] = packed.astype(BF16).reshape(N_EXPERTS, rows, half)

    by_window(small, compact)

    def wait_previous(rows):
        for e in range(N_EXPERTS):
            seg_copy(1 - slot, e, 0, rows).wait()

    @pl.when(t > 0)
    def _():
        by_window(small_ref[jnp.maximum(t - 1, 0)] == 1, wait_previous)

    def start_all(rows):
        for e in range(N_EXPERTS):
            seg_copy(slot, e, e * cap + run_ref[e], rows).start()

    by_window(small, start_all)
    for e in range(N_EXPERTS):
        offs_ref[t * N_EXPERTS + e] = e * cap + run_ref[e]
        run_ref[e] = run_ref[e] + jnp.bitwise_and(cnt[e, 0] + (SEG_ALIGN - 1), -SEG_ALIGN)

    def wait_current(rows):
        for e in range(N_EXPERTS):
            seg_copy(slot, e, 0, rows).wait()

    @pl.when(t == n_tiles - 1)
    def _():
        zbuf_ref[...] = jnp.zeros_like(zbuf_ref)
        by_window(small, wait_current)
        tails = []
        for e in range(N_EXPERTS):
            lens_ref[e] = run_ref[e]
            tail = pl.multiple_of(e * cap + run_ref[e], SEG_ALIGN)
            tails.append(pltpu.make_async_copy(zbuf_ref, xs_hbm.at[pl.ds(tail, MOE_HALF)], zsem_ref.at[e]))
            tails[-1].start()
        for cp in tails:
            cp.wait()


def _dispatch(h, logits):
    m = h.shape[0]
    n_tiles = m // MOE_TR
    cap = _expert_cap(m)
    smem = pl.BlockSpec(memory_space=pltpu.SMEM)
    return pl.pallas_call(
        functools.partial(_dispatch_kernel, n_tiles=n_tiles, cap=cap),
        grid=(n_tiles,),
        in_specs=[pl.BlockSpec((MOE_TR, D_MODEL), lambda t: (t, 0)),
                  pl.BlockSpec((LOGIT_ROWS, MOE_TR), lambda t: (0, t))],
        out_specs=[pl.BlockSpec((MOE_TR, LANES), lambda t: (t, 0)), smem, smem, smem,
                   pl.BlockSpec(memory_space=pl.ANY)],
        out_shape=[jax.ShapeDtypeStruct((m, LANES), F32),
                   jax.ShapeDtypeStruct((n_tiles * N_EXPERTS,), jnp.int32),
                   jax.ShapeDtypeStruct((N_EXPERTS,), jnp.int32),
                   jax.ShapeDtypeStruct((n_tiles,), jnp.int32),
                   jax.ShapeDtypeStruct((N_EXPERTS * cap, D_MODEL), BF16)],
        scratch_shapes=[pltpu.SMEM((N_EXPERTS,), jnp.int32),
                        pltpu.VMEM((2, N_EXPERTS, MOE_TR, D_MODEL), BF16),
                        pltpu.VMEM((MOE_HALF, D_MODEL), BF16),
                        pltpu.SemaphoreType.DMA((2, N_EXPERTS)),
                        pltpu.SemaphoreType.DMA((N_EXPERTS,))],
        compiler_params=_params(("arbitrary",)),
        name="moe_dispatch",
    )(h, logits)


def _expert_ffn_kernel(exp_ref, blk_ref, units_ref, x_ref, wg_ref, wu_ref, wd_ref, y_ref, acc_ref):
    w = pl.program_id(0)
    f = pl.program_id(1)

    def tile(rows):
        acc = acc_ref.at[pl.ds(0, rows)]

        @pl.when(f == 0)
        def _():
            acc[...] = jnp.zeros((rows, D_MODEL), F32)

        _swiglu_chunks(acc, x_ref[0:rows, :], lambda c: wg_ref[0, :, c].astype(BF16),
                       lambda c: wu_ref[0, :, c].astype(BF16), lambda c: wd_ref[0, c, :].astype(BF16), MOE_CHUNKS)

        @pl.when(f == pl.num_programs(1) - 1)
        def _():
            y_ref[0:rows, :] = acc[...].astype(BF16)

    for units in range(1, MOE_UNITS + 1):
        pl.when(units_ref[w] == units)(functools.partial(tile, units * MOE_HALF))


def _expert_ffn(xs, exp_w, blk_w, units_w, w_gate, w_up, w_down):
    d_ff = w_gate.shape[-1]
    n_f = d_ff // MOE_TF
    fcol = lambda w, f, exp, blk, units: jnp.where(units[w] != 0, f, n_f - 1)
    grid_spec = pltpu.PrefetchScalarGridSpec(
        num_scalar_prefetch=3,
        grid=(exp_w.shape[0], n_f),
        in_specs=[pl.BlockSpec((MOE_TM, D_MODEL), lambda w, f, exp, blk, valid: (blk[w], 0)),
                  pl.BlockSpec((1, D_MODEL, MOE_TF), lambda w, f, exp, blk, valid: (exp[w], 0, fcol(w, f, exp, blk, valid))),
                  pl.BlockSpec((1, D_MODEL, MOE_TF), lambda w, f, exp, blk, valid: (exp[w], 0, fcol(w, f, exp, blk, valid))),
                  pl.BlockSpec((1, MOE_TF, D_MODEL), lambda w, f, exp, blk, valid: (exp[w], fcol(w, f, exp, blk, valid), 0))],
        out_specs=pl.BlockSpec((MOE_TM, D_MODEL), lambda w, f, exp, blk, valid: (blk[w], 0)),
        scratch_shapes=[pltpu.VMEM((MOE_TM, D_MODEL), F32)])
    return pl.pallas_call(
        _expert_ffn_kernel,
        grid_spec=grid_spec,
        out_shape=jax.ShapeDtypeStruct(xs.shape, BF16),
        compiler_params=_params(("arbitrary", "arbitrary")),
        name="moe_expert_ffn",
    )(exp_w, blk_w, units_w, xs, w_gate, w_up, w_down)


def _combine_kernel(starts_ref, shifts_ref, small_ref, info_ref, h_ref, g_ref, b_ref, ys_hbm, o_ref, ybuf_ref,
                    sem_ref, *, n_tiles):
    t = pl.program_id(0)
    tr = MOE_TR
    slot = t % 2

    def seg_copy(tile, sl, e, rows):
        off = pl.multiple_of(starts_ref[tile * N_EXPERTS + e], SEG_ALIGN)
        return pltpu.make_async_copy(ys_hbm.at[pl.ds(off, rows)], ybuf_ref.at[sl, e, pl.ds(0, rows)],
                                     sem_ref.at[sl, e])

    def by_window(tile, fn):
        flag = small_ref[tile] == 1
        pl.when(flag)(functools.partial(fn, MOE_WIN_SMALL))
        pl.when(jnp.logical_not(flag))(functools.partial(fn, tr))

    def fetch(tile, sl):
        def start(rows):
            for e in range(N_EXPERTS):
                seg_copy(tile, sl, e, rows).start()
        by_window(tile, start)

    @pl.when(t == 0)
    def _():
        fetch(0, 0)

    @pl.when(t + 1 < n_tiles)
    def _():
        fetch(jnp.minimum(t + 1, n_tiles - 1), 1 - slot)

    def gather(rows):
        for e in range(N_EXPERTS):
            seg_copy(t, slot, e, rows).wait()
        info = info_ref[...]
        src_row = lax.broadcasted_iota(jnp.int32, (tr, rows), 1).astype(F32)
        acc = ALPHA * h_ref[...]
        for e in range(N_EXPERTS):
            rank_c = info[:, e:e + 1] + shifts_ref[t * N_EXPERTS + e].astype(F32)
            gate_c = info[:, INFO_GATE + e:INFO_GATE + e + 1]
            onehot = jnp.where((rank_c == src_row) & (gate_c != 0.0), 1.0, 0.0).astype(BF16)
            acc = acc + gate_c * jnp.dot(onehot, ybuf_ref[slot, e, 0:rows, :], preferred_element_type=F32)
        o_ref[...] = _layer_norm(acc, g_ref[...], b_ref[...])

    by_window(t, gather)


def _combine(starts, shifts, small, info, h, ys, ln_g, ln_b):
    m = h.shape[0]
    n_tiles = m // MOE_TR
    grid_spec = pltpu.PrefetchScalarGridSpec(
        num_scalar_prefetch=3,
        grid=(n_tiles,),
        in_specs=[pl.BlockSpec((MOE_TR, LANES), lambda t, *_: (t, 0)),
                  pl.BlockSpec((MOE_TR, D_MODEL), lambda t, *_: (t, 0)),
                  pl.BlockSpec((1, D_MODEL), lambda t, *_: (0, 0)),
                  pl.BlockSpec((1, D_MODEL), lambda t, *_: (0, 0)),
                  pl.BlockSpec(memory_space=pl.ANY)],
        out_specs=pl.BlockSpec((MOE_TR, D_MODEL), lambda t, *_: (t, 0)),
        scratch_shapes=[pltpu.VMEM((2, N_EXPERTS, MOE_TR, D_MODEL), BF16),
                        pltpu.SemaphoreType.DMA((2, N_EXPERTS))])
    return pl.pallas_call(
        functools.partial(_combine_kernel, n_tiles=n_tiles),
        grid_spec=grid_spec,
        out_shape=jax.ShapeDtypeStruct((m, D_MODEL), F32),
        compiler_params=_params(("arbitrary",)),
        name="moe_combine_ln2",
    )(starts, shifts, small, info, h, ln_g.reshape(1, D_MODEL), ln_b.reshape(1, D_MODEL), ys)


def _work_list(lens, m):
    cap = _expert_cap(m)
    n_tiles = m // MOE_TR
    max_rows = 2 * m + (SEG_ALIGN - 1) * min(N_EXPERTS * n_tiles, 2 * m)
    w_max = -(-(max_rows + N_EXPERTS * MOE_HALF) // MOE_TM) + N_EXPERTS
    rows_e = _rows_computed(lens)
    tiles_e = -(-rows_e // MOE_TM)
    ends = jnp.cumsum(tiles_e)
    w = jnp.arange(w_max, dtype=jnp.int32)
    wc = jnp.minimum(w, ends[-1] - 1)
    exp_w = jnp.sum((wc[:, None] >= ends[None, :]).astype(jnp.int32), axis=1)
    tile_w = wc - (ends - tiles_e)[exp_w]
    blk_w = exp_w * (cap // MOE_TM) + tile_w
    units_w = jnp.minimum((rows_e[exp_w] - tile_w * MOE_TM) // MOE_HALF, MOE_UNITS)
    units_w = jnp.where(w < ends[-1], units_w, 0)
    return exp_w.astype(jnp.int32), blk_w.astype(jnp.int32), units_w.astype(jnp.int32)


def _read_windows(offs, small, lens, m):
    cap = _expert_cap(m)
    region = jnp.arange(N_EXPERTS, dtype=jnp.int32) * cap
    rel = offs.reshape(-1, N_EXPERTS) - region
    width = jnp.where(small == 1, MOE_WIN_SMALL, MOE_TR)[:, None]
    start = jnp.minimum(rel, _rows_computed(lens)[None, :] - width)
    return (start + region).reshape(-1).astype(jnp.int32), (rel - start).reshape(-1).astype(jnp.int32)


def _moe_layer(h, logits, w_gate, w_up, w_down, ln_g, ln_b):
    m = h.shape[0]
    info, offs, lens, small, xs = _dispatch(h, logits)
    exp_w, blk_w, units_w = _work_list(lens, m)
    ys = _expert_ffn(xs, exp_w, blk_w, units_w, w_gate, w_up, w_down)
    starts, shifts = _read_windows(offs, small, lens, m)
    return _combine(starts, shifts, small, info, h, ys, ln_g, ln_b)


def _pad_groups(a):
    lead = a.shape[:-1]
    a = a.reshape(*lead, SSD_GROUPS, GROUP_W)
    a = jnp.pad(a, [(0, 0)] * len(lead) + [(0, 0), (0, GROUP_PAD - GROUP_W)])
    return a.reshape(*lead, SSD_PAD)


def _head_lanes(v):
    return jnp.pad(v, (0, LANES - SSD_HEADS)).reshape(1, LANES)


D_IN = 3 * ATT_W + SSD_W + (SSD_W + 2 * SSD_GROUPS * SSD_STATE) + SSD_HEADS + 2 * GM_W
PERM_ROWS = 256


def _permute_w_kernel(w_ref, o_ref):
    src_z = 3 * ATT_W
    src_xs = src_z + SSD_W
    src_b = src_xs + SSD_W
    src_c = src_b + SSD_GROUPS * SSD_STATE
    src_dt = src_c + SSD_GROUPS * SSD_STATE
    src_gu = src_dt + SSD_HEADS
    src_gv = src_gu + GM_W
    o_ref[...] = jnp.zeros_like(o_ref)

    def put(dst, src, width):
        o_ref[:, dst:dst + width] = w_ref[0, :, src:src + width].astype(BF16)

    for g in range(SSD_GROUPS):
        put(COL_Z + g * GROUP_PAD, src_z + g * GROUP_W, GROUP_W)
        put(COL_XS + g * GROUP_PAD, src_xs + g * GROUP_W, GROUP_W)
    put(COL_B, src_b, SSD_GROUPS * SSD_STATE)
    put(COL_C, src_c, SSD_GROUPS * SSD_STATE)
    put(COL_GU, src_gu, GM_W)
    put(COL_GV, src_gv, GM_W)
    put(COL_Q, 0, ATT_W)
    put(COL_K, ATT_W, ATT_W)
    put(COL_V, 2 * ATT_W, ATT_W)
    put(COL_DT, src_dt, SSD_HEADS)


def _permute_w_in(w_in, layer):
    return pl.pallas_call(
        _permute_w_kernel,
        grid=(D_MODEL // PERM_ROWS,),
        in_specs=[pl.BlockSpec((1, PERM_ROWS, D_IN), lambda i: (layer, i, 0))],
        out_specs=pl.BlockSpec((PERM_ROWS, D_W), lambda i: (i, 0)),
        out_shape=jax.ShapeDtypeStruct((D_MODEL, D_W), BF16),
        compiler_params=_params(("parallel",)),
        name="permute_w_in",
    )(w_in)


def _layer_params(i, w_in, conv_w, conv_b, dt_bias, a_log, d_skip, ssd_norm_g, att_norm_g, gm_ln_g, gm_ln_b,
                  gm_w_s, gm_b_s, gm_norm_g, w_out, ln1_g, ln1_b):
    cw, cb = conv_w[i], conv_b[i]
    nbc = SSD_GROUPS * SSD_STATE
    wo = w_out[i]
    return {
        "w_in": _permute_w_in(w_in, i),
        "conv_w_xs": _pad_groups(cw[:, :SSD_W]),
        "conv_w_b": cw[:, SSD_W:SSD_W + nbc],
        "conv_w_c": cw[:, SSD_W + nbc:],
        "conv_b_xs": _pad_groups(cb[:SSD_W]).reshape(1, SSD_PAD),
        "conv_b_b": cb[SSD_W:SSD_W + nbc].reshape(1, nbc),
        "conv_b_c": cb[SSD_W + nbc:].reshape(1, nbc),
        "dt_bias": _head_lanes(dt_bias[i]),
        "a_log": _head_lanes(a_log[i]),
        "d_skip": _pad_groups(jnp.repeat(d_skip[i], SSD_HEAD_DIM)).reshape(1, SSD_PAD),
        "ssd_norm_g": _pad_groups(ssd_norm_g[i]).reshape(1, SSD_PAD),
        "att_norm_g": att_norm_g[i].reshape(1, ATT_W),
        "gm_ln_g": gm_ln_g[i].reshape(1, GM_W),
        "gm_ln_b": gm_ln_b[i].reshape(1, GM_W),
        "gm_w_s": gm_w_s[i],
        "gm_b_s": jnp.repeat(gm_b_s[i].T, GM_W // GM_GROUPS, axis=1),
        "gm_norm_g": gm_norm_g[i].reshape(1, GM_W),
        "w_out_att": wo[:ATT_W].astype(BF16),
        "w_out_ssd": _pad_groups(wo[ATT_W:ATT_W + SSD_W].T).T.astype(BF16),
        "w_out_gm": wo[ATT_W + SSD_W:].astype(BF16),
        "ln1_g": ln1_g[i].reshape(1, D_MODEL),
        "ln1_b": ln1_b[i].reshape(1, D_MODEL),
    }


def _rope_tables(positions):
    inv = ROPE_THETA ** (-jnp.arange(0, ROPE_DIM, 2, dtype=F32) / ROPE_DIM)
    d = jnp.arange(LANES) % HEAD_DIM
    inv_lane = jnp.where(d < ROPE_DIM, inv[d % ROPE_HALF], 0.0)
    ang = positions.astype(F32)[..., None] * inv_lane
    cos, sin = jnp.cos(ang), jnp.sin(ang)
    return cos, jnp.where(d < ROPE_HALF, -sin, 0.0), jnp.where(d >= ROPE_HALF, sin, 0.0)


def kernel(x, positions, ln_in_g, ln_in_b, w_in, conv_w, conv_b, dt_bias, a_log, d_skip, ssd_norm_g, att_norm_g, gm_ln_g, gm_ln_b, gm_w_s, gm_b_s, gm_norm_g, w_out, ln1_g, ln1_b, ln2_g, ln2_b, ffn_w_gate, ffn_w_up, ffn_w_down, router_w, moe_w_gate, moe_w_up, moe_w_down):
    bsz, seq, _ = x.shape
    m = bsz * seq
    rope_c, rope_sa, rope_sb = _rope_tables(positions)
    h = x.reshape(m, D_MODEL)
    for i in range(DEPTH):
        p = _layer_params(i, w_in, conv_w, conv_b, dt_bias, a_log, d_skip, ssd_norm_g, att_norm_g, gm_ln_g,
                          gm_ln_b, gm_w_s, gm_b_s, gm_norm_g, w_out, ln1_g, ln1_b)
        if i == 0:
            proj, gm, h = _in_proj(h, p["w_in"], p, ln=(ln_in_g, ln_in_b))
        else:
            proj, gm = _in_proj(h, p["w_in"], p)
        proj3 = proj.reshape(bsz, seq, D_PROJ)
        att = _attention(proj3, rope_c, rope_sa, rope_sb).reshape(m, ATT_W)
        ssd = _ssd(proj3, p).reshape(m, SSD_PAD)
        j = i // 2
        if i % 2 == 0:
            h = _mix_ffn(att, ssd, gm, h, p, ffn_w_gate[j].astype(BF16), ffn_w_up[j].astype(BF16),
                         ffn_w_down[j].astype(BF16), ln2_g[i], ln2_b[i])
        else:
            h, logits = _out_proj(att, ssd, gm, h, p, w_router=router_w[j])
            h = _moe_layer(h, logits, moe_w_gate[j], moe_w_up[j], moe_w_down[j], ln2_g[i], ln2_b[i])
    return h.reshape(bsz, seq, D_MODEL)
```

```python
import functools

import jax
import jax.numpy as jnp
from jax import lax
from jax.experimental import pallas as pl
from jax.experimental.pallas import tpu as pltpu

F32 = jnp.float32
BF16 = jnp.bfloat16
HIGHEST = lax.Precision.HIGHEST

D_MODEL = 1024
ATT_HEADS = 6
HEAD_DIM = 64
ATT_W = ATT_HEADS * HEAD_DIM
ROPE_DIM = HEAD_DIM // 4
ROPE_HALF = ROPE_DIM // 2
ROPE_THETA = 500000.0
MOBA_BLOCK = 256
MOBA_TOPK = 3
SSD_HEADS = 6
SSD_HEAD_DIM = 64
SSD_W = SSD_HEADS * SSD_HEAD_DIM
SSD_GROUPS = 2
SSD_STATE = 128
SSD_CONV = 4
SSD_CHUNK = 128
GM_GROUPS = 4
GM_W = 256
GM_CHUNK = 128
N_EXPERTS = 8
DEPTH = 2
ALPHA = (2.0 * DEPTH) ** 0.25
NEG = -1e30
LN_EPS = 1e-5

LANES = 128
GROUP_W = SSD_W // SSD_GROUPS
GROUP_PAD = 256
SSD_PAD = SSD_GROUPS * GROUP_PAD
HEADS_PER_GROUP = SSD_HEADS // SSD_GROUPS

COL_Z = 0
COL_XS = 512
COL_B = 1024
COL_C = 1280
COL_Q = 1536
COL_K = 1920
COL_V = 2304
COL_DT = 2688
D_PROJ = 2816
COL_GU = 2816
COL_GV = 3072
D_W = 3328

VMEM_LIMIT = 56 * 1024 * 1024


def _params(sem, vmem=VMEM_LIMIT):
    return pltpu.CompilerParams(dimension_semantics=sem, vmem_limit_bytes=vmem)


def _layer_norm(x, g, b):
    mu = jnp.mean(x, axis=-1, keepdims=True)
    xc = x - mu
    var = jnp.mean(xc * xc, axis=-1, keepdims=True)
    return xc * lax.rsqrt(var + LN_EPS) * g + b


def _silu(x):
    half = 0.5 * x
    return half + half * jnp.tanh(half)


def _split_bf16(x):
    hi = x.astype(BF16)
    return hi, (x - hi.astype(F32)).astype(BF16)


def _dot_bf16x3(a, b):
    a_hi, a_lo = _split_bf16(a)
    b_hi, b_lo = _split_bf16(b)
    dot = functools.partial(jnp.dot, preferred_element_type=F32)
    return dot(a_hi, b_hi) + (dot(a_lo, b_hi) + dot(a_hi, b_lo))


def _gelu_tanh(x):
    return 0.5 * x * (1.0 + jnp.tanh(0.7978845608028654 * (x + 0.044715 * x * x * x)))


PROJ_TM = 512
PROJ_CH = 256
PROJ_TAIL = 4


def _in_proj_kernel(*refs, apply_ln):
    if apply_ln:
        x_ref, g_ref, b_ref, w_ref, lng_ref, lnb_ref, ws_ref, bs_ref, ng_ref, proj_ref, gm_ref, h_ref = refs
        x = _layer_norm(x_ref[...], g_ref[...], b_ref[...])
        h_ref[...] = x
    else:
        x_ref, w_ref, lng_ref, lnb_ref, ws_ref, bs_ref, ng_ref, proj_ref, gm_ref = refs
        x = x_ref[...]
    xb = x.astype(BF16)

    def project(j):
        cols = slice(j * PROJ_CH, (j + 1) * PROJ_CH)
        proj_ref[:, cols] = jnp.dot(xb, w_ref[:, cols], preferred_element_type=F32)

    gu = jnp.dot(xb, w_ref[:, COL_GU:COL_GU + GM_W], preferred_element_type=F32)
    gv = jnp.dot(xb, w_ref[:, COL_GV:COL_GV + GM_W], preferred_element_type=F32)
    u, vb = _gmlp_activate(gu, gv, lng_ref, lnb_ref)
    n_chunks = D_PROJ // PROJ_CH
    for j in range(n_chunks - PROJ_TAIL):
        project(j)
    _gmlp_mix(u, vb, ws_ref, bs_ref, ng_ref, gm_ref)
    for j in range(n_chunks - PROJ_TAIL, n_chunks):
        project(j)


def _in_proj(x, w, p, ln=None):
    m = x.shape[0]
    row = lambda i: (i, 0)
    const = lambda i: (0, 0)
    whole = lambda a: pl.BlockSpec(a.shape, lambda i: (0,) * a.ndim)
    in_specs = [pl.BlockSpec((PROJ_TM, D_MODEL), row)]
    args = [x]
    out_shape = [jax.ShapeDtypeStruct((m, D_PROJ), F32), jax.ShapeDtypeStruct((m, GM_W), BF16)]
    out_specs = [pl.BlockSpec((PROJ_TM, D_PROJ), row), pl.BlockSpec((PROJ_TM, GM_W), row)]
    if ln is not None:
        in_specs += [pl.BlockSpec((1, D_MODEL), const), pl.BlockSpec((1, D_MODEL), const)]
        args += [ln[0].reshape(1, D_MODEL), ln[1].reshape(1, D_MODEL)]
        out_shape.append(jax.ShapeDtypeStruct((m, D_MODEL), F32))
        out_specs.append(pl.BlockSpec((PROJ_TM, D_MODEL), row))
    in_specs.append(pl.BlockSpec((D_MODEL, D_W), const))
    args.append(w)
    gm_small = [p["gm_ln_g"], p["gm_ln_b"], p["gm_w_s"], p["gm_b_s"], p["gm_norm_g"]]
    in_specs += [whole(a) for a in gm_small]
    args += gm_small
    return pl.pallas_call(
        functools.partial(_in_proj_kernel, apply_ln=ln is not None),
        grid=(m // PROJ_TM,),
        in_specs=in_specs,
        out_specs=out_specs,
        out_shape=out_shape,
        compiler_params=_params(("parallel",)),
        name="in_proj",
    )(*args)


VT_ROWS = HEAD_DIM + 16
LOG2E = 1.4426950408889634


ATT_VALUE_LAG = 6


def _attn_kernel(q_ref, k_ref, v_ref, c_ref, sa_ref, sb_ref, o_ref, kb_ref, vt_ref, *, seq):
    nb = seq // MOBA_BLOCK
    tq = MOBA_BLOCK
    lane = lax.broadcasted_iota(jnp.int32, (1, LANES), 1)

    def rope(x, rows):
        return (x * c_ref[0, rows, :] + pltpu.roll(x, LANES - ROPE_HALF, 1) * sa_ref[0, rows, :]
                + pltpu.roll(x, ROPE_HALF, 1) * sb_ref[0, rows, :])

    k = rope(k_ref[0], slice(None))
    kb_ref[...] = k.astype(BF16)
    k_mean = jnp.mean(k.reshape(nb, MOBA_BLOCK, LANES), axis=1)
    v_t = v_ref[0].T
    ones_rows = jnp.where(lax.broadcasted_iota(jnp.int32, (VT_ROWS - HEAD_DIM, seq), 0) == 0, 1.0, 0.0)
    for hh in range(2):
        vt_ref[hh] = jnp.concatenate([v_t[hh * HEAD_DIM:(hh + 1) * HEAD_DIM], ones_rows], axis=0).astype(BF16)

    qscale = HEAD_DIM ** -0.5 * LOG2E
    causal_t = (lax.broadcasted_iota(jnp.int32, (tq, tq), 0) <= lax.broadcasted_iota(jnp.int32, (tq, tq), 1))
    blk_row = lax.broadcasted_iota(jnp.int32, (nb, tq), 0)

    chain_q = {}
    chain_bias = {}
    chain_m = {}
    chain_o = {}
    q_tiles = {}

    def start_chain(j, hh):
        if hh == 0:
            rows = slice(j * tq, (j + 1) * tq)
            q_tiles[j] = rope(q_ref[0, rows, :], rows) * qscale
        q = q_tiles[j] if hh == 0 else q_tiles.pop(j)
        head_lanes = (lane < HEAD_DIM) if hh == 0 else (lane >= HEAD_DIM)
        qh_t = jnp.where(head_lanes, q, 0.0).T
        chain_q[(j, hh)] = qh_t.astype(BF16)
        chain_bias[(j, hh)] = None
        if j > MOBA_TOPK:
            gate_t = _dot_bf16x3(k_mean, qh_t)
            rank = jnp.zeros((nb, tq), F32)
            for m in range(j):
                g_m = gate_t[m:m + 1, :]
                beats = (g_m > gate_t) | ((g_m == gate_t) & (blk_row > m))
                rank = rank + jnp.where(beats & (blk_row != m), 1.0, 0.0)
            bias = jnp.where((blk_row < j) & (rank >= MOBA_TOPK), NEG, 0.0)
            chain_bias[(j, hh)] = [bias[n:n + 1, :] for n in range(j)]

    def scores(j, hh, n):
        return jnp.dot(kb_ref[n * tq:(n + 1) * tq, :], chain_q[(j, hh)], preferred_element_type=F32)

    def probs(j, hh, n, s_t):
        key = (j, hh)
        if n == j:
            s_t = jnp.where(causal_t, s_t, NEG)
            m_new = jnp.max(s_t, axis=0, keepdims=True)
            alpha = None
            shift = m_new
        else:
            bias_n = 0.0 if chain_bias[key] is None else chain_bias[key][n]
            m_new = jnp.maximum(chain_m[key], jnp.max(s_t, axis=0, keepdims=True) + bias_n)
            alpha = jnp.exp2(chain_m[key] - m_new)
            shift = m_new - bias_n
        chain_m[key] = m_new
        return jnp.exp2(s_t - shift).astype(BF16), alpha

    def values(j, hh, n, p_t, alpha):
        key = (j, hh)
        part = jnp.dot(vt_ref[hh, :, n * tq:(n + 1) * tq], p_t, preferred_element_type=F32)
        chain_o[key] = part if alpha is None else chain_o[key] * alpha + part

    def finish(j):
        halves = [chain_o.pop((j, hh)) for hh in range(2)]
        out_t = jnp.concatenate([o[0:HEAD_DIM] / o[HEAD_DIM:HEAD_DIM + 1] for o in halves], axis=0)
        o_ref[0, j * tq:(j + 1) * tq, :] = out_t.T

    value_lag = ATT_VALUE_LAG
    units = []
    for j in range(nb):
        for n in [j] + list(range(j)):
            units += [(j, 0, n), (j, 1, n)]
    s_state = {}
    p_state = {}
    for step in range(len(units) + value_lag):
        if step >= value_lag:
            j, hh, n = unit = units[step - value_lag]
            values(j, hh, n, *p_state.pop(unit))
            last_block = n == (j - 1 if j > 0 else 0)
            if hh == 1 and last_block:
                finish(j)
        if step < len(units):
            j, hh, n = unit = units[step]
            if n == j:
                start_chain(j, hh)
            s_state[unit] = scores(j, hh, n)
        if 1 <= step <= len(units):
            j, hh, n = unit = units[step - 1]
            p_state[unit] = probs(j, hh, n, s_state.pop(unit))


def _attention(proj3, rope_c, rope_sa, rope_sb):
    bsz, seq, _ = proj3.shape
    blk = (1, seq, LANES)
    col = lambda base: (lambda b, p: (b, 0, base // LANES + p))
    tab = lambda b, p: (b, 0, 0)
    return pl.pallas_call(
        functools.partial(_attn_kernel, seq=seq),
        grid=(bsz, ATT_W // LANES),
        in_specs=[pl.BlockSpec(blk, col(COL_Q)), pl.BlockSpec(blk, col(COL_K)), pl.BlockSpec(blk, col(COL_V)),
                  pl.BlockSpec(blk, tab), pl.BlockSpec(blk, tab), pl.BlockSpec(blk, tab)],
        out_specs=pl.BlockSpec(blk, lambda b, p: (b, 0, p)),
        out_shape=jax.ShapeDtypeStruct((bsz, seq, ATT_W), F32),
        scratch_shapes=[pltpu.VMEM((seq, LANES), BF16), pltpu.VMEM((2, VT_ROWS, seq), BF16)],
        compiler_params=_params(("parallel", "arbitrary")),
        name="moba_attention",
    )(proj3, proj3, proj3, rope_c, rope_sa, rope_sb)


SSD_UNROLL = 4
assert HEADS_PER_GROUP == 3 and GROUP_PAD == 2 * LANES and SSD_HEAD_DIM * 2 == LANES


def _expand_heads(small):
    low = lax.broadcasted_iota(jnp.int32, (1, LANES), 1) < SSD_HEAD_DIM
    slabs = []
    for g in range(SSD_GROUPS):
        h0 = g * HEADS_PER_GROUP
        slabs.append(jnp.where(low, small[:, h0:h0 + 1], small[:, h0 + 1:h0 + 2]))
        slabs.append(jnp.where(low, small[:, h0 + 2:h0 + 3], 0.0))
    return jnp.concatenate(slabs, axis=1)


def _ssd_kernel(z_ref, xs_ref, b_ref, c_ref, dt_ref, cwx_ref, cwb_ref, cwc_ref, cbx_ref, cbb_ref, cbc_ref,
                dtb_ref, alog_ref, dskip_ref, ng_ref, o_ref, state_ref, *, seq):
    q = SSD_CHUNK
    n = SSD_STATE
    state_ref[...] = jnp.zeros_like(state_ref)
    row_i = lax.broadcasted_iota(jnp.int32, (q, q), 0)
    col_i = lax.broadcasted_iota(jnp.int32, (q, q), 1)
    lower = row_i >= col_i
    tril = jnp.where(lower, 1.0, 0.0)
    slab = lax.broadcasted_iota(jnp.int32, (1, GROUP_PAD), 1) // SSD_HEAD_DIM
    a_neg = -jnp.exp(alog_ref[...]) * LOG2E

    def conv_silu(ref, w_ref, bias_ref, c):
        start = pl.multiple_of(c * q, q)
        cur = ref[0, pl.ds(start, q), :]
        prev_start = pl.multiple_of(jnp.maximum(c * q - 8, 0), 8)
        prev = jnp.where(c > 0, ref[0, pl.ds(prev_start, 8), :], 0.0)
        full = jnp.concatenate([prev, cur], axis=0)
        y = bias_ref[...] + w_ref[SSD_CONV - 1:SSD_CONV, :] * cur
        for i in range(SSD_CONV - 1):
            shift = SSD_CONV - 1 - i
            y = y + w_ref[i:i + 1, :] * pltpu.roll(full, shift, 0)[8:, :]
        return _silu(y)

    def chunk(c, carry):
        start = pl.multiple_of(c * q, q)
        xs = conv_silu(xs_ref, cwx_ref, cbx_ref, c)
        bm = conv_silu(b_ref, cwb_ref, cbb_ref, c)
        cm = conv_silu(c_ref, cwc_ref, cbc_ref, c)
        x = dt_ref[0, pl.ds(start, q), :] + dtb_ref[...]
        dt = jnp.maximum(x, 0.0) + jnp.log(1.0 + jnp.exp(-jnp.abs(x)))
        a = dt * a_neg
        a_cum = jnp.dot(tril, a, precision=HIGHEST, preferred_element_type=F32)
        a_cum_t = a_cum.T
        dt_e = _expand_heads(dt)
        acum_e = _expand_heads(a_cum)
        alast_e = acum_e[q - 1:q, :]
        xdt = xs * dt_e
        decay_e = jnp.exp2(alast_e - acum_e)
        y_parts = []
        for g in range(SSD_GROUPS):
            gl = slice(g * GROUP_PAD, (g + 1) * GROUP_PAD)
            nl = slice(g * n, (g + 1) * n)
            b_g = bm[:, nl].astype(BF16)
            b_gt = bm[:, nl].T.astype(BF16)
            c_g = cm[:, nl].astype(BF16)
            cb = lax.dot_general(c_g, b_g, (((1,), (1,)), ((), ())), preferred_element_type=F32)
            xdt_g = xdt[:, gl]
            xdt_b = xdt_g.astype(BF16)
            y_g = jnp.zeros((q, GROUP_PAD), F32)
            for r in range(HEADS_PER_GROUP):
                h = g * HEADS_PER_GROUP + r
                seg = a_cum[:, h:h + 1] - a_cum_t[h:h + 1, :]
                l_h = jnp.where(lower, jnp.exp2(jnp.minimum(seg, 0.0)), 0.0)
                y_h = jnp.dot((cb * l_h).astype(BF16), xdt_b, preferred_element_type=F32)
                y_g = jnp.where(slab == r, y_h, y_g)
            prev = state_ref[g]
            y_off = jnp.dot(c_g, prev.astype(BF16), preferred_element_type=F32) * jnp.exp2(acum_e[:, gl])
            new = jnp.dot(b_gt, (xdt_g * decay_e[:, gl]).astype(BF16), preferred_element_type=F32)
            state_ref[g] = prev * jnp.exp2(alast_e[:, gl]) + new
            y_parts.append(y_g + y_off)
        y = jnp.concatenate(y_parts, axis=1) + dskip_ref[...] * xs
        yz = y * _silu(z_ref[0, pl.ds(start, q), :])
        outs = []
        for g in range(SSD_GROUPS):
            yg = yz[:, g * GROUP_PAD:(g + 1) * GROUP_PAD]
            ms = jnp.sum(yg * yg, axis=-1, keepdims=True) * (1.0 / GROUP_W)
            outs.append(yg * lax.rsqrt(ms + LN_EPS))
        o_ref[0, pl.ds(start, q), :] = (jnp.concatenate(outs, axis=1) * ng_ref[...]).astype(BF16)
        return carry

    lax.fori_loop(0, seq // q, chunk, 0, unroll=SSD_UNROLL)


def _ssd(proj3, p):
    bsz, seq, _ = proj3.shape
    slab = lambda width, base: pl.BlockSpec((1, seq, width), lambda b: (b, 0, base // width))
    const2 = lambda a: pl.BlockSpec(a.shape, lambda b: (0, 0))
    small = [p["conv_w_xs"], p["conv_w_b"], p["conv_w_c"], p["conv_b_xs"], p["conv_b_b"], p["conv_b_c"],
             p["dt_bias"], p["a_log"], p["d_skip"], p["ssd_norm_g"]]
    return pl.pallas_call(
        functools.partial(_ssd_kernel, seq=seq),
        grid=(bsz,),
        in_specs=[slab(SSD_PAD, COL_Z), slab(SSD_PAD, COL_XS), slab(2 * SSD_STATE, COL_B),
                  slab(2 * SSD_STATE, COL_C), slab(LANES, COL_DT)] + [const2(a) for a in small],
        out_specs=pl.BlockSpec((1, seq, SSD_PAD), lambda b: (b, 0, 0)),
        out_shape=jax.ShapeDtypeStruct((bsz, seq, SSD_PAD), BF16),
        scratch_shapes=[pltpu.VMEM((SSD_GROUPS, SSD_STATE, GROUP_PAD), F32)],
        compiler_params=_params(("parallel",)),
        name="ssd",
    )(proj3, proj3, proj3, proj3, proj3, *small)


def _gmlp_activate(gu, gv, lng_ref, lnb_ref):
    return _gelu_tanh(gu), _layer_norm(_gelu_tanh(gv), lng_ref[...], lnb_ref[...]).astype(BF16)


def _gmlp_mix(u, vb, ws_ref, bs_ref, ng_ref, o_ref):
    q = GM_CHUNK
    row_i = lax.broadcasted_iota(jnp.int32, (q, q), 0)
    col_i = lax.broadcasted_iota(jnp.int32, (q, q), 1)
    lower = row_i >= col_i
    grp = lax.broadcasted_iota(jnp.int32, (1, GM_W), 1) // (GM_W // GM_GROUPS)
    w = [jnp.where(lower, ws_ref[g], 0.0).astype(BF16) for g in range(GM_GROUPS)]
    for c in range(u.shape[0] // q):
        rows = slice(c * q, (c + 1) * q)
        mixed = jnp.zeros((q, GM_W), F32)
        for g in range(GM_GROUPS):
            mixed = jnp.where(grp == g, jnp.dot(w[g], vb[rows, :], preferred_element_type=F32), mixed)
        gm = u[rows, :] * (mixed + bs_ref[...])
        ms = jnp.mean(gm * gm, axis=-1, keepdims=True)
        o_ref[rows, :] = (gm * lax.rsqrt(ms + LN_EPS) * ng_ref[...]).astype(BF16)


OUT_TM = 1024
LOGIT_ROWS = 16


def _out_proj_kernel(att_ref, ssd_ref, gm_ref, h_ref, ag_ref, wa_ref, ws_ref, wg_ref, g_ref, b_ref, o_ref,
                     router=None):
    att = att_ref[...]
    ms = jnp.mean(att * att, axis=-1, keepdims=True)
    att = att * lax.rsqrt(ms + LN_EPS) * ag_ref[...]
    mix = jnp.dot(att.astype(BF16), wa_ref[...], preferred_element_type=F32)
    mix = mix + jnp.dot(ssd_ref[...], ws_ref[...], preferred_element_type=F32)
    mix = mix + jnp.dot(gm_ref[...], wg_ref[...], preferred_element_type=F32)
    h1 = _layer_norm(ALPHA * h_ref[...] + mix, g_ref[...], b_ref[...])
    o_ref[...] = h1
    if router:
        wr_ref, logit_ref = router
        logit_ref[...] = _dot_bf16x3(wr_ref[...], h1.T)


def _out_proj_kernel_routed(att_ref, ssd_ref, gm_ref, h_ref, ag_ref, wa_ref, ws_ref, wg_ref, g_ref, b_ref, wr_ref,
                            o_ref, logit_ref):
    _out_proj_kernel(att_ref, ssd_ref, gm_ref, h_ref, ag_ref, wa_ref, ws_ref, wg_ref, g_ref, b_ref, o_ref,
                     router=(wr_ref, logit_ref))


def _out_proj(att, ssd, gm, h, p, w_router=None):
    m = h.shape[0]
    row = lambda i: (i, 0)
    const = lambda a: pl.BlockSpec(a.shape, lambda i: (0, 0))
    small = [p["att_norm_g"], p["w_out_att"], p["w_out_ssd"], p["w_out_gm"], p["ln1_g"], p["ln1_b"]]
    out_specs = [pl.BlockSpec((OUT_TM, D_MODEL), row)]
    out_shape = [jax.ShapeDtypeStruct((m, D_MODEL), F32)]
    body = _out_proj_kernel
    if w_router is not None:
        small.append(jnp.pad(w_router.T, ((0, LOGIT_ROWS - N_EXPERTS), (0, 0))))
        out_specs.append(pl.BlockSpec((LOGIT_ROWS, OUT_TM), lambda i: (0, i)))
        out_shape.append(jax.ShapeDtypeStruct((LOGIT_ROWS, m), F32))
        body = _out_proj_kernel_routed
    return pl.pallas_call(
        body,
        grid=(m // OUT_TM,),
        in_specs=[pl.BlockSpec((OUT_TM, ATT_W), row), pl.BlockSpec((OUT_TM, SSD_PAD), row),
                  pl.BlockSpec((OUT_TM, GM_W), row), pl.BlockSpec((OUT_TM, D_MODEL), row)]
                 + [const(a) for a in small],
        out_specs=out_specs,
        out_shape=out_shape,
        compiler_params=_params(("parallel",)),
        name="out_proj_ln1",
    )(att, ssd, gm, h, *small)


FFN_TM = 1024


def _swiglu_chunks(acc_ref, xb, w_gate, w_up, w_down, widths):
    dot = functools.partial(jnp.dot, preferred_element_type=F32)
    bounds = [sum(widths[:c]) for c in range(len(widths) + 1)]
    chunk = lambda c: slice(bounds[c], bounds[c + 1])
    up = lambda c: (dot(xb, w_gate(chunk(c))), dot(xb, w_up(chunk(c))))
    pending = up(0)
    for c in range(len(widths)):
        hg, hu = pending
        if c + 1 < len(widths):
            pending = up(c + 1)
        acc_ref[...] += dot((_silu(hg) * hu).astype(BF16), w_down(chunk(c)))


FFN_CHUNK = 512


def _ffn_kernel(x_ref, wg_ref, wu_ref, wd_ref, g_ref, b_ref, o_ref, acc_ref):
    d_ff = wg_ref.shape[-1]
    widths = (FFN_CHUNK,) * (d_ff // FFN_CHUNK) + ((d_ff % FFN_CHUNK,) if d_ff % FFN_CHUNK else ())
    x = x_ref[...]
    acc_ref[...] = ALPHA * x
    _swiglu_chunks(acc_ref, x.astype(BF16), lambda c: wg_ref[:, c], lambda c: wu_ref[:, c],
                   lambda c: wd_ref[c, :], widths)
    o_ref[...] = _layer_norm(acc_ref[...], g_ref[...], b_ref[...])


def _ffn(x, wg, wu, wd, ln_g, ln_b):
    m = x.shape[0]
    resident = lambda a: pl.BlockSpec(a.shape, lambda i: (0, 0), pipeline_mode=pl.Buffered(1))
    return pl.pallas_call(
        _ffn_kernel,
        grid=(m // FFN_TM,),
        in_specs=[pl.BlockSpec((FFN_TM, D_MODEL), lambda i: (i, 0)), resident(wg), resident(wu), resident(wd),
                  pl.BlockSpec((1, D_MODEL), lambda i: (0, 0)),
                  pl.BlockSpec((1, D_MODEL), lambda i: (0, 0))],
        out_specs=pl.BlockSpec((FFN_TM, D_MODEL), lambda i: (i, 0)),
        out_shape=jax.ShapeDtypeStruct((m, D_MODEL), F32),
        scratch_shapes=[pltpu.VMEM((FFN_TM, D_MODEL), F32)],
        compiler_params=_params(("parallel",)),
        name="ffn_ln2",
    )(x, wg, wu, wd, ln_g.reshape(1, D_MODEL), ln_b.reshape(1, D_MODEL))


MOE_TR = 256
MOE_HALF = 512
MOE_UNITS = 3
MOE_TM = MOE_UNITS * MOE_HALF
MOE_TF = 512
MOE_CHUNKS = (256, 256)
SEG_ALIGN = 16
MOE_WIN_SMALL = 128
INFO_GATE = N_EXPERTS


def _expert_cap(m):
    n_tiles = m // MOE_TR
    rows = m + (SEG_ALIGN - 1) * n_tiles + MOE_TR + MOE_HALF
    return -(-rows // MOE_TM) * MOE_TM


def _rows_computed(lens):
    return jnp.maximum(-(-lens // MOE_HALF), 1) * MOE_HALF


def _dispatch_kernel(h_ref, logit_ref, info_ref, offs_ref, lens_ref, small_ref, xs_hbm, run_ref, xbuf_ref, zbuf_ref,
                     sem_ref, zsem_ref, *, n_tiles, cap):
    t = pl.program_id(0)
    tr = MOE_TR
    slot = t % 2

    @pl.when(t == 0)
    def _():
        for e in range(N_EXPERTS):
            run_ref[e] = 0

    xb = h_ref[...].astype(BF16)
    lg = logit_ref[0:N_EXPERTS, :]
    row = lax.broadcasted_iota(jnp.int32, (N_EXPERTS, tr), 0)
    m1 = jnp.max(lg, axis=0, keepdims=True)
    i1 = jnp.min(jnp.where(lg == m1, row, N_EXPERTS), axis=0, keepdims=True)
    rest = jnp.where(row == i1, -jnp.inf, lg)
    m2 = jnp.max(rest, axis=0, keepdims=True)
    i2 = jnp.min(jnp.where(rest == m2, row, N_EXPERTS), axis=0, keepdims=True)
    e2 = jnp.exp(m2 - m1)
    p1 = 1.0 / (1.0 + e2)
    gate_t = jnp.where(row == i1, p1, 0.0) + jnp.where(row == i2, e2 * p1, 0.0)
    sel_t = jnp.where((row == i1) | (row == i2), 1.0, 0.0)
    before = (lax.broadcasted_iota(jnp.int32, (tr, tr), 0) < lax.broadcasted_iota(jnp.int32, (tr, tr), 1))
    rank_t = jnp.dot(sel_t.astype(BF16), jnp.where(before, 1.0, 0.0).astype(BF16),
                     preferred_element_type=F32)
    cnt = jnp.sum(sel_t, axis=1, keepdims=True).astype(jnp.int32)
    info_t = jnp.concatenate([rank_t, gate_t, jnp.zeros((LANES - 2 * N_EXPERTS, tr), F32)], axis=0)
    info_ref[...] = info_t.T

    small = jnp.max(cnt) <= MOE_WIN_SMALL
    small_ref[t] = small.astype(jnp.int32)

    def seg_copy(sl, e, off, rows):
        return pltpu.make_async_copy(xbuf_ref.at[sl, e, pl.ds(0, rows)],
                                     xs_hbm.at[pl.ds(pl.multiple_of(off, SEG_ALIGN), rows)], sem_ref.at[sl, e])

    def by_window(flag, fn):
        pl.when(flag)(functools.partial(fn, MOE_WIN_SMALL))
        pl.when(jnp.logical_not(flag))(functools.partial(fn, tr))

    def compact(rows):
        dst_row = lax.broadcasted_iota(jnp.int32, (rows, tr), 0).astype(F32)
        onehots = [jnp.where((rank_t[e:e + 1, :] == dst_row) & (sel_t[e:e + 1, :] > 0.0), 1.0, 0.0).astype(BF16)
                   for e in range(N_EXPERTS)]
        onehot_all = jnp.concatenate(onehots, axis=0)
        half = D_MODEL // 2
        for c in range(2):
            packed = jnp.dot(onehot_all, xb[:, c * half:(c + 1) * half], preferred_element_type=F32)
            xbuf_ref[slot, :, 0:rows, c * half:(c + 1) * half] = packed.astype(BF16).reshape(N_EXPERTS, rows, half)

    by_window(small, compact)

    def wait_previous(rows):
        for e in range(N_EXPERTS):
            seg_copy(1 - slot, e, 0, rows).wait()

    @pl.when(t > 0)
    def _():
        by_window(small_ref[jnp.maximum(t - 1, 0)] == 1, wait_previous)

    def start_all(rows):
        for e in range(N_EXPERTS):
            seg_copy(slot, e, e * cap + run_ref[e], rows).start()

    by_window(small, start_all)
    for e in range(N_EXPERTS):
        offs_ref[t * N_EXPERTS + e] = e * cap + run_ref[e]
        run_ref[e] = run_ref[e] + jnp.bitwise_and(cnt[e, 0] + (SEG_ALIGN - 1), -SEG_ALIGN)

    def wait_current(rows):
        for e in range(N_EXPERTS):
            seg_copy(slot, e, 0, rows).wait()

    @pl.when(t == n_tiles - 1)
    def _():
        zbuf_ref[...] = jnp.zeros_like(zbuf_ref)
        by_window(small, wait_current)
        tails = []
        for e in range(N_EXPERTS):
            lens_ref[e] = run_ref[e]
            tail = pl.multiple_of(e * cap + run_ref[e], SEG_ALIGN)
            tails.append(pltpu.make_async_copy(zbuf_ref, xs_hbm.at[pl.ds(tail, MOE_HALF)], zsem_ref.at[e]))
            tails[-1].start()
        for cp in tails:
            cp.wait()


def _dispatch(h, logits):
    m = h.shape[0]
    n_tiles = m // MOE_TR
    cap = _expert_cap(m)
    smem = pl.BlockSpec(memory_space=pltpu.SMEM)
    return pl.pallas_call(
        functools.partial(_dispatch_kernel, n_tiles=n_tiles, cap=cap),
        grid=(n_tiles,),
        in_specs=[pl.BlockSpec((MOE_TR, D_MODEL), lambda t: (t, 0)),
                  pl.BlockSpec((LOGIT_ROWS, MOE_TR), lambda t: (0, t))],
        out_specs=[pl.BlockSpec((MOE_TR, LANES), lambda t: (t, 0)), smem, smem, smem,
                   pl.BlockSpec(memory_space=pl.ANY)],
        out_shape=[jax.ShapeDtypeStruct((m, LANES), F32),
                   jax.ShapeDtypeStruct((n_tiles * N_EXPERTS,), jnp.int32),
                   jax.ShapeDtypeStruct((N_EXPERTS,), jnp.int32),
                   jax.ShapeDtypeStruct((n_tiles,), jnp.int32),
                   jax.ShapeDtypeStruct((N_EXPERTS * cap, D_MODEL), BF16)],
        scratch_shapes=[pltpu.SMEM((N_EXPERTS,), jnp.int32),
                        pltpu.VMEM((2, N_EXPERTS, MOE_TR, D_MODEL), BF16),
                        pltpu.VMEM((MOE_HALF, D_MODEL), BF16),
                        pltpu.SemaphoreType.DMA((2, N_EXPERTS)),
                        pltpu.SemaphoreType.DMA((N_EXPERTS,))],
        compiler_params=_params(("arbitrary",)),
        name="moe_dispatch",
    )(h, logits)


def _expert_ffn_kernel(exp_ref, blk_ref, units_ref, x_ref, wg_ref, wu_ref, wd_ref, y_ref, acc_ref):
    w = pl.program_id(0)
    f = pl.program_id(1)

    def tile(rows):
        acc = acc_ref.at[pl.ds(0, rows)]

        @pl.when(f == 0)
        def _():
            acc[...] = jnp.zeros((rows, D_MODEL), F32)

        _swiglu_chunks(acc, x_ref[0:rows, :], lambda c: wg_ref[0, :, c].astype(BF16),
                       lambda c: wu_ref[0, :, c].astype(BF16), lambda c: wd_ref[0, c, :].astype(BF16), MOE_CHUNKS)

        @pl.when(f == pl.num_programs(1) - 1)
        def _():
            y_ref[0:rows, :] = acc[...].astype(BF16)

    for units in range(1, MOE_UNITS + 1):
        pl.when(units_ref[w] == units)(functools.partial(tile, units * MOE_HALF))


def _expert_ffn(xs, exp_w, blk_w, units_w, w_gate, w_up, w_down):
    d_ff = w_gate.shape[-1]
    n_f = d_ff // MOE_TF
    fcol = lambda w, f, exp, blk, units: jnp.where(units[w] != 0, f, n_f - 1)
    grid_spec = pltpu.PrefetchScalarGridSpec(
        num_scalar_prefetch=3,
        grid=(exp_w.shape[0], n_f),
        in_specs=[pl.BlockSpec((MOE_TM, D_MODEL), lambda w, f, exp, blk, valid: (blk[w], 0)),
                  pl.BlockSpec((1, D_MODEL, MOE_TF), lambda w, f, exp, blk, valid: (exp[w], 0, fcol(w, f, exp, blk, valid))),
                  pl.BlockSpec((1, D_MODEL, MOE_TF), lambda w, f, exp, blk, valid: (exp[w], 0, fcol(w, f, exp, blk, valid))),
                  pl.BlockSpec((1, MOE_TF, D_MODEL), lambda w, f, exp, blk, valid: (exp[w], fcol(w, f, exp, blk, valid), 0))],
        out_specs=pl.BlockSpec((MOE_TM, D_MODEL), lambda w, f, exp, blk, valid: (blk[w], 0)),
        scratch_shapes=[pltpu.VMEM((MOE_TM, D_MODEL), F32)])
    return pl.pallas_call(
        _expert_ffn_kernel,
        grid_spec=grid_spec,
        out_shape=jax.ShapeDtypeStruct(xs.shape, BF16),
        compiler_params=_params(("arbitrary", "arbitrary")),
        name="moe_expert_ffn",
    )(exp_w, blk_w, units_w, xs, w_gate, w_up, w_down)


def _combine_kernel(starts_ref, shifts_ref, small_ref, info_ref, h_ref, g_ref, b_ref, ys_hbm, o_ref, ybuf_ref,
                    sem_ref, *, n_tiles):
    t = pl.program_id(0)
    tr = MOE_TR
    slot = t % 2

    def seg_copy(tile, sl, e, rows):
        off = pl.multiple_of(starts_ref[tile * N_EXPERTS + e], SEG_ALIGN)
        return pltpu.make_async_copy(ys_hbm.at[pl.ds(off, rows)], ybuf_ref.at[sl, e, pl.ds(0, rows)],
                                     sem_ref.at[sl, e])

    def by_window(tile, fn):
        flag = small_ref[tile] == 1
        pl.when(flag)(functools.partial(fn, MOE_WIN_SMALL))
        pl.when(jnp.logical_not(flag))(functools.partial(fn, tr))

    def fetch(tile, sl):
        def start(rows):
            for e in range(N_EXPERTS):
                seg_copy(tile, sl, e, rows).start()
        by_window(tile, start)

    @pl.when(t == 0)
    def _():
        fetch(0, 0)

    @pl.when(t + 1 < n_tiles)
    def _():
        fetch(jnp.minimum(t + 1, n_tiles - 1), 1 - slot)

    def gather(rows):
        for e in range(N_EXPERTS):
            seg_copy(t, slot, e, rows).wait()
        info = info_ref[...]
        src_row = lax.broadcasted_iota(jnp.int32, (tr, rows), 1).astype(F32)
        acc = ALPHA * h_ref[...]
        for e in range(N_EXPERTS):
            rank_c = info[:, e:e + 1] + shifts_ref[t * N_EXPERTS + e].astype(F32)
            gate_c = info[:, INFO_GATE + e:INFO_GATE + e + 1]
            onehot = jnp.where((rank_c == src_row) & (gate_c != 0.0), 1.0, 0.0).astype(BF16)
            acc = acc + gate_c * jnp.dot(onehot, ybuf_ref[slot, e, 0:rows, :], preferred_element_type=F32)
        o_ref[...] = _layer_norm(acc, g_ref[...], b_ref[...])

    by_window(t, gather)


def _combine(starts, shifts, small, info, h, ys, ln_g, ln_b):
    m = h.shape[0]
    n_tiles = m // MOE_TR
    grid_spec = pltpu.PrefetchScalarGridSpec(
        num_scalar_prefetch=3,
        grid=(n_tiles,),
        in_specs=[pl.BlockSpec((MOE_TR, LANES), lambda t, *_: (t, 0)),
                  pl.BlockSpec((MOE_TR, D_MODEL), lambda t, *_: (t, 0)),
                  pl.BlockSpec((1, D_MODEL), lambda t, *_: (0, 0)),
                  pl.BlockSpec((1, D_MODEL), lambda t, *_: (0, 0)),
                  pl.BlockSpec(memory_space=pl.ANY)],
        out_specs=pl.BlockSpec((MOE_TR, D_MODEL), lambda t, *_: (t, 0)),
        scratch_shapes=[pltpu.VMEM((2, N_EXPERTS, MOE_TR, D_MODEL), BF16),
                        pltpu.SemaphoreType.DMA((2, N_EXPERTS))])
    return pl.pallas_call(
        functools.partial(_combine_kernel, n_tiles=n_tiles),
        grid_spec=grid_spec,
        out_shape=jax.ShapeDtypeStruct((m, D_MODEL), F32),
        compiler_params=_params(("arbitrary",)),
        name="moe_combine_ln2",
    )(starts, shifts, small, info, h, ln_g.reshape(1, D_MODEL), ln_b.reshape(1, D_MODEL), ys)


def _work_list(lens, m):
    cap = _expert_cap(m)
    n_tiles = m // MOE_TR
    max_rows = 2 * m + (SEG_ALIGN - 1) * min(N_EXPERTS * n_tiles, 2 * m)
    w_max = -(-(max_rows + N_EXPERTS * MOE_HALF) // MOE_TM) + N_EXPERTS
    rows_e = _rows_computed(lens)
    tiles_e = -(-rows_e // MOE_TM)
    ends = jnp.cumsum(tiles_e)
    w = jnp.arange(w_max, dtype=jnp.int32)
    wc = jnp.minimum(w, ends[-1] - 1)
    exp_w = jnp.sum((wc[:, None] >= ends[None, :]).astype(jnp.int32), axis=1)
    tile_w = wc - (ends - tiles_e)[exp_w]
    blk_w = exp_w * (cap // MOE_TM) + tile_w
    units_w = jnp.minimum((rows_e[exp_w] - tile_w * MOE_TM) // MOE_HALF, MOE_UNITS)
    units_w = jnp.where(w < ends[-1], units_w, 0)
    return exp_w.astype(jnp.int32), blk_w.astype(jnp.int32), units_w.astype(jnp.int32)


def _read_windows(offs, small, lens, m):
    cap = _expert_cap(m)
    region = jnp.arange(N_EXPERTS, dtype=jnp.int32) * cap
    rel = offs.reshape(-1, N_EXPERTS) - region
    width = jnp.where(small == 1, MOE_WIN_SMALL, MOE_TR)[:, None]
    start = jnp.minimum(rel, _rows_computed(lens)[None, :] - width)
    return (start + region).reshape(-1).astype(jnp.int32), (rel - start).reshape(-1).astype(jnp.int32)


def _moe_layer(h, logits, w_gate, w_up, w_down, ln_g, ln_b):
    m = h.shape[0]
    info, offs, lens, small, xs = _dispatch(h, logits)
    exp_w, blk_w, units_w = _work_list(lens, m)
    ys = _expert_ffn(xs, exp_w, blk_w, units_w, w_gate, w_up, w_down)
    starts, shifts = _read_windows(offs, small, lens, m)
    return _combine(starts, shifts, small, info, h, ys, ln_g, ln_b)


def _pad_groups(a):
    lead = a.shape[:-1]
    a = a.reshape(*lead, SSD_GROUPS, GROUP_W)
    a = jnp.pad(a, [(0, 0)] * len(lead) + [(0, 0), (0, GROUP_PAD - GROUP_W)])
    return a.reshape(*lead, SSD_PAD)


def _head_lanes(v):
    return jnp.pad(v, (0, LANES - SSD_HEADS)).reshape(1, LANES)


D_IN = 3 * ATT_W + SSD_W + (SSD_W + 2 * SSD_GROUPS * SSD_STATE) + SSD_HEADS + 2 * GM_W
PERM_ROWS = 256


def _permute_w_kernel(w_ref, o_ref):
    src_z = 3 * ATT_W
    src_xs = src_z + SSD_W
    src_b = src_xs + SSD_W
    src_c = src_b + SSD_GROUPS * SSD_STATE
    src_dt = src_c + SSD_GROUPS * SSD_STATE
    src_gu = src_dt + SSD_HEADS
    src_gv = src_gu + GM_W
    o_ref[...] = jnp.zeros_like(o_ref)

    def put(dst, src, width):
        o_ref[:, dst:dst + width] = w_ref[0, :, src:src + width].astype(BF16)

    for g in range(SSD_GROUPS):
        put(COL_Z + g * GROUP_PAD, src_z + g * GROUP_W, GROUP_W)
        put(COL_XS + g * GROUP_PAD, src_xs + g * GROUP_W, GROUP_W)
    put(COL_B, src_b, SSD_GROUPS * SSD_STATE)
    put(COL_C, src_c, SSD_GROUPS * SSD_STATE)
    put(COL_GU, src_gu, GM_W)
    put(COL_GV, src_gv, GM_W)
    put(COL_Q, 0, ATT_W)
    put(COL_K, ATT_W, ATT_W)
    put(COL_V, 2 * ATT_W, ATT_W)
    put(COL_DT, src_dt, SSD_HEADS)


def _permute_w_in(w_in, layer):
    return pl.pallas_call(
        _permute_w_kernel,
        grid=(D_MODEL // PERM_ROWS,),
        in_specs=[pl.BlockSpec((1, PERM_ROWS, D_IN), lambda i: (layer, i, 0))],
        out_specs=pl.BlockSpec((PERM_ROWS, D_W), lambda i: (i, 0)),
        out_shape=jax.ShapeDtypeStruct((D_MODEL, D_W), BF16),
        compiler_params=_params(("parallel",)),
        name="permute_w_in",
    )(w_in)


def _layer_params(i, w_in, conv_w, conv_b, dt_bias, a_log, d_skip, ssd_norm_g, att_norm_g, gm_ln_g, gm_ln_b,
                  gm_w_s, gm_b_s, gm_norm_g, w_out, ln1_g, ln1_b):
    cw, cb = conv_w[i], conv_b[i]
    nbc = SSD_GROUPS * SSD_STATE
    wo = w_out[i]
    return {
        "w_in": _permute_w_in(w_in, i),
        "conv_w_xs": _pad_groups(cw[:, :SSD_W]),
        "conv_w_b": cw[:, SSD_W:SSD_W + nbc],
        "conv_w_c": cw[:, SSD_W + nbc:],
        "conv_b_xs": _pad_groups(cb[:SSD_W]).reshape(1, SSD_PAD),
        "conv_b_b": cb[SSD_W:SSD_W + nbc].reshape(1, nbc),
        "conv_b_c": cb[SSD_W + nbc:].reshape(1, nbc),
        "dt_bias": _head_lanes(dt_bias[i]),
        "a_log": _head_lanes(a_log[i]),
        "d_skip": _pad_groups(jnp.repeat(d_skip[i], SSD_HEAD_DIM)).reshape(1, SSD_PAD),
        "ssd_norm_g": _pad_groups(ssd_norm_g[i]).reshape(1, SSD_PAD),
        "att_norm_g": att_norm_g[i].reshape(1, ATT_W),
        "gm_ln_g": gm_ln_g[i].reshape(1, GM_W),
        "gm_ln_b": gm_ln_b[i].reshape(1, GM_W),
        "gm_w_s": gm_w_s[i],
        "gm_b_s": jnp.repeat(gm_b_s[i].T, GM_W // GM_GROUPS, axis=1),
        "gm_norm_g": gm_norm_g[i].reshape(1, GM_W),
        "w_out_att": wo[:ATT_W].astype(BF16),
        "w_out_ssd": _pad_groups(wo[ATT_W:ATT_W + SSD_W].T).T.astype(BF16),
        "w_out_gm": wo[ATT_W + SSD_W:].astype(BF16),
        "ln1_g": ln1_g[i].reshape(1, D_MODEL),
        "ln1_b": ln1_b[i].reshape(1, D_MODEL),
    }


def _rope_tables(positions):
    inv = ROPE_THETA ** (-jnp.arange(0, ROPE_DIM, 2, dtype=F32) / ROPE_DIM)
    d = jnp.arange(LANES) % HEAD_DIM
    inv_lane = jnp.where(d < ROPE_DIM, inv[d % ROPE_HALF], 0.0)
    ang = positions.astype(F32)[..., None] * inv_lane
    cos, sin = jnp.cos(ang), jnp.sin(ang)
    return cos, jnp.where(d < ROPE_HALF, -sin, 0.0), jnp.where(d >= ROPE_HALF, sin, 0.0)


def kernel(x, positions, ln_in_g, ln_in_b, w_in, conv_w, conv_b, dt_bias, a_log, d_skip, ssd_norm_g, att_norm_g, gm_ln_g, gm_ln_b, gm_w_s, gm_b_s, gm_norm_g, w_out, ln1_g, ln1_b, ln2_g, ln2_b, ffn_w_gate, ffn_w_up, ffn_w_down, router_w, moe_w_gate, moe_w_up, moe_w_down):
    bsz, seq, _ = x.shape
    m = bsz * seq
    rope_c, rope_sa, rope_sb = _rope_tables(positions)
    h = x.reshape(m, D_MODEL)
    for i in range(DEPTH):
        p = _layer_params(i, w_in, conv_w, conv_b, dt_bias, a_log, d_skip, ssd_norm_g, att_norm_g, gm_ln_g,
                          gm_ln_b, gm_w_s, gm_b_s, gm_norm_g, w_out, ln1_g, ln1_b)
        if i == 0:
            proj, gm, h = _in_proj(h, p["w_in"], p, ln=(ln_in_g, ln_in_b))
        else:
            proj, gm = _in_proj(h, p["w_in"], p)
        proj3 = proj.reshape(bsz, seq, D_PROJ)
        att = _attention(proj3, rope_c, rope_sa, rope_sb).reshape(m, ATT_W)
        ssd = _ssd(proj3, p).reshape(m, SSD_PAD)
        j = i // 2
        if i % 2 == 0:
            (h,) = _out_proj(att, ssd, gm, h, p)
            h = _ffn(h, ffn_w_gate[j].astype(BF16), ffn_w_up[j].astype(BF16), ffn_w_down[j].astype(BF16),
                     ln2_g[i], ln2_b[i])
        else:
            h, logits = _out_proj(att, ssd, gm, h, p, w_router=router_w[j])
            h = _moe_layer(h, logits, moe_w_gate[j], moe_w_up[j], moe_w_down[j], ln2_g[i], ln2_b[i])
    return h.reshape(bsz, seq, D_MODEL)
```

```python
import functools

import jax
import jax.numpy as jnp
from jax import lax
from jax.experimental import pallas as pl
from jax.experimental.pallas import tpu as pltpu

F32 = jnp.float32
BF16 = jnp.bfloat16
HIGHEST = lax.Precision.HIGHEST

D_MODEL = 1024
ATT_HEADS = 6
HEAD_DIM = 64
ATT_W = ATT_HEADS * HEAD_DIM
ROPE_DIM = HEAD_DIM // 4
ROPE_HALF = ROPE_DIM // 2
ROPE_THETA = 500000.0
MOBA_BLOCK = 256
MOBA_TOPK = 3
SSD_HEADS = 6
SSD_HEAD_DIM = 64
SSD_W = SSD_HEADS * SSD_HEAD_DIM
SSD_GROUPS = 2
SSD_STATE = 128
SSD_CONV = 4
SSD_CHUNK = 128
GM_GROUPS = 4
GM_W = 256
GM_CHUNK = 128
N_EXPERTS = 8
DEPTH = 2
ALPHA = (2.0 * DEPTH) ** 0.25
NEG = -1e30
LN_EPS = 1e-5

LANES = 128
GROUP_W = SSD_W // SSD_GROUPS
GROUP_PAD = 256
SSD_PAD = SSD_GROUPS * GROUP_PAD
HEADS_PER_GROUP = SSD_HEADS // SSD_GROUPS

COL_Z = 0
COL_XS = 512
COL_B = 1024
COL_C = 1280
COL_Q = 1536
COL_K = 1920
COL_V = 2304
COL_DT = 2688
D_PROJ = 2816
COL_GU = 2816
COL_GV = 3072
D_W = 3328

VMEM_LIMIT = 56 * 1024 * 1024


def _params(sem, vmem=VMEM_LIMIT):
    return pltpu.CompilerParams(dimension_semantics=sem, vmem_limit_bytes=vmem)


def _layer_norm(x, g, b):
    mu = jnp.mean(x, axis=-1, keepdims=True)
    xc = x - mu
    var = jnp.mean(xc * xc, axis=-1, keepdims=True)
    return xc * lax.rsqrt(var + LN_EPS) * g + b


def _silu(x):
    half = 0.5 * x
    return half + half * jnp.tanh(half)


def _split_bf16(x):
    hi = x.astype(BF16)
    return hi, (x - hi.astype(F32)).astype(BF16)


def _dot_bf16x3(a, b):
    a_hi, a_lo = _split_bf16(a)
    b_hi, b_lo = _split_bf16(b)
    dot = functools.partial(jnp.dot, preferred_element_type=F32)
    return dot(a_hi, b_hi) + (dot(a_lo, b_hi) + dot(a_hi, b_lo))


def _gelu_tanh(x):
    return 0.5 * x * (1.0 + jnp.tanh(0.7978845608028654 * (x + 0.044715 * x * x * x)))


PROJ_TM = 512
PROJ_CH = 256
PROJ_TAIL = 4


def _in_proj_kernel(*refs, apply_ln):
    if apply_ln:
        x_ref, g_ref, b_ref, w_ref, lng_ref, lnb_ref, ws_ref, bs_ref, ng_ref, proj_ref, gm_ref, h_ref = refs
        x = _layer_norm(x_ref[...], g_ref[...], b_ref[...])
        h_ref[...] = x
    else:
        x_ref, w_ref, lng_ref, lnb_ref, ws_ref, bs_ref, ng_ref, proj_ref, gm_ref = refs
        x = x_ref[...]
    xb = x.astype(BF16)

    def project(j):
        cols = slice(j * PROJ_CH, (j + 1) * PROJ_CH)
        proj_ref[:, cols] = jnp.dot(xb, w_ref[:, cols], preferred_element_type=F32)

    gu = jnp.dot(xb, w_ref[:, COL_GU:COL_GU + GM_W], preferred_element_type=F32)
    gv = jnp.dot(xb, w_ref[:, COL_GV:COL_GV + GM_W], preferred_element_type=F32)
    u, vb = _gmlp_activate(gu, gv, lng_ref, lnb_ref)
    n_chunks = D_PROJ // PROJ_CH
    for j in range(n_chunks - PROJ_TAIL):
        project(j)
    _gmlp_mix(u, vb, ws_ref, bs_ref, ng_ref, gm_ref)
    for j in range(n_chunks - PROJ_TAIL, n_chunks):
        project(j)


def _in_proj(x, w, p, ln=None):
    m = x.shape[0]
    row = lambda i: (i, 0)
    const = lambda i: (0, 0)
    whole = lambda a: pl.BlockSpec(a.shape, lambda i: (0,) * a.ndim)
    in_specs = [pl.BlockSpec((PROJ_TM, D_MODEL), row)]
    args = [x]
    out_shape = [jax.ShapeDtypeStruct((m, D_PROJ), F32), jax.ShapeDtypeStruct((m, GM_W), BF16)]
    out_specs = [pl.BlockSpec((PROJ_TM, D_PROJ), row), pl.BlockSpec((PROJ_TM, GM_W), row)]
    if ln is not None:
        in_specs += [pl.BlockSpec((1, D_MODEL), const), pl.BlockSpec((1, D_MODEL), const)]
        args += [ln[0].reshape(1, D_MODEL), ln[1].reshape(1, D_MODEL)]
        out_shape.append(jax.ShapeDtypeStruct((m, D_MODEL), F32))
        out_specs.append(pl.BlockSpec((PROJ_TM, D_MODEL), row))
    in_specs.append(pl.BlockSpec((D_MODEL, D_W), const))
    args.append(w)
    gm_small = [p["gm_ln_g"], p["gm_ln_b"], p["gm_w_s"], p["gm_b_s"], p["gm_norm_g"]]
    in_specs += [whole(a) for a in gm_small]
    args += gm_small
    return pl.pallas_call(
        functools.partial(_in_proj_kernel, apply_ln=ln is not None),
        grid=(m // PROJ_TM,),
        in_specs=in_specs,
        out_specs=out_specs,
        out_shape=out_shape,
        compiler_params=_params(("parallel",)),
        name="in_proj",
    )(*args)


VT_ROWS = HEAD_DIM + 16
LOG2E = 1.4426950408889634


ATT_VALUE_LAG = 6


def _attn_kernel(q_ref, k_ref, v_ref, c_ref, sa_ref, sb_ref, o_ref, kb_ref, vt_ref, *, seq):
    nb = seq // MOBA_BLOCK
    tq = MOBA_BLOCK
    lane = lax.broadcasted_iota(jnp.int32, (1, LANES), 1)

    def rope(x, rows):
        return (x * c_ref[0, rows, :] + pltpu.roll(x, LANES - ROPE_HALF, 1) * sa_ref[0, rows, :]
                + pltpu.roll(x, ROPE_HALF, 1) * sb_ref[0, rows, :])

    k = rope(k_ref[0], slice(None))
    kb_ref[...] = k.astype(BF16)
    k_mean = jnp.mean(k.reshape(nb, MOBA_BLOCK, LANES), axis=1)
    v_t = v_ref[0].T
    ones_rows = jnp.where(lax.broadcasted_iota(jnp.int32, (VT_ROWS - HEAD_DIM, seq), 0) == 0, 1.0, 0.0)
    for hh in range(2):
        vt_ref[hh] = jnp.concatenate([v_t[hh * HEAD_DIM:(hh + 1) * HEAD_DIM], ones_rows], axis=0).astype(BF16)

    qscale = HEAD_DIM ** -0.5 * LOG2E
    causal_t = (lax.broadcasted_iota(jnp.int32, (tq, tq), 0) <= lax.broadcasted_iota(jnp.int32, (tq, tq), 1))
    blk_row = lax.broadcasted_iota(jnp.int32, (nb, tq), 0)

    chain_q = {}
    chain_bias = {}
    chain_m = {}
    chain_o = {}
    q_tiles = {}

    def start_chain(j, hh):
        if hh == 0:
            rows = slice(j * tq, (j + 1) * tq)
            q_tiles[j] = rope(q_ref[0, rows, :], rows) * qscale
        q = q_tiles[j] if hh == 0 else q_tiles.pop(j)
        head_lanes = (lane < HEAD_DIM) if hh == 0 else (lane >= HEAD_DIM)
        qh_t = jnp.where(head_lanes, q, 0.0).T
        chain_q[(j, hh)] = qh_t.astype(BF16)
        chain_bias[(j, hh)] = None
        if j > MOBA_TOPK:
            gate_t = _dot_bf16x3(k_mean, qh_t)
            rank = jnp.zeros((nb, tq), F32)
            for m in range(j):
                g_m = gate_t[m:m + 1, :]
                beats = (g_m > gate_t) | ((g_m == gate_t) & (blk_row > m))
                rank = rank + jnp.where(beats & (blk_row != m), 1.0, 0.0)
            bias = jnp.where((blk_row < j) & (rank >= MOBA_TOPK), NEG, 0.0)
            chain_bias[(j, hh)] = [bias[n:n + 1, :] for n in range(j)]

    def scores(j, hh, n):
        return jnp.dot(kb_ref[n * tq:(n + 1) * tq, :], chain_q[(j, hh)], preferred_element_type=F32)

    def probs(j, hh, n, s_t):
        key = (j, hh)
        if n == j:
            s_t = jnp.where(causal_t, s_t, NEG)
            m_new = jnp.max(s_t, axis=0, keepdims=True)
            alpha = None
            shift = m_new
        else:
            bias_n = 0.0 if chain_bias[key] is None else chain_bias[key][n]
            m_new = jnp.maximum(chain_m[key], jnp.max(s_t, axis=0, keepdims=True) + bias_n)
            alpha = jnp.exp2(chain_m[key] - m_new)
            shift = m_new - bias_n
        chain_m[key] = m_new
        return jnp.exp2(s_t - shift).astype(BF16), alpha

    def values(j, hh, n, p_t, alpha):
        key = (j, hh)
        part = jnp.dot(vt_ref[hh, :, n * tq:(n + 1) * tq], p_t, preferred_element_type=F32)
        chain_o[key] = part if alpha is None else chain_o[key] * alpha + part

    def finish(j):
        halves = [chain_o.pop((j, hh)) for hh in range(2)]
        out_t = jnp.concatenate([o[0:HEAD_DIM] / o[HEAD_DIM:HEAD_DIM + 1] for o in halves], axis=0)
        o_ref[0, j * tq:(j + 1) * tq, :] = out_t.T

    value_lag = ATT_VALUE_LAG
    units = []
    for j in range(nb):
        for n in [j] + list(range(j)):
            units += [(j, 0, n), (j, 1, n)]
    s_state = {}
    p_state = {}
    for step in range(len(units) + value_lag):
        if step >= value_lag:
            j, hh, n = unit = units[step - value_lag]
            values(j, hh, n, *p_state.pop(unit))
            last_block = n == (j - 1 if j > 0 else 0)
            if hh == 1 and last_block:
                finish(j)
        if step < len(units):
            j, hh, n = unit = units[step]
            if n == j:
                start_chain(j, hh)
            s_state[unit] = scores(j, hh, n)
        if 1 <= step <= len(units):
            j, hh, n = unit = units[step - 1]
            p_state[unit] = probs(j, hh, n, s_state.pop(unit))


def _attention(proj3, rope_c, rope_sa, rope_sb):
    bsz, seq, _ = proj3.shape
    blk = (1, seq, LANES)
    col = lambda base: (lambda b, p: (b, 0, base // LANES + p))
    tab = lambda b, p: (b, 0, 0)
    return pl.pallas_call(
        functools.partial(_attn_kernel, seq=seq),
        grid=(bsz, ATT_W // LANES),
        in_specs=[pl.BlockSpec(blk, col(COL_Q)), pl.BlockSpec(blk, col(COL_K)), pl.BlockSpec(blk, col(COL_V)),
                  pl.BlockSpec(blk, tab), pl.BlockSpec(blk, tab), pl.BlockSpec(blk, tab)],
        out_specs=pl.BlockSpec(blk, lambda b, p: (b, 0, p)),
        out_shape=jax.ShapeDtypeStruct((bsz, seq, ATT_W), F32),
        scratch_shapes=[pltpu.VMEM((seq, LANES), BF16), pltpu.VMEM((2, VT_ROWS, seq), BF16)],
        compiler_params=_params(("parallel", "arbitrary")),
        name="moba_attention",
    )(proj3, proj3, proj3, rope_c, rope_sa, rope_sb)


SSD_UNROLL = 4
assert HEADS_PER_GROUP == 3 and GROUP_PAD == 2 * LANES and SSD_HEAD_DIM * 2 == LANES


def _expand_heads(small):
    low = lax.broadcasted_iota(jnp.int32, (1, LANES), 1) < SSD_HEAD_DIM
    slabs = []
    for g in range(SSD_GROUPS):
        h0 = g * HEADS_PER_GROUP
        slabs.append(jnp.where(low, small[:, h0:h0 + 1], small[:, h0 + 1:h0 + 2]))
        slabs.append(jnp.where(low, small[:, h0 + 2:h0 + 3], 0.0))
    return jnp.concatenate(slabs, axis=1)


def _ssd_kernel(z_ref, xs_ref, b_ref, c_ref, dt_ref, cwx_ref, cwb_ref, cwc_ref, cbx_ref, cbb_ref, cbc_ref,
                dtb_ref, alog_ref, dskip_ref, ng_ref, o_ref, state_ref, *, seq):
    q = SSD_CHUNK
    n = SSD_STATE
    state_ref[...] = jnp.zeros_like(state_ref)
    row_i = lax.broadcasted_iota(jnp.int32, (q, q), 0)
    col_i = lax.broadcasted_iota(jnp.int32, (q, q), 1)
    lower = row_i >= col_i
    tril = jnp.where(lower, 1.0, 0.0)
    slab = lax.broadcasted_iota(jnp.int32, (1, GROUP_PAD), 1) // SSD_HEAD_DIM
    a_neg = -jnp.exp(alog_ref[...])

    def conv_silu(ref, w_ref, bias_ref, c):
        start = pl.multiple_of(c * q, q)
        cur = ref[0, pl.ds(start, q), :]
        prev_start = pl.multiple_of(jnp.maximum(c * q - 8, 0), 8)
        prev = jnp.where(c > 0, ref[0, pl.ds(prev_start, 8), :], 0.0)
        full = jnp.concatenate([prev, cur], axis=0)
        y = bias_ref[...] + w_ref[SSD_CONV - 1:SSD_CONV, :] * cur
        for i in range(SSD_CONV - 1):
            shift = SSD_CONV - 1 - i
            y = y + w_ref[i:i + 1, :] * pltpu.roll(full, shift, 0)[8:, :]
        return _silu(y)

    def chunk(c, carry):
        start = pl.multiple_of(c * q, q)
        xs = conv_silu(xs_ref, cwx_ref, cbx_ref, c)
        bm = conv_silu(b_ref, cwb_ref, cbb_ref, c)
        cm = conv_silu(c_ref, cwc_ref, cbc_ref, c)
        x = dt_ref[0, pl.ds(start, q), :] + dtb_ref[...]
        dt = jnp.maximum(x, 0.0) + jnp.log(1.0 + jnp.exp(-jnp.abs(x)))
        a = dt * a_neg
        a_cum = jnp.dot(tril, a, precision=HIGHEST, preferred_element_type=F32)
        a_cum_t = a_cum.T
        dt_e = _expand_heads(dt)
        acum_e = _expand_heads(a_cum)
        alast_e = acum_e[q - 1:q, :]
        xdt = xs * dt_e
        decay_e = jnp.exp(alast_e - acum_e)
        y_parts = []
        for g in range(SSD_GROUPS):
            gl = slice(g * GROUP_PAD, (g + 1) * GROUP_PAD)
            nl = slice(g * n, (g + 1) * n)
            b_g = bm[:, nl].astype(BF16)
            b_gt = bm[:, nl].T.astype(BF16)
            c_g = cm[:, nl].astype(BF16)
            cb = lax.dot_general(c_g, b_g, (((1,), (1,)), ((), ())), preferred_element_type=F32)
            xdt_g = xdt[:, gl]
            xdt_b = xdt_g.astype(BF16)
            y_g = jnp.zeros((q, GROUP_PAD), F32)
            for r in range(HEADS_PER_GROUP):
                h = g * HEADS_PER_GROUP + r
                seg = a_cum[:, h:h + 1] - a_cum_t[h:h + 1, :]
                l_h = jnp.where(lower, jnp.exp(jnp.minimum(seg, 0.0)), 0.0)
                y_h = jnp.dot((cb * l_h).astype(BF16), xdt_b, preferred_element_type=F32)
                y_g = jnp.where(slab == r, y_h, y_g)
            prev = state_ref[g]
            y_off = jnp.dot(c_g, prev.astype(BF16), preferred_element_type=F32) * jnp.exp(acum_e[:, gl])
            new = jnp.dot(b_gt, (xdt_g * decay_e[:, gl]).astype(BF16), preferred_element_type=F32)
            state_ref[g] = prev * jnp.exp(alast_e[:, gl]) + new
            y_parts.append(y_g + y_off)
        y = jnp.concatenate(y_parts, axis=1) + dskip_ref[...] * xs
        yz = y * _silu(z_ref[0, pl.ds(start, q), :])
        outs = []
        for g in range(SSD_GROUPS):
            yg = yz[:, g * GROUP_PAD:(g + 1) * GROUP_PAD]
            ms = jnp.sum(yg * yg, axis=-1, keepdims=True) * (1.0 / GROUP_W)
            outs.append(yg * lax.rsqrt(ms + LN_EPS))
        o_ref[0, pl.ds(start, q), :] = (jnp.concatenate(outs, axis=1) * ng_ref[...]).astype(BF16)
        return carry

    lax.fori_loop(0, seq // q, chunk, 0, unroll=SSD_UNROLL)


def _ssd(proj3, p):
    bsz, seq, _ = proj3.shape
    slab = lambda width, base: pl.BlockSpec((1, seq, width), lambda b: (b, 0, base // width))
    const2 = lambda a: pl.BlockSpec(a.shape, lambda b: (0, 0))
    small = [p["conv_w_xs"], p["conv_w_b"], p["conv_w_c"], p["conv_b_xs"], p["conv_b_b"], p["conv_b_c"],
             p["dt_bias"], p["a_log"], p["d_skip"], p["ssd_norm_g"]]
    return pl.pallas_call(
        functools.partial(_ssd_kernel, seq=seq),
        grid=(bsz,),
        in_specs=[slab(SSD_PAD, COL_Z), slab(SSD_PAD, COL_XS), slab(2 * SSD_STATE, COL_B),
                  slab(2 * SSD_STATE, COL_C), slab(LANES, COL_DT)] + [const2(a) for a in small],
        out_specs=pl.BlockSpec((1, seq, SSD_PAD), lambda b: (b, 0, 0)),
        out_shape=jax.ShapeDtypeStruct((bsz, seq, SSD_PAD), BF16),
        scratch_shapes=[pltpu.VMEM((SSD_GROUPS, SSD_STATE, GROUP_PAD), F32)],
        compiler_params=_params(("parallel",)),
        name="ssd",
    )(proj3, proj3, proj3, proj3, proj3, *small)


def _gmlp_activate(gu, gv, lng_ref, lnb_ref):
    return _gelu_tanh(gu), _layer_norm(_gelu_tanh(gv), lng_ref[...], lnb_ref[...]).astype(BF16)


def _gmlp_mix(u, vb, ws_ref, bs_ref, ng_ref, o_ref):
    q = GM_CHUNK
    row_i = lax.broadcasted_iota(jnp.int32, (q, q), 0)
    col_i = lax.broadcasted_iota(jnp.int32, (q, q), 1)
    lower = row_i >= col_i
    grp = lax.broadcasted_iota(jnp.int32, (1, GM_W), 1) // (GM_W // GM_GROUPS)
    w = [jnp.where(lower, ws_ref[g], 0.0).astype(BF16) for g in range(GM_GROUPS)]
    for c in range(u.shape[0] // q):
        rows = slice(c * q, (c + 1) * q)
        mixed = jnp.zeros((q, GM_W), F32)
        for g in range(GM_GROUPS):
            mixed = jnp.where(grp == g, jnp.dot(w[g], vb[rows, :], preferred_element_type=F32), mixed)
        gm = u[rows, :] * (mixed + bs_ref[...])
        ms = jnp.mean(gm * gm, axis=-1, keepdims=True)
        o_ref[rows, :] = (gm * lax.rsqrt(ms + LN_EPS) * ng_ref[...]).astype(BF16)


OUT_TM = 1024
LOGIT_ROWS = 16


def _out_proj_kernel(att_ref, ssd_ref, gm_ref, h_ref, ag_ref, wa_ref, ws_ref, wg_ref, g_ref, b_ref, o_ref,
                     router=None):
    att = att_ref[...]
    ms = jnp.mean(att * att, axis=-1, keepdims=True)
    att = att * lax.rsqrt(ms + LN_EPS) * ag_ref[...]
    mix = jnp.dot(att.astype(BF16), wa_ref[...], preferred_element_type=F32)
    mix = mix + jnp.dot(ssd_ref[...], ws_ref[...], preferred_element_type=F32)
    mix = mix + jnp.dot(gm_ref[...], wg_ref[...], preferred_element_type=F32)
    h1 = _layer_norm(ALPHA * h_ref[...] + mix, g_ref[...], b_ref[...])
    o_ref[...] = h1
    if router:
        wr_ref, logit_ref = router
        logit_ref[...] = _dot_bf16x3(wr_ref[...], h1.T)


def _out_proj_kernel_routed(att_ref, ssd_ref, gm_ref, h_ref, ag_ref, wa_ref, ws_ref, wg_ref, g_ref, b_ref, wr_ref,
                            o_ref, logit_ref):
    _out_proj_kernel(att_ref, ssd_ref, gm_ref, h_ref, ag_ref, wa_ref, ws_ref, wg_ref, g_ref, b_ref, o_ref,
                     router=(wr_ref, logit_ref))


def _out_proj(att, ssd, gm, h, p, w_router=None):
    m = h.shape[0]
    row = lambda i: (i, 0)
    const = lambda a: pl.BlockSpec(a.shape, lambda i: (0, 0))
    small = [p["att_norm_g"], p["w_out_att"], p["w_out_ssd"], p["w_out_gm"], p["ln1_g"], p["ln1_b"]]
    out_specs = [pl.BlockSpec((OUT_TM, D_MODEL), row)]
    out_shape = [jax.ShapeDtypeStruct((m, D_MODEL), F32)]
    body = _out_proj_kernel
    if w_router is not None:
        small.append(jnp.pad(w_router.T, ((0, LOGIT_ROWS - N_EXPERTS), (0, 0))))
        out_specs.append(pl.BlockSpec((LOGIT_ROWS, OUT_TM), lambda i: (0, i)))
        out_shape.append(jax.ShapeDtypeStruct((LOGIT_ROWS, m), F32))
        body = _out_proj_kernel_routed
    return pl.pallas_call(
        body,
        grid=(m // OUT_TM,),
        in_specs=[pl.BlockSpec((OUT_TM, ATT_W), row), pl.BlockSpec((OUT_TM, SSD_PAD), row),
                  pl.BlockSpec((OUT_TM, GM_W), row), pl.BlockSpec((OUT_TM, D_MODEL), row)]
                 + [const(a) for a in small],
        out_specs=out_specs,
        out_shape=out_shape,
        compiler_params=_params(("parallel",)),
        name="out_proj_ln1",
    )(att, ssd, gm, h, *small)


FFN_TM = 1024


def _swiglu_chunks(acc_ref, xb, w_gate, w_up, w_down, widths):
    dot = functools.partial(jnp.dot, preferred_element_type=F32)
    bounds = [sum(widths[:c]) for c in range(len(widths) + 1)]
    chunk = lambda c: slice(bounds[c], bounds[c + 1])
    up = lambda c: (dot(xb, w_gate(chunk(c))), dot(xb, w_up(chunk(c))))
    pending = up(0)
    for c in range(len(widths)):
        hg, hu = pending
        if c + 1 < len(widths):
            pending = up(c + 1)
        acc_ref[...] += dot((_silu(hg) * hu).astype(BF16), w_down(chunk(c)))


FFN_CHUNK = 512


def _ffn_kernel(x_ref, wg_ref, wu_ref, wd_ref, g_ref, b_ref, o_ref, acc_ref):
    d_ff = wg_ref.shape[-1]
    widths = (FFN_CHUNK,) * (d_ff // FFN_CHUNK) + ((d_ff % FFN_CHUNK,) if d_ff % FFN_CHUNK else ())
    x = x_ref[...]
    acc_ref[...] = ALPHA * x
    _swiglu_chunks(acc_ref, x.astype(BF16), lambda c: wg_ref[:, c], lambda c: wu_ref[:, c],
                   lambda c: wd_ref[c, :], widths)
    o_ref[...] = _layer_norm(acc_ref[...], g_ref[...], b_ref[...])


def _ffn(x, wg, wu, wd, ln_g, ln_b):
    m = x.shape[0]
    resident = lambda a: pl.BlockSpec(a.shape, lambda i: (0, 0), pipeline_mode=pl.Buffered(1))
    return pl.pallas_call(
        _ffn_kernel,
        grid=(m // FFN_TM,),
        in_specs=[pl.BlockSpec((FFN_TM, D_MODEL), lambda i: (i, 0)), resident(wg), resident(wu), resident(wd),
                  pl.BlockSpec((1, D_MODEL), lambda i: (0, 0)),
                  pl.BlockSpec((1, D_MODEL), lambda i: (0, 0))],
        out_specs=pl.BlockSpec((FFN_TM, D_MODEL), lambda i: (i, 0)),
        out_shape=jax.ShapeDtypeStruct((m, D_MODEL), F32),
        scratch_shapes=[pltpu.VMEM((FFN_TM, D_MODEL), F32)],
        compiler_params=_params(("parallel",)),
        name="ffn_ln2",
    )(x, wg, wu, wd, ln_g.reshape(1, D_MODEL), ln_b.reshape(1, D_MODEL))


MOE_TR = 256
MOE_HALF = 512
MOE_UNITS = 3
MOE_TM = MOE_UNITS * MOE_HALF
MOE_TF = 512
MOE_CHUNKS = (256, 256)
SEG_ALIGN = 16
MOE_WIN_SMALL = 96
INFO_GATE = N_EXPERTS


def _expert_cap(m):
    n_tiles = m // MOE_TR
    rows = m + (SEG_ALIGN - 1) * n_tiles + MOE_TR + MOE_HALF
    return -(-rows // MOE_TM) * MOE_TM


def _rows_computed(lens):
    return jnp.maximum(-(-lens // MOE_HALF), 1) * MOE_HALF


def _dispatch_kernel(h_ref, logit_ref, info_ref, offs_ref, lens_ref, small_ref, xs_hbm, run_ref, xbuf_ref, zbuf_ref,
                     sem_ref, zsem_ref, *, n_tiles, cap):
    t = pl.program_id(0)
    tr = MOE_TR
    slot = t % 2

    @pl.when(t == 0)
    def _():
        for e in range(N_EXPERTS):
            run_ref[e] = 0

    xb = h_ref[...].astype(BF16)
    lg = logit_ref[0:N_EXPERTS, :]
    row = lax.broadcasted_iota(jnp.int32, (N_EXPERTS, tr), 0)
    m1 = jnp.max(lg, axis=0, keepdims=True)
    i1 = jnp.min(jnp.where(lg == m1, row, N_EXPERTS), axis=0, keepdims=True)
    rest = jnp.where(row == i1, -jnp.inf, lg)
    m2 = jnp.max(rest, axis=0, keepdims=True)
    i2 = jnp.min(jnp.where(rest == m2, row, N_EXPERTS), axis=0, keepdims=True)
    e2 = jnp.exp(m2 - m1)
    p1 = 1.0 / (1.0 + e2)
    gate_t = jnp.where(row == i1, p1, 0.0) + jnp.where(row == i2, e2 * p1, 0.0)
    sel_t = jnp.where((row == i1) | (row == i2), 1.0, 0.0)
    before = (lax.broadcasted_iota(jnp.int32, (tr, tr), 0) < lax.broadcasted_iota(jnp.int32, (tr, tr), 1))
    rank_t = jnp.dot(sel_t.astype(BF16), jnp.where(before, 1.0, 0.0).astype(BF16),
                     preferred_element_type=F32)
    cnt = jnp.sum(sel_t, axis=1, keepdims=True).astype(jnp.int32)
    info_t = jnp.concatenate([rank_t, gate_t, jnp.zeros((LANES - 2 * N_EXPERTS, tr), F32)], axis=0)
    info_ref[...] = info_t.T

    small = jnp.max(cnt) <= MOE_WIN_SMALL
    small_ref[t] = small.astype(jnp.int32)

    def seg_copy(sl, e, off, rows):
        return pltpu.make_async_copy(xbuf_ref.at[sl, e, pl.ds(0, rows)],
                                     xs_hbm.at[pl.ds(pl.multiple_of(off, SEG_ALIGN), rows)], sem_ref.at[sl, e])

    def by_window(flag, fn):
        pl.when(flag)(functools.partial(fn, MOE_WIN_SMALL))
        pl.when(jnp.logical_not(flag))(functools.partial(fn, tr))

    def compact(rows):
        dst_row = lax.broadcasted_iota(jnp.int32, (rows, tr), 0).astype(F32)
        onehots = [jnp.where((rank_t[e:e + 1, :] == dst_row) & (sel_t[e:e + 1, :] > 0.0), 1.0, 0.0).astype(BF16)
                   for e in range(N_EXPERTS)]
        onehot_all = jnp.concatenate(onehots, axis=0)
        half = D_MODEL // 2
        for c in range(2):
            packed = jnp.dot(onehot_all, xb[:, c * half:(c + 1) * half], preferred_element_type=F32)
            xbuf_ref[slot, :, 0:rows, c * half:(c + 1) * half] = packed.astype(BF16).reshape(N_EXPERTS, rows, half)

    by_window(small, compact)

    def wait_previous(rows):
        for e in range(N_EXPERTS):
            seg_copy(1 - slot, e, 0, rows).wait()

    @pl.when(t > 0)
    def _():
        by_window(small_ref[jnp.maximum(t - 1, 0)] == 1, wait_previous)

    def start_all(rows):
        for e in range(N_EXPERTS):
            seg_copy(slot, e, e * cap + run_ref[e], rows).start()

    by_window(small, start_all)
    for e in range(N_EXPERTS):
        offs_ref[t * N_EXPERTS + e] = e * cap + run_ref[e]
        run_ref[e] = run_ref[e] + jnp.bitwise_and(cnt[e, 0] + (SEG_ALIGN - 1), -SEG_ALIGN)

    def wait_current(rows):
        for e in range(N_EXPERTS):
            seg_copy(slot, e, 0, rows).wait()

    @pl.when(t == n_tiles - 1)
    def _():
        zbuf_ref[...] = jnp.zeros_like(zbuf_ref)
        by_window(small, wait_current)
        tails = []
        for e in range(N_EXPERTS):
            lens_ref[e] = run_ref[e]
            tail = pl.multiple_of(e * cap + run_ref[e], SEG_ALIGN)
            tails.append(pltpu.make_async_copy(zbuf_ref, xs_hbm.at[pl.ds(tail, MOE_HALF)], zsem_ref.at[e]))
            tails[-1].start()
        for cp in tails:
            cp.wait()


def _dispatch(h, logits):
    m = h.shape[0]
    n_tiles = m // MOE_TR
    cap = _expert_cap(m)
    smem = pl.BlockSpec(memory_space=pltpu.SMEM)
    return pl.pallas_call(
        functools.partial(_dispatch_kernel, n_tiles=n_tiles, cap=cap),
        grid=(n_tiles,),
        in_specs=[pl.BlockSpec((MOE_TR, D_MODEL), lambda t: (t, 0)),
                  pl.BlockSpec((LOGIT_ROWS, MOE_TR), lambda t: (0, t))],
        out_specs=[pl.BlockSpec((MOE_TR, LANES), lambda t: (t, 0)), smem, smem, smem,
                   pl.BlockSpec(memory_space=pl.ANY)],
        out_shape=[jax.ShapeDtypeStruct((m, LANES), F32),
                   jax.ShapeDtypeStruct((n_tiles * N_EXPERTS,), jnp.int32),
                   jax.ShapeDtypeStruct((N_EXPERTS,), jnp.int32),
                   jax.ShapeDtypeStruct((n_tiles,), jnp.int32),
                   jax.ShapeDtypeStruct((N_EXPERTS * cap, D_MODEL), BF16)],
        scratch_shapes=[pltpu.SMEM((N_EXPERTS,), jnp.int32),
                        pltpu.VMEM((2, N_EXPERTS, MOE_TR, D_MODEL), BF16),
                        pltpu.VMEM((MOE_HALF, D_MODEL), BF16),
                        pltpu.SemaphoreType.DMA((2, N_EXPERTS)),
                        pltpu.SemaphoreType.DMA((N_EXPERTS,))],
        compiler_params=_params(("arbitrary",)),
        name="moe_dispatch",
    )(h, logits)


def _expert_ffn_kernel(exp_ref, blk_ref, units_ref, x_ref, wg_ref, wu_ref, wd_ref, y_ref, acc_ref):
    w = pl.program_id(0)
    f = pl.program_id(1)

    def tile(rows):
        acc = acc_ref.at[pl.ds(0, rows)]

        @pl.when(f == 0)
        def _():
            acc[...] = jnp.zeros((rows, D_MODEL), F32)

        _swiglu_chunks(acc, x_ref[0:rows, :], lambda c: wg_ref[0, :, c].astype(BF16),
                       lambda c: wu_ref[0, :, c].astype(BF16), lambda c: wd_ref[0, c, :].astype(BF16), MOE_CHUNKS)

        @pl.when(f == pl.num_programs(1) - 1)
        def _():
            y_ref[0:rows, :] = acc[...].astype(BF16)

    for units in range(1, MOE_UNITS + 1):
        pl.when(units_ref[w] == units)(functools.partial(tile, units * MOE_HALF))


def _expert_ffn(xs, exp_w, blk_w, units_w, w_gate, w_up, w_down):
    d_ff = w_gate.shape[-1]
    n_f = d_ff // MOE_TF
    fcol = lambda w, f, exp, blk, units: jnp.where(units[w] != 0, f, n_f - 1)
    grid_spec = pltpu.PrefetchScalarGridSpec(
        num_scalar_prefetch=3,
        grid=(exp_w.shape[0], n_f),
        in_specs=[pl.BlockSpec((MOE_TM, D_MODEL), lambda w, f, exp, blk, valid: (blk[w], 0)),
                  pl.BlockSpec((1, D_MODEL, MOE_TF), lambda w, f, exp, blk, valid: (exp[w], 0, fcol(w, f, exp, blk, valid))),
                  pl.BlockSpec((1, D_MODEL, MOE_TF), lambda w, f, exp, blk, valid: (exp[w], 0, fcol(w, f, exp, blk, valid))),
                  pl.BlockSpec((1, MOE_TF, D_MODEL), lambda w, f, exp, blk, valid: (exp[w], fcol(w, f, exp, blk, valid), 0))],
        out_specs=pl.BlockSpec((MOE_TM, D_MODEL), lambda w, f, exp, blk, valid: (blk[w], 0)),
        scratch_shapes=[pltpu.VMEM((MOE_TM, D_MODEL), F32)])
    return pl.pallas_call(
        _expert_ffn_kernel,
        grid_spec=grid_spec,
        out_shape=jax.ShapeDtypeStruct(xs.shape, BF16),
        compiler_params=_params(("arbitrary", "arbitrary")),
        name="moe_expert_ffn",
    )(exp_w, blk_w, units_w, xs, w_gate, w_up, w_down)


def _combine_kernel(starts_ref, shifts_ref, small_ref, info_ref, h_ref, g_ref, b_ref, ys_hbm, o_ref, ybuf_ref,
                    sem_ref, *, n_tiles):
    t = pl.program_id(0)
    tr = MOE_TR
    slot = t % 2

    def seg_copy(tile, sl, e, rows):
        off = pl.multiple_of(starts_ref[tile * N_EXPERTS + e], SEG_ALIGN)
        return pltpu.make_async_copy(ys_hbm.at[pl.ds(off, rows)], ybuf_ref.at[sl, e, pl.ds(0, rows)],
                                     sem_ref.at[sl, e])

    def by_window(tile, fn):
        flag = small_ref[tile] == 1
        pl.when(flag)(functools.partial(fn, MOE_WIN_SMALL))
        pl.when(jnp.logical_not(flag))(functools.partial(fn, tr))

    def fetch(tile, sl):
        def start(rows):
            for e in range(N_EXPERTS):
                seg_copy(tile, sl, e, rows).start()
        by_window(tile, start)

    @pl.when(t == 0)
    def _():
        fetch(0, 0)

    @pl.when(t + 1 < n_tiles)
    def _():
        fetch(jnp.minimum(t + 1, n_tiles - 1), 1 - slot)

    def gather(rows):
        for e in range(N_EXPERTS):
            seg_copy(t, slot, e, rows).wait()
        info = info_ref[...]
        src_row = lax.broadcasted_iota(jnp.int32, (tr, rows), 1).astype(F32)
        acc = ALPHA * h_ref[...]
        for e in range(N_EXPERTS):
            rank_c = info[:, e:e + 1] + shifts_ref[t * N_EXPERTS + e].astype(F32)
            gate_c = info[:, INFO_GATE + e:INFO_GATE + e + 1]
            onehot = jnp.where((rank_c == src_row) & (gate_c != 0.0), 1.0, 0.0).astype(BF16)
            acc = acc + gate_c * jnp.dot(onehot, ybuf_ref[slot, e, 0:rows, :], preferred_element_type=F32)
        o_ref[...] = _layer_norm(acc, g_ref[...], b_ref[...])

    by_window(t, gather)


def _combine(starts, shifts, small, info, h, ys, ln_g, ln_b):
    m = h.shape[0]
    n_tiles = m // MOE_TR
    grid_spec = pltpu.PrefetchScalarGridSpec(
        num_scalar_prefetch=3,
        grid=(n_tiles,),
        in_specs=[pl.BlockSpec((MOE_TR, LANES), lambda t, *_: (t, 0)),
                  pl.BlockSpec((MOE_TR, D_MODEL), lambda t, *_: (t, 0)),
                  pl.BlockSpec((1, D_MODEL), lambda t, *_: (0, 0)),
                  pl.BlockSpec((1, D_MODEL), lambda t, *_: (0, 0)),
                  pl.BlockSpec(memory_space=pl.ANY)],
        out_specs=pl.BlockSpec((MOE_TR, D_MODEL), lambda t, *_: (t, 0)),
        scratch_shapes=[pltpu.VMEM((2, N_EXPERTS, MOE_TR, D_MODEL), BF16),
                        pltpu.SemaphoreType.DMA((2, N_EXPERTS))])
    return pl.pallas_call(
        functools.partial(_combine_kernel, n_tiles=n_tiles),
        grid_spec=grid_spec,
        out_shape=jax.ShapeDtypeStruct((m, D_MODEL), F32),
        compiler_params=_params(("arbitrary",)),
        name="moe_combine_ln2",
    )(starts, shifts, small, info, h, ln_g.reshape(1, D_MODEL), ln_b.reshape(1, D_MODEL), ys)


def _work_list(lens, m):
    cap = _expert_cap(m)
    n_tiles = m // MOE_TR
    max_rows = 2 * m + (SEG_ALIGN - 1) * min(N_EXPERTS * n_tiles, 2 * m)
    w_max = -(-(max_rows + N_EXPERTS * MOE_HALF) // MOE_TM) + N_EXPERTS
    rows_e = _rows_computed(lens)
    tiles_e = -(-rows_e // MOE_TM)
    ends = jnp.cumsum(tiles_e)
    w = jnp.arange(w_max, dtype=jnp.int32)
    wc = jnp.minimum(w, ends[-1] - 1)
    exp_w = jnp.sum((wc[:, None] >= ends[None, :]).astype(jnp.int32), axis=1)
    tile_w = wc - (ends - tiles_e)[exp_w]
    blk_w = exp_w * (cap // MOE_TM) + tile_w
    units_w = jnp.minimum((rows_e[exp_w] - tile_w * MOE_TM) // MOE_HALF, MOE_UNITS)
    units_w = jnp.where(w < ends[-1], units_w, 0)
    return exp_w.astype(jnp.int32), blk_w.astype(jnp.int32), units_w.astype(jnp.int32)


def _read_windows(offs, small, lens, m):
    cap = _expert_cap(m)
    region = jnp.arange(N_EXPERTS, dtype=jnp.int32) * cap
    rel = offs.reshape(-1, N_EXPERTS) - region
    width = jnp.where(small == 1, MOE_WIN_SMALL, MOE_TR)[:, None]
    start = jnp.minimum(rel, _rows_computed(lens)[None, :] - width)
    return (start + region).reshape(-1).astype(jnp.int32), (rel - start).reshape(-1).astype(jnp.int32)


def _moe_layer(h, logits, w_gate, w_up, w_down, ln_g, ln_b):
    m = h.shape[0]
    info, offs, lens, small, xs = _dispatch(h, logits)
    exp_w, blk_w, units_w = _work_list(lens, m)
    ys = _expert_ffn(xs, exp_w, blk_w, units_w, w_gate, w_up, w_down)
    starts, shifts = _read_windows(offs, small, lens, m)
    return _combine(starts, shifts, small, info, h, ys, ln_g, ln_b)


def _pad_groups(a):
    lead = a.shape[:-1]
    a = a.reshape(*lead, SSD_GROUPS, GROUP_W)
    a = jnp.pad(a, [(0, 0)] * len(lead) + [(0, 0), (0, GROUP_PAD - GROUP_W)])
    return a.reshape(*lead, SSD_PAD)


def _head_lanes(v):
    return jnp.pad(v, (0, LANES - SSD_HEADS)).reshape(1, LANES)


D_IN = 3 * ATT_W + SSD_W + (SSD_W + 2 * SSD_GROUPS * SSD_STATE) + SSD_HEADS + 2 * GM_W
PERM_ROWS = 256


def _permute_w_kernel(w_ref, o_ref):
    src_z = 3 * ATT_W
    src_xs = src_z + SSD_W
    src_b = src_xs + SSD_W
    src_c = src_b + SSD_GROUPS * SSD_STATE
    src_dt = src_c + SSD_GROUPS * SSD_STATE
    src_gu = src_dt + SSD_HEADS
    src_gv = src_gu + GM_W
    o_ref[...] = jnp.zeros_like(o_ref)

    def put(dst, src, width):
        o_ref[:, dst:dst + width] = w_ref[0, :, src:src + width].astype(BF16)

    for g in range(SSD_GROUPS):
        put(COL_Z + g * GROUP_PAD, src_z + g * GROUP_W, GROUP_W)
        put(COL_XS + g * GROUP_PAD, src_xs + g * GROUP_W, GROUP_W)
    put(COL_B, src_b, SSD_GROUPS * SSD_STATE)
    put(COL_C, src_c, SSD_GROUPS * SSD_STATE)
    put(COL_GU, src_gu, GM_W)
    put(COL_GV, src_gv, GM_W)
    put(COL_Q, 0, ATT_W)
    put(COL_K, ATT_W, ATT_W)
    put(COL_V, 2 * ATT_W, ATT_W)
    put(COL_DT, src_dt, SSD_HEADS)


def _permute_w_in(w_in, layer):
    return pl.pallas_call(
        _permute_w_kernel,
        grid=(D_MODEL // PERM_ROWS,),
        in_specs=[pl.BlockSpec((1, PERM_ROWS, D_IN), lambda i: (layer, i, 0))],
        out_specs=pl.BlockSpec((PERM_ROWS, D_W), lambda i: (i, 0)),
        out_shape=jax.ShapeDtypeStruct((D_MODEL, D_W), BF16),
        compiler_params=_params(("parallel",)),
        name="permute_w_in",
    )(w_in)


def _layer_params(i, w_in, conv_w, conv_b, dt_bias, a_log, d_skip, ssd_norm_g, att_norm_g, gm_ln_g, gm_ln_b,
                  gm_w_s, gm_b_s, gm_norm_g, w_out, ln1_g, ln1_b):
    cw, cb = conv_w[i], conv_b[i]
    nbc = SSD_GROUPS * SSD_STATE
    wo = w_out[i]
    return {
        "w_in": _permute_w_in(w_in, i),
        "conv_w_xs": _pad_groups(cw[:, :SSD_W]),
        "conv_w_b": cw[:, SSD_W:SSD_W + nbc],
        "conv_w_c": cw[:, SSD_W + nbc:],
        "conv_b_xs": _pad_groups(cb[:SSD_W]).reshape(1, SSD_PAD),
        "conv_b_b": cb[SSD_W:SSD_W + nbc].reshape(1, nbc),
        "conv_b_c": cb[SSD_W + nbc:].reshape(1, nbc),
        "dt_bias": _head_lanes(dt_bias[i]),
        "a_log": _head_lanes(a_log[i]),
        "d_skip": _pad_groups(jnp.repeat(d_skip[i], SSD_HEAD_DIM)).reshape(1, SSD_PAD),
        "ssd_norm_g": _pad_groups(ssd_norm_g[i]).reshape(1, SSD_PAD),
        "att_norm_g": att_norm_g[i].reshape(1, ATT_W),
        "gm_ln_g": gm_ln_g[i].reshape(1, GM_W),
        "gm_ln_b": gm_ln_b[i].reshape(1, GM_W),
        "gm_w_s": gm_w_s[i],
        "gm_b_s": jnp.repeat(gm_b_s[i].T, GM_W // GM_GROUPS, axis=1),
        "gm_norm_g": gm_norm_g[i].reshape(1, GM_W),
        "w_out_att": wo[:ATT_W].astype(BF16),
        "w_out_ssd": _pad_groups(wo[ATT_W:ATT_W + SSD_W].T).T.astype(BF16),
        "w_out_gm": wo[ATT_W + SSD_W:].astype(BF16),
        "ln1_g": ln1_g[i].reshape(1, D_MODEL),
        "ln1_b": ln1_b[i].reshape(1, D_MODEL),
    }


def _rope_tables(positions):
    inv = ROPE_THETA ** (-jnp.arange(0, ROPE_DIM, 2, dtype=F32) / ROPE_DIM)
    d = jnp.arange(LANES) % HEAD_DIM
    inv_lane = jnp.where(d < ROPE_DIM, inv[d % ROPE_HALF], 0.0)
    ang = positions.astype(F32)[..., None] * inv_lane
    cos, sin = jnp.cos(ang), jnp.sin(ang)
    return cos, jnp.where(d < ROPE_HALF, -sin, 0.0), jnp.where(d >= ROPE_HALF, sin, 0.0)


def kernel(x, positions, ln_in_g, ln_in_b, w_in, conv_w, conv_b, dt_bias, a_log, d_skip, ssd_norm_g, att_norm_g, gm_ln_g, gm_ln_b, gm_w_s, gm_b_s, gm_norm_g, w_out, ln1_g, ln1_b, ln2_g, ln2_b, ffn_w_gate, ffn_w_up, ffn_w_down, router_w, moe_w_gate, moe_w_up, moe_w_down):
    bsz, seq, _ = x.shape
    m = bsz * seq
    rope_c, rope_sa, rope_sb = _rope_tables(positions)
    h = x.reshape(m, D_MODEL)
    for i in range(DEPTH):
        p = _layer_params(i, w_in, conv_w, conv_b, dt_bias, a_log, d_skip, ssd_norm_g, att_norm_g, gm_ln_g,
                          gm_ln_b, gm_w_s, gm_b_s, gm_norm_g, w_out, ln1_g, ln1_b)
        if i == 0:
            proj, gm, h = _in_proj(h, p["w_in"], p, ln=(ln_in_g, ln_in_b))
        else:
            proj, gm = _in_proj(h, p["w_in"], p)
        proj3 = proj.reshape(bsz, seq, D_PROJ)
        att = _attention(proj3, rope_c, rope_sa, rope_sb).reshape(m, ATT_W)
        ssd = _ssd(proj3, p).reshape(m, SSD_PAD)
        j = i // 2
        if i % 2 == 0:
            (h,) = _out_proj(att, ssd, gm, h, p)
            h = _ffn(h, ffn_w_gate[j].astype(BF16), ffn_w_up[j].astype(BF16), ffn_w_down[j].astype(BF16),
                     ln2_g[i], ln2_b[i])
        else:
            h, logits = _out_proj(att, ssd, gm, h, p, w_router=router_w[j])
            h = _moe_layer(h, logits, moe_w_gate[j], moe_w_up[j], moe_w_down[j], ln2_g[i], ln2_b[i])
    return h.reshape(bsz, seq, D_MODEL)
```
